```python
import jax
import jax.numpy as jnp
from jax import lax
import numpy as np

D_MODEL = 2048
BATCH = 8
SEQ = 2048
DEPTH = 2

CHUNK = 64
EPS = 1e-6
NEG = -1e30
CONV_K = 4

HD_A = 128
W_A = D_MODEL // 2
H_A = W_A // HD_A
LOOKBACK = 8
BAND = (LOOKBACK + 1) * CHUNK
REL_MAX = 256
H_B = 4
W_B = D_MODEL // 2
HD_B = W_B // H_B
FORGET_BIAS = 3.0
W_C = D_MODEL // 2
N_BLK_C = 8
BLK_C = W_C // N_BLK_C
C_RG = 8.0
HD_D = 128
W_D = D_MODEL // 2
H_D = W_D // HD_D

EV_SIZES = (W_A, W_A, W_A, W_A, W_B, W_B, W_B, W_B, W_B, 2 * H_B)
OD_SIZES = (W_C, W_C, W_D, W_D, W_D, W_D, H_D, H_D)
EV_IN = sum(EV_SIZES)
OD_IN = sum(OD_SIZES)
W_EV = W_A + W_B
W_OD = W_C + W_D
N_EVEN = (DEPTH + 1) // 2
N_ODD = DEPTH // 2

kernel_name = "hybrid_chunkattn_mlstm_rglru_gdn"


def split_cols(t, sizes):
    offs = np.cumsum(sizes)[:-1].tolist()
    return jnp.split(t, offs, axis=-1)


def rmsnorm(x, g):
    xf = x.astype(jnp.float32)
    y = xf * lax.rsqrt(jnp.mean(xf * xf, axis=-1, keepdims=True) + EPS)
    return (y * g.astype(jnp.float32)).astype(x.dtype)


def l2norm(x):
    return x * lax.rsqrt(jnp.sum(x * x, axis=-1, keepdims=True) + EPS)


def causal_dwconv(x, w):
    s = x.shape[1]
    xp = jnp.pad(x, ((0, 0), (CONV_K - 1, 0), (0, 0)))
    return sum(xp[:, j:j + s] * w[j] for j in range(CONV_K))


def to_chunks(t):
    b, s, h = t.shape[:3]
    t = t.reshape(b, s // CHUNK, CHUNK, h, *t.shape[3:])
    return jnp.moveaxis(t, (1, 3), (0, 2))


def from_chunks(t):
    nc, b, h, l, d = t.shape
    return jnp.moveaxis(t, (0, 2), (1, 3)).reshape(b, nc * l, h * d)


def chunk_rel_attention(q, k, v, qn_g, kn_g, rel_bias):
    b, s, h, d = q.shape
    nc = s // CHUNK
    q = rmsnorm(q, qn_g).astype(jnp.float32) * (d ** -0.5)
    k = rmsnorm(k, kn_g).astype(jnp.float32)
    v = v.astype(jnp.float32)
    qc = q.reshape(b, nc, CHUNK, h, d)
    pad = ((0, 0), (LOOKBACK, 0), (0, 0), (0, 0), (0, 0))
    kp = jnp.pad(k.reshape(b, nc, CHUNK, h, d), pad)
    vp = jnp.pad(v.reshape(b, nc, CHUNK, h, d), pad)
    idx = jnp.arange(nc)[:, None] + jnp.arange(LOOKBACK + 1)[None, :]
    kb = kp[:, idx].reshape(b, nc, BAND, h, d)
    vb = vp[:, idx].reshape(b, nc, BAND, h, d)
    sc = jnp.einsum('bnqhd,bnkhd->bhnqk', qc, kb)
    qpos = LOOKBACK * CHUNK + jnp.arange(CHUNK)
    kpos = jnp.arange(BAND)
    rel = jnp.clip(qpos[:, None] - kpos[None, :], -REL_MAX, REL_MAX) + REL_MAX
    sc = sc + rel_bias.astype(jnp.float32)[:, rel][None, :, None]
    valid = (idx - LOOKBACK) >= 0
    valid = jnp.repeat(valid, CHUNK, axis=1)
    sc = jnp.where(valid[None, None, :, None, :], sc, NEG)
    p = jax.nn.softmax(sc, axis=-1)
    o = jnp.einsum('bhnqk,bnkhd->bnqhd', p, vb)
    return o.reshape(b, s, h * d)


def mlstm_chunkwise(q, k, v, li, lf):
    b, s, h, d = q.shape
    q = q * (d ** -0.5)
    qc, kc, vc = to_chunks(q), to_chunks(k), to_chunks(v)
    lic, lfc = to_chunks(li), to_chunks(lf)
    tril = jnp.tril(jnp.ones((CHUNK, CHUNK), dtype=bool))

    def step(carry, inp):
        cm, nm, m = carry
        qn, kn, vn, lin, lfn = inp
        bcum = jnp.cumsum(lfn, axis=-1)
        dmat = bcum[..., :, None] - bcum[..., None, :] + lin[..., None, :]
        dmat = jnp.where(tril, dmat, NEG)
        m_t = jnp.maximum(bcum + m[..., None], jnp.max(dmat, axis=-1))
        w_inter = jnp.exp(bcum + m[..., None] - m_t)
        p = jnp.exp(dmat - m_t[..., None]) * jnp.einsum('bhtd,bhsd->bhts', qn, kn)
        num = w_inter[..., None] * jnp.einsum('bhtk,bhkv->bhtv', qn, cm) + jnp.einsum('bhts,bhsv->bhtv', p, vn)
        den = w_inter * jnp.einsum('bhtk,bhk->bht', qn, nm) + jnp.sum(p, axis=-1)
        hout = num / jnp.maximum(jnp.abs(den), jnp.exp(-m_t))[..., None]
        b_last = bcum[..., -1]
        gs = b_last[..., None] - bcum + lin
        m_new = jnp.maximum(b_last + m, jnp.max(gs, axis=-1))
        w_c = jnp.exp(b_last + m - m_new)
        ws = jnp.exp(gs - m_new[..., None])
        cm = w_c[..., None, None] * cm + jnp.einsum('bhs,bhsk,bhsv->bhkv', ws, kn, vn)
        nm = w_c[..., None] * nm + jnp.einsum('bhs,bhsk->bhk', ws, kn)
        return (cm, nm, m_new), hout

    init = (jnp.zeros((b, h, d, d), jnp.float32), jnp.zeros((b, h, d), jnp.float32), jnp.zeros((b, h), jnp.float32))
    _, hs = lax.scan(step, init, (qc, kc, vc, lic, lfc))
    return from_chunks(hs)


def gated_delta_chunkwise(q, k, v, beta, g):
    b, s, h, d = q.shape
    qc = to_chunks(q) * (d ** -0.5)
    kc, vc = to_chunks(k), to_chunks(v)
    bc, gc = to_chunks(beta), to_chunks(g)
    tril = jnp.tril(jnp.ones((CHUNK, CHUNK), dtype=bool))
    strict = jnp.tril(jnp.ones((CHUNK, CHUNK), dtype=bool), k=-1)
    decay = jnp.cumsum(gc, axis=-1)
    gam = jnp.exp(jnp.where(tril, decay[..., :, None] - decay[..., None, :], NEG))
    kbeta = kc * bc[..., None]
    a_mat = jnp.where(strict, jnp.einsum('nbhtd,nbhsd->nbhts', kbeta, kc) * gam, 0.0)
    eye = jnp.eye(CHUNK, dtype=jnp.float32)
    t_inv = lax.linalg.triangular_solve(eye + a_mat, jnp.broadcast_to(eye, a_mat.shape), left_side=True, lower=True, unit_diagonal=True)
    u = jnp.einsum('nbhts,nbhsv->nbhtv', t_inv, vc * bc[..., None])
    w = jnp.einsum('nbhts,nbhsk->nbhtk', t_inv, kbeta * jnp.exp(decay)[..., None])
    qd = qc * jnp.exp(decay)[..., None]
    qk = jnp.einsum('nbhtd,nbhsd->nbhts', qc, kc) * gam
    kd = kc * jnp.exp(decay[..., -1:] - decay)[..., None]
    d_last = jnp.exp(decay[..., -1])

    def step(st, inp):
        un, wn, qdn, qkn, kdn, dln = inp
        v_new = un - jnp.einsum('bhtk,bhkv->bhtv', wn, st)
        o = jnp.einsum('bhtk,bhkv->bhtv', qdn, st) + jnp.einsum('bhts,bhsv->bhtv', qkn, v_new)
        st = dln[..., None, None] * st + jnp.einsum('bhsk,bhsv->bhkv', kdn, v_new)
        return st, o

    init = jnp.zeros((b, h, d, d), jnp.float32)
    _, os_ = lax.scan(step, init, (u, w, qd, qk, kd, d_last))
    return jnp.moveaxis(os_, (0, 2), (1, 3)).reshape(b, s, h, d)


def rg_lru(xconv, gate_w, gate_b, lam):
    b, s, _ = xconv.shape
    gates = jnp.einsum('bsnc,ncd->bsnd', xconv.reshape(b, s, N_BLK_C, BLK_C), gate_w.astype(jnp.float32))
    gb = gate_b.astype(jnp.float32)
    r = jax.nn.sigmoid(gates[..., :BLK_C].reshape(b, s, W_C) + gb[:W_C])
    i = jax.nn.sigmoid(gates[..., BLK_C:].reshape(b, s, W_C) + gb[W_C:])
    log_a = -C_RG * r * jax.nn.softplus(-lam.astype(jnp.float32))
    a = jnp.exp(log_a)
    inp = jnp.sqrt(-jnp.expm1(2.0 * log_a)) * (i * xconv)

    def combine(e1, e2):
        a1, b1 = e1
        a2, b2 = e2
        return a1 * a2, a2 * b1 + b2

    _, hs = lax.associative_scan(combine, (a, inp), axis=1)
    return hs


def even_layer(x, norm_g, w_in, if_bias, qn_g, kn_g, rel_bias, w_out):
    b, s, _ = x.shape
    proj = rmsnorm(x, norm_g) @ w_in
    qa, ka, va, za, qb, kb, vb, ob, zb, gif = split_cols(proj, EV_SIZES)
    ya = chunk_rel_attention(qa.reshape(b, s, H_A, HD_A), ka.reshape(b, s, H_A, HD_A), va.reshape(b, s, H_A, HD_A), qn_g, kn_g, rel_bias)
    ya = ya * jax.nn.silu(za.astype(jnp.float32))
    gif = gif.astype(jnp.float32) + if_bias.astype(jnp.float32)
    li = gif[..., :H_B]
    lf = jax.nn.log_sigmoid(gif[..., H_B:])
    f32 = jnp.float32
    hb = mlstm_chunkwise(qb.astype(f32).reshape(b, s, H_B, HD_B), kb.astype(f32).reshape(b, s, H_B, HD_B), vb.astype(f32).reshape(b, s, H_B, HD_B), li, lf)
    yb = jax.nn.sigmoid(ob.astype(f32)) * hb * jax.nn.silu(zb.astype(f32))
    y = jnp.concatenate([ya, yb], axis=-1).astype(x.dtype)
    return y @ w_out


def odd_layer(x, norm_g, w_in, conv_c_w, conv_c_b, gate_w, gate_b, lam, conv_d_w, a_log, dt_bias, onorm_g, w_out):
    b, s, _ = x.shape
    f32 = jnp.float32
    proj = rmsnorm(x, norm_g) @ w_in
    xc, zc, qd, kd, vd, zd, a_pre, b_pre = split_cols(proj, OD_SIZES)
    xconv = causal_dwconv(xc.astype(f32), conv_c_w.astype(f32)) + conv_c_b.astype(f32)
    yc = rg_lru(xconv, gate_w, gate_b, lam) * jax.nn.silu(zc.astype(f32))
    qkv = jax.nn.silu(causal_dwconv(jnp.concatenate([qd, kd, vd], axis=-1).astype(f32), conv_d_w.astype(f32)))
    q, k, v = jnp.split(qkv, 3, axis=-1)
    q = l2norm(q.reshape(b, s, H_D, HD_D))
    k = l2norm(k.reshape(b, s, H_D, HD_D))
    v = v.reshape(b, s, H_D, HD_D)
    beta = jax.nn.sigmoid(b_pre.astype(f32))
    g = -jnp.exp(a_log.astype(f32)) * jax.nn.softplus(a_pre.astype(f32) + dt_bias.astype(f32))
    od = gated_delta_chunkwise(q, k, v, beta, g)
    yd = rmsnorm(od, onorm_g).reshape(b, s, W_D) * jax.nn.silu(zd.astype(f32))
    y = jnp.concatenate([yc, yd], axis=-1).astype(x.dtype)
    return y @ w_out


def setup_inputs(seed: int = 0) -> dict:
    key = jax.random.key(seed)
    ks = jax.random.split(key, 24)
    f32 = jnp.float32
    ne, no = N_EVEN, N_ODD

    def nrm(k, shape, scale):
        return scale * jax.random.normal(k, shape, f32)

    x = nrm(ks[0], (BATCH, SEQ, D_MODEL), 1.0)
    ev_norm = 1.0 + nrm(ks[1], (ne, D_MODEL), 0.05)
    ev_w_in = nrm(ks[2], (ne, D_MODEL, EV_IN), D_MODEL ** -0.5)
    ev_if_bias = jnp.concatenate([nrm(ks[3], (ne, H_B), 0.1), FORGET_BIAS + nrm(ks[4], (ne, H_B), 0.5)], axis=-1)
    ev_qn_gain = 1.0 + nrm(ks[5], (ne, HD_A), 0.05)
    ev_kn_gain = 1.0 + nrm(ks[6], (ne, HD_A), 0.05)
    ev_rel_bias = nrm(ks[7], (ne, H_A, 2 * REL_MAX + 1), 0.2)
    ev_w_out = nrm(ks[8], (ne, W_EV, D_MODEL), W_EV ** -0.5)
    od_norm = 1.0 + nrm(ks[9], (no, D_MODEL), 0.05)
    od_w_in = nrm(ks[10], (no, D_MODEL, OD_IN), D_MODEL ** -0.5)
    od_conv_c_w = nrm(ks[11], (no, CONV_K, W_C), CONV_K ** -0.5)
    od_conv_c_b = nrm(ks[12], (no, W_C), 0.01)
    od_gate_w = nrm(ks[13], (no, N_BLK_C, BLK_C, 2 * BLK_C), BLK_C ** -0.5)
    od_gate_b = nrm(ks[14], (no, 2 * W_C), 0.1)
    a0 = jax.random.uniform(ks[15], (no, W_C), f32, minval=0.9, maxval=0.999)
    sig = a0 ** (1.0 / C_RG)
    od_lambda = jnp.log(sig) - jnp.log1p(-sig)
    od_conv_d_w = nrm(ks[16], (no, CONV_K, 3 * W_D), CONV_K ** -0.5)
    od_a_log = jnp.log(jax.random.uniform(ks[17], (no, H_D), f32, minval=1.0, maxval=16.0))
    u = jax.random.uniform(ks[18], (no, H_D), f32)
    dt = jnp.exp(np.log(0.001) + u * (np.log(0.1) - np.log(0.001)))
    od_dt_bias = dt + jnp.log(-jnp.expm1(-dt))
    od_onorm = 1.0 + nrm(ks[19], (no, HD_D), 0.05)
    od_w_out = nrm(ks[20], (no, W_OD, D_MODEL), W_OD ** -0.5)
    return {"x": x, "ev_norm": ev_norm, "ev_w_in": ev_w_in, "ev_if_bias": ev_if_bias, "ev_qn_gain": ev_qn_gain, "ev_kn_gain": ev_kn_gain, "ev_rel_bias": ev_rel_bias, "ev_w_out": ev_w_out, "od_norm": od_norm, "od_w_in": od_w_in, "od_conv_c_w": od_conv_c_w, "od_conv_c_b": od_conv_c_b, "od_gate_w": od_gate_w, "od_gate_b": od_gate_b, "od_lambda": od_lambda, "od_conv_d_w": od_conv_d_w, "od_a_log": od_a_log, "od_dt_bias": od_dt_bias, "od_onorm": od_onorm, "od_w_out": od_w_out}


def reference(x, ev_norm, ev_w_in, ev_if_bias, ev_qn_gain, ev_kn_gain, ev_rel_bias, ev_w_out, od_norm, od_w_in, od_conv_c_w, od_conv_c_b, od_gate_w, od_gate_b, od_lambda, od_conv_d_w, od_a_log, od_dt_bias, od_onorm, od_w_out):
    for layer in range(DEPTH):
        j = layer // 2
        if layer % 2 == 0:
            x = x + even_layer(x, ev_norm[j], ev_w_in[j], ev_if_bias[j], ev_qn_gain[j], ev_kn_gain[j], ev_rel_bias[j], ev_w_out[j])
        else:
            x = x + odd_layer(x, od_norm[j], od_w_in[j], od_conv_c_w[j], od_conv_c_b[j], od_gate_w[j], od_gate_b[j], od_lambda[j], od_conv_d_w[j], od_a_log[j], od_dt_bias[j], od_onorm[j], od_w_out[j])
    return x
```

```python
import functools

import numpy as np
import jax
import jax.numpy as jnp
from jax import lax
from jax.experimental import pallas as pl
from jax.experimental.pallas import tpu as pltpu

F32 = jnp.float32
BF16 = jnp.bfloat16

CHUNK = 64
EPS = 1e-6
NEG = -1e30
CONV_K = 4
LANES = 128
HD_A = 128
LOOKBACK = 8
BAND = (LOOKBACK + 1) * CHUNK
REL_MAX = 256
REL_W = 640
H_B = 4
N_BLK_C = 8
C_RG = 8.0
HD_D = 128

VMEM_LIMIT = 56 * 1024 * 1024


def _params(n_axes):
    return pltpu.CompilerParams(dimension_semantics=("arbitrary",) * n_axes, vmem_limit_bytes=VMEM_LIMIT)


def _mm(a, b):
    return jnp.dot(a.astype(BF16), b.astype(BF16), preferred_element_type=F32)


def _mm_nt(a, b):
    return lax.dot_general(a.astype(BF16), b.astype(BF16), (((1,), (1,)), ((), ())), preferred_element_type=F32)


def _mm_tn(a, b):
    return lax.dot_general(a.astype(BF16), b.astype(BF16), (((0,), (0,)), ((), ())), preferred_element_type=F32)


def _mm_f32(a, b):
    return jnp.dot(a, b, preferred_element_type=F32, precision=lax.Precision.HIGHEST)


def _softplus(x):
    return jnp.maximum(x, 0.0) + jnp.log1p(jnp.exp(-jnp.abs(x)))


def _sigmoid(x):
    return 1.0 / (1.0 + jnp.exp(-x))


def _silu(x):
    return x * _sigmoid(x)


def _iota2(shape, axis):
    return lax.broadcasted_iota(jnp.int32, shape, axis)


def _chunk_tril(n):
    r, c = _iota2((n, n), 0), _iota2((n, n), 1)
    same = (r >> 6) == (c >> 6)
    return jnp.where(same & (c <= r), 1.0, 0.0).astype(F32)


def _triu64():
    r, c = _iota2((CHUNK, CHUNK), 0), _iota2((CHUNK, CHUNK), 1)
    return jnp.where(r <= c, 1.0, 0.0).astype(F32)


def _inproj_body(x_ref, g_ref, w_ref, wg_ref, o_ref, og_ref, xn_ref):
    @pl.when(pl.program_id(1) == 0)
    def _():
        x = x_ref[...]
        ms = jnp.mean(x * x, axis=-1, keepdims=True)
        xn = ((x * lax.rsqrt(ms + EPS)) * g_ref[...]).astype(BF16)
        xn_ref[...] = xn
        og_ref[...] = jnp.dot(xn, wg_ref[...], preferred_element_type=F32)

    o_ref[...] = jnp.dot(xn_ref[...], w_ref[...], preferred_element_type=F32).astype(o_ref.dtype)


def _inproj(x2, g, w_main, w_gate, tm, tn):
    m, d = x2.shape
    n = w_main.shape[1]
    return pl.pallas_call(
        _inproj_body,
        grid=(m // tm, n // tn),
        in_specs=[
            pl.BlockSpec((tm, d), lambda i, j: (i, 0)),
            pl.BlockSpec((1, d), lambda i, j: (0, 0)),
            pl.BlockSpec((d, tn), lambda i, j: (0, j)),
            pl.BlockSpec((d, LANES), lambda i, j: (0, 0)),
        ],
        out_specs=[
            pl.BlockSpec((tm, tn), lambda i, j: (i, j)),
            pl.BlockSpec((tm, LANES), lambda i, j: (i, 0)),
        ],
        out_shape=[jax.ShapeDtypeStruct((m, n), BF16), jax.ShapeDtypeStruct((m, LANES), F32)],
        scratch_shapes=[pltpu.VMEM((tm, d), BF16)],
        compiler_params=_params(2),
        name="inproj",
    )(x2, g, w_main, w_gate)


def _outproj_body(ya_ref, yb_ref, wa_ref, wb_ref, x_ref, o_ref):
    acc = jnp.dot(ya_ref[...], wa_ref[...], preferred_element_type=F32)
    acc = acc + jnp.dot(yb_ref[...], wb_ref[...], preferred_element_type=F32)
    o_ref[...] = x_ref[...] + acc


def _outproj(ya, yb, w, x2, tm, tn):
    m, d = x2.shape
    kh = ya.shape[1]
    return pl.pallas_call(
        _outproj_body,
        grid=(d // tn, m // tm),
        in_specs=[
            pl.BlockSpec((tm, kh), lambda j, i: (i, 0)),
            pl.BlockSpec((tm, kh), lambda j, i: (i, 0)),
            pl.BlockSpec((kh, tn), lambda j, i: (0, j)),
            pl.BlockSpec((kh, tn), lambda j, i: (1, j)),
            pl.BlockSpec((tm, tn), lambda j, i: (i, j)),
        ],
        out_specs=pl.BlockSpec((tm, tn), lambda j, i: (i, j)),
        out_shape=jax.ShapeDtypeStruct((m, d), F32),
        compiler_params=_params(2),
        name="outproj",
    )(ya, yb, w, w, x2)


def _attn_body(q_ref, k_ref, v_ref, z_ref, qg_ref, kg_ref, c_ref, o_ref, qs, ks, vs, bias_s):
    s = q_ref.shape[0]
    pad = LOOKBACK * CHUNK
    q = q_ref[...].astype(F32)
    q = q * lax.rsqrt(jnp.mean(q * q, axis=-1, keepdims=True) + EPS) * qg_ref[...] * (HD_A ** -0.5)
    qs[...] = q.astype(BF16)
    k = k_ref[...].astype(F32)
    k = k * lax.rsqrt(jnp.mean(k * k, axis=-1, keepdims=True) + EPS) * kg_ref[...]
    ks[0:pad, :] = jnp.zeros((pad, HD_A), BF16)
    ks[pad:pad + s, :] = k.astype(BF16)
    vs[0:pad, :] = jnp.zeros((pad, HD_A), BF16)
    vs[pad:pad + s, :] = v_ref[...]
    cb = jnp.broadcast_to(c_ref[...], (CHUNK, REL_W))
    bias_s[...] = pltpu.roll(cb, 0, 1, stride=1, stride_axis=0)[:, :BAND]
    col = _iota2((CHUNK, BAND), 1)

    def chunk(n, carry):
        r0 = pl.multiple_of(n * CHUNK, CHUNK)
        qn = qs[pl.ds(r0, CHUNK), :]
        kb = ks[pl.ds(r0, BAND), :]
        vb = vs[pl.ds(r0, BAND), :]
        sc = _mm_nt(qn, kb) + bias_s[...]
        sc = jnp.where(col >= (LOOKBACK - n) * CHUNK, sc, NEG)
        mx = jnp.max(sc, axis=-1, keepdims=True)
        p = jnp.exp(sc - mx)
        den = jnp.sum(p, axis=-1, keepdims=True)
        o = _mm(p, vb) / den
        z = z_ref[pl.ds(r0, CHUNK), :].astype(F32)
        o_ref[pl.ds(r0, CHUNK), :] = (o * _silu(z)).astype(o_ref.dtype)
        return carry

    lax.fori_loop(0, s // CHUNK, chunk, 0)


def _attn(p3, qg, kg, crel):
    b, s, _ = p3.shape
    h_a = crel.shape[0]

    def col_spec(sec):
        return pl.BlockSpec((None, s, HD_A), lambda bi, h, sec=sec: (bi, 0, sec * h_a + h))

    return pl.pallas_call(
        _attn_body,
        grid=(b, h_a),
        in_specs=[
            col_spec(0), col_spec(1), col_spec(2), col_spec(3),
            pl.BlockSpec((1, HD_A), lambda bi, h: (0, 0)),
            pl.BlockSpec((1, HD_A), lambda bi, h: (0, 0)),
            pl.BlockSpec((None, 1, REL_W), lambda bi, h: (h, 0, 0)),
        ],
        out_specs=pl.BlockSpec((None, s, HD_A), lambda bi, h: (bi, 0, h)),
        out_shape=jax.ShapeDtypeStruct((b, s, h_a * HD_A), BF16),
        scratch_shapes=[
            pltpu.VMEM((s, HD_A), BF16),
            pltpu.VMEM((s + LOOKBACK * CHUNK, HD_A), BF16),
            pltpu.VMEM((s + LOOKBACK * CHUNK, HD_A), BF16),
            pltpu.VMEM((CHUNK, BAND), F32),
        ],
        compiler_params=_params(2),
        name="chunk_attn",
    )(p3, p3, p3, p3, qg, kg, crel)


def _mlstm_body(q_ref, k_ref, v_ref, o_ref, z_ref, g_ref, gt_ref, bc_ref, br_ref, y_ref,
                c_st, n_st, m_st, col_b, col_i, row_b, row_i):
    ts = q_ref.shape[0]
    nct = ts // CHUNK
    hd = q_ref.shape[1] // H_B
    scale = hd ** -0.5

    @pl.when(pl.program_id(1) == 0)
    def _():
        c_st[...] = jnp.zeros_like(c_st)
        n_st[...] = jnp.zeros_like(n_st)
        m_st[...] = jnp.zeros_like(m_st)

    g = g_ref[...] + bc_ref[...]
    col_i[...] = g
    col_b[...] = _mm_f32(_chunk_tril(ts), -_softplus(-g))
    gt = gt_ref[...] + br_ref[...]
    lf_r = -_softplus(-gt[H_B:2 * H_B])
    row_b[...] = _mm_f32(lf_r.reshape(H_B * nct, CHUNK), _triu64()).reshape(H_B, nct, CHUNK)
    row_i[...] = gt[0:H_B]
    tril = _iota2((CHUNK, CHUNK), 1) <= _iota2((CHUNK, CHUNK), 0)

    def chunk(c, carry):
        rows = pl.ds(pl.multiple_of(c * CHUNK, CHUNK), CHUNK)
        bc_all = col_b[rows, :]
        li_all = col_i[rows, :]
        for h in range(H_B):
            hc = slice(h * hd, (h + 1) * hd)
            bc = bc_all[:, H_B + h:H_B + h + 1]
            lic = li_all[:, h:h + 1]
            br = row_b[h, pl.ds(c, 1), :]
            lir = row_i[h, pl.ds(c, 1), :]
            m = m_st[h][0:1, 0:1]
            qn, kn, vn = q_ref[rows, hc], k_ref[rows, hc], v_ref[rows, hc]
            dmat = jnp.where(tril, bc - br + lir, NEG)
            mt = jnp.maximum(bc + m, jnp.max(dmat, axis=-1, keepdims=True))
            w_inter = jnp.exp(bc + m - mt)
            p = jnp.exp(dmat - mt) * (_mm_nt(qn, kn) * scale)
            cm = c_st[h]
            nm = n_st[h][0:1, :]
            num = w_inter * (_mm(qn, cm) * scale) + _mm(p, vn)
            qn_dot = jnp.sum(qn.astype(F32) * nm, axis=-1, keepdims=True) * scale
            den = w_inter * qn_dot + jnp.sum(p, axis=-1, keepdims=True)
            hout = num / jnp.maximum(jnp.abs(den), jnp.exp(-mt))
            b_last = bc[CHUNK - 1:CHUNK, :]
            gs = b_last - br + lir
            m_new = jnp.maximum(b_last + m, jnp.max(gs, axis=-1, keepdims=True))
            w_c = jnp.exp(b_last + m - m_new)
            wk = jnp.exp(b_last - bc + lic - m_new) * kn.astype(F32)
            c_st[h] = w_c * cm + _mm_tn(wk, vn)
            n_st[h] = jnp.broadcast_to(w_c * nm + jnp.sum(wk, axis=0, keepdims=True), n_st.shape[1:])
            m_st[h] = jnp.broadcast_to(m_new, m_st.shape[1:])
            og = o_ref[rows, hc].astype(F32)
            zg = z_ref[rows, hc].astype(F32)
            y_ref[rows, hc] = (_sigmoid(og) * hout * _silu(zg)).astype(y_ref.dtype)
        return carry

    lax.fori_loop(0, nct, chunk, 0)


def _mlstm(p3, g3, gt4, bias_col, bias_row, ts):
    b, s, _ = p3.shape
    w_b = 1024
    hd = w_b // H_B
    nct = ts // CHUNK

    def col_spec(sec):
        return pl.BlockSpec((None, ts, w_b), lambda bi, j, sec=sec: (bi, j, 4 + sec))

    return pl.pallas_call(
        _mlstm_body,
        grid=(b, s // ts),
        in_specs=[
            col_spec(0), col_spec(1), col_spec(2), col_spec(3), col_spec(4),
            pl.BlockSpec((None, ts, LANES), lambda bi, j: (bi, j, 0)),
            pl.BlockSpec((None, 2 * H_B, nct, CHUNK), lambda bi, j: (bi, 0, j, 0)),
            pl.BlockSpec((1, LANES), lambda bi, j: (0, 0)),
            pl.BlockSpec((2 * H_B, 1, 1), lambda bi, j: (0, 0, 0)),
        ],
        out_specs=pl.BlockSpec((None, ts, w_b), lambda bi, j: (bi, j, 0)),
        out_shape=jax.ShapeDtypeStruct((b, s, w_b), BF16),
        scratch_shapes=[
            pltpu.VMEM((H_B, hd, hd), F32),
            pltpu.VMEM((H_B, 8, hd), F32),
            pltpu.VMEM((H_B, 8, LANES), F32),
            pltpu.VMEM((ts, LANES), F32),
            pltpu.VMEM((ts, LANES), F32),
            pltpu.VMEM((H_B, nct, CHUNK), F32),
            pltpu.VMEM((H_B, nct, CHUNK), F32),
        ],
        compiler_params=_params(2),
        name="mlstm",
    )(p3, p3, p3, p3, p3, g3, gt4, bias_col, bias_row)


def _shift_rows(x, d, row):
    return jnp.where(row >= d, pltpu.roll(x, d, 0), 0.0)


def _rglru_body(x_ref, z_ref, cw_ref, cb_ref, gw_ref, gbr_ref, gbi_ref, lam_ref, y_ref):
    s, blk = x_ref.shape
    x = x_ref[...].astype(F32)
    row = _iota2((s, blk), 0)
    cw = cw_ref[...]
    xc = cw[3:4, :] * x + cb_ref[...]
    for d in range(1, CONV_K):
        xc = xc + cw[CONV_K - 1 - d:CONV_K - d, :] * _shift_rows(x, d, row)
    gates = _mm(xc, gw_ref[...])
    r = _sigmoid(gates[:, :blk] + gbr_ref[...])
    i = _sigmoid(gates[:, blk:] + gbi_ref[...])
    log_a = -C_RG * r * _softplus(-lam_ref[...])
    a = jnp.exp(log_a)
    bb = jnp.sqrt(-jnp.tanh(log_a) * (a * a + 1.0)) * (i * xc)
    d = 1
    while d < s:
        a_sh = pltpu.roll(a, d, 0)
        b_sh = pltpu.roll(bb, d, 0)
        keep = row >= d
        bb = jnp.where(keep, a * b_sh + bb, bb)
        a = jnp.where(keep, a * a_sh, a)
        d *= 2
    z = z_ref[...].astype(F32)
    y_ref[...] = (bb * _silu(z)).astype(y_ref.dtype)


def _rglru(p3, cw, cb, gw, gb, lam):
    b, s, _ = p3.shape
    w_c = cw.shape[1]
    blk = w_c // N_BLK_C
    return pl.pallas_call(
        _rglru_body,
        grid=(b, N_BLK_C),
        in_specs=[
            pl.BlockSpec((None, s, blk), lambda bi, n: (bi, 0, n)),
            pl.BlockSpec((None, s, blk), lambda bi, n: (bi, 0, N_BLK_C + n)),
            pl.BlockSpec((CONV_K, blk), lambda bi, n: (0, n)),
            pl.BlockSpec((1, blk), lambda bi, n: (0, n)),
            pl.BlockSpec((None, blk, 2 * blk), lambda bi, n: (n, 0, 0)),
            pl.BlockSpec((1, blk), lambda bi, n: (0, n)),
            pl.BlockSpec((1, blk), lambda bi, n: (0, N_BLK_C + n)),
            pl.BlockSpec((1, blk), lambda bi, n: (0, n)),
        ],
        out_specs=pl.BlockSpec((None, s, blk), lambda bi, n: (bi, 0, n)),
        out_shape=jax.ShapeDtypeStruct((b, s, w_c), BF16),
        compiler_params=_params(2),
        name="rglru",
    )(p3, p3, cw, cb, gw, gb, gb, lam)


def _unit_lower_inverse(a_strict):
    r, c = _iota2((CHUNK, CHUNK), 0), _iota2((CHUNK, CHUNK), 1)
    eye = jnp.where(r == c, 1.0, 0.0).astype(F32)

    def pair_mask(sh):
        rb, cb_ = r >> sh, c >> sh
        return ((rb & 1) == 1) & (cb_ == rb - 1)

    t = eye - jnp.where(pair_mask(0), a_strict, 0.0)
    for sh in range(1, 6):
        am = jnp.where(pair_mask(sh), a_strict, 0.0)
        t = t - _mm(_mm(t, am), t)
    return t


def _gdn_body(q_ref, k_ref, v_ref, z_ref, g_ref, gt_ref, cw_ref, al_c_ref, dt_c_ref, al_r_ref, dt_r_ref, on_ref,
              y_ref, s_st, xe, qkv, col_d, col_b, row_d):
    ts = q_ref.shape[0]
    nct = ts // CHUNK
    w_d = q_ref.shape[1]
    h_d = w_d // HD_D
    scale = HD_D ** -0.5

    @pl.when(pl.program_id(1) == 0)
    def _():
        s_st[...] = jnp.zeros_like(s_st)
        xe[0:8, :] = jnp.zeros((8, 3 * w_d), F32)

    xe[8:8 + ts, 0:w_d] = q_ref[...].astype(F32)
    xe[8:8 + ts, w_d:2 * w_d] = k_ref[...].astype(F32)
    xe[8:8 + ts, 2 * w_d:3 * w_d] = v_ref[...].astype(F32)
    for cbk in range(3 * h_d):
        cs = slice(cbk * HD_D, (cbk + 1) * HD_D)
        acc = cw_ref[CONV_K - 1:CONV_K, cs] * xe[8:8 + ts, cs]
        for d in range(1, CONV_K):
            acc = acc + cw_ref[CONV_K - 1 - d:CONV_K - d, cs] * xe[8 - d:8 - d + ts, cs]
        acc = _silu(acc)
        if cbk < 2 * h_d:
            acc = acc * lax.rsqrt(jnp.sum(acc * acc, axis=-1, keepdims=True) + EPS)
        qkv[:, cs] = acc
    xe[0:8, :] = xe[ts:ts + 8, :]

    g = g_ref[...]
    col_b[...] = _sigmoid(g)
    col_d[...] = _mm_f32(_chunk_tril(ts), -jnp.exp(al_c_ref[...]) * _softplus(g + dt_c_ref[...]))
    gt = gt_ref[...]
    g_r = -jnp.exp(al_r_ref[...]) * _softplus(gt[0:h_d] + dt_r_ref[...])
    row_d[...] = _mm_f32(g_r.reshape(h_d * nct, CHUNK), _triu64()).reshape(h_d, nct, CHUNK)

    r_i, c_i = _iota2((CHUNK, CHUNK), 0), _iota2((CHUNK, CHUNK), 1)
    tril = c_i <= r_i
    strict = c_i < r_i

    def chunk(c, carry):
        rows = pl.ds(pl.multiple_of(c * CHUNK, CHUNK), CHUNK)
        dc_all = col_d[rows, :]
        beta_all = col_b[rows, :]
        for h in range(h_d):
            hq = slice(h * HD_D, (h + 1) * HD_D)
            hk = slice(w_d + h * HD_D, w_d + (h + 1) * HD_D)
            hv = slice(2 * w_d + h * HD_D, 2 * w_d + (h + 1) * HD_D)
            dc = dc_all[:, h:h + 1]
            beta = beta_all[:, h_d + h:h_d + h + 1]
            dr = row_d[h, pl.ds(c, 1), :]
            q, k, v = qkv[rows, hq], qkv[rows, hk], qkv[rows, hv]
            gam = jnp.exp(jnp.where(tril, dc - dr, NEG))
            kb = k * beta
            a_mat = jnp.where(strict, _mm_nt(kb, k) * gam, 0.0)
            t_inv = _unit_lower_inverse(a_mat)
            ed = jnp.exp(dc)
            uw = _mm(t_inv, jnp.concatenate([v * beta, kb * ed], axis=1))
            u, w = uw[:, :HD_D], uw[:, HD_D:]
            qs = q * scale
            qk = _mm_nt(qs, k) * gam
            dl = dc[CHUNK - 1:CHUNK, :]
            kd = k * jnp.exp(dl - dc)
            st = s_st[h]
            ws = _mm(jnp.concatenate([w, qs * ed], axis=0), st)
            v_new = u - ws[:CHUNK]
            o = ws[CHUNK:] + _mm(qk, v_new)
            s_st[h] = jnp.exp(dl) * st + _mm_tn(kd, v_new)
            on = o * lax.rsqrt(jnp.mean(o * o, axis=-1, keepdims=True) + EPS) * on_ref[...]
            z = z_ref[rows, hq].astype(F32)
            y_ref[rows, hq] = (on * _silu(z)).astype(y_ref.dtype)
        return carry

    lax.fori_loop(0, nct, chunk, 0)


def _gdn(p3, g3, gt4, cw, al_c, dt_c, al_r, dt_r, onorm, ts):
    b, s, _ = p3.shape
    w_d = 1024
    h_d = w_d // HD_D
    nct = ts // CHUNK

    def col_spec(sec):
        return pl.BlockSpec((None, ts, w_d), lambda bi, j, sec=sec: (bi, j, 2 + sec))

    return pl.pallas_call(
        _gdn_body,
        grid=(b, s // ts),
        in_specs=[
            col_spec(0), col_spec(1), col_spec(2), col_spec(3),
            pl.BlockSpec((None, ts, LANES), lambda bi, j: (bi, j, 0)),
            pl.BlockSpec((None, 2 * h_d, nct, CHUNK), lambda bi, j: (bi, 0, j, 0)),
            pl.BlockSpec((CONV_K, 3 * w_d), lambda bi, j: (0, 0)),
            pl.BlockSpec((1, LANES), lambda bi, j: (0, 0)),
            pl.BlockSpec((1, LANES), lambda bi, j: (0, 0)),
            pl.BlockSpec((h_d, 1, 1), lambda bi, j: (0, 0, 0)),
            pl.BlockSpec((h_d, 1, 1), lambda bi, j: (0, 0, 0)),
            pl.BlockSpec((1, HD_D), lambda bi, j: (0, 0)),
        ],
        out_specs=pl.BlockSpec((None, ts, w_d), lambda bi, j: (bi, j, 0)),
        out_shape=jax.ShapeDtypeStruct((b, s, w_d), BF16),
        scratch_shapes=[
            pltpu.VMEM((h_d, HD_D, HD_D), F32),
            pltpu.VMEM((ts + 8, 3 * w_d), F32),
            pltpu.VMEM((ts, 3 * w_d), F32),
            pltpu.VMEM((ts, LANES), F32),
            pltpu.VMEM((ts, LANES), F32),
            pltpu.VMEM((h_d, nct, CHUNK), F32),
        ],
        compiler_params=_params(2),
        name="gated_delta",
    )(p3, p3, p3, p3, g3, gt4, cw, al_c, dt_c, al_r, dt_r, onorm)


def _rel_row(rel_bias):
    t = (np.arange(REL_W) + CHUNK - 1) % REL_W
    idx = np.clip(LOOKBACK * CHUNK + CHUNK - 1 - t, -REL_MAX, REL_MAX) + REL_MAX
    return rel_bias[:, idx].astype(F32)[:, None, :]


def _pad_lanes(v, n=LANES):
    return jnp.pad(v, ((0, 0), (0, n - v.shape[1])))


def _gates_t(g3, n):
    b, s, _ = g3.shape
    return jnp.transpose(g3[:, :, :n], (0, 2, 1)).reshape(b, n, s // CHUNK, CHUNK)


def kernel(x, ev_norm, ev_w_in, ev_if_bias, ev_qn_gain, ev_kn_gain, ev_rel_bias, ev_w_out, od_norm, od_w_in, od_conv_c_w, od_conv_c_b, od_gate_w, od_gate_b, od_lambda, od_conv_d_w, od_a_log, od_dt_bias, od_onorm, od_w_out):
    b, s, d = x.shape
    m = b * s
    half = d // 2
    tm = min(1024, m)
    ts = min(512, s)
    x2 = x.reshape(m, d)
    depth = ev_norm.shape[0] + od_norm.shape[0]
    for layer in range(depth):
        j = layer // 2
        if layer % 2 == 0:
            n_main = 9 * half
            w_in = ev_w_in[j]
            p, g = _inproj(x2, ev_norm[j].reshape(1, d), w_in[:, :n_main].astype(BF16),
                           _pad_lanes(w_in[:, n_main:]).astype(BF16), tm, 1024)
            p3, g3 = p.reshape(b, s, n_main), g.reshape(b, s, LANES)
            ya = _attn(p3, ev_qn_gain[j].reshape(1, HD_A), ev_kn_gain[j].reshape(1, HD_A), _rel_row(ev_rel_bias[j]))
            bias = ev_if_bias[j].astype(F32)
            yb = _mlstm(p3, g3, _gates_t(g3, 2 * H_B), _pad_lanes(bias.reshape(1, -1)), bias.reshape(-1, 1, 1), ts)
            x2 = _outproj(ya.reshape(m, half), yb.reshape(m, half), ev_w_out[j].astype(BF16), x2, tm, 1024)
        else:
            n_main = 6 * half
            h_d = half // HD_D
            w_in = od_w_in[j]
            p, g = _inproj(x2, od_norm[j].reshape(1, d), w_in[:, :n_main].astype(BF16),
                           _pad_lanes(w_in[:, n_main:]).astype(BF16), tm, 1024)
            p3, g3 = p.reshape(b, s, n_main), g.reshape(b, s, LANES)
            yc = _rglru(p3, od_conv_c_w[j], od_conv_c_b[j].reshape(1, -1), od_gate_w[j].astype(BF16),
                        od_gate_b[j].reshape(1, -1), od_lambda[j].reshape(1, -1))
            al, dt = od_a_log[j].astype(F32), od_dt_bias[j].astype(F32)
            yd = _gdn(p3, g3, _gates_t(g3, 2 * h_d), od_conv_d_w[j], _pad_lanes(al.reshape(1, -1)),
                      _pad_lanes(dt.reshape(1, -1)), al.reshape(-1, 1, 1), dt.reshape(-1, 1, 1),
                      od_onorm[j].reshape(1, HD_D), ts)
            x2 = _outproj(yc.reshape(m, half), yd.reshape(m, half), od_w_out[j].astype(BF16), x2, tm, 1024)
    return x2.reshape(b, s, d)
```

```python
import functools

import numpy as np
import jax
import jax.numpy as jnp
from jax import lax
from jax.experimental import pallas as pl
from jax.experimental.pallas import tpu as pltpu

F32 = jnp.float32
BF16 = jnp.bfloat16

CHUNK = 64
EPS = 1e-6
NEG = -1e30
CONV_K = 4
LANES = 128
HD_A = 128
LOOKBACK = 8
BAND = (LOOKBACK + 1) * CHUNK
REL_MAX = 256
REL_W = 640
ATTN_GROUP = 4
H_B = 4
N_BLK_C = 8
C_RG = 8.0
HD_D = 128

VMEM_LIMIT = 56 * 1024 * 1024


def _params(n_axes):
    return pltpu.CompilerParams(dimension_semantics=("arbitrary",) * n_axes, vmem_limit_bytes=VMEM_LIMIT)


def _mm(a, b):
    return jnp.dot(a.astype(BF16), b.astype(BF16), preferred_element_type=F32)


def _mm_nt(a, b):
    return lax.dot_general(a.astype(BF16), b.astype(BF16), (((1,), (1,)), ((), ())), preferred_element_type=F32)


def _mm_tn(a, b):
    return lax.dot_general(a.astype(BF16), b.astype(BF16), (((0,), (0,)), ((), ())), preferred_element_type=F32)


def _mm_f32(a, b):
    return jnp.dot(a, b, preferred_element_type=F32, precision=lax.Precision.HIGHEST)


def _softplus(x):
    return jnp.maximum(x, 0.0) + jnp.log1p(jnp.exp(-jnp.abs(x)))


def _sigmoid(x):
    return 1.0 / (1.0 + jnp.exp(-x))


def _silu(x):
    return x * _sigmoid(x)


def _iota2(shape, axis):
    return lax.broadcasted_iota(jnp.int32, shape, axis)


def _chunk_tril(n):
    r, c = _iota2((n, n), 0), _iota2((n, n), 1)
    same = (r >> 6) == (c >> 6)
    return jnp.where(same & (c <= r), 1.0, 0.0).astype(F32)


def _triu64():
    r, c = _iota2((CHUNK, CHUNK), 0), _iota2((CHUNK, CHUNK), 1)
    return jnp.where(r <= c, 1.0, 0.0).astype(F32)


def _inproj_body(x_ref, g_ref, w_ref, wg_ref, o_ref, og_ref, xn_ref):
    @pl.when(pl.program_id(1) == 0)
    def _():
        x = x_ref[...]
        ms = jnp.mean(x * x, axis=-1, keepdims=True)
        xn = ((x * lax.rsqrt(ms + EPS)) * g_ref[...]).astype(BF16)
        xn_ref[...] = xn
        og_ref[...] = jnp.dot(xn, wg_ref[...], preferred_element_type=F32)

    o_ref[...] = jnp.dot(xn_ref[...], w_ref[...], preferred_element_type=F32).astype(o_ref.dtype)


def _inproj(x2, g, w_main, w_gate, tm, tn):
    m, d = x2.shape
    n = w_main.shape[1]
    return pl.pallas_call(
        _inproj_body,
        grid=(m // tm, n // tn),
        in_specs=[
            pl.BlockSpec((tm, d), lambda i, j: (i, 0)),
            pl.BlockSpec((1, d), lambda i, j: (0, 0)),
            pl.BlockSpec((d, tn), lambda i, j: (0, j)),
            pl.BlockSpec((d, LANES), lambda i, j: (0, 0)),
        ],
        out_specs=[
            pl.BlockSpec((tm, tn), lambda i, j: (i, j)),
            pl.BlockSpec((tm, LANES), lambda i, j: (i, 0)),
        ],
        out_shape=[jax.ShapeDtypeStruct((m, n), BF16), jax.ShapeDtypeStruct((m, LANES), F32)],
        scratch_shapes=[pltpu.VMEM((tm, d), BF16)],
        compiler_params=_params(2),
        name="inproj",
    )(x2, g, w_main, w_gate)


def _outproj_body(ya_ref, yb_ref, wa_ref, wb_ref, x_ref, o_ref):
    acc = jnp.dot(ya_ref[...], wa_ref[...], preferred_element_type=F32)
    acc = acc + jnp.dot(yb_ref[...], wb_ref[...], preferred_element_type=F32)
    o_ref[...] = x_ref[...] + acc


def _outproj(ya, yb, w, x2, tm, tn):
    m, d = x2.shape
    kh = ya.shape[1]
    return pl.pallas_call(
        _outproj_body,
        grid=(d // tn, m // tm),
        in_specs=[
            pl.BlockSpec((tm, kh), lambda j, i: (i, 0)),
            pl.BlockSpec((tm, kh), lambda j, i: (i, 0)),
            pl.BlockSpec((kh, tn), lambda j, i: (0, j)),
            pl.BlockSpec((kh, tn), lambda j, i: (1, j)),
            pl.BlockSpec((tm, tn), lambda j, i: (i, j)),
        ],
        out_specs=pl.BlockSpec((tm, tn), lambda j, i: (i, j)),
        out_shape=jax.ShapeDtypeStruct((m, d), F32),
        compiler_params=_params(2),
        name="outproj",
    )(ya, yb, w, w, x2)


def _attn_body(q_ref, k_ref, v_ref, z_ref, qg_ref, kg_ref, c_ref, o_ref, qs, ks, vs, bias_s):
    s = q_ref.shape[0]
    pad = LOOKBACK * CHUNK
    q = q_ref[...].astype(F32)
    q = q * lax.rsqrt(jnp.mean(q * q, axis=-1, keepdims=True) + EPS) * qg_ref[...] * (HD_A ** -0.5)
    qs[...] = q.astype(BF16)
    k = k_ref[...].astype(F32)
    k = k * lax.rsqrt(jnp.mean(k * k, axis=-1, keepdims=True) + EPS) * kg_ref[...]
    ks[0:pad, :] = jnp.zeros((pad, HD_A), BF16)
    ks[pad:pad + s, :] = k.astype(BF16)
    vs[0:pad, :] = jnp.zeros((pad, HD_A), BF16)
    vs[pad:pad + s, :] = v_ref[...]
    cb = jnp.broadcast_to(c_ref[...], (CHUNK, REL_W))
    bias_s[...] = pltpu.roll(cb, 0, 1, stride=1, stride_axis=0)[:, :BAND]
    col = _iota2((CHUNK, BAND), 1)

    def chunks(it, carry):
        gs = range(ATTN_GROUP)
        n = [it * ATTN_GROUP + g for g in gs]
        r0 = [pl.multiple_of(n[g] * CHUNK, CHUNK) for g in gs]
        sc = [_mm_nt(qs[pl.ds(r0[g], CHUNK), :], ks[pl.ds(r0[g], BAND), :]) for g in gs]
        sc = [jnp.where(col >= (LOOKBACK - n[g]) * CHUNK, sc[g] + bias_s[...], NEG) for g in gs]
        p = [jnp.exp(sc[g] - jnp.max(sc[g], axis=-1, keepdims=True)) for g in gs]
        pv = [_mm(p[g], vs[pl.ds(r0[g], BAND), :]) for g in gs]
        for g in gs:
            o = pv[g] / jnp.sum(p[g], axis=-1, keepdims=True)
            z = z_ref[pl.ds(r0[g], CHUNK), :].astype(F32)
            o_ref[pl.ds(r0[g], CHUNK), :] = (o * _silu(z)).astype(o_ref.dtype)
        return carry

    lax.fori_loop(0, s // (CHUNK * ATTN_GROUP), chunks, 0)


def _attn(p3, qg, kg, crel):
    b, s, _ = p3.shape
    h_a = crel.shape[0]

    def col_spec(sec):
        return pl.BlockSpec((None, s, HD_A), lambda bi, h, sec=sec: (bi, 0, sec * h_a + h))

    return pl.pallas_call(
        _attn_body,
        grid=(b, h_a),
        in_specs=[
            col_spec(0), col_spec(1), col_spec(2), col_spec(3),
            pl.BlockSpec((1, HD_A), lambda bi, h: (0, 0)),
            pl.BlockSpec((1, HD_A), lambda bi, h: (0, 0)),
            pl.BlockSpec((None, 1, REL_W), lambda bi, h: (h, 0, 0)),
        ],
        out_specs=pl.BlockSpec((None, s, HD_A), lambda bi, h: (bi, 0, h)),
        out_shape=jax.ShapeDtypeStruct((b, s, h_a * HD_A), BF16),
        scratch_shapes=[
            pltpu.VMEM((s, HD_A), BF16),
            pltpu.VMEM((s + LOOKBACK * CHUNK, HD_A), BF16),
            pltpu.VMEM((s + LOOKBACK * CHUNK, HD_A), BF16),
            pltpu.VMEM((CHUNK, BAND), F32),
        ],
        compiler_params=_params(2),
        name="chunk_attn",
    )(p3, p3, p3, p3, qg, kg, crel)


def _mlstm_body(q_ref, k_ref, v_ref, o_ref, z_ref, g_ref, gt_ref, bc_ref, br_ref, y_ref,
                c_st, n_st, m_st, col_b, col_i, row_b, row_i):
    ts = q_ref.shape[0]
    nct = ts // CHUNK
    hd = q_ref.shape[1] // H_B
    scale = hd ** -0.5

    @pl.when(pl.program_id(1) == 0)
    def _():
        c_st[...] = jnp.zeros_like(c_st)
        n_st[...] = jnp.zeros_like(n_st)
        m_st[...] = jnp.zeros_like(m_st)

    g = g_ref[...] + bc_ref[...]
    col_i[...] = g
    col_b[...] = _mm_f32(_chunk_tril(ts), -_softplus(-g))
    gt = gt_ref[...] + br_ref[...]
    lf_r = -_softplus(-gt[H_B:2 * H_B])
    row_b[...] = _mm_f32(lf_r.reshape(H_B * nct, CHUNK), _triu64()).reshape(H_B, nct, CHUNK)
    row_i[...] = gt[0:H_B]
    tril = _iota2((CHUNK, CHUNK), 1) <= _iota2((CHUNK, CHUNK), 0)

    def chunk(c, carry):
        rows = pl.ds(pl.multiple_of(c * CHUNK, CHUNK), CHUNK)
        bc_all = col_b[rows, :]
        li_all = col_i[rows, :]
        hs = range(H_B)
        hc = [slice(h * hd, (h + 1) * hd) for h in hs]
        bc = [bc_all[:, H_B + h:H_B + h + 1] for h in hs]
        lic = [li_all[:, h:h + 1] for h in hs]
        br = [row_b[h, pl.ds(c, 1), :] for h in hs]
        lir = [row_i[h, pl.ds(c, 1), :] for h in hs]
        m = [m_st[h][0:1, 0:1] for h in hs]
        qn = [q_ref[rows, hc[h]] for h in hs]
        kn = [k_ref[rows, hc[h]] for h in hs]
        vn = [v_ref[rows, hc[h]] for h in hs]
        cm = [c_st[h] for h in hs]
        nm = [n_st[h][0:1, :] for h in hs]
        qk = [_mm_nt(qn[h], kn[h]) for h in hs]
        qc = [_mm(qn[h], cm[h]) for h in hs]
        dmat = [jnp.where(tril, bc[h] - br[h] + lir[h], NEG) for h in hs]
        mt = [jnp.maximum(bc[h] + m[h], jnp.max(dmat[h], axis=-1, keepdims=True)) for h in hs]
        w_inter = [jnp.exp(bc[h] + m[h] - mt[h]) for h in hs]
        p = [jnp.exp(dmat[h] - mt[h]) * (qk[h] * scale) for h in hs]
        pv = [_mm(p[h], vn[h]) for h in hs]
        b_last = [bc[h][CHUNK - 1:CHUNK, :] for h in hs]
        m_new = [jnp.maximum(b_last[h] + m[h], jnp.max(b_last[h] - br[h] + lir[h], axis=-1, keepdims=True)) for h in hs]
        wk = [jnp.exp(b_last[h] - bc[h] + lic[h] - m_new[h]) * kn[h].astype(F32) for h in hs]
        upd = [_mm_tn(wk[h], vn[h]) for h in hs]
        for h in hs:
            num = w_inter[h] * (qc[h] * scale) + pv[h]
            qn_dot = jnp.sum(qn[h].astype(F32) * nm[h], axis=-1, keepdims=True) * scale
            den = w_inter[h] * qn_dot + jnp.sum(p[h], axis=-1, keepdims=True)
            hout = num / jnp.maximum(jnp.abs(den), jnp.exp(-mt[h]))
            w_c = jnp.exp(b_last[h] + m[h] - m_new[h])
            c_st[h] = w_c * cm[h] + upd[h]
            n_st[h] = jnp.broadcast_to(w_c * nm[h] + jnp.sum(wk[h], axis=0, keepdims=True), n_st.shape[1:])
            m_st[h] = jnp.broadcast_to(m_new[h], m_st.shape[1:])
            og = o_ref[rows, hc[h]].astype(F32)
            zg = z_ref[rows, hc[h]].astype(F32)
            y_ref[rows, hc[h]] = (_sigmoid(og) * hout * _silu(zg)).astype(y_ref.dtype)
        return carry

    lax.fori_loop(0, nct, chunk, 0)


def _mlstm(p3, g3, gt4, bias_col, bias_row, ts):
    b, s, _ = p3.shape
    w_b = 1024
    hd = w_b // H_B
    nct = ts // CHUNK

    def col_spec(sec):
        return pl.BlockSpec((None, ts, w_b), lambda bi, j, sec=sec: (bi, j, 4 + sec))

    return pl.pallas_call(
        _mlstm_body,
        grid=(b, s // ts),
        in_specs=[
            col_spec(0), col_spec(1), col_spec(2), col_spec(3), col_spec(4),
            pl.BlockSpec((None, ts, LANES), lambda bi, j: (bi, j, 0)),
            pl.BlockSpec((None, 2 * H_B, nct, CHUNK), lambda bi, j: (bi, 0, j, 0)),
            pl.BlockSpec((1, LANES), lambda bi, j: (0, 0)),
            pl.BlockSpec((2 * H_B, 1, 1), lambda bi, j: (0, 0, 0)),
        ],
        out_specs=pl.BlockSpec((None, ts, w_b), lambda bi, j: (bi, j, 0)),
        out_shape=jax.ShapeDtypeStruct((b, s, w_b), BF16),
        scratch_shapes=[
            pltpu.VMEM((H_B, hd, hd), F32),
            pltpu.VMEM((H_B, 8, hd), F32),
            pltpu.VMEM((H_B, 8, LANES), F32),
            pltpu.VMEM((ts, LANES), F32),
            pltpu.VMEM((ts, LANES), F32),
            pltpu.VMEM((H_B, nct, CHUNK), F32),
            pltpu.VMEM((H_B, nct, CHUNK), F32),
        ],
        compiler_params=_params(2),
        name="mlstm",
    )(p3, p3, p3, p3, p3, g3, gt4, bias_col, bias_row)


def _shift_rows(x, d, row):
    return jnp.where(row >= d, pltpu.roll(x, d, 0), 0.0)


def _rglru_body(x_ref, z_ref, cw_ref, cb_ref, gw_ref, gbr_ref, gbi_ref, lam_ref, y_ref):
    s, blk = x_ref.shape
    x = x_ref[...].astype(F32)
    row = _iota2((s, blk), 0)
    cw = cw_ref[...]
    xc = cw[3:4, :] * x + cb_ref[...]
    for d in range(1, CONV_K):
        xc = xc + cw[CONV_K - 1 - d:CONV_K - d, :] * _shift_rows(x, d, row)
    gates = _mm(xc, gw_ref[...])
    r = _sigmoid(gates[:, :blk] + gbr_ref[...])
    i = _sigmoid(gates[:, blk:] + gbi_ref[...])
    log_a = -C_RG * r * _softplus(-lam_ref[...])
    a = jnp.exp(log_a)
    bb = jnp.sqrt(-jnp.tanh(log_a) * (a * a + 1.0)) * (i * xc)
    d = 1
    while d < s:
        a_sh = pltpu.roll(a, d, 0)
        b_sh = pltpu.roll(bb, d, 0)
        keep = row >= d
        bb = jnp.where(keep, a * b_sh + bb, bb)
        a = jnp.where(keep, a * a_sh, a)
        d *= 2
    z = z_ref[...].astype(F32)
    y_ref[...] = (bb * _silu(z)).astype(y_ref.dtype)


def _rglru(p3, cw, cb, gw, gb, lam):
    b, s, _ = p3.shape
    w_c = cw.shape[1]
    blk = w_c // N_BLK_C
    return pl.pallas_call(
        _rglru_body,
        grid=(b, N_BLK_C),
        in_specs=[
            pl.BlockSpec((None, s, blk), lambda bi, n: (bi, 0, n)),
            pl.BlockSpec((None, s, blk), lambda bi, n: (bi, 0, N_BLK_C + n)),
            pl.BlockSpec((CONV_K, blk), lambda bi, n: (0, n)),
            pl.BlockSpec((1, blk), lambda bi, n: (0, n)),
            pl.BlockSpec((None, blk, 2 * blk), lambda bi, n: (n, 0, 0)),
            pl.BlockSpec((1, blk), lambda bi, n: (0, n)),
            pl.BlockSpec((1, blk), lambda bi, n: (0, N_BLK_C + n)),
            pl.BlockSpec((1, blk), lambda bi, n: (0, n)),
        ],
        out_specs=pl.BlockSpec((None, s, blk), lambda bi, n: (bi, 0, n)),
        out_shape=jax.ShapeDtypeStruct((b, s, w_c), BF16),
        compiler_params=_params(2),
        name="rglru",
    )(p3, p3, cw, cb, gw, gb, gb, lam)


def _unit_lower_inverse(a_list):
    r, c = _iota2((CHUNK, CHUNK), 0), _iota2((CHUNK, CHUNK), 1)
    eye = jnp.where(r == c, 1.0, 0.0).astype(F32)

    def pair_mask(sh):
        rb, cb_ = r >> sh, c >> sh
        return ((rb & 1) == 1) & (cb_ == rb - 1)

    ts = [eye - jnp.where(pair_mask(0), a, 0.0) for a in a_list]
    for sh in range(1, 6):
        mask = pair_mask(sh)
        x1 = [_mm(t, jnp.where(mask, a, 0.0)) for t, a in zip(ts, a_list)]
        x2 = [_mm(x, t) for x, t in zip(x1, ts)]
        ts = [t - x for t, x in zip(ts, x2)]
    return ts


def _gdn_body(q_ref, k_ref, v_ref, z_ref, g_ref, gt_ref, cw_ref, al_c_ref, dt_c_ref, al_r_ref, dt_r_ref, on_ref,
              y_ref, s_st, xe, qkv, col_d, col_b, row_d):
    ts = q_ref.shape[0]
    nct = ts // CHUNK
    w_d = q_ref.shape[1]
    h_d = w_d // HD_D
    scale = HD_D ** -0.5

    @pl.when(pl.program_id(1) == 0)
    def _():
        s_st[...] = jnp.zeros_like(s_st)
        xe[0:8, :] = jnp.zeros((8, 3 * w_d), F32)

    xe[8:8 + ts, 0:w_d] = q_ref[...].astype(F32)
    xe[8:8 + ts, w_d:2 * w_d] = k_ref[...].astype(F32)
    xe[8:8 + ts, 2 * w_d:3 * w_d] = v_ref[...].astype(F32)
    for cbk in range(3 * h_d):
        cs = slice(cbk * HD_D, (cbk + 1) * HD_D)
        acc = cw_ref[CONV_K - 1:CONV_K, cs] * xe[8:8 + ts, cs]
        for d in range(1, CONV_K):
            acc = acc + cw_ref[CONV_K - 1 - d:CONV_K - d, cs] * xe[8 - d:8 - d + ts, cs]
        acc = _silu(acc)
        if cbk < 2 * h_d:
            acc = acc * lax.rsqrt(jnp.sum(acc * acc, axis=-1, keepdims=True) + EPS)
        qkv[:, cs] = acc
    xe[0:8, :] = xe[ts:ts + 8, :]

    g = g_ref[...]
    col_b[...] = _sigmoid(g)
    col_d[...] = _mm_f32(_chunk_tril(ts), -jnp.exp(al_c_ref[...]) * _softplus(g + dt_c_ref[...]))
    gt = gt_ref[...]
    g_r = -jnp.exp(al_r_ref[...]) * _softplus(gt[0:h_d] + dt_r_ref[...])
    row_d[...] = _mm_f32(g_r.reshape(h_d * nct, CHUNK), _triu64()).reshape(h_d, nct, CHUNK)

    r_i, c_i = _iota2((CHUNK, CHUNK), 0), _iota2((CHUNK, CHUNK), 1)
    tril = c_i <= r_i
    strict = c_i < r_i

    def chunk(c, carry):
        rows = pl.ds(pl.multiple_of(c * CHUNK, CHUNK), CHUNK)
        dc_all = col_d[rows, :]
        beta_all = col_b[rows, :]
        hs = range(h_d)
        dc = [dc_all[:, h:h + 1] for h in hs]
        beta = [beta_all[:, h_d + h:h_d + h + 1] for h in hs]
        q = [qkv[rows, h * HD_D:(h + 1) * HD_D] for h in hs]
        k = [qkv[rows, w_d + h * HD_D:w_d + (h + 1) * HD_D] for h in hs]
        v = [qkv[rows, 2 * w_d + h * HD_D:2 * w_d + (h + 1) * HD_D] for h in hs]
        gam = [jnp.exp(jnp.where(tril, dc[h] - row_d[h, pl.ds(c, 1), :], NEG)) for h in hs]
        kb = [k[h] * beta[h] for h in hs]
        kk = [_mm_nt(kb[h], k[h]) for h in hs]
        t_inv = _unit_lower_inverse([jnp.where(strict, kk[h] * gam[h], 0.0) for h in hs])
        ed = [jnp.exp(dc[h]) for h in hs]
        uw = [_mm(t_inv[h], jnp.concatenate([v[h] * beta[h], kb[h] * ed[h]], axis=1)) for h in hs]
        qs = [q[h] * scale for h in hs]
        qk = [_mm_nt(qs[h], k[h]) * gam[h] for h in hs]
        dl = [dc[h][CHUNK - 1:CHUNK, :] for h in hs]
        st = [s_st[h] for h in hs]
        ws = [_mm(jnp.concatenate([uw[h][:, HD_D:], qs[h] * ed[h]], axis=0), st[h]) for h in hs]
        v_new = [uw[h][:, :HD_D] - ws[h][:CHUNK] for h in hs]
        o = [ws[h][CHUNK:] + _mm(qk[h], v_new[h]) for h in hs]
        upd = [_mm_tn(k[h] * jnp.exp(dl[h] - dc[h]), v_new[h]) for h in hs]
        for h in hs:
            s_st[h] = jnp.exp(dl[h]) * st[h] + upd[h]
            on = o[h] * lax.rsqrt(jnp.mean(o[h] * o[h], axis=-1, keepdims=True) + EPS) * on_ref[...]
            z = z_ref[rows, h * HD_D:(h + 1) * HD_D].astype(F32)
            y_ref[rows, h * HD_D:(h + 1) * HD_D] = (on * _silu(z)).astype(y_ref.dtype)
        return carry

    lax.fori_loop(0, nct, chunk, 0)


def _gdn(p3, g3, gt4, cw, al_c, dt_c, al_r, dt_r, onorm, ts):
    b, s, _ = p3.shape
    w_d = 1024
    h_d = w_d // HD_D
    nct = ts // CHUNK

    def col_spec(sec):
        return pl.BlockSpec((None, ts, w_d), lambda bi, j, sec=sec: (bi, j, 2 + sec))

    return pl.pallas_call(
        _gdn_body,
        grid=(b, s // ts),
        in_specs=[
            col_spec(0), col_spec(1), col_spec(2), col_spec(3),
            pl.BlockSpec((None, ts, LANES), lambda bi, j: (bi, j, 0)),
            pl.BlockSpec((None, 2 * h_d, nct, CHUNK), lambda bi, j: (bi, 0, j, 0)),
            pl.BlockSpec((CONV_K, 3 * w_d), lambda bi, j: (0, 0)),
            pl.BlockSpec((1, LANES), lambda bi, j: (0, 0)),
            pl.BlockSpec((1, LANES), lambda bi, j: (0, 0)),
            pl.BlockSpec((h_d, 1, 1), lambda bi, j: (0, 0, 0)),
            pl.BlockSpec((h_d, 1, 1), lambda bi, j: (0, 0, 0)),
            pl.BlockSpec((1, HD_D), lambda bi, j: (0, 0)),
        ],
        out_specs=pl.BlockSpec((None, ts, w_d), lambda bi, j: (bi, j, 0)),
        out_shape=jax.ShapeDtypeStruct((b, s, w_d), BF16),
        scratch_shapes=[
            pltpu.VMEM((h_d, HD_D, HD_D), F32),
            pltpu.VMEM((ts + 8, 3 * w_d), F32),
            pltpu.VMEM((ts, 3 * w_d), F32),
            pltpu.VMEM((ts, LANES), F32),
            pltpu.VMEM((ts, LANES), F32),
            pltpu.VMEM((h_d, nct, CHUNK), F32),
        ],
        compiler_params=_params(2),
        name="gated_delta",
    )(p3, p3, p3, p3, g3, gt4, cw, al_c, dt_c, al_r, dt_r, onorm)


def _rel_row(rel_bias):
    t = (np.arange(REL_W) + CHUNK - 1) % REL_W
    idx = np.clip(LOOKBACK * CHUNK + CHUNK - 1 - t, -REL_MAX, REL_MAX) + REL_MAX
    return rel_bias[:, idx].astype(F32)[:, None, :]


def _pad_lanes(v, n=LANES):
    return jnp.pad(v, ((0, 0), (0, n - v.shape[1])))


def _gates_t(g3, n):
    b, s, _ = g3.shape
    return jnp.transpose(g3[:, :, :n], (0, 2, 1)).reshape(b, n, s // CHUNK, CHUNK)


def kernel(x, ev_norm, ev_w_in, ev_if_bias, ev_qn_gain, ev_kn_gain, ev_rel_bias, ev_w_out, od_norm, od_w_in, od_conv_c_w, od_conv_c_b, od_gate_w, od_gate_b, od_lambda, od_conv_d_w, od_a_log, od_dt_bias, od_onorm, od_w_out):
    b, s, d = x.shape
    m = b * s
    half = d // 2
    tm = min(1024, m)
    ts = min(512, s)
    x2 = x.reshape(m, d)
    depth = ev_norm.shape[0] + od_norm.shape[0]
    for layer in range(depth):
        j = layer // 2
        if layer % 2 == 0:
            n_main = 9 * half
            w_in = ev_w_in[j]
            p, g = _inproj(x2, ev_norm[j].reshape(1, d), w_in[:, :n_main].astype(BF16),
                           _pad_lanes(w_in[:, n_main:]).astype(BF16), tm, 1024)
            p3, g3 = p.reshape(b, s, n_main), g.reshape(b, s, LANES)
            ya = _attn(p3, ev_qn_gain[j].reshape(1, HD_A), ev_kn_gain[j].reshape(1, HD_A), _rel_row(ev_rel_bias[j]))
            bias = ev_if_bias[j].astype(F32)
            yb = _mlstm(p3, g3, _gates_t(g3, 2 * H_B), _pad_lanes(bias.reshape(1, -1)), bias.reshape(-1, 1, 1), ts)
            x2 = _outproj(ya.reshape(m, half), yb.reshape(m, half), ev_w_out[j].astype(BF16), x2, tm, 1024)
        else:
            n_main = 6 * half
            h_d = half // HD_D
            w_in = od_w_in[j]
            p, g = _inproj(x2, od_norm[j].reshape(1, d), w_in[:, :n_main].astype(BF16),
                           _pad_lanes(w_in[:, n_main:]).astype(BF16), tm, 1024)
            p3, g3 = p.reshape(b, s, n_main), g.reshape(b, s, LANES)
            yc = _rglru(p3, od_conv_c_w[j], od_conv_c_b[j].reshape(1, -1), od_gate_w[j].astype(BF16),
                        od_gate_b[j].reshape(1, -1), od_lambda[j].reshape(1, -1))
            al, dt = od_a_log[j].astype(F32), od_dt_bias[j].astype(F32)
            yd = _gdn(p3, g3, _gates_t(g3, 2 * h_d), od_conv_d_w[j], _pad_lanes(al.reshape(1, -1)),
                      _pad_lanes(dt.reshape(1, -1)), al.reshape(-1, 1, 1), dt.reshape(-1, 1, 1),
                      od_onorm[j].reshape(1, HD_D), ts)
            x2 = _outproj(yc.reshape(m, half), yd.reshape(m, half), od_w_out[j].astype(BF16), x2, tm, 1024)
    return x2.reshape(b, s, d)
```

```python
import functools

import numpy as np
import jax
import jax.numpy as jnp
from jax import lax
from jax.experimental import pallas as pl
from jax.experimental.pallas import tpu as pltpu

F32 = jnp.float32
BF16 = jnp.bfloat16

CHUNK = 64
EPS = 1e-6
NEG = -1e30
CONV_K = 4
LANES = 128
HD_A = 128
LOOKBACK = 8
BAND = (LOOKBACK + 1) * CHUNK
REL_MAX = 256
REL_W = 640
ATTN_GROUP = 8
H_B = 4
MLSTM_L = 256
N_BLK_C = 8
C_RG = 8.0
RG_COLS = 512
HD_D = 128

VMEM_LIMIT = 56 * 1024 * 1024


def _params(n_axes):
    return pltpu.CompilerParams(dimension_semantics=("arbitrary",) * n_axes, vmem_limit_bytes=VMEM_LIMIT)


def _mm(a, b):
    return jnp.dot(a.astype(BF16), b.astype(BF16), preferred_element_type=F32)


def _mm_nt(a, b):
    return lax.dot_general(a.astype(BF16), b.astype(BF16), (((1,), (1,)), ((), ())), preferred_element_type=F32)


def _mm_tn(a, b):
    return lax.dot_general(a.astype(BF16), b.astype(BF16), (((0,), (0,)), ((), ())), preferred_element_type=F32)


def _mm_f32(a, b):
    return jnp.dot(a, b, preferred_element_type=F32, precision=lax.Precision.HIGHEST)


def _softplus(x):
    return jnp.maximum(x, 0.0) + jnp.log1p(jnp.exp(-jnp.abs(x)))


def _sigmoid(x):
    return 1.0 / (1.0 + jnp.exp(-x))


def _silu(x):
    return x * _sigmoid(x)


def _iota2(shape, axis):
    return lax.broadcasted_iota(jnp.int32, shape, axis)


def _chunk_tril(n, length):
    sh = length.bit_length() - 1
    r, c = _iota2((n, n), 0), _iota2((n, n), 1)
    same = (r >> sh) == (c >> sh)
    return jnp.where(same & (c <= r), 1.0, 0.0).astype(F32)


def _triu(n):
    r, c = _iota2((n, n), 0), _iota2((n, n), 1)
    return jnp.where(r <= c, 1.0, 0.0).astype(F32)


def _inproj_body(x_ref, g_ref, w_ref, wg_ref, o_ref, og_ref, xn_ref):
    @pl.when(pl.program_id(1) == 0)
    def _():
        x = x_ref[...]
        ms = jnp.mean(x * x, axis=-1, keepdims=True)
        xn = ((x * lax.rsqrt(ms + EPS)) * g_ref[...]).astype(BF16)
        xn_ref[...] = xn
        og_ref[...] = jnp.dot(xn, wg_ref[...], preferred_element_type=F32)

    o_ref[...] = jnp.dot(xn_ref[...], w_ref[...], preferred_element_type=F32).astype(o_ref.dtype)


def _inproj(x2, g, w_main, w_gate, n, tm, tn):
    m, d = x2.shape
    return pl.pallas_call(
        _inproj_body,
        grid=(m // tm, n // tn),
        in_specs=[
            pl.BlockSpec((tm, d), lambda i, j: (i, 0)),
            pl.BlockSpec((1, d), lambda i, j: (0, 0)),
            pl.BlockSpec((d, tn), lambda i, j: (0, j)),
            pl.BlockSpec((d, LANES), lambda i, j: (0, 0)),
        ],
        out_specs=[
            pl.BlockSpec((tm, tn), lambda i, j: (i, j)),
            pl.BlockSpec((tm, LANES), lambda i, j: (i, 0)),
        ],
        out_shape=[jax.ShapeDtypeStruct((m, n), BF16), jax.ShapeDtypeStruct((m, LANES), F32)],
        scratch_shapes=[pltpu.VMEM((tm, d), BF16)],
        compiler_params=_params(2),
        name="inproj",
    )(x2, g, w_main, w_gate)


def _outproj_body(ya_ref, yb_ref, wa_ref, wb_ref, x_ref, o_ref):
    acc = jnp.dot(ya_ref[...], wa_ref[...], preferred_element_type=F32)
    acc = acc + jnp.dot(yb_ref[...], wb_ref[...], preferred_element_type=F32)
    o_ref[...] = x_ref[...] + acc


def _outproj(ya, yb, w, x2, tm, tn):
    m, d = x2.shape
    kh = ya.shape[1]
    return pl.pallas_call(
        _outproj_body,
        grid=(d // tn, m // tm),
        in_specs=[
            pl.BlockSpec((tm, kh), lambda j, i: (i, 0)),
            pl.BlockSpec((tm, kh), lambda j, i: (i, 0)),
            pl.BlockSpec((kh, tn), lambda j, i: (0, j)),
            pl.BlockSpec((kh, tn), lambda j, i: (1, j)),
            pl.BlockSpec((tm, tn), lambda j, i: (i, j)),
        ],
        out_specs=pl.BlockSpec((tm, tn), lambda j, i: (i, j)),
        out_shape=jax.ShapeDtypeStruct((m, d), F32),
        compiler_params=_params(2),
        name="outproj",
    )(ya, yb, w, w, x2)


def _attn_body(q_ref, k_ref, v_ref, z_ref, qg_ref, kg_ref, c_ref, o_ref, qs, ks, vs, bias_s):
    s = q_ref.shape[0]
    pad = LOOKBACK * CHUNK
    q = q_ref[...].astype(F32)
    q = q * lax.rsqrt(jnp.mean(q * q, axis=-1, keepdims=True) + EPS) * qg_ref[...] * (HD_A ** -0.5)
    qs[...] = q.astype(BF16)
    k = k_ref[...].astype(F32)
    k = k * lax.rsqrt(jnp.mean(k * k, axis=-1, keepdims=True) + EPS) * kg_ref[...]
    ks[0:pad, :] = jnp.zeros((pad, HD_A), BF16)
    ks[pad:pad + s, :] = k.astype(BF16)
    vs[0:pad, :] = jnp.zeros((pad, HD_A), BF16)
    vs[pad:pad + s, :] = v_ref[...]
    cb = jnp.broadcast_to(c_ref[...], (CHUNK, REL_W))
    bias_s[...] = pltpu.roll(cb, 0, 1, stride=1, stride_axis=0)[:, :BAND]
    col = _iota2((CHUNK, BAND), 1)

    def chunks(it, carry):
        gs = range(ATTN_GROUP)
        n = [it * ATTN_GROUP + g for g in gs]
        r0 = [pl.multiple_of(n[g] * CHUNK, CHUNK) for g in gs]
        sc = [_mm_nt(qs[pl.ds(r0[g], CHUNK), :], ks[pl.ds(r0[g], BAND), :]) for g in gs]
        sc = [jnp.where(col >= (LOOKBACK - n[g]) * CHUNK, sc[g] + bias_s[...], NEG) for g in gs]
        p = [jnp.exp(sc[g] - jnp.max(sc[g], axis=-1, keepdims=True)) for g in gs]
        pv = [_mm(p[g], vs[pl.ds(r0[g], BAND), :]) for g in gs]
        for g in gs:
            o = pv[g] / jnp.sum(p[g], axis=-1, keepdims=True)
            z = z_ref[pl.ds(r0[g], CHUNK), :].astype(F32)
            o_ref[pl.ds(r0[g], CHUNK), :] = (o * _silu(z)).astype(o_ref.dtype)
        return carry

    lax.fori_loop(0, s // (CHUNK * ATTN_GROUP), chunks, 0)


def _attn(p3, qg, kg, crel):
    b, s, _ = p3.shape
    h_a = crel.shape[0]

    def col_spec(sec):
        return pl.BlockSpec((None, s, HD_A), lambda bi, h, sec=sec: (bi, 0, sec * h_a + h))

    return pl.pallas_call(
        _attn_body,
        grid=(b, h_a),
        in_specs=[
            col_spec(0), col_spec(1), col_spec(2), col_spec(3),
            pl.BlockSpec((1, HD_A), lambda bi, h: (0, 0)),
            pl.BlockSpec((1, HD_A), lambda bi, h: (0, 0)),
            pl.BlockSpec((None, 1, REL_W), lambda bi, h: (h, 0, 0)),
        ],
        out_specs=pl.BlockSpec((None, s, HD_A), lambda bi, h: (bi, 0, h)),
        out_shape=jax.ShapeDtypeStruct((b, s, h_a * HD_A), BF16),
        scratch_shapes=[
            pltpu.VMEM((s, HD_A), BF16),
            pltpu.VMEM((s + LOOKBACK * CHUNK, HD_A), BF16),
            pltpu.VMEM((s + LOOKBACK * CHUNK, HD_A), BF16),
            pltpu.VMEM((CHUNK, BAND), F32),
        ],
        compiler_params=_params(2),
        name="chunk_attn",
    )(p3, p3, p3, p3, qg, kg, crel)


def _mlstm_body(q_ref, k_ref, v_ref, o_ref, z_ref, g_ref, gt_ref, bc_ref, br_ref, y_ref,
                c_st, n_st, m_st, col_b, col_i, row_b, row_i):
    ts = q_ref.shape[0]
    ln = MLSTM_L
    nct = ts // ln
    hd = q_ref.shape[1] // H_B
    scale = hd ** -0.5
    j = pl.program_id(1)
    hs = range(H_B)
    hc = [slice(h * hd, (h + 1) * hd) for h in hs]

    @pl.when(j == 0)
    def _():
        c_st[...] = jnp.zeros_like(c_st)
        n_st[...] = jnp.zeros_like(n_st)
        m_st[...] = jnp.zeros_like(m_st)
        gt = gt_ref[...] + br_ref[...]
        for h in hs:
            row_b[h] = _mm_f32(-_softplus(-gt[H_B + h]), _triu(ln))
            row_i[h] = gt[h]

    g = g_ref[...] + bc_ref[...]
    col_i[...] = g
    col_b[...] = _mm_f32(_chunk_tril(ts, ln), -_softplus(-g))
    tril = _iota2((ln, ln), 1) <= _iota2((ln, ln), 0)
    m = [m_st[h][0:1, 0:1] for h in hs]
    cm = [c_st[h] for h in hs]
    nm = [n_st[h][0:1, :] for h in hs]
    for c in range(nct):
        rows = slice(c * ln, (c + 1) * ln)
        ci = j * nct + c
        bc_all = col_b[rows, :]
        li_all = col_i[rows, :]
        bc = [bc_all[:, H_B + h:H_B + h + 1] for h in hs]
        lic = [li_all[:, h:h + 1] for h in hs]
        br = [row_b[h, pl.ds(ci, 1), :] for h in hs]
        lir = [row_i[h, pl.ds(ci, 1), :] for h in hs]
        qn = [q_ref[rows, hc[h]] for h in hs]
        kn = [k_ref[rows, hc[h]] for h in hs]
        vn = [v_ref[rows, hc[h]] for h in hs]
        qk = [_mm_nt(qn[h], kn[h]) for h in hs]
        qc = [_mm(qn[h], cm[h]) for h in hs]
        dmat = [jnp.where(tril, bc[h] - br[h] + lir[h], NEG) for h in hs]
        mt = [jnp.maximum(bc[h] + m[h], jnp.max(dmat[h], axis=-1, keepdims=True)) for h in hs]
        w_inter = [jnp.exp(bc[h] + m[h] - mt[h]) for h in hs]
        p = [jnp.exp(dmat[h] - mt[h]) * (qk[h] * scale) for h in hs]
        pv = [_mm(p[h], vn[h]) for h in hs]
        b_last = [bc[h][ln - 1:ln, :] for h in hs]
        m_new = [jnp.maximum(b_last[h] + m[h], jnp.max(b_last[h] - br[h] + lir[h], axis=-1, keepdims=True)) for h in hs]
        wk = [jnp.exp(b_last[h] - bc[h] + lic[h] - m_new[h]) * kn[h].astype(F32) for h in hs]
        upd = [_mm_tn(wk[h], vn[h]) for h in hs]
        for h in hs:
            num = w_inter[h] * (qc[h] * scale) + pv[h]
            qn_dot = jnp.sum(qn[h].astype(F32) * nm[h], axis=-1, keepdims=True) * scale
            den = w_inter[h] * qn_dot + jnp.sum(p[h], axis=-1, keepdims=True)
            hout = num / jnp.maximum(jnp.abs(den), jnp.exp(-mt[h]))
            og = o_ref[rows, hc[h]].astype(F32)
            zg = z_ref[rows, hc[h]].astype(F32)
            y_ref[rows, hc[h]] = (_sigmoid(og) * hout * _silu(zg)).astype(y_ref.dtype)
            w_c = jnp.exp(b_last[h] + m[h] - m_new[h])
            cm[h] = w_c * cm[h] + upd[h]
            nm[h] = w_c * nm[h] + jnp.sum(wk[h], axis=0, keepdims=True)
            m[h] = m_new[h]
    for h in hs:
        c_st[h] = cm[h]
        n_st[h] = jnp.broadcast_to(nm[h], n_st.shape[1:])
        m_st[h] = jnp.broadcast_to(m[h], m_st.shape[1:])


def _mlstm(p3, g3, gt4, bias_col, bias_row, ts):
    b, s, _ = p3.shape
    w_b = 1024
    hd = w_b // H_B
    ncs = s // MLSTM_L

    def col_spec(sec):
        return pl.BlockSpec((None, ts, w_b), lambda bi, j, sec=sec: (bi, j, 4 + sec))

    return pl.pallas_call(
        _mlstm_body,
        grid=(b, s // ts),
        in_specs=[
            col_spec(0), col_spec(1), col_spec(2), col_spec(3), col_spec(4),
            pl.BlockSpec((None, ts, LANES), lambda bi, j: (bi, j, 0)),
            pl.BlockSpec((None, 2 * H_B, ncs, MLSTM_L), lambda bi, j: (bi, 0, 0, 0)),
            pl.BlockSpec((1, LANES), lambda bi, j: (0, 0)),
            pl.BlockSpec((2 * H_B, 1, 1), lambda bi, j: (0, 0, 0)),
        ],
        out_specs=pl.BlockSpec((None, ts, w_b), lambda bi, j: (bi, j, 0)),
        out_shape=jax.ShapeDtypeStruct((b, s, w_b), BF16),
        scratch_shapes=[
            pltpu.VMEM((H_B, hd, hd), F32),
            pltpu.VMEM((H_B, 8, hd), F32),
            pltpu.VMEM((H_B, 8, LANES), F32),
            pltpu.VMEM((ts, LANES), F32),
            pltpu.VMEM((ts, LANES), F32),
            pltpu.VMEM((H_B, ncs, MLSTM_L), F32),
            pltpu.VMEM((H_B, ncs, MLSTM_L), F32),
        ],
        compiler_params=_params(2),
        name="mlstm",
    )(p3, p3, p3, p3, p3, g3, gt4, bias_col, bias_row)


def _rglru_body(x_ref, z_ref, cw_ref, cb_ref, gw_ref, gbr_ref, gbi_ref, lam_ref, y_ref, a_s, b_s):
    s, wc = x_ref.shape
    blk = gw_ref.shape[1]
    row = _iota2((s, blk), 0)
    sub = row & 7
    for n in range(wc // blk):
        cs = slice(n * blk, (n + 1) * blk)
        x = x_ref[:, cs].astype(F32)
        cw = cw_ref[:, cs]
        xc = cw[CONV_K - 1:CONV_K, :] * x + cb_ref[:, cs]
        for d in range(1, CONV_K):
            xc = xc + cw[CONV_K - 1 - d:CONV_K - d, :] * jnp.where(row >= d, pltpu.roll(x, d, 0), 0.0)
        gates = _mm(xc, gw_ref[n])
        r = _sigmoid(gates[:, :blk] + gbr_ref[:, cs])
        i = _sigmoid(gates[:, blk:] + gbi_ref[:, cs])
        log_a = -C_RG * r * _softplus(-lam_ref[:, cs])
        a = jnp.exp(log_a)
        bb = jnp.sqrt(-jnp.tanh(log_a) * (a * a + 1.0)) * (i * xc)
        for d in (1, 2, 4):
            keep = sub >= d
            a_sh = jnp.where(keep, pltpu.roll(a, d, 0), 1.0)
            b_sh = jnp.where(keep, pltpu.roll(bb, d, 0), 0.0)
            bb = a * b_sh + bb
            a = a * a_sh
        a_s[:, cs] = a
        b_s[:, cs] = bb

    def group(gi, carry):
        rows = pl.ds(pl.multiple_of(gi * 8, 8), 8)
        h = a_s[rows, :] * carry + b_s[rows, :]
        b_s[rows, :] = h
        return jnp.broadcast_to(h[7:8, :], h.shape)

    lax.fori_loop(0, s // 8, group, jnp.zeros((8, wc), F32), unroll=8)
    z = z_ref[...].astype(F32)
    y_ref[...] = (b_s[...] * _silu(z)).astype(y_ref.dtype)


def _rglru(p3, cw, cb, gw, gb, lam):
    b, s, _ = p3.shape
    w_c = cw.shape[1]
    blk = w_c // N_BLK_C
    nsp = w_c // RG_COLS
    return pl.pallas_call(
        _rglru_body,
        grid=(b, nsp),
        in_specs=[
            pl.BlockSpec((None, s, RG_COLS), lambda bi, n: (bi, 0, n)),
            pl.BlockSpec((None, s, RG_COLS), lambda bi, n: (bi, 0, nsp + n)),
            pl.BlockSpec((CONV_K, RG_COLS), lambda bi, n: (0, n)),
            pl.BlockSpec((1, RG_COLS), lambda bi, n: (0, n)),
            pl.BlockSpec((RG_COLS // blk, blk, 2 * blk), lambda bi, n: (n, 0, 0)),
            pl.BlockSpec((1, RG_COLS), lambda bi, n: (0, n)),
            pl.BlockSpec((1, RG_COLS), lambda bi, n: (0, nsp + n)),
            pl.BlockSpec((1, RG_COLS), lambda bi, n: (0, n)),
        ],
        out_specs=pl.BlockSpec((None, s, RG_COLS), lambda bi, n: (bi, 0, n)),
        out_shape=jax.ShapeDtypeStruct((b, s, w_c), BF16),
        scratch_shapes=[pltpu.VMEM((s, RG_COLS), F32), pltpu.VMEM((s, RG_COLS), F32)],
        compiler_params=_params(2),
        name="rglru",
    )(p3, p3, cw, cb, gw, gb, gb, lam)


def _unit_lower_inverse(a_list):
    r, c = _iota2((CHUNK, CHUNK), 0), _iota2((CHUNK, CHUNK), 1)
    eye = jnp.where(r == c, 1.0, 0.0).astype(F32)

    def pair_mask(sh):
        rb, cb_ = r >> sh, c >> sh
        return ((rb & 1) == 1) & (cb_ == rb - 1)

    ts = [eye - jnp.where(pair_mask(0), a, 0.0) for a in a_list]
    for sh in range(1, 6):
        mask = pair_mask(sh)
        x1 = [_mm(t, jnp.where(mask, a, 0.0)) for t, a in zip(ts, a_list)]
        x2 = [_mm(x, t) for x, t in zip(x1, ts)]
        ts = [t - x for t, x in zip(ts, x2)]
    return ts


def _gdn_body(q_ref, k_ref, v_ref, z_ref, g_ref, gt_ref, cw_ref, al_c_ref, dt_c_ref, al_r_ref, dt_r_ref, on_ref,
              y_ref, s_st, xe, qkv, col_d, col_b, row_d):
    ts = q_ref.shape[0]
    nct = ts // CHUNK
    w_d = q_ref.shape[1]
    h_d = w_d // HD_D
    scale = HD_D ** -0.5

    @pl.when(pl.program_id(1) == 0)
    def _():
        s_st[...] = jnp.zeros_like(s_st)
        xe[0:8, :] = jnp.zeros((8, 3 * w_d), F32)

    xe[8:8 + ts, 0:w_d] = q_ref[...].astype(F32)
    xe[8:8 + ts, w_d:2 * w_d] = k_ref[...].astype(F32)
    xe[8:8 + ts, 2 * w_d:3 * w_d] = v_ref[...].astype(F32)
    for cbk in range(3 * h_d):
        cs = slice(cbk * HD_D, (cbk + 1) * HD_D)
        acc = cw_ref[CONV_K - 1:CONV_K, cs] * xe[8:8 + ts, cs]
        for d in range(1, CONV_K):
            acc = acc + cw_ref[CONV_K - 1 - d:CONV_K - d, cs] * xe[8 - d:8 - d + ts, cs]
        acc = _silu(acc)
        if cbk < 2 * h_d:
            acc = acc * lax.rsqrt(jnp.sum(acc * acc, axis=-1, keepdims=True) + EPS)
        qkv[:, cs] = acc
    xe[0:8, :] = xe[ts:ts + 8, :]

    g = g_ref[...]
    col_b[...] = _sigmoid(g)
    col_d[...] = _mm_f32(_chunk_tril(ts, CHUNK), -jnp.exp(al_c_ref[...]) * _softplus(g + dt_c_ref[...]))
    gt = gt_ref[...]
    g_r = -jnp.exp(al_r_ref[...]) * _softplus(gt[0:h_d] + dt_r_ref[...])
    row_d[...] = _mm_f32(g_r.reshape(h_d * nct, CHUNK), _triu(CHUNK)).reshape(h_d, nct, CHUNK)

    r_i, c_i = _iota2((CHUNK, CHUNK), 0), _iota2((CHUNK, CHUNK), 1)
    tril = c_i <= r_i
    strict = c_i < r_i

    def chunk(c, carry):
        rows = pl.ds(pl.multiple_of(c * CHUNK, CHUNK), CHUNK)
        dc_all = col_d[rows, :]
        beta_all = col_b[rows, :]
        hs = range(h_d)
        dc = [dc_all[:, h:h + 1] for h in hs]
        beta = [beta_all[:, h_d + h:h_d + h + 1] for h in hs]
        q = [qkv[rows, h * HD_D:(h + 1) * HD_D] for h in hs]
        k = [qkv[rows, w_d + h * HD_D:w_d + (h + 1) * HD_D] for h in hs]
        v = [qkv[rows, 2 * w_d + h * HD_D:2 * w_d + (h + 1) * HD_D] for h in hs]
        gam = [jnp.exp(jnp.where(tril, dc[h] - row_d[h, pl.ds(c, 1), :], NEG)) for h in hs]
        kb = [k[h] * beta[h] for h in hs]
        kk = [_mm_nt(kb[h], k[h]) for h in hs]
        t_inv = _unit_lower_inverse([jnp.where(strict, kk[h] * gam[h], 0.0) for h in hs])
        ed = [jnp.exp(dc[h]) for h in hs]
        uw = [_mm(t_inv[h], jnp.concatenate([v[h] * beta[h], kb[h] * ed[h]], axis=1)) for h in hs]
        qs = [q[h] * scale for h in hs]
        qk = [_mm_nt(qs[h], k[h]) * gam[h] for h in hs]
        dl = [dc[h][CHUNK - 1:CHUNK, :] for h in hs]
        st = [s_st[h] for h in hs]
        ws = [_mm(jnp.concatenate([uw[h][:, HD_D:], qs[h] * ed[h]], axis=0), st[h]) for h in hs]
        v_new = [uw[h][:, :HD_D] - ws[h][:CHUNK] for h in hs]
        o = [ws[h][CHUNK:] + _mm(qk[h], v_new[h]) for h in hs]
        upd = [_mm_tn(k[h] * jnp.exp(dl[h] - dc[h]), v_new[h]) for h in hs]
        for h in hs:
            s_st[h] = jnp.exp(dl[h]) * st[h] + upd[h]
            on = o[h] * lax.rsqrt(jnp.mean(o[h] * o[h], axis=-1, keepdims=True) + EPS) * on_ref[...]
            z = z_ref[rows, h * HD_D:(h + 1) * HD_D].astype(F32)
            y_ref[rows, h * HD_D:(h + 1) * HD_D] = (on * _silu(z)).astype(y_ref.dtype)
        return carry

    lax.fori_loop(0, nct, chunk, 0)


def _gdn(p3, g3, gt4, cw, al_c, dt_c, al_r, dt_r, onorm, ts):
    b, s, _ = p3.shape
    w_d = 1024
    h_d = w_d // HD_D
    nct = ts // CHUNK

    def col_spec(sec):
        return pl.BlockSpec((None, ts, w_d), lambda bi, j, sec=sec: (bi, j, 2 + sec))

    return pl.pallas_call(
        _gdn_body,
        grid=(b, s // ts),
        in_specs=[
            col_spec(0), col_spec(1), col_spec(2), col_spec(3),
            pl.BlockSpec((None, ts, LANES), lambda bi, j: (bi, j, 0)),
            pl.BlockSpec((None, 2 * h_d, nct, CHUNK), lambda bi, j: (bi, 0, j, 0)),
            pl.BlockSpec((CONV_K, 3 * w_d), lambda bi, j: (0, 0)),
            pl.BlockSpec((1, LANES), lambda bi, j: (0, 0)),
            pl.BlockSpec((1, LANES), lambda bi, j: (0, 0)),
            pl.BlockSpec((h_d, 1, 1), lambda bi, j: (0, 0, 0)),
            pl.BlockSpec((h_d, 1, 1), lambda bi, j: (0, 0, 0)),
            pl.BlockSpec((1, HD_D), lambda bi, j: (0, 0)),
        ],
        out_specs=pl.BlockSpec((None, ts, w_d), lambda bi, j: (bi, j, 0)),
        out_shape=jax.ShapeDtypeStruct((b, s, w_d), BF16),
        scratch_shapes=[
            pltpu.VMEM((h_d, HD_D, HD_D), F32),
            pltpu.VMEM((ts + 8, 3 * w_d), F32),
            pltpu.VMEM((ts, 3 * w_d), F32),
            pltpu.VMEM((ts, LANES), F32),
            pltpu.VMEM((ts, LANES), F32),
            pltpu.VMEM((h_d, nct, CHUNK), F32),
        ],
        compiler_params=_params(2),
        name="gated_delta",
    )(p3, p3, p3, p3, g3, gt4, cw, al_c, dt_c, al_r, dt_r, onorm)


def _rel_row(rel_bias):
    t = (np.arange(REL_W) + CHUNK - 1) % REL_W
    idx = np.clip(LOOKBACK * CHUNK + CHUNK - 1 - t, -REL_MAX, REL_MAX) + REL_MAX
    return rel_bias[:, idx].astype(F32)[:, None, :]


def _pad_lanes(v, n=LANES):
    return jnp.pad(v, ((0, 0), (0, n - v.shape[1])))


def _gates_t(g3, n, length):
    b, s, _ = g3.shape
    return jnp.transpose(g3[:, :, :n], (0, 2, 1)).reshape(b, n, s // length, length)


def kernel(x, ev_norm, ev_w_in, ev_if_bias, ev_qn_gain, ev_kn_gain, ev_rel_bias, ev_w_out, od_norm, od_w_in, od_conv_c_w, od_conv_c_b, od_gate_w, od_gate_b, od_lambda, od_conv_d_w, od_a_log, od_dt_bias, od_onorm, od_w_out):
    b, s, d = x.shape
    m = b * s
    half = d // 2
    tm = min(1024, m)
    ts = min(512, s)
    x2 = x.reshape(m, d)
    depth = ev_norm.shape[0] + od_norm.shape[0]
    for layer in range(depth):
        j = layer // 2
        if layer % 2 == 0:
            n_main = 9 * half
            w_in = ev_w_in[j].astype(BF16)
            p, g = _inproj(x2, ev_norm[j].reshape(1, d), w_in, _pad_lanes(w_in[:, n_main:]), n_main, tm, 1024)
            p3, g3 = p.reshape(b, s, n_main), g.reshape(b, s, LANES)
            ya = _attn(p3, ev_qn_gain[j].reshape(1, HD_A), ev_kn_gain[j].reshape(1, HD_A), _rel_row(ev_rel_bias[j]))
            bias = ev_if_bias[j].astype(F32)
            yb = _mlstm(p3, g3, _gates_t(g3, 2 * H_B, MLSTM_L), _pad_lanes(bias.reshape(1, -1)), bias.reshape(-1, 1, 1), ts)
            x2 = _outproj(ya.reshape(m, half), yb.reshape(m, half), ev_w_out[j].astype(BF16), x2, tm, 1024)
        else:
            n_main = 6 * half
            h_d = half // HD_D
            w_in = od_w_in[j].astype(BF16)
            p, g = _inproj(x2, od_norm[j].reshape(1, d), w_in, _pad_lanes(w_in[:, n_main:]), n_main, tm, 1024)
            p3, g3 = p.reshape(b, s, n_main), g.reshape(b, s, LANES)
            yc = _rglru(p3, od_conv_c_w[j], od_conv_c_b[j].reshape(1, -1), od_gate_w[j].astype(BF16),
                        od_gate_b[j].reshape(1, -1), od_lambda[j].reshape(1, -1))
            al, dt = od_a_log[j].astype(F32), od_dt_bias[j].astype(F32)
            yd = _gdn(p3, g3, _gates_t(g3, 2 * h_d, CHUNK), od_conv_d_w[j], _pad_lanes(al.reshape(1, -1)),
                      _pad_lanes(dt.reshape(1, -1)), al.reshape(-1, 1, 1), dt.reshape(-1, 1, 1),
                      od_onorm[j].reshape(1, HD_D), ts)
            x2 = _outproj(yc.reshape(m, half), yd.reshape(m, half), od_w_out[j].astype(BF16), x2, tm, 1024)
    return x2.reshape(b, s, d)
```

```python
import functools

import numpy as np
import jax
import jax.numpy as jnp
from jax import lax
from jax.experimental import pallas as pl
from jax.experimental.pallas import tpu as pltpu

F32 = jnp.float32
BF16 = jnp.bfloat16

CHUNK = 64
EPS = 1e-6
NEG = -1e30
CONV_K = 4
LOG2E = 1.4426950408889634
LANES = 128
HD_A = 128
LOOKBACK = 8
BAND = (LOOKBACK + 1) * CHUNK
REL_MAX = 256
REL_W = 640
ATTN_GROUP = 8
H_B = 4
MLSTM_L = 256
N_BLK_C = 8
C_RG = 8.0
RG_COLS = 512
HD_D = 128
GDN_GROUP = 4

VMEM_LIMIT = 56 * 1024 * 1024


def _params(n_axes):
    return pltpu.CompilerParams(dimension_semantics=("arbitrary",) * n_axes, vmem_limit_bytes=VMEM_LIMIT)


def _mm(a, b):
    return jnp.dot(a.astype(BF16), b.astype(BF16), preferred_element_type=F32)


def _mm_nt(a, b):
    return lax.dot_general(a.astype(BF16), b.astype(BF16), (((1,), (1,)), ((), ())), preferred_element_type=F32)


def _mm_tn(a, b):
    return lax.dot_general(a.astype(BF16), b.astype(BF16), (((0,), (0,)), ((), ())), preferred_element_type=F32)


def _mm_f32(a, b):
    return jnp.dot(a, b, preferred_element_type=F32, precision=lax.Precision.HIGHEST)


def _log1p(e):
    u = 1.0 + e
    return jnp.where(u == 1.0, e, jnp.log(u) * (e / (u - 1.0)))


def _softplus(x):
    return jnp.maximum(x, 0.0) + _log1p(jnp.exp(-jnp.abs(x)))


def _sigmoid(x):
    return 1.0 / (1.0 + jnp.exp2(x * -LOG2E))


def _silu(x):
    return x * _sigmoid(x)


def _iota2(shape, axis):
    return lax.broadcasted_iota(jnp.int32, shape, axis)


def _chunk_tril(n, length):
    sh = length.bit_length() - 1
    r, c = _iota2((n, n), 0), _iota2((n, n), 1)
    same = (r >> sh) == (c >> sh)
    return jnp.where(same & (c <= r), 1.0, 0.0).astype(F32)


def _triu(n):
    r, c = _iota2((n, n), 0), _iota2((n, n), 1)
    return jnp.where(r <= c, 1.0, 0.0).astype(F32)


def _inproj_body(x_ref, g_ref, w_ref, wg_ref, o_ref, og_ref, xn_ref):
    @pl.when(pl.program_id(1) == 0)
    def _():
        x = x_ref[...]
        ms = jnp.mean(x * x, axis=-1, keepdims=True)
        xn = ((x * lax.rsqrt(ms + EPS)) * g_ref[...]).astype(BF16)
        xn_ref[...] = xn
        og_ref[...] = jnp.dot(xn, wg_ref[...], preferred_element_type=F32)

    o_ref[...] = jnp.dot(xn_ref[...], w_ref[...], preferred_element_type=F32).astype(o_ref.dtype)


def _inproj(x2, g, w_main, w_gate, n, tm, tn):
    m, d = x2.shape
    return pl.pallas_call(
        _inproj_body,
        grid=(m // tm, n // tn),
        in_specs=[
            pl.BlockSpec((tm, d), lambda i, j: (i, 0)),
            pl.BlockSpec((1, d), lambda i, j: (0, 0)),
            pl.BlockSpec((d, tn), lambda i, j: (0, j)),
            pl.BlockSpec((d, LANES), lambda i, j: (0, 0)),
        ],
        out_specs=[
            pl.BlockSpec((tm, tn), lambda i, j: (i, j)),
            pl.BlockSpec((tm, LANES), lambda i, j: (i, 0)),
        ],
        out_shape=[jax.ShapeDtypeStruct((m, n), BF16), jax.ShapeDtypeStruct((m, LANES), F32)],
        scratch_shapes=[pltpu.VMEM((tm, d), BF16)],
        compiler_params=_params(2),
        name="inproj",
    )(x2, g, w_main, w_gate)


def _outproj_body(ya_ref, yb_ref, wa_ref, wb_ref, x_ref, o_ref):
    acc = jnp.dot(ya_ref[...], wa_ref[...], preferred_element_type=F32)
    acc = acc + jnp.dot(yb_ref[...], wb_ref[...], preferred_element_type=F32)
    o_ref[...] = x_ref[...] + acc


def _outproj(ya, yb, w, x2, tm, tn):
    m, d = x2.shape
    kh = ya.shape[1]
    return pl.pallas_call(
        _outproj_body,
        grid=(d // tn, m // tm),
        in_specs=[
            pl.BlockSpec((tm, kh), lambda j, i: (i, 0)),
            pl.BlockSpec((tm, kh), lambda j, i: (i, 0)),
            pl.BlockSpec((kh, tn), lambda j, i: (0, j)),
            pl.BlockSpec((kh, tn), lambda j, i: (1, j)),
            pl.BlockSpec((tm, tn), lambda j, i: (i, j)),
        ],
        out_specs=pl.BlockSpec((tm, tn), lambda j, i: (i, j)),
        out_shape=jax.ShapeDtypeStruct((m, d), F32),
        compiler_params=_params(2),
        name="outproj",
    )(ya, yb, w, w, x2)


def _attn_body(q_ref, k_ref, v_ref, z_ref, qg_ref, kg_ref, c_ref, o_ref, qs, ks, vs, bias_s):
    s = q_ref.shape[0]
    pad = LOOKBACK * CHUNK
    q = q_ref[...].astype(F32)
    q = q * lax.rsqrt(jnp.mean(q * q, axis=-1, keepdims=True) + EPS) * qg_ref[...] * (HD_A ** -0.5)
    qs[...] = q.astype(BF16)
    k = k_ref[...].astype(F32)
    k = k * lax.rsqrt(jnp.mean(k * k, axis=-1, keepdims=True) + EPS) * kg_ref[...]
    ks[0:pad, :] = jnp.zeros((pad, HD_A), BF16)
    ks[pad:pad + s, :] = k.astype(BF16)
    vs[0:pad, :] = jnp.zeros((pad, HD_A), BF16)
    vs[pad:pad + s, :] = v_ref[...]
    cb = jnp.broadcast_to(c_ref[...], (CHUNK, REL_W))
    bias_s[...] = pltpu.roll(cb, 0, 1, stride=1, stride_axis=0)[:, :BAND]
    col = _iota2((CHUNK, BAND), 1)

    def chunks(it, carry):
        gs = range(ATTN_GROUP)
        n = [it * ATTN_GROUP + g for g in gs]
        r0 = [pl.multiple_of(n[g] * CHUNK, CHUNK) for g in gs]
        sc = [_mm_nt(qs[pl.ds(r0[g], CHUNK), :], ks[pl.ds(r0[g], BAND), :]) for g in gs]
        sc = [jnp.where(col >= (LOOKBACK - n[g]) * CHUNK, sc[g] + bias_s[...], NEG) for g in gs]
        p = [jnp.exp(sc[g] - jnp.max(sc[g], axis=-1, keepdims=True)) for g in gs]
        pv = [_mm(p[g], vs[pl.ds(r0[g], BAND), :]) for g in gs]
        for g in gs:
            o = pv[g] / jnp.sum(p[g], axis=-1, keepdims=True)
            z = z_ref[pl.ds(r0[g], CHUNK), :].astype(F32)
            o_ref[pl.ds(r0[g], CHUNK), :] = (o * _silu(z)).astype(o_ref.dtype)
        return carry

    lax.fori_loop(0, s // (CHUNK * ATTN_GROUP), chunks, 0)


def _attn(p3, qg, kg, crel):
    b, s, _ = p3.shape
    h_a = crel.shape[0]

    def col_spec(sec):
        return pl.BlockSpec((None, s, HD_A), lambda bi, h, sec=sec: (bi, 0, sec * h_a + h))

    return pl.pallas_call(
        _attn_body,
        grid=(b, h_a),
        in_specs=[
            col_spec(0), col_spec(1), col_spec(2), col_spec(3),
            pl.BlockSpec((1, HD_A), lambda bi, h: (0, 0)),
            pl.BlockSpec((1, HD_A), lambda bi, h: (0, 0)),
            pl.BlockSpec((None, 1, REL_W), lambda bi, h: (h, 0, 0)),
        ],
        out_specs=pl.BlockSpec((None, s, HD_A), lambda bi, h: (bi, 0, h)),
        out_shape=jax.ShapeDtypeStruct((b, s, h_a * HD_A), BF16),
        scratch_shapes=[
            pltpu.VMEM((s, HD_A), BF16),
            pltpu.VMEM((s + LOOKBACK * CHUNK, HD_A), BF16),
            pltpu.VMEM((s + LOOKBACK * CHUNK, HD_A), BF16),
            pltpu.VMEM((CHUNK, BAND), F32),
        ],
        compiler_params=_params(2),
        name="chunk_attn",
    )(p3, p3, p3, p3, qg, kg, crel)


def _mlstm_body(q_ref, k_ref, v_ref, o_ref, z_ref, g_ref, gt_ref, bc_ref, br_ref, y_ref,
                c_st, n_st, m_st, col_b, col_i, row_b, row_i):
    ts = q_ref.shape[0]
    ln = MLSTM_L
    nct = ts // ln
    hd = q_ref.shape[1] // H_B
    scale = hd ** -0.5
    j = pl.program_id(1)
    hs = range(H_B)
    hc = [slice(h * hd, (h + 1) * hd) for h in hs]

    @pl.when(j == 0)
    def _():
        c_st[...] = jnp.zeros_like(c_st)
        n_st[...] = jnp.zeros_like(n_st)
        m_st[...] = jnp.zeros_like(m_st)
        gt = gt_ref[...] + br_ref[...]
        for h in hs:
            row_b[h] = _mm_f32(-_softplus(-gt[H_B + h]), _triu(ln))
            row_i[h] = gt[h]

    g = g_ref[...] + bc_ref[...]
    col_i[...] = g
    col_b[...] = _mm_f32(_chunk_tril(ts, ln), -_softplus(-g))
    tril = _iota2((ln, ln), 1) <= _iota2((ln, ln), 0)
    m = [m_st[h][0:1, 0:1] for h in hs]
    cm = [c_st[h] for h in hs]
    nm = [n_st[h][0:1, :] for h in hs]
    for c in range(nct):
        rows = slice(c * ln, (c + 1) * ln)
        ci = j * nct + c
        bc_all = col_b[rows, :]
        li_all = col_i[rows, :]
        bc = [bc_all[:, H_B + h:H_B + h + 1] for h in hs]
        lic = [li_all[:, h:h + 1] for h in hs]
        br = [row_b[h, pl.ds(ci, 1), :] for h in hs]
        lir = [row_i[h, pl.ds(ci, 1), :] for h in hs]
        qn = [q_ref[rows, hc[h]] for h in hs]
        kn = [k_ref[rows, hc[h]] for h in hs]
        vn = [v_ref[rows, hc[h]] for h in hs]
        qk = [_mm_nt(qn[h], kn[h]) for h in hs]
        qc = [_mm(qn[h], cm[h]) for h in hs]
        dmat = [jnp.where(tril, bc[h] - br[h] + lir[h], NEG) for h in hs]
        mt = [jnp.maximum(bc[h] + m[h], jnp.max(dmat[h], axis=-1, keepdims=True)) for h in hs]
        w_inter = [jnp.exp(bc[h] + m[h] - mt[h]) for h in hs]
        p = [jnp.exp(dmat[h] - mt[h]) * (qk[h] * scale) for h in hs]
        pv = [_mm(p[h], vn[h]) for h in hs]
        b_last = [bc[h][ln - 1:ln, :] for h in hs]
        m_new = [jnp.maximum(b_last[h] + m[h], jnp.max(b_last[h] - br[h] + lir[h], axis=-1, keepdims=True)) for h in hs]
        wk = [jnp.exp(b_last[h] - bc[h] + lic[h] - m_new[h]) * kn[h].astype(F32) for h in hs]
        upd = [_mm_tn(wk[h], vn[h]) for h in hs]
        for h in hs:
            num = w_inter[h] * (qc[h] * scale) + pv[h]
            qn_dot = jnp.sum(qn[h].astype(F32) * nm[h], axis=-1, keepdims=True) * scale
            den = w_inter[h] * qn_dot + jnp.sum(p[h], axis=-1, keepdims=True)
            hout = num / jnp.maximum(jnp.abs(den), jnp.exp(-mt[h]))
            og = o_ref[rows, hc[h]].astype(F32)
            zg = z_ref[rows, hc[h]].astype(F32)
            y_ref[rows, hc[h]] = (_sigmoid(og) * hout * _silu(zg)).astype(y_ref.dtype)
            w_c = jnp.exp(b_last[h] + m[h] - m_new[h])
            cm[h] = w_c * cm[h] + upd[h]
            nm[h] = w_c * nm[h] + jnp.sum(wk[h], axis=0, keepdims=True)
            m[h] = m_new[h]
    for h in hs:
        c_st[h] = cm[h]
        n_st[h] = jnp.broadcast_to(nm[h], n_st.shape[1:])
        m_st[h] = jnp.broadcast_to(m[h], m_st.shape[1:])


def _mlstm(p3, g3, gt4, bias_col, bias_row, ts):
    b, s, _ = p3.shape
    w_b = 1024
    hd = w_b // H_B
    ncs = s // MLSTM_L

    def col_spec(sec):
        return pl.BlockSpec((None, ts, w_b), lambda bi, j, sec=sec: (bi, j, 4 + sec))

    return pl.pallas_call(
        _mlstm_body,
        grid=(b, s // ts),
        in_specs=[
            col_spec(0), col_spec(1), col_spec(2), col_spec(3), col_spec(4),
            pl.BlockSpec((None, ts, LANES), lambda bi, j: (bi, j, 0)),
            pl.BlockSpec((None, 2 * H_B, ncs, MLSTM_L), lambda bi, j: (bi, 0, 0, 0)),
            pl.BlockSpec((1, LANES), lambda bi, j: (0, 0)),
            pl.BlockSpec((2 * H_B, 1, 1), lambda bi, j: (0, 0, 0)),
        ],
        out_specs=pl.BlockSpec((None, ts, w_b), lambda bi, j: (bi, j, 0)),
        out_shape=jax.ShapeDtypeStruct((b, s, w_b), BF16),
        scratch_shapes=[
            pltpu.VMEM((H_B, hd, hd), F32),
            pltpu.VMEM((H_B, 8, hd), F32),
            pltpu.VMEM((H_B, 8, LANES), F32),
            pltpu.VMEM((ts, LANES), F32),
            pltpu.VMEM((ts, LANES), F32),
            pltpu.VMEM((H_B, ncs, MLSTM_L), F32),
            pltpu.VMEM((H_B, ncs, MLSTM_L), F32),
        ],
        compiler_params=_params(2),
        name="mlstm",
    )(p3, p3, p3, p3, p3, g3, gt4, bias_col, bias_row)


def _rglru_body(x_ref, z_ref, cw_ref, cb_ref, gw_ref, gbr_ref, gbi_ref, lam_ref, y_ref, a_s, b_s):
    s, wc = x_ref.shape
    blk = gw_ref.shape[1]
    row8 = _iota2((8, blk), 0)
    sub = _iota2((s // 8, 8, blk), 1)
    for n in range(wc // blk):
        cs = slice(n * blk, (n + 1) * blk)
        x = x_ref[:, cs].astype(F32)
        cw = cw_ref[:, cs]
        xc = cw[CONV_K - 1:CONV_K, :] * x + cb_ref[:, cs]
        for d in range(1, CONV_K):
            xs = pltpu.roll(x, d, 0)
            xs = jnp.concatenate([jnp.where(row8 >= d, xs[0:8], 0.0), xs[8:]], axis=0)
            xc = xc + cw[CONV_K - 1 - d:CONV_K - d, :] * xs
        gates = _mm(xc, gw_ref[n])
        r = _sigmoid(gates[:, :blk] + gbr_ref[:, cs])
        i = _sigmoid(gates[:, blk:] + gbi_ref[:, cs])
        log_a = -C_RG * r * _softplus(-lam_ref[:, cs])
        a = jnp.exp(log_a)
        var = -jnp.tanh(log_a) * (a * a + 1.0)
        bb = jnp.where(var > 0.0, var * lax.rsqrt(var), 0.0) * (i * xc)
        a = a.reshape(s // 8, 8, blk)
        bb = bb.reshape(s // 8, 8, blk)
        for d in (1, 2, 4):
            keep = sub >= d
            a_sh = jnp.where(keep, pltpu.roll(a, d, 1), 1.0)
            b_sh = jnp.where(keep, pltpu.roll(bb, d, 1), 0.0)
            bb = a * b_sh + bb
            a = a * a_sh
        a_s[:, cs] = a.reshape(s, blk)
        b_s[:, cs] = bb.reshape(s, blk)

    def group(gi, carry):
        rows = pl.ds(pl.multiple_of(gi * 8, 8), 8)
        h = a_s[rows, :] * carry + b_s[rows, :]
        b_s[rows, :] = h
        return jnp.broadcast_to(h[7:8, :], h.shape)

    lax.fori_loop(0, s // 8, group, jnp.zeros((8, wc), F32), unroll=8)
    z = z_ref[...].astype(F32)
    y_ref[...] = (b_s[...] * _silu(z)).astype(y_ref.dtype)


def _rglru(p3, cw, cb, gw, gb, lam):
    b, s, _ = p3.shape
    w_c = cw.shape[1]
    blk = w_c // N_BLK_C
    nsp = w_c // RG_COLS
    return pl.pallas_call(
        _rglru_body,
        grid=(b, nsp),
        in_specs=[
            pl.BlockSpec((None, s, RG_COLS), lambda bi, n: (bi, 0, n)),
            pl.BlockSpec((None, s, RG_COLS), lambda bi, n: (bi, 0, nsp + n)),
            pl.BlockSpec((CONV_K, RG_COLS), lambda bi, n: (0, n)),
            pl.BlockSpec((1, RG_COLS), lambda bi, n: (0, n)),
            pl.BlockSpec((RG_COLS // blk, blk, 2 * blk), lambda bi, n: (n, 0, 0)),
            pl.BlockSpec((1, RG_COLS), lambda bi, n: (0, n)),
            pl.BlockSpec((1, RG_COLS), lambda bi, n: (0, nsp + n)),
            pl.BlockSpec((1, RG_COLS), lambda bi, n: (0, n)),
        ],
        out_specs=pl.BlockSpec((None, s, RG_COLS), lambda bi, n: (bi, 0, n)),
        out_shape=jax.ShapeDtypeStruct((b, s, w_c), BF16),
        scratch_shapes=[pltpu.VMEM((s, RG_COLS), F32), pltpu.VMEM((s, RG_COLS), F32)],
        compiler_params=_params(2),
        name="rglru",
    )(p3, p3, cw, cb, gw, gb, gb, lam)


def _unit_lower_inverse(a_list):
    r, c = _iota2((CHUNK, CHUNK), 0), _iota2((CHUNK, CHUNK), 1)
    eye = jnp.where(r == c, 1.0, 0.0).astype(F32)

    def pair_mask(sh):
        rb, cb_ = r >> sh, c >> sh
        return ((rb & 1) == 1) & (cb_ == rb - 1)

    ts = [eye - jnp.where(pair_mask(0), a, 0.0) for a in a_list]
    for sh in range(1, 6):
        mask = pair_mask(sh)
        x1 = [_mm(t, jnp.where(mask, a, 0.0)) for t, a in zip(ts, a_list)]
        x2 = [_mm(x, t) for x, t in zip(x1, ts)]
        ts = [t - x for t, x in zip(ts, x2)]
    return ts


def _gdn_body(q_ref, k_ref, v_ref, z_ref, g_ref, gt_ref, cw_ref, al_c_ref, dt_c_ref, al_r_ref, dt_r_ref, on_ref,
              y_ref, s_st, xe, qkv, col_d, col_b, row_d, lhs_s, o0_s, n_s):
    ts = q_ref.shape[0]
    nct = ts // CHUNK
    w_d = q_ref.shape[1]
    h_d = w_d // HD_D
    scale = HD_D ** -0.5

    @pl.when(pl.program_id(1) == 0)
    def _():
        s_st[...] = jnp.zeros_like(s_st)
        xe[0:8, :] = jnp.zeros((8, 3 * w_d), F32)

    xe[8:8 + ts, 0:w_d] = q_ref[...].astype(F32)
    xe[8:8 + ts, w_d:2 * w_d] = k_ref[...].astype(F32)
    xe[8:8 + ts, 2 * w_d:3 * w_d] = v_ref[...].astype(F32)
    def conv_rows(rb, carry):
        r0 = pl.multiple_of(rb * CHUNK, CHUNK)
        for cbk in range(3 * h_d):
            cs = slice(cbk * HD_D, (cbk + 1) * HD_D)
            xw = xe[pl.ds(r0, CHUNK + 8), cs]
            acc = cw_ref[CONV_K - 1:CONV_K, cs] * xw[8:]
            for d in range(1, CONV_K):
                acc = acc + cw_ref[CONV_K - 1 - d:CONV_K - d, cs] * xw[8 - d:8 - d + CHUNK]
            acc = _silu(acc)
            if cbk < 2 * h_d:
                acc = acc * lax.rsqrt(jnp.sum(acc * acc, axis=-1, keepdims=True) + EPS)
            qkv[pl.ds(r0, CHUNK), cs] = acc
        return carry

    lax.fori_loop(0, nct, conv_rows, 0)
    xe[0:8, :] = xe[ts:ts + 8, :]

    g = g_ref[...]
    col_b[...] = _sigmoid(g)
    col_d[...] = _mm_f32(_chunk_tril(ts, CHUNK), -jnp.exp(al_c_ref[...]) * _softplus(g + dt_c_ref[...]))
    gt = gt_ref[...]
    g_r = -jnp.exp(al_r_ref[...]) * _softplus(gt[0:h_d] + dt_r_ref[...])
    row_d[...] = _mm_f32(g_r.reshape(h_d * nct, CHUNK), _triu(CHUNK)).reshape(h_d, nct, CHUNK)

    r_i, c_i = _iota2((CHUNK, CHUNK), 0), _iota2((CHUNK, CHUNK), 1)
    tril = c_i <= r_i
    strict = c_i < r_i

    hs = range(h_d)

    def group(gi, carry):
        probs = [(cc, h) for cc in range(GDN_GROUP) for h in hs]
        cidx = [gi * GDN_GROUP + cc for cc in range(GDN_GROUP)]
        rows = [pl.ds(pl.multiple_of(ci * CHUNK, CHUNK), CHUNK) for ci in cidx]
        dc_all = [col_d[r, :] for r in rows]
        beta_all = [col_b[r, :] for r in rows]
        dc = [dc_all[cc][:, h:h + 1] for cc, h in probs]
        beta = [beta_all[cc][:, h_d + h:h_d + h + 1] for cc, h in probs]
        q = [qkv[rows[cc], h * HD_D:(h + 1) * HD_D] for cc, h in probs]
        k = [qkv[rows[cc], w_d + h * HD_D:w_d + (h + 1) * HD_D] for cc, h in probs]
        v = [qkv[rows[cc], 2 * w_d + h * HD_D:2 * w_d + (h + 1) * HD_D] for cc, h in probs]
        ps = range(len(probs))
        gam = [jnp.exp(jnp.where(tril, dc[i] - row_d[probs[i][1], pl.ds(cidx[probs[i][0]], 1), :], NEG)) for i in ps]
        kb = [k[i] * beta[i] for i in ps]
        kk = [_mm_nt(kb[i], k[i]) for i in ps]
        t_inv = _unit_lower_inverse([jnp.where(strict, kk[i] * gam[i], 0.0) for i in ps])
        ed = [jnp.exp(dc[i]) for i in ps]
        uw = [_mm(t_inv[i], jnp.concatenate([v[i] * beta[i], kb[i] * ed[i]], axis=1)) for i in ps]
        qs = [q[i] * scale for i in ps]
        qk = [_mm_nt(qs[i], k[i]) * gam[i] for i in ps]
        kd = [k[i] * jnp.exp(dc[i][CHUNK - 1:CHUNK, :] - dc[i]) for i in ps]
        x1 = [_mm(qk[i], uw[i]) for i in ps]
        x2 = [_mm_tn(kd[i], uw[i]) for i in ps]
        for i, (cc, h) in enumerate(probs):
            lhs_s[cc, h] = jnp.concatenate([qs[i] * ed[i] - x1[i][:, HD_D:], x2[i][:, HD_D:]], axis=0).astype(BF16)
            o0_s[cc, h] = x1[i][:, :HD_D]
            n_s[cc, h] = x2[i][:, :HD_D]

        def step(cc, carry2):
            c = gi * GDN_GROUP + cc
            rws = pl.ds(pl.multiple_of(c * CHUNK, CHUNK), CHUNK)
            d_last = jnp.exp(col_d[pl.ds(c * CHUNK + CHUNK - 1, 1), :])
            st = [s_st[h] for h in hs]
            r = [_mm(lhs_s[cc, h], st[h]) for h in hs]
            for h in hs:
                o = o0_s[cc, h] + r[h][:CHUNK]
                s_st[h] = d_last[:, h:h + 1] * st[h] + n_s[cc, h] - r[h][CHUNK:]
                on = o * lax.rsqrt(jnp.mean(o * o, axis=-1, keepdims=True) + EPS) * on_ref[...]
                z = z_ref[rws, h * HD_D:(h + 1) * HD_D].astype(F32)
                y_ref[rws, h * HD_D:(h + 1) * HD_D] = (on * _silu(z)).astype(y_ref.dtype)
            return carry2

        lax.fori_loop(0, GDN_GROUP, step, 0)
        return carry

    lax.fori_loop(0, nct // GDN_GROUP, group, 0)


def _gdn(p3, g3, gt4, cw, al_c, dt_c, al_r, dt_r, onorm, ts):
    b, s, _ = p3.shape
    w_d = 1024
    h_d = w_d // HD_D
    nct = ts // CHUNK

    def col_spec(sec):
        return pl.BlockSpec((None, ts, w_d), lambda bi, j, sec=sec: (bi, j, 2 + sec))

    return pl.pallas_call(
        _gdn_body,
        grid=(b, s // ts),
        in_specs=[
            col_spec(0), col_spec(1), col_spec(2), col_spec(3),
            pl.BlockSpec((None, ts, LANES), lambda bi, j: (bi, j, 0)),
            pl.BlockSpec((None, 2 * h_d, nct, CHUNK), lambda bi, j: (bi, 0, j, 0)),
            pl.BlockSpec((CONV_K, 3 * w_d), lambda bi, j: (0, 0)),
            pl.BlockSpec((1, LANES), lambda bi, j: (0, 0)),
            pl.BlockSpec((1, LANES), lambda bi, j: (0, 0)),
            pl.BlockSpec((h_d, 1, 1), lambda bi, j: (0, 0, 0)),
            pl.BlockSpec((h_d, 1, 1), lambda bi, j: (0, 0, 0)),
            pl.BlockSpec((1, HD_D), lambda bi, j: (0, 0)),
        ],
        out_specs=pl.BlockSpec((None, ts, w_d), lambda bi, j: (bi, j, 0)),
        out_shape=jax.ShapeDtypeStruct((b, s, w_d), BF16),
        scratch_shapes=[
            pltpu.VMEM((h_d, HD_D, HD_D), F32),
            pltpu.VMEM((ts + 8, 3 * w_d), F32),
            pltpu.VMEM((ts, 3 * w_d), F32),
            pltpu.VMEM((ts, LANES), F32),
            pltpu.VMEM((ts, LANES), F32),
            pltpu.VMEM((h_d, nct, CHUNK), F32),
            pltpu.VMEM((GDN_GROUP, h_d, CHUNK + HD_D, HD_D), BF16),
            pltpu.VMEM((GDN_GROUP, h_d, CHUNK, HD_D), F32),
            pltpu.VMEM((GDN_GROUP, h_d, HD_D, HD_D), F32),
        ],
        compiler_params=_params(2),
        name="gated_delta",
    )(p3, p3, p3, p3, g3, gt4, cw, al_c, dt_c, al_r, dt_r, onorm)


def _rel_row(rel_bias):
    t = (np.arange(REL_W) + CHUNK - 1) % REL_W
    idx = np.clip(LOOKBACK * CHUNK + CHUNK - 1 - t, -REL_MAX, REL_MAX) + REL_MAX
    return rel_bias[:, idx].astype(F32)[:, None, :]


def _pad_lanes(v, n=LANES):
    return jnp.pad(v, ((0, 0), (0, n - v.shape[1])))


def _gates_t(g3, n, length):
    b, s, _ = g3.shape
    return jnp.transpose(g3[:, :, :n], (0, 2, 1)).reshape(b, n, s // length, length)


def kernel(x, ev_norm, ev_w_in, ev_if_bias, ev_qn_gain, ev_kn_gain, ev_rel_bias, ev_w_out, od_norm, od_w_in, od_conv_c_w, od_conv_c_b, od_gate_w, od_gate_b, od_lambda, od_conv_d_w, od_a_log, od_dt_bias, od_onorm, od_w_out):
    b, s, d = x.shape
    m = b * s
    half = d // 2
    tm = min(1024, m)
    ts = min(512, s)
    x2 = x.reshape(m, d)
    depth = ev_norm.shape[0] + od_norm.shape[0]
    for layer in range(depth):
        j = layer // 2
        if layer % 2 == 0:
            n_main = 9 * half
            w_in = ev_w_in[j].astype(BF16)
            p, g = _inproj(x2, ev_norm[j].reshape(1, d), w_in, _pad_lanes(w_in[:, n_main:]), n_main, tm, 1024)
            p3, g3 = p.reshape(b, s, n_main), g.reshape(b, s, LANES)
            ya = _attn(p3, ev_qn_gain[j].reshape(1, HD_A), ev_kn_gain[j].reshape(1, HD_A), _rel_row(ev_rel_bias[j]))
            bias = ev_if_bias[j].astype(F32)
            yb = _mlstm(p3, g3, _gates_t(g3, 2 * H_B, MLSTM_L), _pad_lanes(bias.reshape(1, -1)), bias.reshape(-1, 1, 1), ts)
            x2 = _outproj(ya.reshape(m, half), yb.reshape(m, half), ev_w_out[j].astype(BF16), x2, tm, 1024)
        else:
            n_main = 6 * half
            h_d = half // HD_D
            w_in = od_w_in[j].astype(BF16)
            p, g = _inproj(x2, od_norm[j].reshape(1, d), w_in, _pad_lanes(w_in[:, n_main:]), n_main, tm, 1024)
            p3, g3 = p.reshape(b, s, n_main), g.reshape(b, s, LANES)
            yc = _rglru(p3, od_conv_c_w[j], od_conv_c_b[j].reshape(1, -1), od_gate_w[j].astype(BF16),
                        od_gate_b[j].reshape(1, -1), od_lambda[j].reshape(1, -1))
            al, dt = od_a_log[j].astype(F32), od_dt_bias[j].astype(F32)
            yd = _gdn(p3, g3, _gates_t(g3, 2 * h_d, CHUNK), od_conv_d_w[j], _pad_lanes(al.reshape(1, -1)),
                      _pad_lanes(dt.reshape(1, -1)), al.reshape(-1, 1, 1), dt.reshape(-1, 1, 1),
                      od_onorm[j].reshape(1, HD_D), ts)
            x2 = _outproj(yc.reshape(m, half), yd.reshape(m, half), od_w_out[j].astype(BF16), x2, tm, 1024)
    return x2.reshape(b, s, d)
```

```python
import functools

import numpy as np
import jax
import jax.numpy as jnp
from jax import lax
from jax.experimental import pallas as pl
from jax.experimental.pallas import tpu as pltpu

F32 = jnp.float32
BF16 = jnp.bfloat16

CHUNK = 64
EPS = 1e-6
NEG = -1e30
CONV_K = 4
LOG2E = 1.4426950408889634
LANES = 128
HD_A = 128
LOOKBACK = 8
BAND = (LOOKBACK + 1) * CHUNK
REL_MAX = 256
REL_W = 640
ATTN_GROUP = 16
H_B = 4
MLSTM_L = 256
N_BLK_C = 8
C_RG = 8.0
RG_COLS = 512
HD_D = 128
GDN_GROUP = 4

VMEM_LIMIT = 56 * 1024 * 1024


def _params(n_axes):
    return pltpu.CompilerParams(dimension_semantics=("arbitrary",) * n_axes, vmem_limit_bytes=VMEM_LIMIT)


def _mm(a, b):
    return jnp.dot(a.astype(BF16), b.astype(BF16), preferred_element_type=F32)


def _mm_nt(a, b):
    return lax.dot_general(a.astype(BF16), b.astype(BF16), (((1,), (1,)), ((), ())), preferred_element_type=F32)


def _mm_tn(a, b):
    return lax.dot_general(a.astype(BF16), b.astype(BF16), (((0,), (0,)), ((), ())), preferred_element_type=F32)


def _mm_f32(a, b):
    return jnp.dot(a, b, preferred_element_type=F32, precision=lax.Precision.HIGHEST)


def _log1p(e):
    u = 1.0 + e
    return jnp.where(u == 1.0, e, jnp.log(u) * (e / (u - 1.0)))


def _softplus(x):
    return jnp.maximum(x, 0.0) + _log1p(jnp.exp(-jnp.abs(x)))


def _sigmoid(x):
    return 1.0 / (1.0 + jnp.exp2(x * -LOG2E))


def _silu(x):
    return x * _sigmoid(x)


def _iota2(shape, axis):
    return lax.broadcasted_iota(jnp.int32, shape, axis)


def _chunk_tril(n, length):
    sh = length.bit_length() - 1
    r, c = _iota2((n, n), 0), _iota2((n, n), 1)
    same = (r >> sh) == (c >> sh)
    return jnp.where(same & (c <= r), 1.0, 0.0).astype(F32)


def _triu(n):
    r, c = _iota2((n, n), 0), _iota2((n, n), 1)
    return jnp.where(r <= c, 1.0, 0.0).astype(F32)


def _inproj_body(x_ref, g_ref, w_ref, wg_ref, o_ref, og_ref, xn_ref):
    @pl.when(pl.program_id(1) == 0)
    def _():
        x = x_ref[...]
        ms = jnp.mean(x * x, axis=-1, keepdims=True)
        xn = ((x * lax.rsqrt(ms + EPS)) * g_ref[...]).astype(BF16)
        xn_ref[...] = xn
        og_ref[...] = jnp.dot(xn, wg_ref[...], preferred_element_type=F32)

    o_ref[...] = jnp.dot(xn_ref[...], w_ref[...], preferred_element_type=F32).astype(o_ref.dtype)


def _inproj(x2, g, w_main, w_gate, n, tm, tn):
    m, d = x2.shape
    return pl.pallas_call(
        _inproj_body,
        grid=(m // tm, n // tn),
        in_specs=[
            pl.BlockSpec((tm, d), lambda i, j: (i, 0)),
            pl.BlockSpec((1, d), lambda i, j: (0, 0)),
            pl.BlockSpec((d, tn), lambda i, j: (0, j)),
            pl.BlockSpec((d, LANES), lambda i, j: (0, 0)),
        ],
        out_specs=[
            pl.BlockSpec((tm, tn), lambda i, j: (i, j)),
            pl.BlockSpec((tm, LANES), lambda i, j: (i, 0)),
        ],
        out_shape=[jax.ShapeDtypeStruct((m, n), BF16), jax.ShapeDtypeStruct((m, LANES), F32)],
        scratch_shapes=[pltpu.VMEM((tm, d), BF16)],
        compiler_params=_params(2),
        name="inproj",
    )(x2, g, w_main, w_gate)


def _outproj_body(ya_ref, yb_ref, wa_ref, wb_ref, x_ref, o_ref, wa_s, wb_s):
    @pl.when(pl.program_id(1) == 0)
    def _():
        wa_s[...] = wa_ref[...].astype(BF16)
        wb_s[...] = wb_ref[...].astype(BF16)

    acc = jnp.dot(ya_ref[...], wa_s[...], preferred_element_type=F32)
    acc = acc + jnp.dot(yb_ref[...], wb_s[...], preferred_element_type=F32)
    o_ref[...] = x_ref[...] + acc


def _outproj(ya, yb, w, x2, tm, tn):
    m, d = x2.shape
    kh = ya.shape[1]
    return pl.pallas_call(
        _outproj_body,
        grid=(d // tn, m // tm),
        in_specs=[
            pl.BlockSpec((tm, kh), lambda j, i: (i, 0)),
            pl.BlockSpec((tm, kh), lambda j, i: (i, 0)),
            pl.BlockSpec((kh, tn), lambda j, i: (0, j)),
            pl.BlockSpec((kh, tn), lambda j, i: (1, j)),
            pl.BlockSpec((tm, tn), lambda j, i: (i, j)),
        ],
        out_specs=pl.BlockSpec((tm, tn), lambda j, i: (i, j)),
        out_shape=jax.ShapeDtypeStruct((m, d), F32),
        scratch_shapes=[pltpu.VMEM((kh, tn), BF16), pltpu.VMEM((kh, tn), BF16)],
        compiler_params=_params(2),
        name="outproj",
    )(ya, yb, w, w, x2)


def _attn_body(q_ref, k_ref, v_ref, z_ref, qg_ref, kg_ref, c_ref, o_ref, qs, ks, vs, bias_s):
    s = q_ref.shape[0]
    pad = LOOKBACK * CHUNK
    q = q_ref[...].astype(F32)
    q = q * lax.rsqrt(jnp.mean(q * q, axis=-1, keepdims=True) + EPS) * qg_ref[...] * (HD_A ** -0.5 * LOG2E)
    qs[...] = q.astype(BF16)
    k = k_ref[...].astype(F32)
    k = k * lax.rsqrt(jnp.mean(k * k, axis=-1, keepdims=True) + EPS) * kg_ref[...]
    ks[0:pad, :] = jnp.zeros((pad, HD_A), BF16)
    ks[pad:pad + s, :] = k.astype(BF16)
    vs[0:pad, :] = jnp.zeros((pad, HD_A), BF16)
    vs[pad:pad + s, :] = v_ref[...]
    cb = jnp.broadcast_to(c_ref[...] * LOG2E, (CHUNK, REL_W))
    bias_s[...] = pltpu.roll(cb, 0, 1, stride=1, stride_axis=0)[:, :BAND]
    col = _iota2((CHUNK, BAND), 1)

    def chunks(it, masked):
        gs = range(ATTN_GROUP)
        n = [it * ATTN_GROUP + g for g in gs]
        r0 = [pl.multiple_of(n[g] * CHUNK, CHUNK) for g in gs]
        sc = [_mm_nt(qs[pl.ds(r0[g], CHUNK), :], ks[pl.ds(r0[g], BAND), :]) + bias_s[...] for g in gs]
        if masked:
            sc = [jnp.where(col >= (LOOKBACK - n[g]) * CHUNK, sc[g], NEG) for g in gs]
        p = [jnp.exp2(sc[g] - jnp.max(sc[g], axis=-1, keepdims=True)) for g in gs]
        pv = [_mm(p[g], vs[pl.ds(r0[g], BAND), :]) for g in gs]
        for g in gs:
            o = pv[g] * (1.0 / jnp.sum(p[g], axis=-1, keepdims=True))
            z = z_ref[pl.ds(r0[g], CHUNK), :].astype(F32)
            o_ref[pl.ds(r0[g], CHUNK), :] = (o * _silu(z)).astype(o_ref.dtype)

    assert ATTN_GROUP >= LOOKBACK
    chunks(0, True)

    def rest(it, carry):
        chunks(it, False)
        return carry

    lax.fori_loop(1, s // (CHUNK * ATTN_GROUP), rest, 0)


def _attn(p3, qg, kg, crel):
    b, s, _ = p3.shape
    h_a = crel.shape[0]

    def col_spec(sec):
        return pl.BlockSpec((None, s, HD_A), lambda bi, h, sec=sec: (bi, 0, sec * h_a + h))

    return pl.pallas_call(
        _attn_body,
        grid=(b, h_a),
        in_specs=[
            col_spec(0), col_spec(1), col_spec(2), col_spec(3),
            pl.BlockSpec((1, HD_A), lambda bi, h: (0, 0)),
            pl.BlockSpec((1, HD_A), lambda bi, h: (0, 0)),
            pl.BlockSpec((None, 1, REL_W), lambda bi, h: (h, 0, 0)),
        ],
        out_specs=pl.BlockSpec((None, s, HD_A), lambda bi, h: (bi, 0, h)),
        out_shape=jax.ShapeDtypeStruct((b, s, h_a * HD_A), BF16),
        scratch_shapes=[
            pltpu.VMEM((s, HD_A), BF16),
            pltpu.VMEM((s + LOOKBACK * CHUNK, HD_A), BF16),
            pltpu.VMEM((s + LOOKBACK * CHUNK, HD_A), BF16),
            pltpu.VMEM((CHUNK, BAND), F32),
        ],
        compiler_params=_params(2),
        name="chunk_attn",
    )(p3, p3, p3, p3, qg, kg, crel)


def _mlstm_body(q_ref, k_ref, v_ref, o_ref, z_ref, g_ref, gt_ref, bc_ref, br_ref, y_ref,
                c_st, n_st, m_st, col_b, col_i, row_b, row_i):
    ts = q_ref.shape[0]
    ln = MLSTM_L
    nct = ts // ln
    hd = q_ref.shape[1] // H_B
    scale = hd ** -0.5
    j = pl.program_id(1)
    hs = range(H_B)
    hc = [slice(h * hd, (h + 1) * hd) for h in hs]

    @pl.when(j == 0)
    def _():
        c_st[...] = jnp.zeros_like(c_st)
        n_st[...] = jnp.zeros_like(n_st)
        m_st[...] = jnp.zeros_like(m_st)
        gt = gt_ref[...] + br_ref[...]
        for h in hs:
            row_b[h] = _mm_f32(-_softplus(-gt[H_B + h]), _triu(ln)) * LOG2E
            row_i[h] = gt[h] * LOG2E

    g = g_ref[...] + bc_ref[...]
    col_i[...] = g * LOG2E
    col_b[...] = _mm_f32(_chunk_tril(ts, ln), -_softplus(-g)) * LOG2E
    tril = _iota2((ln, ln), 1) <= _iota2((ln, ln), 0)
    log2_scale = float(np.log2(scale))
    m = [m_st[h][0:1, 0:1] for h in hs]
    cm = [c_st[h] for h in hs]
    nm = [n_st[h][0:1, :] for h in hs]
    for c in range(nct):
        rows = slice(c * ln, (c + 1) * ln)
        ci = j * nct + c
        bc_all = col_b[rows, :]
        li_all = col_i[rows, :]
        bc = [bc_all[:, H_B + h:H_B + h + 1] for h in hs]
        lic = [li_all[:, h:h + 1] for h in hs]
        rl = [row_i[h, pl.ds(ci, 1), :] - row_b[h, pl.ds(ci, 1), :] for h in hs]
        qn = [q_ref[rows, hc[h]] for h in hs]
        kn = [k_ref[rows, hc[h]] for h in hs]
        vn = [v_ref[rows, hc[h]] for h in hs]
        qk = [_mm_nt(qn[h], kn[h]) for h in hs]
        qc = [_mm(qn[h], cm[h]) for h in hs]
        dmat = [jnp.where(tril, bc[h] + rl[h], NEG) for h in hs]
        mt = [jnp.maximum(bc[h] + m[h], jnp.max(dmat[h], axis=-1, keepdims=True)) for h in hs]
        cmt = [mt[h] - log2_scale for h in hs]
        w_inter = [jnp.exp2(bc[h] + m[h] - cmt[h]) for h in hs]
        p = [jnp.exp2(dmat[h] - cmt[h]) * qk[h] for h in hs]
        pv = [_mm(p[h], vn[h]) for h in hs]
        b_last = [bc[h][ln - 1:ln, :] for h in hs]
        m_new = [jnp.maximum(b_last[h] + m[h], jnp.max(b_last[h] + rl[h], axis=-1, keepdims=True)) for h in hs]
        wk = [jnp.exp2(b_last[h] - bc[h] + lic[h] - m_new[h]) * kn[h].astype(F32) for h in hs]
        upd = [_mm_tn(wk[h], vn[h]) for h in hs]
        for h in hs:
            num = w_inter[h] * qc[h] + pv[h]
            qn_dot = jnp.sum(qn[h].astype(F32) * nm[h], axis=-1, keepdims=True)
            den = w_inter[h] * qn_dot + jnp.sum(p[h], axis=-1, keepdims=True)
            hout = num * (1.0 / jnp.maximum(jnp.abs(den), jnp.exp2(-mt[h])))
            og = o_ref[rows, hc[h]].astype(F32)
            zg = z_ref[rows, hc[h]].astype(F32)
            gate = zg * (1.0 / ((1.0 + jnp.exp2(og * -LOG2E)) * (1.0 + jnp.exp2(zg * -LOG2E))))
            y_ref[rows, hc[h]] = (hout * gate).astype(y_ref.dtype)
            w_c = jnp.exp2(b_last[h] + m[h] - m_new[h])
            cm[h] = w_c * cm[h] + upd[h]
            nm[h] = w_c * nm[h] + jnp.sum(wk[h], axis=0, keepdims=True)
            m[h] = m_new[h]
    for h in hs:
        c_st[h] = cm[h]
        n_st[h] = jnp.broadcast_to(nm[h], n_st.shape[1:])
        m_st[h] = jnp.broadcast_to(m[h], m_st.shape[1:])


def _mlstm(p3, g3, gt4, bias_col, bias_row, ts):
    b, s, _ = p3.shape
    w_b = 1024
    hd = w_b // H_B
    ncs = s // MLSTM_L

    def col_spec(sec):
        return pl.BlockSpec((None, ts, w_b), lambda bi, j, sec=sec: (bi, j, 4 + sec))

    return pl.pallas_call(
        _mlstm_body,
        grid=(b, s // ts),
        in_specs=[
            col_spec(0), col_spec(1), col_spec(2), col_spec(3), col_spec(4),
            pl.BlockSpec((None, ts, LANES), lambda bi, j: (bi, j, 0)),
            pl.BlockSpec((None, 2 * H_B, ncs, MLSTM_L), lambda bi, j: (bi, 0, 0, 0)),
            pl.BlockSpec((1, LANES), lambda bi, j: (0, 0)),
            pl.BlockSpec((2 * H_B, 1, 1), lambda bi, j: (0, 0, 0)),
        ],
        out_specs=pl.BlockSpec((None, ts, w_b), lambda bi, j: (bi, j, 0)),
        out_shape=jax.ShapeDtypeStruct((b, s, w_b), BF16),
        scratch_shapes=[
            pltpu.VMEM((H_B, hd, hd), F32),
            pltpu.VMEM((H_B, 8, hd), F32),
            pltpu.VMEM((H_B, 8, LANES), F32),
            pltpu.VMEM((ts, LANES), F32),
            pltpu.VMEM((ts, LANES), F32),
            pltpu.VMEM((H_B, ncs, MLSTM_L), F32),
            pltpu.VMEM((H_B, ncs, MLSTM_L), F32),
        ],
        compiler_params=_params(2),
        name="mlstm",
    )(p3, p3, p3, p3, p3, g3, gt4, bias_col, bias_row)


def _rglru_body(x_ref, z_ref, cw_ref, cb_ref, gw_ref, gbr_ref, gbi_ref, lam_ref, y_ref, a_s, b_s):
    s, wc = x_ref.shape
    blk = gw_ref.shape[1]
    row8 = _iota2((8, blk), 0)
    sub = _iota2((s // 8, 8, blk), 1)
    for n in range(wc // blk):
        cs = slice(n * blk, (n + 1) * blk)
        x = x_ref[:, cs].astype(F32)
        cw = cw_ref[:, cs]
        xc = cw[CONV_K - 1:CONV_K, :] * x + cb_ref[:, cs]
        for d in range(1, CONV_K):
            xs = pltpu.roll(x, d, 0)
            xs = jnp.concatenate([jnp.where(row8 >= d, xs[0:8], 0.0), xs[8:]], axis=0)
            xc = xc + cw[CONV_K - 1 - d:CONV_K - d, :] * xs
        gates = _mm(xc, gw_ref[n])
        r = _sigmoid(gates[:, :blk] + gbr_ref[:, cs])
        i = _sigmoid(gates[:, blk:] + gbi_ref[:, cs])
        log_a = -C_RG * r * _softplus(-lam_ref[:, cs])
        a = jnp.exp(log_a)
        var = -jnp.tanh(log_a) * (a * a + 1.0)
        bb = jnp.where(var > 0.0, var * lax.rsqrt(var), 0.0) * (i * xc)
        a = a.reshape(s // 8, 8, blk)
        bb = bb.reshape(s // 8, 8, blk)
        for d in (1, 2, 4):
            keep = sub >= d
            a_sh = jnp.where(keep, pltpu.roll(a, d, 1), 1.0)
            b_sh = jnp.where(keep, pltpu.roll(bb, d, 1), 0.0)
            bb = a * b_sh + bb
            a = a * a_sh
        a_s[:, cs] = a.reshape(s, blk)
        b_s[:, cs] = bb.reshape(s, blk)

    def group(gi, carry):
        rows = pl.ds(pl.multiple_of(gi * 8, 8), 8)
        h = a_s[rows, :] * carry + b_s[rows, :]
        b_s[rows, :] = h
        return jnp.broadcast_to(h[7:8, :], h.shape)

    lax.fori_loop(0, s // 8, group, jnp.zeros((8, wc), F32), unroll=8)
    z = z_ref[...].astype(F32)
    y_ref[...] = (b_s[...] * _silu(z)).astype(y_ref.dtype)


def _rglru(p3, cw, cb, gw, gb, lam):
    b, s, _ = p3.shape
    w_c = cw.shape[1]
    blk = w_c // N_BLK_C
    nsp = w_c // RG_COLS
    return pl.pallas_call(
        _rglru_body,
        grid=(b, nsp),
        in_specs=[
            pl.BlockSpec((None, s, RG_COLS), lambda bi, n: (bi, 0, n)),
            pl.BlockSpec((None, s, RG_COLS), lambda bi, n: (bi, 0, nsp + n)),
            pl.BlockSpec((CONV_K, RG_COLS), lambda bi, n: (0, n)),
            pl.BlockSpec((1, RG_COLS), lambda bi, n: (0, n)),
            pl.BlockSpec((RG_COLS // blk, blk, 2 * blk), lambda bi, n: (n, 0, 0)),
            pl.BlockSpec((1, RG_COLS), lambda bi, n: (0, n)),
            pl.BlockSpec((1, RG_COLS), lambda bi, n: (0, nsp + n)),
            pl.BlockSpec((1, RG_COLS), lambda bi, n: (0, n)),
        ],
        out_specs=pl.BlockSpec((None, s, RG_COLS), lambda bi, n: (bi, 0, n)),
        out_shape=jax.ShapeDtypeStruct((b, s, w_c), BF16),
        scratch_shapes=[pltpu.VMEM((s, RG_COLS), F32), pltpu.VMEM((s, RG_COLS), F32)],
        compiler_params=_params(2),
        name="rglru",
    )(p3, p3, cw, cb, gw, gb, gb, lam)


def _unit_lower_inverse(a_list):
    r, c = _iota2((CHUNK, CHUNK), 0), _iota2((CHUNK, CHUNK), 1)
    eye = jnp.where(r == c, 1.0, 0.0).astype(F32)

    def pair_mask(sh):
        rb, cb_ = r >> sh, c >> sh
        return ((rb & 1) == 1) & (cb_ == rb - 1)

    ts = [eye - jnp.where(pair_mask(0), a, 0.0) for a in a_list]
    for sh in range(1, 6):
        mask = pair_mask(sh)
        x1 = [_mm(t, jnp.where(mask, a, 0.0)) for t, a in zip(ts, a_list)]
        x2 = [_mm(x, t) for x, t in zip(x1, ts)]
        ts = [t - x for t, x in zip(ts, x2)]
    return ts


def _gdn_body(q_ref, k_ref, v_ref, z_ref, g_ref, gt_ref, cw_ref, al_c_ref, dt_c_ref, al_r_ref, dt_r_ref, on_ref,
              y_ref, s_st, xe, qkv, col_d, col_b, row_d, lhs_s, o0_s, n_s):
    ts = q_ref.shape[0]
    nct = ts // CHUNK
    w_d = q_ref.shape[1]
    h_d = w_d // HD_D
    scale = HD_D ** -0.5

    @pl.when(pl.program_id(1) == 0)
    def _():
        s_st[...] = jnp.zeros_like(s_st)
        xe[0:8, :] = jnp.zeros((8, 3 * w_d), F32)

    xe[8:8 + ts, 0:w_d] = q_ref[...].astype(F32)
    xe[8:8 + ts, w_d:2 * w_d] = k_ref[...].astype(F32)
    xe[8:8 + ts, 2 * w_d:3 * w_d] = v_ref[...].astype(F32)
    def conv_rows(rb, carry):
        r0 = pl.multiple_of(rb * CHUNK, CHUNK)
        for cbk in range(3 * h_d):
            cs = slice(cbk * HD_D, (cbk + 1) * HD_D)
            xw = xe[pl.ds(r0, CHUNK + 8), cs]
            acc = cw_ref[CONV_K - 1:CONV_K, cs] * xw[8:]
            for d in range(1, CONV_K):
                acc = acc + cw_ref[CONV_K - 1 - d:CONV_K - d, cs] * xw[8 - d:8 - d + CHUNK]
            acc = _silu(acc)
            if cbk < 2 * h_d:
                acc = acc * lax.rsqrt(jnp.sum(acc * acc, axis=-1, keepdims=True) + EPS)
            qkv[pl.ds(r0, CHUNK), cs] = acc
        return carry

    lax.fori_loop(0, nct, conv_rows, 0)
    xe[0:8, :] = xe[ts:ts + 8, :]

    g = g_ref[...]
    col_b[...] = _sigmoid(g)
    col_d[...] = _mm_f32(_chunk_tril(ts, CHUNK), -jnp.exp(al_c_ref[...]) * _softplus(g + dt_c_ref[...]))
    gt = gt_ref[...]
    g_r = -jnp.exp(al_r_ref[...]) * _softplus(gt[0:h_d] + dt_r_ref[...])
    row_d[...] = _mm_f32(g_r.reshape(h_d * nct, CHUNK), _triu(CHUNK)).reshape(h_d, nct, CHUNK)

    r_i, c_i = _iota2((CHUNK, CHUNK), 0), _iota2((CHUNK, CHUNK), 1)
    tril = c_i <= r_i
    strict = c_i < r_i

    hs = range(h_d)

    def group(gi, carry):
        probs = [(cc, h) for cc in range(GDN_GROUP) for h in hs]
        cidx = [gi * GDN_GROUP + cc for cc in range(GDN_GROUP)]
        rows = [pl.ds(pl.multiple_of(ci * CHUNK, CHUNK), CHUNK) for ci in cidx]
        dc_all = [col_d[r, :] for r in rows]
        beta_all = [col_b[r, :] for r in rows]
        dc = [dc_all[cc][:, h:h + 1] for cc, h in probs]
        beta = [beta_all[cc][:, h_d + h:h_d + h + 1] for cc, h in probs]
        q = [qkv[rows[cc], h * HD_D:(h + 1) * HD_D] for cc, h in probs]
        k = [qkv[rows[cc], w_d + h * HD_D:w_d + (h + 1) * HD_D] for cc, h in probs]
        v = [qkv[rows[cc], 2 * w_d + h * HD_D:2 * w_d + (h + 1) * HD_D] for cc, h in probs]
        ps = range(len(probs))
        gam = [jnp.exp(jnp.where(tril, dc[i] - row_d[probs[i][1], pl.ds(cidx[probs[i][0]], 1), :], NEG)) for i in ps]
        kb = [k[i] * beta[i] for i in ps]
        kk = [_mm_nt(kb[i], k[i]) for i in ps]
        t_inv = _unit_lower_inverse([jnp.where(strict, kk[i] * gam[i], 0.0) for i in ps])
        ed = [jnp.exp(dc[i]) for i in ps]
        uw = [_mm(t_inv[i], jnp.concatenate([v[i] * beta[i], kb[i] * ed[i]], axis=1)) for i in ps]
        qs = [q[i] * scale for i in ps]
        qk = [_mm_nt(qs[i], k[i]) * gam[i] for i in ps]
        kd = [k[i] * jnp.exp(dc[i][CHUNK - 1:CHUNK, :] - dc[i]) for i in ps]
        x1 = [_mm(qk[i], uw[i]) for i in ps]
        x2 = [_mm_tn(kd[i], uw[i]) for i in ps]
        for i, (cc, h) in enumerate(probs):
            lhs_s[cc, h] = jnp.concatenate([qs[i] * ed[i] - x1[i][:, HD_D:], x2[i][:, HD_D:]], axis=0).astype(BF16)
            o0_s[cc, h] = x1[i][:, :HD_D]
            n_s[cc, h] = x2[i][:, :HD_D]

        def step(cc, carry2):
            c = gi * GDN_GROUP + cc
            rws = pl.ds(pl.multiple_of(c * CHUNK, CHUNK), CHUNK)
            d_last = jnp.exp(col_d[pl.ds(c * CHUNK + CHUNK - 1, 1), :])
            st = [s_st[h] for h in hs]
            r = [_mm(lhs_s[cc, h], st[h]) for h in hs]
            for h in hs:
                o = o0_s[cc, h] + r[h][:CHUNK]
                s_st[h] = d_last[:, h:h + 1] * st[h] + n_s[cc, h] - r[h][CHUNK:]
                on = o * lax.rsqrt(jnp.mean(o * o, axis=-1, keepdims=True) + EPS) * on_ref[...]
                z = z_ref[rws, h * HD_D:(h + 1) * HD_D].astype(F32)
                y_ref[rws, h * HD_D:(h + 1) * HD_D] = (on * _silu(z)).astype(y_ref.dtype)
            return carry2

        lax.fori_loop(0, GDN_GROUP, step, 0)
        return carry

    lax.fori_loop(0, nct // GDN_GROUP, group, 0)


def _gdn(p3, g3, gt4, cw, al_c, dt_c, al_r, dt_r, onorm, ts):
    b, s, _ = p3.shape
    w_d = 1024
    h_d = w_d // HD_D
    nct = ts // CHUNK

    def col_spec(sec):
        return pl.BlockSpec((None, ts, w_d), lambda bi, j, sec=sec: (bi, j, 2 + sec))

    return pl.pallas_call(
        _gdn_body,
        grid=(b, s // ts),
        in_specs=[
            col_spec(0), col_spec(1), col_spec(2), col_spec(3),
            pl.BlockSpec((None, ts, LANES), lambda bi, j: (bi, j, 0)),
            pl.BlockSpec((None, 2 * h_d, nct, CHUNK), lambda bi, j: (bi, 0, j, 0)),
            pl.BlockSpec((CONV_K, 3 * w_d), lambda bi, j: (0, 0)),
            pl.BlockSpec((1, LANES), lambda bi, j: (0, 0)),
            pl.BlockSpec((1, LANES), lambda bi, j: (0, 0)),
            pl.BlockSpec((h_d, 1, 1), lambda bi, j: (0, 0, 0)),
            pl.BlockSpec((h_d, 1, 1), lambda bi, j: (0, 0, 0)),
            pl.BlockSpec((1, HD_D), lambda bi, j: (0, 0)),
        ],
        out_specs=pl.BlockSpec((None, ts, w_d), lambda bi, j: (bi, j, 0)),
        out_shape=jax.ShapeDtypeStruct((b, s, w_d), BF16),
        scratch_shapes=[
            pltpu.VMEM((h_d, HD_D, HD_D), F32),
            pltpu.VMEM((ts + 8, 3 * w_d), F32),
            pltpu.VMEM((ts, 3 * w_d), F32),
            pltpu.VMEM((ts, LANES), F32),
            pltpu.VMEM((ts, LANES), F32),
            pltpu.VMEM((h_d, nct, CHUNK), F32),
            pltpu.VMEM((GDN_GROUP, h_d, CHUNK + HD_D, HD_D), BF16),
            pltpu.VMEM((GDN_GROUP, h_d, CHUNK, HD_D), F32),
            pltpu.VMEM((GDN_GROUP, h_d, HD_D, HD_D), F32),
        ],
        compiler_params=_params(2),
        name="gated_delta",
    )(p3, p3, p3, p3, g3, gt4, cw, al_c, dt_c, al_r, dt_r, onorm)


def _rel_row(rel_bias):
    t = (np.arange(REL_W) + CHUNK - 1) % REL_W
    idx = np.clip(LOOKBACK * CHUNK + CHUNK - 1 - t, -REL_MAX, REL_MAX) + REL_MAX
    return rel_bias[:, idx].astype(F32)[:, None, :]


def _pad_lanes(v, n=LANES):
    return jnp.pad(v, ((0, 0), (0, n - v.shape[1])))


def _gates_t(g3, n, length):
    b, s, _ = g3.shape
    return jnp.transpose(g3[:, :, :n], (0, 2, 1)).reshape(b, n, s // length, length)


def kernel(x, ev_norm, ev_w_in, ev_if_bias, ev_qn_gain, ev_kn_gain, ev_rel_bias, ev_w_out, od_norm, od_w_in, od_conv_c_w, od_conv_c_b, od_gate_w, od_gate_b, od_lambda, od_conv_d_w, od_a_log, od_dt_bias, od_onorm, od_w_out):
    b, s, d = x.shape
    m = b * s
    half = d // 2
    tm = min(1024, m)
    ts = min(512, s)
    x2 = x.reshape(m, d)
    depth = ev_norm.shape[0] + od_norm.shape[0]
    for layer in range(depth):
        j = layer // 2
        if layer % 2 == 0:
            n_main = 9 * half
            w_in = ev_w_in[j].astype(BF16)
            p, g = _inproj(x2, ev_norm[j].reshape(1, d), w_in, _pad_lanes(w_in[:, n_main:]), n_main, tm, 1024)
            p3, g3 = p.reshape(b, s, n_main), g.reshape(b, s, LANES)
            ya = _attn(p3, ev_qn_gain[j].reshape(1, HD_A), ev_kn_gain[j].reshape(1, HD_A), _rel_row(ev_rel_bias[j]))
            bias = ev_if_bias[j].astype(F32)
            yb = _mlstm(p3, g3, _gates_t(g3, 2 * H_B, MLSTM_L), _pad_lanes(bias.reshape(1, -1)), bias.reshape(-1, 1, 1), ts)
            x2 = _outproj(ya.reshape(m, half), yb.reshape(m, half), ev_w_out[j], x2, tm, 1024)
        else:
            n_main = 6 * half
            h_d = half // HD_D
            w_in = od_w_in[j].astype(BF16)
            p, g = _inproj(x2, od_norm[j].reshape(1, d), w_in, _pad_lanes(w_in[:, n_main:]), n_main, tm, 1024)
            p3, g3 = p.reshape(b, s, n_main), g.reshape(b, s, LANES)
            yc = _rglru(p3, od_conv_c_w[j], od_conv_c_b[j].reshape(1, -1), od_gate_w[j].astype(BF16),
                        od_gate_b[j].reshape(1, -1), od_lambda[j].reshape(1, -1))
            al, dt = od_a_log[j].astype(F32), od_dt_bias[j].astype(F32)
            yd = _gdn(p3, g3, _gates_t(g3, 2 * h_d, CHUNK), od_conv_d_w[j], _pad_lanes(al.reshape(1, -1)),
                      _pad_lanes(dt.reshape(1, -1)), al.reshape(-1, 1, 1), dt.reshape(-1, 1, 1),
                      od_onorm[j].reshape(1, HD_D), ts)
            x2 = _outproj(yc.reshape(m, half), yd.reshape(m, half), od_w_out[j], x2, tm, 1024)
    return x2.reshape(b, s, d)
```

```python
import functools

import numpy as np
import jax
import jax.numpy as jnp
from jax import lax
from jax.experimental import pallas as pl
from jax.experimental.pallas import tpu as pltpu

F32 = jnp.float32
BF16 = jnp.bfloat16

CHUNK = 64
EPS = 1e-6
NEG = -1e30
CONV_K = 4
LOG2E = 1.4426950408889634
LANES = 128
HD_A = 128
LOOKBACK = 8
BAND = (LOOKBACK + 1) * CHUNK
REL_MAX = 256
REL_W = 640
ATTN_GROUP = 16
H_B = 4
MLSTM_L = 256
N_BLK_C = 8
C_RG = 8.0
RG_COLS = 512
HD_D = 128
GDN_GROUP = 4

VMEM_LIMIT = 56 * 1024 * 1024


def _params(n_axes):
    return pltpu.CompilerParams(dimension_semantics=("arbitrary",) * n_axes, vmem_limit_bytes=VMEM_LIMIT)


def _mm(a, b):
    return jnp.dot(a.astype(BF16), b.astype(BF16), preferred_element_type=F32)


def _mm_nt(a, b):
    return lax.dot_general(a.astype(BF16), b.astype(BF16), (((1,), (1,)), ((), ())), preferred_element_type=F32)


def _mm_tn(a, b):
    return lax.dot_general(a.astype(BF16), b.astype(BF16), (((0,), (0,)), ((), ())), preferred_element_type=F32)


def _mm_f32(a, b):
    return jnp.dot(a, b, preferred_element_type=F32, precision=lax.Precision.HIGHEST)


def _log1p(e):
    u = 1.0 + e
    return jnp.where(u == 1.0, e, jnp.log(u) * (e / (u - 1.0)))


def _softplus(x):
    return jnp.maximum(x, 0.0) + _log1p(jnp.exp(-jnp.abs(x)))


def _sigmoid(x):
    return 1.0 / (1.0 + jnp.exp2(x * -LOG2E))


def _silu(x):
    return x * _sigmoid(x)


def _iota2(shape, axis):
    return lax.broadcasted_iota(jnp.int32, shape, axis)


def _chunk_tril(n, length):
    sh = length.bit_length() - 1
    r, c = _iota2((n, n), 0), _iota2((n, n), 1)
    same = (r >> sh) == (c >> sh)
    return jnp.where(same & (c <= r), 1.0, 0.0).astype(F32)


def _triu(n):
    r, c = _iota2((n, n), 0), _iota2((n, n), 1)
    return jnp.where(r <= c, 1.0, 0.0).astype(F32)


def _inproj_body(x_ref, g_ref, w_ref, wg_ref, o_ref, og_ref, xn_ref):
    @pl.when(pl.program_id(1) == 0)
    def _():
        x = x_ref[...]
        ms = jnp.mean(x * x, axis=-1, keepdims=True)
        xn = ((x * lax.rsqrt(ms + EPS)) * g_ref[...]).astype(BF16)
        xn_ref[...] = xn
        og_ref[...] = jnp.dot(xn, wg_ref[...], preferred_element_type=F32)

    o_ref[...] = jnp.dot(xn_ref[...], w_ref[...].astype(BF16), preferred_element_type=F32).astype(o_ref.dtype)


def _inproj(x2, g, w_main, w_gate, n, tm, tn):
    m, d = x2.shape
    return pl.pallas_call(
        _inproj_body,
        grid=(m // tm, n // tn),
        in_specs=[
            pl.BlockSpec((tm, d), lambda i, j: (i, 0)),
            pl.BlockSpec((1, d), lambda i, j: (0, 0)),
            pl.BlockSpec((d, tn), lambda i, j: (0, j)),
            pl.BlockSpec((d, LANES), lambda i, j: (0, 0)),
        ],
        out_specs=[
            pl.BlockSpec((tm, tn), lambda i, j: (i, j)),
            pl.BlockSpec((tm, LANES), lambda i, j: (i, 0)),
        ],
        out_shape=[jax.ShapeDtypeStruct((m, n), BF16), jax.ShapeDtypeStruct((m, LANES), F32)],
        scratch_shapes=[pltpu.VMEM((tm, d), BF16)],
        compiler_params=_params(2),
        name="inproj",
    )(x2, g, w_main, w_gate)


def _outproj_body(ya_ref, yb_ref, wa_ref, wb_ref, x_ref, o_ref, wa_s, wb_s):
    @pl.when(pl.program_id(1) == 0)
    def _():
        wa_s[...] = wa_ref[...].astype(BF16)
        wb_s[...] = wb_ref[...].astype(BF16)

    acc = jnp.dot(ya_ref[...], wa_s[...], preferred_element_type=F32)
    acc = acc + jnp.dot(yb_ref[...], wb_s[...], preferred_element_type=F32)
    o_ref[...] = x_ref[...] + acc


def _outproj(ya, yb, w, x2, tm, tn):
    m, d = x2.shape
    kh = ya.shape[1]
    return pl.pallas_call(
        _outproj_body,
        grid=(d // tn, m // tm),
        in_specs=[
            pl.BlockSpec((tm, kh), lambda j, i: (i, 0)),
            pl.BlockSpec((tm, kh), lambda j, i: (i, 0)),
            pl.BlockSpec((kh, tn), lambda j, i: (0, j)),
            pl.BlockSpec((kh, tn), lambda j, i: (1, j)),
            pl.BlockSpec((tm, tn), lambda j, i: (i, j)),
        ],
        out_specs=pl.BlockSpec((tm, tn), lambda j, i: (i, j)),
        out_shape=jax.ShapeDtypeStruct((m, d), F32),
        scratch_shapes=[pltpu.VMEM((kh, tn), BF16), pltpu.VMEM((kh, tn), BF16)],
        compiler_params=_params(2),
        name="outproj",
    )(ya, yb, w, w, x2)


def _attn_body(q_ref, k_ref, v_ref, z_ref, qg_ref, kg_ref, c_ref, o_ref, qs, ks, vs, bias_s):
    s = q_ref.shape[0]
    pad = LOOKBACK * CHUNK
    q = q_ref[...].astype(F32)
    q = q * lax.rsqrt(jnp.mean(q * q, axis=-1, keepdims=True) + EPS) * qg_ref[...] * (HD_A ** -0.5 * LOG2E)
    qs[...] = q.astype(BF16)
    k = k_ref[...].astype(F32)
    k = k * lax.rsqrt(jnp.mean(k * k, axis=-1, keepdims=True) + EPS) * kg_ref[...]
    ks[0:pad, :] = jnp.zeros((pad, HD_A), BF16)
    ks[pad:pad + s, :] = k.astype(BF16)
    vs[0:pad, :] = jnp.zeros((pad, HD_A), BF16)
    vs[pad:pad + s, :] = v_ref[...]
    cb = jnp.broadcast_to(c_ref[...] * LOG2E, (CHUNK, REL_W))
    bias_s[...] = pltpu.roll(cb, 0, 1, stride=1, stride_axis=0)[:, :BAND]
    col = _iota2((CHUNK, BAND), 1)

    def chunks(it, masked):
        gs = range(ATTN_GROUP)
        n = [it * ATTN_GROUP + g for g in gs]
        r0 = [pl.multiple_of(n[g] * CHUNK, CHUNK) for g in gs]
        sc = [_mm_nt(qs[pl.ds(r0[g], CHUNK), :], ks[pl.ds(r0[g], BAND), :]) + bias_s[...] for g in gs]
        if masked:
            sc = [jnp.where(col >= (LOOKBACK - n[g]) * CHUNK, sc[g], NEG) for g in gs]
        p = [jnp.exp2(sc[g] - jnp.max(sc[g], axis=-1, keepdims=True)) for g in gs]
        pv = [_mm(p[g], vs[pl.ds(r0[g], BAND), :]) for g in gs]
        for g in gs:
            o = pv[g] * (1.0 / jnp.sum(p[g], axis=-1, keepdims=True))
            z = z_ref[pl.ds(r0[g], CHUNK), :].astype(F32)
            o_ref[pl.ds(r0[g], CHUNK), :] = (o * _silu(z)).astype(o_ref.dtype)

    assert ATTN_GROUP >= LOOKBACK
    chunks(0, True)

    def rest(it, carry):
        chunks(it, False)
        return carry

    lax.fori_loop(1, s // (CHUNK * ATTN_GROUP), rest, 0)


def _attn(p3, qg, kg, crel):
    b, s, _ = p3.shape
    h_a = crel.shape[0]

    def col_spec(sec):
        return pl.BlockSpec((None, s, HD_A), lambda bi, h, sec=sec: (bi, 0, sec * h_a + h))

    return pl.pallas_call(
        _attn_body,
        grid=(b, h_a),
        in_specs=[
            col_spec(0), col_spec(1), col_spec(2), col_spec(3),
            pl.BlockSpec((1, HD_A), lambda bi, h: (0, 0)),
            pl.BlockSpec((1, HD_A), lambda bi, h: (0, 0)),
            pl.BlockSpec((None, 1, REL_W), lambda bi, h: (h, 0, 0)),
        ],
        out_specs=pl.BlockSpec((None, s, HD_A), lambda bi, h: (bi, 0, h)),
        out_shape=jax.ShapeDtypeStruct((b, s, h_a * HD_A), BF16),
        scratch_shapes=[
            pltpu.VMEM((s, HD_A), BF16),
            pltpu.VMEM((s + LOOKBACK * CHUNK, HD_A), BF16),
            pltpu.VMEM((s + LOOKBACK * CHUNK, HD_A), BF16),
            pltpu.VMEM((CHUNK, BAND), F32),
        ],
        compiler_params=_params(2),
        name="chunk_attn",
    )(p3, p3, p3, p3, qg, kg, crel)


def _mlstm_body(q_ref, k_ref, v_ref, o_ref, z_ref, g_ref, gt_ref, bc_ref, br_ref, y_ref,
                c_st, n_st, m_st, col_b, col_i, row_b, row_i):
    ts = q_ref.shape[0]
    ln = MLSTM_L
    nct = ts // ln
    hd = q_ref.shape[1] // H_B
    scale = hd ** -0.5
    j = pl.program_id(1)
    hs = range(H_B)
    hc = [slice(h * hd, (h + 1) * hd) for h in hs]

    @pl.when(j == 0)
    def _():
        c_st[...] = jnp.zeros_like(c_st)
        n_st[...] = jnp.zeros_like(n_st)
        m_st[...] = jnp.zeros_like(m_st)
        gt = gt_ref[...] + br_ref[...]
        for h in hs:
            row_b[h] = _mm_f32(-_softplus(-gt[H_B + h]), _triu(ln)) * LOG2E
            row_i[h] = gt[h] * LOG2E

    g = g_ref[...] + bc_ref[...]
    col_i[...] = g * LOG2E
    col_b[...] = _mm_f32(_chunk_tril(ts, ln), -_softplus(-g)) * LOG2E
    tril = _iota2((ln, ln), 1) <= _iota2((ln, ln), 0)
    log2_scale = float(np.log2(scale))
    m = [m_st[h][0:1, 0:1] for h in hs]
    cm = [c_st[h] for h in hs]
    nm = [n_st[h][0:1, :] for h in hs]
    for c in range(nct):
        rows = slice(c * ln, (c + 1) * ln)
        ci = j * nct + c
        bc_all = col_b[rows, :]
        li_all = col_i[rows, :]
        bc = [bc_all[:, H_B + h:H_B + h + 1] for h in hs]
        lic = [li_all[:, h:h + 1] for h in hs]
        rl = [row_i[h, pl.ds(ci, 1), :] - row_b[h, pl.ds(ci, 1), :] for h in hs]
        qn = [q_ref[rows, hc[h]] for h in hs]
        kn = [k_ref[rows, hc[h]] for h in hs]
        vn = [v_ref[rows, hc[h]] for h in hs]
        qk = [_mm_nt(qn[h], kn[h]) for h in hs]
        qc = [_mm(qn[h], cm[h]) for h in hs]
        dmat = [jnp.where(tril, bc[h] + rl[h], NEG) for h in hs]
        mt = [jnp.maximum(bc[h] + m[h], jnp.max(dmat[h], axis=-1, keepdims=True)) for h in hs]
        cmt = [mt[h] - log2_scale for h in hs]
        w_inter = [jnp.exp2(bc[h] + m[h] - cmt[h]) for h in hs]
        p = [jnp.exp2(dmat[h] - cmt[h]) * qk[h] for h in hs]
        pv = [_mm(p[h], vn[h]) for h in hs]
        b_last = [bc[h][ln - 1:ln, :] for h in hs]
        m_new = [jnp.maximum(b_last[h] + m[h], jnp.max(b_last[h] + rl[h], axis=-1, keepdims=True)) for h in hs]
        wk = [jnp.exp2(b_last[h] - bc[h] + lic[h] - m_new[h]) * kn[h].astype(F32) for h in hs]
        upd = [_mm_tn(wk[h], vn[h]) for h in hs]
        for h in hs:
            num = w_inter[h] * qc[h] + pv[h]
            qn_dot = jnp.sum(qn[h].astype(F32) * nm[h], axis=-1, keepdims=True)
            den = w_inter[h] * qn_dot + jnp.sum(p[h], axis=-1, keepdims=True)
            hout = num * (1.0 / jnp.maximum(jnp.abs(den), jnp.exp2(-mt[h])))
            og = o_ref[rows, hc[h]].astype(F32)
            zg = z_ref[rows, hc[h]].astype(F32)
            gate = zg * (1.0 / ((1.0 + jnp.exp2(og * -LOG2E)) * (1.0 + jnp.exp2(zg * -LOG2E))))
            y_ref[rows, hc[h]] = (hout * gate).astype(y_ref.dtype)
            w_c = jnp.exp2(b_last[h] + m[h] - m_new[h])
            cm[h] = w_c * cm[h] + upd[h]
            nm[h] = w_c * nm[h] + jnp.sum(wk[h], axis=0, keepdims=True)
            m[h] = m_new[h]
    for h in hs:
        c_st[h] = cm[h]
        n_st[h] = jnp.broadcast_to(nm[h], n_st.shape[1:])
        m_st[h] = jnp.broadcast_to(m[h], m_st.shape[1:])


def _mlstm(p3, g3, gt4, bias_col, bias_row, ts):
    b, s, _ = p3.shape
    w_b = 1024
    hd = w_b // H_B
    ncs = s // MLSTM_L

    def col_spec(sec):
        return pl.BlockSpec((None, ts, w_b), lambda bi, j, sec=sec: (bi, j, 4 + sec))

    return pl.pallas_call(
        _mlstm_body,
        grid=(b, s // ts),
        in_specs=[
            col_spec(0), col_spec(1), col_spec(2), col_spec(3), col_spec(4),
            pl.BlockSpec((None, ts, LANES), lambda bi, j: (bi, j, 0)),
            pl.BlockSpec((None, 2 * H_B, ncs, MLSTM_L), lambda bi, j: (bi, 0, 0, 0)),
            pl.BlockSpec((1, LANES), lambda bi, j: (0, 0)),
            pl.BlockSpec((2 * H_B, 1, 1), lambda bi, j: (0, 0, 0)),
        ],
        out_specs=pl.BlockSpec((None, ts, w_b), lambda bi, j: (bi, j, 0)),
        out_shape=jax.ShapeDtypeStruct((b, s, w_b), BF16),
        scratch_shapes=[
            pltpu.VMEM((H_B, hd, hd), F32),
            pltpu.VMEM((H_B, 8, hd), F32),
            pltpu.VMEM((H_B, 8, LANES), F32),
            pltpu.VMEM((ts, LANES), F32),
            pltpu.VMEM((ts, LANES), F32),
            pltpu.VMEM((H_B, ncs, MLSTM_L), F32),
            pltpu.VMEM((H_B, ncs, MLSTM_L), F32),
        ],
        compiler_params=_params(2),
        name="mlstm",
    )(p3, p3, p3, p3, p3, g3, gt4, bias_col, bias_row)


def _rglru_body(x_ref, z_ref, cw_ref, cb_ref, gw_ref, gbr_ref, gbi_ref, lam_ref, y_ref, a_s, b_s):
    s, wc = x_ref.shape
    blk = gw_ref.shape[1]
    row8 = _iota2((8, blk), 0)
    sub = _iota2((s // 8, 8, blk), 1)
    for n in range(wc // blk):
        cs = slice(n * blk, (n + 1) * blk)
        x = x_ref[:, cs].astype(F32)
        cw = cw_ref[:, cs]
        xc = cw[CONV_K - 1:CONV_K, :] * x + cb_ref[:, cs]
        for d in range(1, CONV_K):
            xs = pltpu.roll(x, d, 0)
            xs = jnp.concatenate([jnp.where(row8 >= d, xs[0:8], 0.0), xs[8:]], axis=0)
            xc = xc + cw[CONV_K - 1 - d:CONV_K - d, :] * xs
        gates = _mm(xc, gw_ref[n])
        r = _sigmoid(gates[:, :blk] + gbr_ref[:, cs])
        i = _sigmoid(gates[:, blk:] + gbi_ref[:, cs])
        log_a = -C_RG * r * _softplus(-lam_ref[:, cs])
        a = jnp.exp(log_a)
        var = -jnp.tanh(log_a) * (a * a + 1.0)
        bb = jnp.where(var > 0.0, var * lax.rsqrt(var), 0.0) * (i * xc)
        a = a.reshape(s // 8, 8, blk)
        bb = bb.reshape(s // 8, 8, blk)
        for d in (1, 2, 4):
            keep = sub >= d
            a_sh = jnp.where(keep, pltpu.roll(a, d, 1), 1.0)
            b_sh = jnp.where(keep, pltpu.roll(bb, d, 1), 0.0)
            bb = a * b_sh + bb
            a = a * a_sh
        a_s[:, cs] = a.reshape(s, blk)
        b_s[:, cs] = bb.reshape(s, blk)

    def group(gi, carry):
        rows = pl.ds(pl.multiple_of(gi * 8, 8), 8)
        h = a_s[rows, :] * carry + b_s[rows, :]
        b_s[rows, :] = h
        return jnp.broadcast_to(h[7:8, :], h.shape)

    lax.fori_loop(0, s // 8, group, jnp.zeros((8, wc), F32), unroll=8)
    z = z_ref[...].astype(F32)
    y_ref[...] = (b_s[...] * _silu(z)).astype(y_ref.dtype)


def _rglru(p3, cw, cb, gw, gb, lam):
    b, s, _ = p3.shape
    w_c = cw.shape[1]
    blk = w_c // N_BLK_C
    nsp = w_c // RG_COLS
    return pl.pallas_call(
        _rglru_body,
        grid=(b, nsp),
        in_specs=[
            pl.BlockSpec((None, s, RG_COLS), lambda bi, n: (bi, 0, n)),
            pl.BlockSpec((None, s, RG_COLS), lambda bi, n: (bi, 0, nsp + n)),
            pl.BlockSpec((CONV_K, RG_COLS), lambda bi, n: (0, n)),
            pl.BlockSpec((1, RG_COLS), lambda bi, n: (0, n)),
            pl.BlockSpec((RG_COLS // blk, blk, 2 * blk), lambda bi, n: (n, 0, 0)),
            pl.BlockSpec((1, RG_COLS), lambda bi, n: (0, n)),
            pl.BlockSpec((1, RG_COLS), lambda bi, n: (0, nsp + n)),
            pl.BlockSpec((1, RG_COLS), lambda bi, n: (0, n)),
        ],
        out_specs=pl.BlockSpec((None, s, RG_COLS), lambda bi, n: (bi, 0, n)),
        out_shape=jax.ShapeDtypeStruct((b, s, w_c), BF16),
        scratch_shapes=[pltpu.VMEM((s, RG_COLS), F32), pltpu.VMEM((s, RG_COLS), F32)],
        compiler_params=_params(2),
        name="rglru",
    )(p3, p3, cw, cb, gw, gb, gb, lam)


def _unit_lower_inverse(a_list):
    r, c = _iota2((CHUNK, CHUNK), 0), _iota2((CHUNK, CHUNK), 1)
    eye = jnp.where(r == c, 1.0, 0.0).astype(F32)

    def pair_mask(sh):
        rb, cb_ = r >> sh, c >> sh
        return ((rb & 1) == 1) & (cb_ == rb - 1)

    ts = [eye - jnp.where(pair_mask(0), a, 0.0) for a in a_list]
    for sh in range(1, 6):
        mask = pair_mask(sh)
        x1 = [_mm(t, jnp.where(mask, a, 0.0)) for t, a in zip(ts, a_list)]
        x2 = [_mm(x, t) for x, t in zip(x1, ts)]
        ts = [t - x for t, x in zip(ts, x2)]
    return ts


def _gdn_body(q_ref, k_ref, v_ref, z_ref, g_ref, gt_ref, cw_ref, al_c_ref, dt_c_ref, al_r_ref, dt_r_ref, on_ref,
              y_ref, s_st, xe, qkv, col_d, col_b, row_d, lhs_s, o0_s, n_s):
    ts = q_ref.shape[0]
    nct = ts // CHUNK
    w_d = q_ref.shape[1]
    h_d = w_d // HD_D
    scale = HD_D ** -0.5

    @pl.when(pl.program_id(1) == 0)
    def _():
        s_st[...] = jnp.zeros_like(s_st)
        xe[0:8, :] = jnp.zeros((8, 3 * w_d), F32)

    xe[8:8 + ts, 0:w_d] = q_ref[...].astype(F32)
    xe[8:8 + ts, w_d:2 * w_d] = k_ref[...].astype(F32)
    xe[8:8 + ts, 2 * w_d:3 * w_d] = v_ref[...].astype(F32)
    def conv_rows(rb, carry):
        r0 = pl.multiple_of(rb * CHUNK, CHUNK)
        for cbk in range(3 * h_d):
            cs = slice(cbk * HD_D, (cbk + 1) * HD_D)
            xw = xe[pl.ds(r0, CHUNK + 8), cs]
            acc = cw_ref[CONV_K - 1:CONV_K, cs] * xw[8:]
            for d in range(1, CONV_K):
                acc = acc + cw_ref[CONV_K - 1 - d:CONV_K - d, cs] * xw[8 - d:8 - d + CHUNK]
            acc = _silu(acc)
            if cbk < 2 * h_d:
                acc = acc * lax.rsqrt(jnp.sum(acc * acc, axis=-1, keepdims=True) + EPS)
            qkv[pl.ds(r0, CHUNK), cs] = acc
        return carry

    lax.fori_loop(0, nct, conv_rows, 0)
    xe[0:8, :] = xe[ts:ts + 8, :]

    g = g_ref[...]
    col_b[...] = _sigmoid(g)
    col_d[...] = _mm_f32(_chunk_tril(ts, CHUNK), -jnp.exp(al_c_ref[...]) * _softplus(g + dt_c_ref[...]))
    gt = gt_ref[...]
    g_r = -jnp.exp(al_r_ref[...]) * _softplus(gt[0:h_d] + dt_r_ref[...])
    row_d[...] = _mm_f32(g_r.reshape(h_d * nct, CHUNK), _triu(CHUNK)).reshape(h_d, nct, CHUNK)

    r_i, c_i = _iota2((CHUNK, CHUNK), 0), _iota2((CHUNK, CHUNK), 1)
    tril = c_i <= r_i
    strict = c_i < r_i

    hs = range(h_d)

    def group(gi, carry):
        probs = [(cc, h) for cc in range(GDN_GROUP) for h in hs]
        cidx = [gi * GDN_GROUP + cc for cc in range(GDN_GROUP)]
        rows = [pl.ds(pl.multiple_of(ci * CHUNK, CHUNK), CHUNK) for ci in cidx]
        dc_all = [col_d[r, :] for r in rows]
        beta_all = [col_b[r, :] for r in rows]
        dc = [dc_all[cc][:, h:h + 1] for cc, h in probs]
        beta = [beta_all[cc][:, h_d + h:h_d + h + 1] for cc, h in probs]
        q = [qkv[rows[cc], h * HD_D:(h + 1) * HD_D] for cc, h in probs]
        k = [qkv[rows[cc], w_d + h * HD_D:w_d + (h + 1) * HD_D] for cc, h in probs]
        v = [qkv[rows[cc], 2 * w_d + h * HD_D:2 * w_d + (h + 1) * HD_D] for cc, h in probs]
        ps = range(len(probs))
        gam = [jnp.exp(jnp.where(tril, dc[i] - row_d[probs[i][1], pl.ds(cidx[probs[i][0]], 1), :], NEG)) for i in ps]
        kb = [k[i] * beta[i] for i in ps]
        kk = [_mm_nt(kb[i], k[i]) for i in ps]
        t_inv = _unit_lower_inverse([jnp.where(strict, kk[i] * gam[i], 0.0) for i in ps])
        ed = [jnp.exp(dc[i]) for i in ps]
        uw = [_mm(t_inv[i], jnp.concatenate([v[i] * beta[i], kb[i] * ed[i]], axis=1)) for i in ps]
        qs = [q[i] * scale for i in ps]
        qk = [_mm_nt(qs[i], k[i]) * gam[i] for i in ps]
        kd = [k[i] * jnp.exp(dc[i][CHUNK - 1:CHUNK, :] - dc[i]) for i in ps]
        x1 = [_mm(qk[i], uw[i]) for i in ps]
        x2 = [_mm_tn(kd[i], uw[i]) for i in ps]
        for i, (cc, h) in enumerate(probs):
            lhs_s[cc, h] = jnp.concatenate([qs[i] * ed[i] - x1[i][:, HD_D:], x2[i][:, HD_D:]], axis=0).astype(BF16)
            o0_s[cc, h] = x1[i][:, :HD_D]
            n_s[cc, h] = x2[i][:, :HD_D]

        def step(cc, carry2):
            c = gi * GDN_GROUP + cc
            rws = pl.ds(pl.multiple_of(c * CHUNK, CHUNK), CHUNK)
            d_last = jnp.exp(col_d[pl.ds(c * CHUNK + CHUNK - 1, 1), :])
            st = [s_st[h] for h in hs]
            r = [_mm(lhs_s[cc, h], st[h]) for h in hs]
            for h in hs:
                o = o0_s[cc, h] + r[h][:CHUNK]
                s_st[h] = d_last[:, h:h + 1] * st[h] + n_s[cc, h] - r[h][CHUNK:]
                on = o * lax.rsqrt(jnp.mean(o * o, axis=-1, keepdims=True) + EPS) * on_ref[...]
                z = z_ref[rws, h * HD_D:(h + 1) * HD_D].astype(F32)
                y_ref[rws, h * HD_D:(h + 1) * HD_D] = (on * _silu(z)).astype(y_ref.dtype)
            return carry2

        lax.fori_loop(0, GDN_GROUP, step, 0)
        return carry

    lax.fori_loop(0, nct // GDN_GROUP, group, 0)


def _gdn(p3, g3, gt4, cw, al_c, dt_c, al_r, dt_r, onorm, ts):
    b, s, _ = p3.shape
    w_d = 1024
    h_d = w_d // HD_D
    nct = ts // CHUNK

    def col_spec(sec):
        return pl.BlockSpec((None, ts, w_d), lambda bi, j, sec=sec: (bi, j, 2 + sec))

    return pl.pallas_call(
        _gdn_body,
        grid=(b, s // ts),
        in_specs=[
            col_spec(0), col_spec(1), col_spec(2), col_spec(3),
            pl.BlockSpec((None, ts, LANES), lambda bi, j: (bi, j, 0)),
            pl.BlockSpec((None, 2 * h_d, nct, CHUNK), lambda bi, j: (bi, 0, j, 0)),
            pl.BlockSpec((CONV_K, 3 * w_d), lambda bi, j: (0, 0)),
            pl.BlockSpec((1, LANES), lambda bi, j: (0, 0)),
            pl.BlockSpec((1, LANES), lambda bi, j: (0, 0)),
            pl.BlockSpec((h_d, 1, 1), lambda bi, j: (0, 0, 0)),
            pl.BlockSpec((h_d, 1, 1), lambda bi, j: (0, 0, 0)),
            pl.BlockSpec((1, HD_D), lambda bi, j: (0, 0)),
        ],
        out_specs=pl.BlockSpec((None, ts, w_d), lambda bi, j: (bi, j, 0)),
        out_shape=jax.ShapeDtypeStruct((b, s, w_d), BF16),
        scratch_shapes=[
            pltpu.VMEM((h_d, HD_D, HD_D), F32),
            pltpu.VMEM((ts + 8, 3 * w_d), F32),
            pltpu.VMEM((ts, 3 * w_d), F32),
            pltpu.VMEM((ts, LANES), F32),
            pltpu.VMEM((ts, LANES), F32),
            pltpu.VMEM((h_d, nct, CHUNK), F32),
            pltpu.VMEM((GDN_GROUP, h_d, CHUNK + HD_D, HD_D), BF16),
            pltpu.VMEM((GDN_GROUP, h_d, CHUNK, HD_D), F32),
            pltpu.VMEM((GDN_GROUP, h_d, HD_D, HD_D), F32),
        ],
        compiler_params=_params(2),
        name="gated_delta",
    )(p3, p3, p3, p3, g3, gt4, cw, al_c, dt_c, al_r, dt_r, onorm)


def _rel_row(rel_bias):
    t = (np.arange(REL_W) + CHUNK - 1) % REL_W
    idx = np.clip(LOOKBACK * CHUNK + CHUNK - 1 - t, -REL_MAX, REL_MAX) + REL_MAX
    return rel_bias[:, idx].astype(F32)[:, None, :]


def _pad_lanes(v, n=LANES):
    return jnp.pad(v, ((0, 0), (0, n - v.shape[1])))


def _gates_t(g3, n, length):
    b, s, _ = g3.shape
    return jnp.transpose(g3[:, :, :n], (0, 2, 1)).reshape(b, n, s // length, length)


def kernel(x, ev_norm, ev_w_in, ev_if_bias, ev_qn_gain, ev_kn_gain, ev_rel_bias, ev_w_out, od_norm, od_w_in, od_conv_c_w, od_conv_c_b, od_gate_w, od_gate_b, od_lambda, od_conv_d_w, od_a_log, od_dt_bias, od_onorm, od_w_out):
    b, s, d = x.shape
    m = b * s
    half = d // 2
    tm = min(1024, m)
    ts = min(512, s)
    x2 = x.reshape(m, d)
    depth = ev_norm.shape[0] + od_norm.shape[0]
    for layer in range(depth):
        j = layer // 2
        if layer % 2 == 0:
            n_main = 9 * half
            w_in = ev_w_in[j]
            p, g = _inproj(x2, ev_norm[j].reshape(1, d), w_in, _pad_lanes(w_in[:, n_main:]).astype(BF16), n_main, tm, 1024)
            p3, g3 = p.reshape(b, s, n_main), g.reshape(b, s, LANES)
            ya = _attn(p3, ev_qn_gain[j].reshape(1, HD_A), ev_kn_gain[j].reshape(1, HD_A), _rel_row(ev_rel_bias[j]))
            bias = ev_if_bias[j].astype(F32)
            yb = _mlstm(p3, g3, _gates_t(g3, 2 * H_B, MLSTM_L), _pad_lanes(bias.reshape(1, -1)), bias.reshape(-1, 1, 1), ts)
            x2 = _outproj(ya.reshape(m, half), yb.reshape(m, half), ev_w_out[j], x2, tm, 1024)
        else:
            n_main = 6 * half
            h_d = half // HD_D
            w_in = od_w_in[j]
            p, g = _inproj(x2, od_norm[j].reshape(1, d), w_in, _pad_lanes(w_in[:, n_main:]).astype(BF16), n_main, tm, 1024)
            p3, g3 = p.reshape(b, s, n_main), g.reshape(b, s, LANES)
            yc = _rglru(p3, od_conv_c_w[j], od_conv_c_b[j].reshape(1, -1), od_gate_w[j].astype(BF16),
                        od_gate_b[j].reshape(1, -1), od_lambda[j].reshape(1, -1))
            al, dt = od_a_log[j].astype(F32), od_dt_bias[j].astype(F32)
            yd = _gdn(p3, g3, _gates_t(g3, 2 * h_d, CHUNK), od_conv_d_w[j], _pad_lanes(al.reshape(1, -1)),
                      _pad_lanes(dt.reshape(1, -1)), al.reshape(-1, 1, 1), dt.reshape(-1, 1, 1),
                      od_onorm[j].reshape(1, HD_D), ts)
            x2 = _outproj(yc.reshape(m, half), yd.reshape(m, half), od_w_out[j], x2, tm, 1024)
    return x2.reshape(b, s, d)
```

```python
import functools

import numpy as np
import jax
import jax.numpy as jnp
from jax import lax
from jax.experimental import pallas as pl
from jax.experimental.pallas import tpu as pltpu

F32 = jnp.float32
BF16 = jnp.bfloat16

CHUNK = 64
EPS = 1e-6
NEG = -1e30
CONV_K = 4
LOG2E = 1.4426950408889634
LANES = 128
HD_A = 128
LOOKBACK = 8
BAND = (LOOKBACK + 1) * CHUNK
REL_MAX = 256
REL_W = 640
ATTN_GROUP = 16
H_B = 4
MLSTM_L = 256
N_BLK_C = 8
C_RG = 8.0
RG_COLS = 512
HD_D = 128
GDN_GROUP = 4

VMEM_LIMIT = 56 * 1024 * 1024


def _params(n_axes):
    return pltpu.CompilerParams(dimension_semantics=("arbitrary",) * n_axes, vmem_limit_bytes=VMEM_LIMIT)


def _mm(a, b):
    return jnp.dot(a.astype(BF16), b.astype(BF16), preferred_element_type=F32)


def _mm_nt(a, b):
    return lax.dot_general(a.astype(BF16), b.astype(BF16), (((1,), (1,)), ((), ())), preferred_element_type=F32)


def _mm_tn(a, b):
    return lax.dot_general(a.astype(BF16), b.astype(BF16), (((0,), (0,)), ((), ())), preferred_element_type=F32)


def _mm_f32(a, b):
    return jnp.dot(a, b, preferred_element_type=F32, precision=lax.Precision.HIGHEST)


def _log1p(e):
    u = 1.0 + e
    return jnp.where(u == 1.0, e, jnp.log(u) * (e / (u - 1.0)))


def _softplus(x):
    return jnp.maximum(x, 0.0) + _log1p(jnp.exp(-jnp.abs(x)))


def _sigmoid(x):
    return 1.0 / (1.0 + jnp.exp2(x * -LOG2E))


def _silu(x):
    return x * _sigmoid(x)


def _iota2(shape, axis):
    return lax.broadcasted_iota(jnp.int32, shape, axis)


def _chunk_tril(n, length):
    sh = length.bit_length() - 1
    r, c = _iota2((n, n), 0), _iota2((n, n), 1)
    same = (r >> sh) == (c >> sh)
    return jnp.where(same & (c <= r), 1.0, 0.0).astype(F32)


def _triu(n):
    r, c = _iota2((n, n), 0), _iota2((n, n), 1)
    return jnp.where(r <= c, 1.0, 0.0).astype(F32)


def _inproj_body(x_ref, g_ref, w_ref, wg_ref, o_ref, og_ref, xn_ref):
    @pl.when(pl.program_id(1) == 0)
    def _():
        x = x_ref[...]
        ms = jnp.mean(x * x, axis=-1, keepdims=True)
        xn = ((x * lax.rsqrt(ms + EPS)) * g_ref[...]).astype(BF16)
        xn_ref[...] = xn
        og_ref[...] = _mm_nt(xn, wg_ref[...])

    o_ref[...] = _mm_nt(xn_ref[...], w_ref[...]).astype(o_ref.dtype)


def _inproj(x2, g, w_t, wg_t, n, tm, tn):
    m, d = x2.shape
    return pl.pallas_call(
        _inproj_body,
        grid=(m // tm, n // tn),
        in_specs=[
            pl.BlockSpec((tm, d), lambda i, j: (i, 0)),
            pl.BlockSpec((1, d), lambda i, j: (0, 0)),
            pl.BlockSpec((tn, d), lambda i, j: (j, 0)),
            pl.BlockSpec((LANES, d), lambda i, j: (0, 0)),
        ],
        out_specs=[
            pl.BlockSpec((tm, tn), lambda i, j: (i, j)),
            pl.BlockSpec((tm, LANES), lambda i, j: (i, 0)),
        ],
        out_shape=[jax.ShapeDtypeStruct((m, n), BF16), jax.ShapeDtypeStruct((m, LANES), F32)],
        scratch_shapes=[pltpu.VMEM((tm, d), BF16)],
        compiler_params=_params(2),
        name="inproj",
    )(x2, g, w_t, wg_t)


def _outproj_body(ya_ref, yb_ref, wa_ref, wb_ref, x_ref, o_ref, wa_s, wb_s):
    @pl.when(pl.program_id(1) == 0)
    def _():
        wa_s[...] = wa_ref[...].astype(BF16)
        wb_s[...] = wb_ref[...].astype(BF16)

    acc = jnp.dot(ya_ref[...], wa_s[...], preferred_element_type=F32)
    acc = acc + jnp.dot(yb_ref[...], wb_s[...], preferred_element_type=F32)
    o_ref[...] = x_ref[...] + acc


def _outproj(ya, yb, w, x2, tm, tn):
    m, d = x2.shape
    kh = ya.shape[1]
    return pl.pallas_call(
        _outproj_body,
        grid=(d // tn, m // tm),
        in_specs=[
            pl.BlockSpec((tm, kh), lambda j, i: (i, 0)),
            pl.BlockSpec((tm, kh), lambda j, i: (i, 0)),
            pl.BlockSpec((kh, tn), lambda j, i: (0, j)),
            pl.BlockSpec((kh, tn), lambda j, i: (1, j)),
            pl.BlockSpec((tm, tn), lambda j, i: (i, j)),
        ],
        out_specs=pl.BlockSpec((tm, tn), lambda j, i: (i, j)),
        out_shape=jax.ShapeDtypeStruct((m, d), F32),
        scratch_shapes=[pltpu.VMEM((kh, tn), BF16), pltpu.VMEM((kh, tn), BF16)],
        compiler_params=_params(2),
        name="outproj",
    )(ya, yb, w, w, x2)


def _attn_body(q_ref, k_ref, v_ref, z_ref, qg_ref, kg_ref, c_ref, o_ref, qs, ks, vs, bias_s):
    s = q_ref.shape[0]
    pad = LOOKBACK * CHUNK
    q = q_ref[...].astype(F32)
    q = q * lax.rsqrt(jnp.mean(q * q, axis=-1, keepdims=True) + EPS) * qg_ref[...] * (HD_A ** -0.5 * LOG2E)
    qs[...] = q.astype(BF16)
    k = k_ref[...].astype(F32)
    k = k * lax.rsqrt(jnp.mean(k * k, axis=-1, keepdims=True) + EPS) * kg_ref[...]
    ks[0:pad, :] = jnp.zeros((pad, HD_A), BF16)
    ks[pad:pad + s, :] = k.astype(BF16)
    vs[0:pad, :] = jnp.zeros((pad, HD_A), BF16)
    vs[pad:pad + s, :] = v_ref[...]
    cb = jnp.broadcast_to(c_ref[...] * LOG2E, (CHUNK, REL_W))
    bias_s[...] = pltpu.roll(cb, 0, 1, stride=1, stride_axis=0)[:, :BAND]
    col = _iota2((CHUNK, BAND), 1)

    def chunks(it, masked):
        gs = range(ATTN_GROUP)
        n = [it * ATTN_GROUP + g for g in gs]
        r0 = [pl.multiple_of(n[g] * CHUNK, CHUNK) for g in gs]
        sc = [_mm_nt(qs[pl.ds(r0[g], CHUNK), :], ks[pl.ds(r0[g], BAND), :]) + bias_s[...] for g in gs]
        if masked:
            sc = [jnp.where(col >= (LOOKBACK - n[g]) * CHUNK, sc[g], NEG) for g in gs]
        p = [jnp.exp2(sc[g] - jnp.max(sc[g], axis=-1, keepdims=True)) for g in gs]
        pv = [_mm(p[g], vs[pl.ds(r0[g], BAND), :]) for g in gs]
        for g in gs:
            o = pv[g] * (1.0 / jnp.sum(p[g], axis=-1, keepdims=True))
            z = z_ref[pl.ds(r0[g], CHUNK), :].astype(F32)
            o_ref[pl.ds(r0[g], CHUNK), :] = (o * _silu(z)).astype(o_ref.dtype)

    assert ATTN_GROUP >= LOOKBACK
    chunks(0, True)

    def rest(it, carry):
        chunks(it, False)
        return carry

    lax.fori_loop(1, s // (CHUNK * ATTN_GROUP), rest, 0)


def _attn(p3, qg, kg, crel):
    b, s, _ = p3.shape
    h_a = crel.shape[0]

    def col_spec(sec):
        return pl.BlockSpec((None, s, HD_A), lambda bi, h, sec=sec: (bi, 0, sec * h_a + h))

    return pl.pallas_call(
        _attn_body,
        grid=(b, h_a),
        in_specs=[
            col_spec(0), col_spec(1), col_spec(2), col_spec(3),
            pl.BlockSpec((1, HD_A), lambda bi, h: (0, 0)),
            pl.BlockSpec((1, HD_A), lambda bi, h: (0, 0)),
            pl.BlockSpec((None, 1, REL_W), lambda bi, h: (h, 0, 0)),
        ],
        out_specs=pl.BlockSpec((None, s, HD_A), lambda bi, h: (bi, 0, h)),
        out_shape=jax.ShapeDtypeStruct((b, s, h_a * HD_A), BF16),
        scratch_shapes=[
            pltpu.VMEM((s, HD_A), BF16),
            pltpu.VMEM((s + LOOKBACK * CHUNK, HD_A), BF16),
            pltpu.VMEM((s + LOOKBACK * CHUNK, HD_A), BF16),
            pltpu.VMEM((CHUNK, BAND), F32),
        ],
        compiler_params=_params(2),
        name="chunk_attn",
    )(p3, p3, p3, p3, qg, kg, crel)


def _mlstm_body(q_ref, k_ref, v_ref, o_ref, z_ref, g_ref, gt_ref, bc_ref, br_ref, y_ref,
                c_st, n_st, m_st, col_b, col_i, row_b, row_i):
    ts = q_ref.shape[0]
    ln = MLSTM_L
    nct = ts // ln
    hd = q_ref.shape[1] // H_B
    scale = hd ** -0.5
    j = pl.program_id(1)
    hs = range(H_B)
    hc = [slice(h * hd, (h + 1) * hd) for h in hs]

    @pl.when(j == 0)
    def _():
        c_st[...] = jnp.zeros_like(c_st)
        n_st[...] = jnp.zeros_like(n_st)
        m_st[...] = jnp.zeros_like(m_st)
        gt = gt_ref[...] + br_ref[...]
        for h in hs:
            row_b[h] = _mm_f32(-_softplus(-gt[H_B + h]), _triu(ln)) * LOG2E
            row_i[h] = gt[h] * LOG2E

    g = g_ref[...] + bc_ref[...]
    col_i[...] = g * LOG2E
    col_b[...] = _mm_f32(_chunk_tril(ts, ln), -_softplus(-g)) * LOG2E
    tril = _iota2((ln, ln), 1) <= _iota2((ln, ln), 0)
    log2_scale = float(np.log2(scale))
    m = [m_st[h][0:1, 0:1] for h in hs]
    cm = [c_st[h] for h in hs]
    nm = [n_st[h][0:1, :] for h in hs]
    for c in range(nct):
        rows = slice(c * ln, (c + 1) * ln)
        ci = j * nct + c
        bc_all = col_b[rows, :]
        li_all = col_i[rows, :]
        bc = [bc_all[:, H_B + h:H_B + h + 1] for h in hs]
        lic = [li_all[:, h:h + 1] for h in hs]
        rl = [row_i[h, pl.ds(ci, 1), :] - row_b[h, pl.ds(ci, 1), :] for h in hs]
        qn = [q_ref[rows, hc[h]] for h in hs]
        kn = [k_ref[rows, hc[h]] for h in hs]
        vn = [v_ref[rows, hc[h]] for h in hs]
        qk = [_mm_nt(qn[h], kn[h]) for h in hs]
        qc = [_mm(qn[h], cm[h]) for h in hs]
        dmat = [jnp.where(tril, bc[h] + rl[h], NEG) for h in hs]
        mt = [jnp.maximum(bc[h] + m[h], jnp.max(dmat[h], axis=-1, keepdims=True)) for h in hs]
        cmt = [mt[h] - log2_scale for h in hs]
        w_inter = [jnp.exp2(bc[h] + m[h] - cmt[h]) for h in hs]
        p = [jnp.exp2(dmat[h] - cmt[h]) * qk[h] for h in hs]
        pv = [_mm(p[h], vn[h]) for h in hs]
        b_last = [bc[h][ln - 1:ln, :] for h in hs]
        m_new = [jnp.maximum(b_last[h] + m[h], jnp.max(b_last[h] + rl[h], axis=-1, keepdims=True)) for h in hs]
        wk = [jnp.exp2(b_last[h] - bc[h] + lic[h] - m_new[h]) * kn[h].astype(F32) for h in hs]
        upd = [_mm_tn(wk[h], vn[h]) for h in hs]
        for h in hs:
            num = w_inter[h] * qc[h] + pv[h]
            qn_dot = jnp.sum(qn[h].astype(F32) * nm[h], axis=-1, keepdims=True)
            den = w_inter[h] * qn_dot + jnp.sum(p[h], axis=-1, keepdims=True)
            hout = num * (1.0 / jnp.maximum(jnp.abs(den), jnp.exp2(-mt[h])))
            og = o_ref[rows, hc[h]].astype(F32)
            zg = z_ref[rows, hc[h]].astype(F32)
            gate = zg * (1.0 / ((1.0 + jnp.exp2(og * -LOG2E)) * (1.0 + jnp.exp2(zg * -LOG2E))))
            y_ref[rows, hc[h]] = (hout * gate).astype(y_ref.dtype)
            w_c = jnp.exp2(b_last[h] + m[h] - m_new[h])
            cm[h] = w_c * cm[h] + upd[h]
            nm[h] = w_c * nm[h] + jnp.sum(wk[h], axis=0, keepdims=True)
            m[h] = m_new[h]
    for h in hs:
        c_st[h] = cm[h]
        n_st[h] = jnp.broadcast_to(nm[h], n_st.shape[1:])
        m_st[h] = jnp.broadcast_to(m[h], m_st.shape[1:])


def _mlstm(p3, g3, gt4, bias_col, bias_row, ts):
    b, s, _ = p3.shape
    w_b = 1024
    hd = w_b // H_B
    ncs = s // MLSTM_L

    def col_spec(sec):
        return pl.BlockSpec((None, ts, w_b), lambda bi, j, sec=sec: (bi, j, 4 + sec))

    return pl.pallas_call(
        _mlstm_body,
        grid=(b, s // ts),
        in_specs=[
            col_spec(0), col_spec(1), col_spec(2), col_spec(3), col_spec(4),
            pl.BlockSpec((None, ts, LANES), lambda bi, j: (bi, j, 0)),
            pl.BlockSpec((None, 2 * H_B, ncs, MLSTM_L), lambda bi, j: (bi, 0, 0, 0)),
            pl.BlockSpec((1, LANES), lambda bi, j: (0, 0)),
            pl.BlockSpec((2 * H_B, 1, 1), lambda bi, j: (0, 0, 0)),
        ],
        out_specs=pl.BlockSpec((None, ts, w_b), lambda bi, j: (bi, j, 0)),
        out_shape=jax.ShapeDtypeStruct((b, s, w_b), BF16),
        scratch_shapes=[
            pltpu.VMEM((H_B, hd, hd), F32),
            pltpu.VMEM((H_B, 8, hd), F32),
            pltpu.VMEM((H_B, 8, LANES), F32),
            pltpu.VMEM((ts, LANES), F32),
            pltpu.VMEM((ts, LANES), F32),
            pltpu.VMEM((H_B, ncs, MLSTM_L), F32),
            pltpu.VMEM((H_B, ncs, MLSTM_L), F32),
        ],
        compiler_params=_params(2),
        name="mlstm",
    )(p3, p3, p3, p3, p3, g3, gt4, bias_col, bias_row)


def _rglru_body(x_ref, z_ref, cw_ref, cb_ref, gw_ref, gbr_ref, gbi_ref, lam_ref, y_ref, a_s, b_s):
    s, wc = x_ref.shape
    blk = gw_ref.shape[1]
    row8 = _iota2((8, blk), 0)
    sub = _iota2((s // 8, 8, blk), 1)
    for n in range(wc // blk):
        cs = slice(n * blk, (n + 1) * blk)
        x = x_ref[:, cs].astype(F32)
        cw = cw_ref[:, cs]
        xc = cw[CONV_K - 1:CONV_K, :] * x + cb_ref[:, cs]
        for d in range(1, CONV_K):
            xs = pltpu.roll(x, d, 0)
            xs = jnp.concatenate([jnp.where(row8 >= d, xs[0:8], 0.0), xs[8:]], axis=0)
            xc = xc + cw[CONV_K - 1 - d:CONV_K - d, :] * xs
        gates = _mm(xc, gw_ref[n])
        r = _sigmoid(gates[:, :blk] + gbr_ref[:, cs])
        i = _sigmoid(gates[:, blk:] + gbi_ref[:, cs])
        log_a = -C_RG * r * _softplus(-lam_ref[:, cs])
        a = jnp.exp(log_a)
        var = -jnp.tanh(log_a) * (a * a + 1.0)
        bb = jnp.where(var > 0.0, var * lax.rsqrt(var), 0.0) * (i * xc)
        a = a.reshape(s // 8, 8, blk)
        bb = bb.reshape(s // 8, 8, blk)
        for d in (1, 2, 4):
            keep = sub >= d
            a_sh = jnp.where(keep, pltpu.roll(a, d, 1), 1.0)
            b_sh = jnp.where(keep, pltpu.roll(bb, d, 1), 0.0)
            bb = a * b_sh + bb
            a = a * a_sh
        a_s[:, cs] = a.reshape(s, blk)
        b_s[:, cs] = bb.reshape(s, blk)

    def group(gi, carry):
        rows = pl.ds(pl.multiple_of(gi * 8, 8), 8)
        h = a_s[rows, :] * carry + b_s[rows, :]
        b_s[rows, :] = h
        return jnp.broadcast_to(h[7:8, :], h.shape)

    lax.fori_loop(0, s // 8, group, jnp.zeros((8, wc), F32), unroll=8)
    z = z_ref[...].astype(F32)
    y_ref[...] = (b_s[...] * _silu(z)).astype(y_ref.dtype)


def _rglru(p3, cw, cb, gw, gb, lam):
    b, s, _ = p3.shape
    w_c = cw.shape[1]
    blk = w_c // N_BLK_C
    nsp = w_c // RG_COLS
    return pl.pallas_call(
        _rglru_body,
        grid=(b, nsp),
        in_specs=[
            pl.BlockSpec((None, s, RG_COLS), lambda bi, n: (bi, 0, n)),
            pl.BlockSpec((None, s, RG_COLS), lambda bi, n: (bi, 0, nsp + n)),
            pl.BlockSpec((CONV_K, RG_COLS), lambda bi, n: (0, n)),
            pl.BlockSpec((1, RG_COLS), lambda bi, n: (0, n)),
            pl.BlockSpec((RG_COLS // blk, blk, 2 * blk), lambda bi, n: (n, 0, 0)),
            pl.BlockSpec((1, RG_COLS), lambda bi, n: (0, n)),
            pl.BlockSpec((1, RG_COLS), lambda bi, n: (0, nsp + n)),
            pl.BlockSpec((1, RG_COLS), lambda bi, n: (0, n)),
        ],
        out_specs=pl.BlockSpec((None, s, RG_COLS), lambda bi, n: (bi, 0, n)),
        out_shape=jax.ShapeDtypeStruct((b, s, w_c), BF16),
        scratch_shapes=[pltpu.VMEM((s, RG_COLS), F32), pltpu.VMEM((s, RG_COLS), F32)],
        compiler_params=_params(2),
        name="rglru",
    )(p3, p3, cw, cb, gw, gb, gb, lam)


def _unit_lower_inverse(a_list):
    r, c = _iota2((CHUNK, CHUNK), 0), _iota2((CHUNK, CHUNK), 1)
    eye = jnp.where(r == c, 1.0, 0.0).astype(F32)

    def pair_mask(sh):
        rb, cb_ = r >> sh, c >> sh
        return ((rb & 1) == 1) & (cb_ == rb - 1)

    ts = [eye - jnp.where(pair_mask(0), a, 0.0) for a in a_list]
    for sh in range(1, 6):
        mask = pair_mask(sh)
        x1 = [_mm(t, jnp.where(mask, a, 0.0)) for t, a in zip(ts, a_list)]
        x2 = [_mm(x, t) for x, t in zip(x1, ts)]
        ts = [t - x for t, x in zip(ts, x2)]
    return ts


def _gdn_body(q_ref, k_ref, v_ref, z_ref, g_ref, gt_ref, cw_ref, al_c_ref, dt_c_ref, al_r_ref, dt_r_ref, on_ref,
              y_ref, s_st, xe, qkv, col_d, col_b, row_d, lhs_s, o0_s, n_s):
    ts = q_ref.shape[0]
    nct = ts // CHUNK
    w_d = q_ref.shape[1]
    h_d = w_d // HD_D
    scale = HD_D ** -0.5

    @pl.when(pl.program_id(1) == 0)
    def _():
        s_st[...] = jnp.zeros_like(s_st)
        xe[0:8, :] = jnp.zeros((8, 3 * w_d), F32)

    xe[8:8 + ts, 0:w_d] = q_ref[...].astype(F32)
    xe[8:8 + ts, w_d:2 * w_d] = k_ref[...].astype(F32)
    xe[8:8 + ts, 2 * w_d:3 * w_d] = v_ref[...].astype(F32)
    def conv_rows(rb, carry):
        r0 = pl.multiple_of(rb * CHUNK, CHUNK)
        for cbk in range(3 * h_d):
            cs = slice(cbk * HD_D, (cbk + 1) * HD_D)
            xw = xe[pl.ds(r0, CHUNK + 8), cs]
            acc = cw_ref[CONV_K - 1:CONV_K, cs] * xw[8:]
            for d in range(1, CONV_K):
                acc = acc + cw_ref[CONV_K - 1 - d:CONV_K - d, cs] * xw[8 - d:8 - d + CHUNK]
            acc = _silu(acc)
            if cbk < 2 * h_d:
                acc = acc * lax.rsqrt(jnp.sum(acc * acc, axis=-1, keepdims=True) + EPS)
            qkv[pl.ds(r0, CHUNK), cs] = acc
        return carry

    lax.fori_loop(0, nct, conv_rows, 0)
    xe[0:8, :] = xe[ts:ts + 8, :]

    g = g_ref[...]
    col_b[...] = _sigmoid(g)
    col_d[...] = _mm_f32(_chunk_tril(ts, CHUNK), -jnp.exp(al_c_ref[...]) * _softplus(g + dt_c_ref[...]))
    gt = gt_ref[...]
    g_r = -jnp.exp(al_r_ref[...]) * _softplus(gt[0:h_d] + dt_r_ref[...])
    row_d[...] = _mm_f32(g_r.reshape(h_d * nct, CHUNK), _triu(CHUNK)).reshape(h_d, nct, CHUNK)

    r_i, c_i = _iota2((CHUNK, CHUNK), 0), _iota2((CHUNK, CHUNK), 1)
    tril = c_i <= r_i
    strict = c_i < r_i

    hs = range(h_d)

    def group(gi, carry):
        probs = [(cc, h) for cc in range(GDN_GROUP) for h in hs]
        cidx = [gi * GDN_GROUP + cc for cc in range(GDN_GROUP)]
        rows = [pl.ds(pl.multiple_of(ci * CHUNK, CHUNK), CHUNK) for ci in cidx]
        dc_all = [col_d[r, :] for r in rows]
        beta_all = [col_b[r, :] for r in rows]
        dc = [dc_all[cc][:, h:h + 1] for cc, h in probs]
        beta = [beta_all[cc][:, h_d + h:h_d + h + 1] for cc, h in probs]
        q = [qkv[rows[cc], h * HD_D:(h + 1) * HD_D] for cc, h in probs]
        k = [qkv[rows[cc], w_d + h * HD_D:w_d + (h + 1) * HD_D] for cc, h in probs]
        v = [qkv[rows[cc], 2 * w_d + h * HD_D:2 * w_d + (h + 1) * HD_D] for cc, h in probs]
        ps = range(len(probs))
        gam = [jnp.exp(jnp.where(tril, dc[i] - row_d[probs[i][1], pl.ds(cidx[probs[i][0]], 1), :], NEG)) for i in ps]
        kb = [k[i] * beta[i] for i in ps]
        kk = [_mm_nt(kb[i], k[i]) for i in ps]
        t_inv = _unit_lower_inverse([jnp.where(strict, kk[i] * gam[i], 0.0) for i in ps])
        ed = [jnp.exp(dc[i]) for i in ps]
        uw = [_mm(t_inv[i], jnp.concatenate([v[i] * beta[i], kb[i] * ed[i]], axis=1)) for i in ps]
        qs = [q[i] * scale for i in ps]
        qk = [_mm_nt(qs[i], k[i]) * gam[i] for i in ps]
        kd = [k[i] * jnp.exp(dc[i][CHUNK - 1:CHUNK, :] - dc[i]) for i in ps]
        x1 = [_mm(qk[i], uw[i]) for i in ps]
        x2 = [_mm_tn(kd[i], uw[i]) for i in ps]
        for i, (cc, h) in enumerate(probs):
            lhs_s[cc, h] = jnp.concatenate([qs[i] * ed[i] - x1[i][:, HD_D:], x2[i][:, HD_D:]], axis=0).astype(BF16)
            o0_s[cc, h] = x1[i][:, :HD_D]
            n_s[cc, h] = x2[i][:, :HD_D]

        def step(cc, carry2):
            c = gi * GDN_GROUP + cc
            rws = pl.ds(pl.multiple_of(c * CHUNK, CHUNK), CHUNK)
            d_last = jnp.exp(col_d[pl.ds(c * CHUNK + CHUNK - 1, 1), :])
            st = [s_st[h] for h in hs]
            r = [_mm(lhs_s[cc, h], st[h]) for h in hs]
            for h in hs:
                o = o0_s[cc, h] + r[h][:CHUNK]
                s_st[h] = d_last[:, h:h + 1] * st[h] + n_s[cc, h] - r[h][CHUNK:]
                on = o * lax.rsqrt(jnp.mean(o * o, axis=-1, keepdims=True) + EPS) * on_ref[...]
                z = z_ref[rws, h * HD_D:(h + 1) * HD_D].astype(F32)
                y_ref[rws, h * HD_D:(h + 1) * HD_D] = (on * _silu(z)).astype(y_ref.dtype)
            return carry2

        lax.fori_loop(0, GDN_GROUP, step, 0)
        return carry

    lax.fori_loop(0, nct // GDN_GROUP, group, 0)


def _gdn(p3, g3, gt4, cw, al_c, dt_c, al_r, dt_r, onorm, ts):
    b, s, _ = p3.shape
    w_d = 1024
    h_d = w_d // HD_D
    nct = ts // CHUNK

    def col_spec(sec):
        return pl.BlockSpec((None, ts, w_d), lambda bi, j, sec=sec: (bi, j, 2 + sec))

    return pl.pallas_call(
        _gdn_body,
        grid=(b, s // ts),
        in_specs=[
            col_spec(0), col_spec(1), col_spec(2), col_spec(3),
            pl.BlockSpec((None, ts, LANES), lambda bi, j: (bi, j, 0)),
            pl.BlockSpec((None, 2 * h_d, nct, CHUNK), lambda bi, j: (bi, 0, j, 0)),
            pl.BlockSpec((CONV_K, 3 * w_d), lambda bi, j: (0, 0)),
            pl.BlockSpec((1, LANES), lambda bi, j: (0, 0)),
            pl.BlockSpec((1, LANES), lambda bi, j: (0, 0)),
            pl.BlockSpec((h_d, 1, 1), lambda bi, j: (0, 0, 0)),
            pl.BlockSpec((h_d, 1, 1), lambda bi, j: (0, 0, 0)),
            pl.BlockSpec((1, HD_D), lambda bi, j: (0, 0)),
        ],
        out_specs=pl.BlockSpec((None, ts, w_d), lambda bi, j: (bi, j, 0)),
        out_shape=jax.ShapeDtypeStruct((b, s, w_d), BF16),
        scratch_shapes=[
            pltpu.VMEM((h_d, HD_D, HD_D), F32),
            pltpu.VMEM((ts + 8, 3 * w_d), F32),
            pltpu.VMEM((ts, 3 * w_d), F32),
            pltpu.VMEM((ts, LANES), F32),
            pltpu.VMEM((ts, LANES), F32),
            pltpu.VMEM((h_d, nct, CHUNK), F32),
            pltpu.VMEM((GDN_GROUP, h_d, CHUNK + HD_D, HD_D), BF16),
            pltpu.VMEM((GDN_GROUP, h_d, CHUNK, HD_D), F32),
            pltpu.VMEM((GDN_GROUP, h_d, HD_D, HD_D), F32),
        ],
        compiler_params=_params(2),
        name="gated_delta",
    )(p3, p3, p3, p3, g3, gt4, cw, al_c, dt_c, al_r, dt_r, onorm)


def _rel_row(rel_bias):
    t = (np.arange(REL_W) + CHUNK - 1) % REL_W
    idx = np.clip(LOOKBACK * CHUNK + CHUNK - 1 - t, -REL_MAX, REL_MAX) + REL_MAX
    return rel_bias[:, idx].astype(F32)[:, None, :]


def _pad_lanes(v, n=LANES):
    return jnp.pad(v, ((0, 0), (0, n - v.shape[1])))


def _pad_rows(v, n=LANES):
    return jnp.pad(v, ((0, n - v.shape[0]), (0, 0)))


def _weight_t(w):
    return jnp.swapaxes(w, 0, 1).astype(BF16)


def _gates_t(g3, n, length):
    b, s, _ = g3.shape
    return jnp.transpose(g3[:, :, :n], (0, 2, 1)).reshape(b, n, s // length, length)


def kernel(x, ev_norm, ev_w_in, ev_if_bias, ev_qn_gain, ev_kn_gain, ev_rel_bias, ev_w_out, od_norm, od_w_in, od_conv_c_w, od_conv_c_b, od_gate_w, od_gate_b, od_lambda, od_conv_d_w, od_a_log, od_dt_bias, od_onorm, od_w_out):
    b, s, d = x.shape
    m = b * s
    half = d // 2
    tm = min(1024, m)
    ts = min(512, s)
    x2 = x.reshape(m, d)
    depth = ev_norm.shape[0] + od_norm.shape[0]
    for layer in range(depth):
        j = layer // 2
        if layer % 2 == 0:
            n_main = 9 * half
            w_t = _weight_t(ev_w_in[j])
            p, g = _inproj(x2, ev_norm[j].reshape(1, d), w_t, _pad_rows(w_t[n_main:]), n_main, tm, 1024)
            p3, g3 = p.reshape(b, s, n_main), g.reshape(b, s, LANES)
            ya = _attn(p3, ev_qn_gain[j].reshape(1, HD_A), ev_kn_gain[j].reshape(1, HD_A), _rel_row(ev_rel_bias[j]))
            bias = ev_if_bias[j].astype(F32)
            yb = _mlstm(p3, g3, _gates_t(g3, 2 * H_B, MLSTM_L), _pad_lanes(bias.reshape(1, -1)), bias.reshape(-1, 1, 1), ts)
            x2 = _outproj(ya.reshape(m, half), yb.reshape(m, half), ev_w_out[j], x2, tm, 1024)
        else:
            n_main = 6 * half
            h_d = half // HD_D
            w_t = _weight_t(od_w_in[j])
            p, g = _inproj(x2, od_norm[j].reshape(1, d), w_t, _pad_rows(w_t[n_main:]), n_main, tm, 1024)
            p3, g3 = p.reshape(b, s, n_main), g.reshape(b, s, LANES)
            yc = _rglru(p3, od_conv_c_w[j], od_conv_c_b[j].reshape(1, -1), od_gate_w[j].astype(BF16),
                        od_gate_b[j].reshape(1, -1), od_lambda[j].reshape(1, -1))
            al, dt = od_a_log[j].astype(F32), od_dt_bias[j].astype(F32)
            yd = _gdn(p3, g3, _gates_t(g3, 2 * h_d, CHUNK), od_conv_d_w[j], _pad_lanes(al.reshape(1, -1)),
                      _pad_lanes(dt.reshape(1, -1)), al.reshape(-1, 1, 1), dt.reshape(-1, 1, 1),
                      od_onorm[j].reshape(1, HD_D), ts)
            x2 = _outproj(yc.reshape(m, half), yd.reshape(m, half), od_w_out[j], x2, tm, 1024)
    return x2.reshape(b, s, d)
```

```python
import functools

import numpy as np
import jax
import jax.numpy as jnp
from jax import lax
from jax.experimental import pallas as pl
from jax.experimental.pallas import tpu as pltpu

F32 = jnp.float32
BF16 = jnp.bfloat16

CHUNK = 64
EPS = 1e-6
NEG = -1e30
CONV_K = 4
LOG2E = 1.4426950408889634
LANES = 128
HD_A = 128
LOOKBACK = 8
BAND = (LOOKBACK + 1) * CHUNK
REL_MAX = 256
REL_W = 640
ATTN_GROUP = 16
H_B = 4
MLSTM_L = 256
N_BLK_C = 8
C_RG = 8.0
RG_COLS = 512
HD_D = 128
GDN_GROUP = 4

VMEM_LIMIT = 56 * 1024 * 1024


def _params(n_axes):
    return pltpu.CompilerParams(dimension_semantics=("arbitrary",) * n_axes, vmem_limit_bytes=VMEM_LIMIT)


def _mm(a, b):
    return jnp.dot(a.astype(BF16), b.astype(BF16), preferred_element_type=F32)


def _mm_nt(a, b):
    return lax.dot_general(a.astype(BF16), b.astype(BF16), (((1,), (1,)), ((), ())), preferred_element_type=F32)


def _mm_tn(a, b):
    return lax.dot_general(a.astype(BF16), b.astype(BF16), (((0,), (0,)), ((), ())), preferred_element_type=F32)


def _mm_f32(a, b):
    return jnp.dot(a, b, preferred_element_type=F32, precision=lax.Precision.HIGHEST)


def _log1p(e):
    u = 1.0 + e
    return jnp.where(u == 1.0, e, jnp.log(u) * (e / (u - 1.0)))


def _softplus(x):
    return jnp.maximum(x, 0.0) + _log1p(jnp.exp(-jnp.abs(x)))


def _sigmoid(x):
    return 1.0 / (1.0 + jnp.exp2(x * -LOG2E))


def _silu(x):
    return x * _sigmoid(x)


def _iota2(shape, axis):
    return lax.broadcasted_iota(jnp.int32, shape, axis)


def _chunk_tril(n, length):
    sh = length.bit_length() - 1
    r, c = _iota2((n, n), 0), _iota2((n, n), 1)
    same = (r >> sh) == (c >> sh)
    return jnp.where(same & (c <= r), 1.0, 0.0).astype(F32)


def _triu(n):
    r, c = _iota2((n, n), 0), _iota2((n, n), 1)
    return jnp.where(r <= c, 1.0, 0.0).astype(F32)


def _inproj_body(x_ref, g_ref, w_ref, wg_ref, o_ref, og_ref, xn_ref):
    @pl.when(pl.program_id(1) == 0)
    def _():
        x = x_ref[...]
        ms = jnp.mean(x * x, axis=-1, keepdims=True)
        xn = ((x * lax.rsqrt(ms + EPS)) * g_ref[...]).astype(BF16)
        xn_ref[...] = xn
        og_ref[...] = _mm_nt(xn, wg_ref[...])

    o_ref[...] = _mm_nt(xn_ref[...], w_ref[...]).astype(o_ref.dtype)


def _inproj(x2, g, w_t, wg_t, n, tm, tn):
    m, d = x2.shape
    return pl.pallas_call(
        _inproj_body,
        grid=(m // tm, n // tn),
        in_specs=[
            pl.BlockSpec((tm, d), lambda i, j: (i, 0)),
            pl.BlockSpec((1, d), lambda i, j: (0, 0)),
            pl.BlockSpec((tn, d), lambda i, j: (j, 0)),
            pl.BlockSpec((LANES, d), lambda i, j: (0, 0)),
        ],
        out_specs=[
            pl.BlockSpec((tm, tn), lambda i, j: (i, j)),
            pl.BlockSpec((tm, LANES), lambda i, j: (i, 0)),
        ],
        out_shape=[jax.ShapeDtypeStruct((m, n), BF16), jax.ShapeDtypeStruct((m, LANES), F32)],
        scratch_shapes=[pltpu.VMEM((tm, d), BF16)],
        compiler_params=_params(2),
        name="inproj",
    )(x2, g, w_t, wg_t)


def _outproj_body(ya_ref, yb_ref, wa_ref, wb_ref, x_ref, o_ref, wa_s, wb_s):
    @pl.when(pl.program_id(1) == 0)
    def _():
        wa_s[...] = wa_ref[...].astype(BF16)
        wb_s[...] = wb_ref[...].astype(BF16)

    acc = jnp.dot(ya_ref[...], wa_s[...], preferred_element_type=F32)
    acc = acc + jnp.dot(yb_ref[...], wb_s[...], preferred_element_type=F32)
    o_ref[...] = x_ref[...] + acc


def _outproj(ya, yb, w, x2, tm, tn):
    m, d = x2.shape
    kh = ya.shape[1]
    return pl.pallas_call(
        _outproj_body,
        grid=(d // tn, m // tm),
        in_specs=[
            pl.BlockSpec((tm, kh), lambda j, i: (i, 0)),
            pl.BlockSpec((tm, kh), lambda j, i: (i, 0)),
            pl.BlockSpec((kh, tn), lambda j, i: (0, j)),
            pl.BlockSpec((kh, tn), lambda j, i: (1, j)),
            pl.BlockSpec((tm, tn), lambda j, i: (i, j)),
        ],
        out_specs=pl.BlockSpec((tm, tn), lambda j, i: (i, j)),
        out_shape=jax.ShapeDtypeStruct((m, d), F32),
        scratch_shapes=[pltpu.VMEM((kh, tn), BF16), pltpu.VMEM((kh, tn), BF16)],
        compiler_params=_params(2),
        name="outproj",
    )(ya, yb, w, w, x2)


def _attn_body(q_ref, k_ref, v_ref, z_ref, qg_ref, kg_ref, c_ref, o_ref, qs, ks, vs, bias_s):
    s = q_ref.shape[0]
    pad = LOOKBACK * CHUNK
    q = q_ref[...].astype(F32)
    q = q * lax.rsqrt(jnp.mean(q * q, axis=-1, keepdims=True) + EPS) * qg_ref[...] * (HD_A ** -0.5 * LOG2E)
    qs[...] = q.astype(BF16)
    k = k_ref[...].astype(F32)
    k = k * lax.rsqrt(jnp.mean(k * k, axis=-1, keepdims=True) + EPS) * kg_ref[...]
    ks[0:pad, :] = jnp.zeros((pad, HD_A), BF16)
    ks[pad:pad + s, :] = k.astype(BF16)
    vs[0:pad, :] = jnp.zeros((pad, HD_A), BF16)
    vs[pad:pad + s, :] = v_ref[...]
    cb = jnp.broadcast_to(c_ref[...] * LOG2E, (CHUNK, REL_W))
    bias_s[...] = pltpu.roll(cb, 0, 1, stride=1, stride_axis=0)[:, :BAND]
    col = _iota2((CHUNK, BAND), 1)

    def chunks(it, masked):
        gs = range(ATTN_GROUP)
        n = [it * ATTN_GROUP + g for g in gs]
        r0 = [pl.multiple_of(n[g] * CHUNK, CHUNK) for g in gs]
        sc = [_mm_nt(qs[pl.ds(r0[g], CHUNK), :], ks[pl.ds(r0[g], BAND), :]) + bias_s[...] for g in gs]
        if masked:
            sc = [jnp.where(col >= (LOOKBACK - n[g]) * CHUNK, sc[g], NEG) for g in gs]
        p = [jnp.exp2(sc[g] - jnp.max(sc[g], axis=-1, keepdims=True)) for g in gs]
        pv = [_mm(p[g], vs[pl.ds(r0[g], BAND), :]) for g in gs]
        for g in gs:
            o = pv[g] * (1.0 / jnp.sum(p[g], axis=-1, keepdims=True))
            z = z_ref[pl.ds(r0[g], CHUNK), :].astype(F32)
            o_ref[pl.ds(r0[g], CHUNK), :] = (o * _silu(z)).astype(o_ref.dtype)

    assert ATTN_GROUP >= LOOKBACK
    chunks(0, True)

    def rest(it, carry):
        chunks(it, False)
        return carry

    lax.fori_loop(1, s // (CHUNK * ATTN_GROUP), rest, 0)


def _attn(p3, qg, kg, crel):
    b, s, _ = p3.shape
    h_a = crel.shape[0]

    def col_spec(sec):
        return pl.BlockSpec((None, s, HD_A), lambda bi, h, sec=sec: (bi, 0, sec * h_a + h))

    return pl.pallas_call(
        _attn_body,
        grid=(b, h_a),
        in_specs=[
            col_spec(0), col_spec(1), col_spec(2), col_spec(3),
            pl.BlockSpec((1, HD_A), lambda bi, h: (0, 0)),
            pl.BlockSpec((1, HD_A), lambda bi, h: (0, 0)),
            pl.BlockSpec((None, 1, REL_W), lambda bi, h: (h, 0, 0)),
        ],
        out_specs=pl.BlockSpec((None, s, HD_A), lambda bi, h: (bi, 0, h)),
        out_shape=jax.ShapeDtypeStruct((b, s, h_a * HD_A), BF16),
        scratch_shapes=[
            pltpu.VMEM((s, HD_A), BF16),
            pltpu.VMEM((s + LOOKBACK * CHUNK, HD_A), BF16),
            pltpu.VMEM((s + LOOKBACK * CHUNK, HD_A), BF16),
            pltpu.VMEM((CHUNK, BAND), F32),
        ],
        compiler_params=_params(2),
        name="chunk_attn",
    )(p3, p3, p3, p3, qg, kg, crel)


def _mlstm_body(q_ref, k_ref, v_ref, o_ref, z_ref, g_ref, gt_ref, bc_ref, br_ref, y_ref,
                c_st, n_st, m_st, col_b, col_i, row_b, row_i):
    ts = q_ref.shape[0]
    ln = MLSTM_L
    nct = ts // ln
    hd = q_ref.shape[1] // H_B
    scale = hd ** -0.5
    j = pl.program_id(1)
    hs = range(H_B)
    hc = [slice(h * hd, (h + 1) * hd) for h in hs]

    @pl.when(j == 0)
    def _():
        c_st[...] = jnp.zeros_like(c_st)
        n_st[...] = jnp.zeros_like(n_st)
        m_st[...] = jnp.zeros_like(m_st)
        gt = gt_ref[...] + br_ref[...]
        for h in hs:
            row_b[h] = _mm_f32(-_softplus(-gt[H_B + h]), _triu(ln)) * LOG2E
            row_i[h] = gt[h] * LOG2E

    g = g_ref[...] + bc_ref[...]
    col_i[...] = g * LOG2E
    col_b[...] = _mm_f32(_chunk_tril(ts, ln), -_softplus(-g)) * LOG2E
    tril = _iota2((ln, ln), 1) <= _iota2((ln, ln), 0)
    log2_scale = float(np.log2(scale))
    m = [m_st[h][0:1, 0:1] for h in hs]
    cm = [c_st[h] for h in hs]
    nm = [n_st[h][0:1, :] for h in hs]
    for c in range(nct):
        rows = slice(c * ln, (c + 1) * ln)
        ci = j * nct + c
        bc_all = col_b[rows, :]
        li_all = col_i[rows, :]
        bc = [bc_all[:, H_B + h:H_B + h + 1] for h in hs]
        lic = [li_all[:, h:h + 1] for h in hs]
        rl = [row_i[h, pl.ds(ci, 1), :] - row_b[h, pl.ds(ci, 1), :] for h in hs]
        qn = [q_ref[rows, hc[h]] for h in hs]
        kn = [k_ref[rows, hc[h]] for h in hs]
        vn = [v_ref[rows, hc[h]] for h in hs]
        qk = [_mm_nt(qn[h], kn[h]) for h in hs]
        qc = [_mm(qn[h], cm[h]) for h in hs]
        dmat = [jnp.where(tril, bc[h] + rl[h], NEG) for h in hs]
        mt = [jnp.maximum(bc[h] + m[h], jnp.max(dmat[h], axis=-1, keepdims=True)) for h in hs]
        cmt = [mt[h] - log2_scale for h in hs]
        w_inter = [jnp.exp2(bc[h] + m[h] - cmt[h]) for h in hs]
        p = [jnp.exp2(dmat[h] - cmt[h]) * qk[h] for h in hs]
        pv = [_mm(p[h], vn[h]) for h in hs]
        b_last = [bc[h][ln - 1:ln, :] for h in hs]
        m_new = [jnp.maximum(b_last[h] + m[h], jnp.max(b_last[h] + rl[h], axis=-1, keepdims=True)) for h in hs]
        wk = [jnp.exp2(b_last[h] - bc[h] + lic[h] - m_new[h]) * kn[h].astype(F32) for h in hs]
        upd = [_mm_tn(wk[h], vn[h]) for h in hs]
        for h in hs:
            num = w_inter[h] * qc[h] + pv[h]
            qn_dot = jnp.sum(qn[h].astype(F32) * nm[h], axis=-1, keepdims=True)
            den = w_inter[h] * qn_dot + jnp.sum(p[h], axis=-1, keepdims=True)
            hout = num * (1.0 / jnp.maximum(jnp.abs(den), jnp.exp2(-mt[h])))
            og = o_ref[rows, hc[h]].astype(F32)
            zg = z_ref[rows, hc[h]].astype(F32)
            gate = zg * (1.0 / ((1.0 + jnp.exp2(og * -LOG2E)) * (1.0 + jnp.exp2(zg * -LOG2E))))
            y_ref[rows, hc[h]] = (hout * gate).astype(y_ref.dtype)
            w_c = jnp.exp2(b_last[h] + m[h] - m_new[h])
            cm[h] = w_c * cm[h] + upd[h]
            nm[h] = w_c * nm[h] + jnp.sum(wk[h], axis=0, keepdims=True)
            m[h] = m_new[h]
    for h in hs:
        c_st[h] = cm[h]
        n_st[h] = jnp.broadcast_to(nm[h], n_st.shape[1:])
        m_st[h] = jnp.broadcast_to(m[h], m_st.shape[1:])


def _mlstm(p3, g3, gt4, bias_col, bias_row, ts):
    b, s, _ = p3.shape
    w_b = 1024
    hd = w_b // H_B
    ncs = s // MLSTM_L

    def col_spec(sec):
        return pl.BlockSpec((None, ts, w_b), lambda bi, j, sec=sec: (bi, j, 4 + sec))

    return pl.pallas_call(
        _mlstm_body,
        grid=(b, s // ts),
        in_specs=[
            col_spec(0), col_spec(1), col_spec(2), col_spec(3), col_spec(4),
            pl.BlockSpec((None, ts, LANES), lambda bi, j: (bi, j, 0)),
            pl.BlockSpec((None, 2 * H_B, ncs, MLSTM_L), lambda bi, j: (bi, 0, 0, 0)),
            pl.BlockSpec((1, LANES), lambda bi, j: (0, 0)),
            pl.BlockSpec((2 * H_B, 1, 1), lambda bi, j: (0, 0, 0)),
        ],
        out_specs=pl.BlockSpec((None, ts, w_b), lambda bi, j: (bi, j, 0)),
        out_shape=jax.ShapeDtypeStruct((b, s, w_b), BF16),
        scratch_shapes=[
            pltpu.VMEM((H_B, hd, hd), F32),
            pltpu.VMEM((H_B, 8, hd), F32),
            pltpu.VMEM((H_B, 8, LANES), F32),
            pltpu.VMEM((ts, LANES), F32),
            pltpu.VMEM((ts, LANES), F32),
            pltpu.VMEM((H_B, ncs, MLSTM_L), F32),
            pltpu.VMEM((H_B, ncs, MLSTM_L), F32),
        ],
        compiler_params=_params(2),
        name="mlstm",
    )(p3, p3, p3, p3, p3, g3, gt4, bias_col, bias_row)


def _rglru_body(x_ref, z_ref, cw_ref, cb_ref, gw_ref, gbr_ref, gbi_ref, lam_ref, y_ref, a_s, b_s):
    s, wc = x_ref.shape
    blk = gw_ref.shape[1]
    row8 = _iota2((8, blk), 0)
    sub = _iota2((s // 8, 8, blk), 1)
    for n in range(wc // blk):
        cs = slice(n * blk, (n + 1) * blk)
        x = x_ref[:, cs].astype(F32)
        cw = cw_ref[:, cs]
        xc = cw[CONV_K - 1:CONV_K, :] * x + cb_ref[:, cs]
        for d in range(1, CONV_K):
            xs = pltpu.roll(x, d, 0)
            xs = jnp.concatenate([jnp.where(row8 >= d, xs[0:8], 0.0), xs[8:]], axis=0)
            xc = xc + cw[CONV_K - 1 - d:CONV_K - d, :] * xs
        gates = _mm(xc, gw_ref[n])
        r = _sigmoid(gates[:, :blk] + gbr_ref[:, cs])
        i = _sigmoid(gates[:, blk:] + gbi_ref[:, cs])
        log_a = -C_RG * r * _softplus(-lam_ref[:, cs])
        a = jnp.exp(log_a)
        var = -jnp.tanh(log_a) * (a * a + 1.0)
        bb = jnp.where(var > 0.0, var * lax.rsqrt(var), 0.0) * (i * xc)
        a = a.reshape(s // 8, 8, blk)
        bb = bb.reshape(s // 8, 8, blk)
        for d in (1, 2, 4):
            keep = sub >= d
            a_sh = jnp.where(keep, pltpu.roll(a, d, 1), 1.0)
            b_sh = jnp.where(keep, pltpu.roll(bb, d, 1), 0.0)
            bb = a * b_sh + bb
            a = a * a_sh
        a_s[:, cs] = a.reshape(s, blk)
        b_s[:, cs] = bb.reshape(s, blk)

    def group(gi, carry):
        rows = pl.ds(pl.multiple_of(gi * 8, 8), 8)
        h = a_s[rows, :] * carry + b_s[rows, :]
        b_s[rows, :] = h
        return jnp.broadcast_to(h[7:8, :], h.shape)

    lax.fori_loop(0, s // 8, group, jnp.zeros((8, wc), F32), unroll=8)
    z = z_ref[...].astype(F32)
    y_ref[...] = (b_s[...] * _silu(z)).astype(y_ref.dtype)


def _rglru(p3, cw, cb, gw, gb, lam):
    b, s, _ = p3.shape
    w_c = cw.shape[1]
    blk = w_c // N_BLK_C
    nsp = w_c // RG_COLS
    return pl.pallas_call(
        _rglru_body,
        grid=(b, nsp),
        in_specs=[
            pl.BlockSpec((None, s, RG_COLS), lambda bi, n: (bi, 0, n)),
            pl.BlockSpec((None, s, RG_COLS), lambda bi, n: (bi, 0, nsp + n)),
            pl.BlockSpec((CONV_K, RG_COLS), lambda bi, n: (0, n)),
            pl.BlockSpec((1, RG_COLS), lambda bi, n: (0, n)),
            pl.BlockSpec((RG_COLS // blk, blk, 2 * blk), lambda bi, n: (n, 0, 0)),
            pl.BlockSpec((1, RG_COLS), lambda bi, n: (0, n)),
            pl.BlockSpec((1, RG_COLS), lambda bi, n: (0, nsp + n)),
            pl.BlockSpec((1, RG_COLS), lambda bi, n: (0, n)),
        ],
        out_specs=pl.BlockSpec((None, s, RG_COLS), lambda bi, n: (bi, 0, n)),
        out_shape=jax.ShapeDtypeStruct((b, s, w_c), BF16),
        scratch_shapes=[pltpu.VMEM((s, RG_COLS), F32), pltpu.VMEM((s, RG_COLS), F32)],
        compiler_params=_params(2),
        name="rglru",
    )(p3, p3, cw, cb, gw, gb, gb, lam)


def _unit_lower_inverse(a_list):
    r, c = _iota2((CHUNK, CHUNK), 0), _iota2((CHUNK, CHUNK), 1)
    eye = jnp.where(r == c, 1.0, 0.0).astype(F32)

    def pair_mask(sh):
        rb, cb_ = r >> sh, c >> sh
        return ((rb & 1) == 1) & (cb_ == rb - 1)

    ts = [eye - jnp.where(pair_mask(0), a, 0.0) for a in a_list]
    for sh in range(1, 6):
        mask = pair_mask(sh)
        x1 = [_mm(t, jnp.where(mask, a, 0.0)) for t, a in zip(ts, a_list)]
        x2 = [_mm(x, t) for x, t in zip(x1, ts)]
        ts = [t - x for t, x in zip(ts, x2)]
    return ts


def _gdn_body(q_ref, k_ref, v_ref, z_ref, g_ref, gt_ref, cw_ref, al_c_ref, dt_c_ref, al_r_ref, dt_r_ref, on_ref,
              y_ref, s_st, tail, qkv, col_d, col_b, row_d, lhs_s, o0_s, n_s, *, nt):
    ts = q_ref.shape[0]
    nct = ts // CHUNK
    w_d = q_ref.shape[1]
    h_d = w_d // HD_D
    scale = HD_D ** -0.5
    t = pl.program_id(0)
    srcs = (q_ref, k_ref, v_ref)

    def conv_block(slot, r0, first, cbk):
        ref = srcs[cbk // h_d]
        ci = slice((cbk % h_d) * HD_D, (cbk % h_d + 1) * HD_D)
        cs = slice(cbk * HD_D, (cbk + 1) * HD_D)
        cur = ref[pl.ds(r0, CHUNK), ci].astype(F32)
        prev = ref[pl.ds(pl.multiple_of(jnp.maximum(r0 - 16, 0), 16), 16), ci].astype(F32)
        xw = jnp.concatenate([jnp.where(first, tail[:, cs], prev), cur], axis=0)
        acc = cw_ref[CONV_K - 1:CONV_K, cs] * cur
        for d in range(1, CONV_K):
            acc = acc + cw_ref[CONV_K - 1 - d:CONV_K - d, cs] * xw[16 - d:16 - d + CHUNK]
        acc = _silu(acc)
        if cbk < 2 * h_d:
            acc = acc * lax.rsqrt(jnp.sum(acc * acc, axis=-1, keepdims=True) + EPS)
        qkv[slot, pl.ds(r0, CHUNK), cs] = acc

    def conv_rows(slot, rb):
        r0 = pl.multiple_of(rb * CHUNK, CHUNK)
        for cbk in range(3 * h_d):
            conv_block(slot, r0, rb == 0, cbk)

    def save_tail():
        keep = lax.rem(t + 1, nt) != 0
        for w, ref in enumerate(srcs):
            tail[:, w * w_d:(w + 1) * w_d] = jnp.where(keep, ref[ts - 16:ts, :].astype(F32), 0.0)

    @pl.when(t == 0)
    def _():
        tail[...] = jnp.zeros_like(tail)

        def rows(rb, carry):
            conv_rows(0, rb)
            return carry

        lax.fori_loop(0, nct, rows, 0)
        save_tail()

    @pl.when(t > 0)
    def _():
        cur_slot = lax.rem(t, 2)
        prv_slot = 1 - cur_slot

        @pl.when(lax.rem(t - 1, nt) == 0)
        def _():
            s_st[...] = jnp.zeros_like(s_st)

        g = g_ref[...]
        col_b[...] = _sigmoid(g)
        col_d[...] = _mm_f32(_chunk_tril(ts, CHUNK), -jnp.exp(al_c_ref[...]) * _softplus(g + dt_c_ref[...]))
        gt = gt_ref[...]
        g_r = -jnp.exp(al_r_ref[...]) * _softplus(gt[0:h_d] + dt_r_ref[...])
        row_d[...] = _mm_f32(g_r.reshape(h_d * nct, CHUNK), _triu(CHUNK)).reshape(h_d, nct, CHUNK)

        r_i, c_i = _iota2((CHUNK, CHUNK), 0), _iota2((CHUNK, CHUNK), 1)
        tril = c_i <= r_i
        strict = c_i < r_i
        hs = range(h_d)

        def group(gi, carry):
            probs = [(cc, h) for cc in range(GDN_GROUP) for h in hs]
            cidx = [gi * GDN_GROUP + cc for cc in range(GDN_GROUP)]
            rows = [pl.ds(pl.multiple_of(ci * CHUNK, CHUNK), CHUNK) for ci in cidx]
            dc_all = [col_d[r, :] for r in rows]
            beta_all = [col_b[r, :] for r in rows]
            dc = [dc_all[cc][:, h:h + 1] for cc, h in probs]
            beta = [beta_all[cc][:, h_d + h:h_d + h + 1] for cc, h in probs]
            q = [qkv[prv_slot, rows[cc], h * HD_D:(h + 1) * HD_D] for cc, h in probs]
            k = [qkv[prv_slot, rows[cc], w_d + h * HD_D:w_d + (h + 1) * HD_D] for cc, h in probs]
            v = [qkv[prv_slot, rows[cc], 2 * w_d + h * HD_D:2 * w_d + (h + 1) * HD_D] for cc, h in probs]
            ps = range(len(probs))
            gam = [jnp.exp(jnp.where(tril, dc[i] - row_d[probs[i][1], pl.ds(cidx[probs[i][0]], 1), :], NEG)) for i in ps]
            kb = [k[i] * beta[i] for i in ps]
            kk = [_mm_nt(kb[i], k[i]) for i in ps]
            t_inv = _unit_lower_inverse([jnp.where(strict, kk[i] * gam[i], 0.0) for i in ps])
            ed = [jnp.exp(dc[i]) for i in ps]
            uw = [_mm(t_inv[i], jnp.concatenate([v[i] * beta[i], kb[i] * ed[i]], axis=1)) for i in ps]
            qs = [q[i] * scale for i in ps]
            qk = [_mm_nt(qs[i], k[i]) * gam[i] for i in ps]
            kd = [k[i] * jnp.exp(dc[i][CHUNK - 1:CHUNK, :] - dc[i]) for i in ps]
            x1 = [_mm(qk[i], uw[i]) for i in ps]
            x2 = [_mm_tn(kd[i], uw[i]) for i in ps]
            for i, (cc, h) in enumerate(probs):
                lhs_s[cc, h] = jnp.concatenate([qs[i] * ed[i] - x1[i][:, HD_D:], x2[i][:, HD_D:]], axis=0).astype(BF16)
                o0_s[cc, h] = x1[i][:, :HD_D]
                n_s[cc, h] = x2[i][:, :HD_D]
            for rr in range(GDN_GROUP):
                conv_rows(cur_slot, gi * GDN_GROUP + rr)

            def step(cc, carry2):
                c = gi * GDN_GROUP + cc
                rws = pl.ds(pl.multiple_of(c * CHUNK, CHUNK), CHUNK)
                d_last = jnp.exp(col_d[pl.ds(c * CHUNK + CHUNK - 1, 1), :])
                st = [s_st[h] for h in hs]
                r = [_mm(lhs_s[cc, h], st[h]) for h in hs]
                for h in hs:
                    o = o0_s[cc, h] + r[h][:CHUNK]
                    s_st[h] = d_last[:, h:h + 1] * st[h] + n_s[cc, h] - r[h][CHUNK:]
                    on = o * lax.rsqrt(jnp.mean(o * o, axis=-1, keepdims=True) + EPS) * on_ref[...]
                    z = z_ref[rws, h * HD_D:(h + 1) * HD_D].astype(F32)
                    y_ref[rws, h * HD_D:(h + 1) * HD_D] = (on * _silu(z)).astype(y_ref.dtype)
                return carry2

            lax.fori_loop(0, GDN_GROUP, step, 0)
            return carry

        lax.fori_loop(0, nct // GDN_GROUP, group, 0)
        save_tail()


def _gdn(p3, g3, gt4, cw, al_c, dt_c, al_r, dt_r, onorm, ts):
    b, s, _ = p3.shape
    w_d = 1024
    h_d = w_d // HD_D
    nct = ts // CHUNK
    nt = s // ts
    n_tiles = b * nt

    def conv_spec(sec):
        def imap(t, sec=sec):
            tc = jnp.minimum(t, n_tiles - 1)
            return (tc // nt, tc % nt, 2 + sec)
        return pl.BlockSpec((None, ts, w_d), imap)

    def prev_map(last):
        def imap(t):
            tp = jnp.maximum(t - 1, 0)
            return (tp // nt, tp % nt, last)
        return imap

    def gt_map(t):
        tp = jnp.maximum(t - 1, 0)
        return (tp // nt, 0, tp % nt, 0)

    return pl.pallas_call(
        functools.partial(_gdn_body, nt=nt),
        grid=(n_tiles + 1,),
        in_specs=[
            conv_spec(0), conv_spec(1), conv_spec(2),
            pl.BlockSpec((None, ts, w_d), prev_map(5)),
            pl.BlockSpec((None, ts, LANES), prev_map(0)),
            pl.BlockSpec((None, 2 * h_d, nct, CHUNK), gt_map),
            pl.BlockSpec((CONV_K, 3 * w_d), lambda t: (0, 0)),
            pl.BlockSpec((1, LANES), lambda t: (0, 0)),
            pl.BlockSpec((1, LANES), lambda t: (0, 0)),
            pl.BlockSpec((h_d, 1, 1), lambda t: (0, 0, 0)),
            pl.BlockSpec((h_d, 1, 1), lambda t: (0, 0, 0)),
            pl.BlockSpec((1, HD_D), lambda t: (0, 0)),
        ],
        out_specs=pl.BlockSpec((None, ts, w_d), prev_map(0)),
        out_shape=jax.ShapeDtypeStruct((b, s, w_d), BF16),
        scratch_shapes=[
            pltpu.VMEM((h_d, HD_D, HD_D), F32),
            pltpu.VMEM((16, 3 * w_d), F32),
            pltpu.VMEM((2, ts, 3 * w_d), F32),
            pltpu.VMEM((ts, LANES), F32),
            pltpu.VMEM((ts, LANES), F32),
            pltpu.VMEM((h_d, nct, CHUNK), F32),
            pltpu.VMEM((GDN_GROUP, h_d, CHUNK + HD_D, HD_D), BF16),
            pltpu.VMEM((GDN_GROUP, h_d, CHUNK, HD_D), F32),
            pltpu.VMEM((GDN_GROUP, h_d, HD_D, HD_D), F32),
        ],
        compiler_params=_params(1),
        name="gated_delta",
    )(p3, p3, p3, p3, g3, gt4, cw, al_c, dt_c, al_r, dt_r, onorm)


def _rel_row(rel_bias):
    t = (np.arange(REL_W) + CHUNK - 1) % REL_W
    idx = np.clip(LOOKBACK * CHUNK + CHUNK - 1 - t, -REL_MAX, REL_MAX) + REL_MAX
    return rel_bias[:, idx].astype(F32)[:, None, :]


def _pad_lanes(v, n=LANES):
    return jnp.pad(v, ((0, 0), (0, n - v.shape[1])))


def _pad_rows(v, n=LANES):
    return jnp.pad(v, ((0, n - v.shape[0]), (0, 0)))


def _weight_t(w):
    return jnp.swapaxes(w, 0, 1).astype(BF16)


def _gates_t(g3, n, length):
    b, s, _ = g3.shape
    return jnp.transpose(g3[:, :, :n], (0, 2, 1)).reshape(b, n, s // length, length)


def kernel(x, ev_norm, ev_w_in, ev_if_bias, ev_qn_gain, ev_kn_gain, ev_rel_bias, ev_w_out, od_norm, od_w_in, od_conv_c_w, od_conv_c_b, od_gate_w, od_gate_b, od_lambda, od_conv_d_w, od_a_log, od_dt_bias, od_onorm, od_w_out):
    b, s, d = x.shape
    m = b * s
    half = d // 2
    tm = min(1024, m)
    ts = min(512, s)
    x2 = x.reshape(m, d)
    depth = ev_norm.shape[0] + od_norm.shape[0]
    for layer in range(depth):
        j = layer // 2
        if layer % 2 == 0:
            n_main = 9 * half
            w_t = _weight_t(ev_w_in[j])
            p, g = _inproj(x2, ev_norm[j].reshape(1, d), w_t, _pad_rows(w_t[n_main:]), n_main, tm, 1536)
            p3, g3 = p.reshape(b, s, n_main), g.reshape(b, s, LANES)
            ya = _attn(p3, ev_qn_gain[j].reshape(1, HD_A), ev_kn_gain[j].reshape(1, HD_A), _rel_row(ev_rel_bias[j]))
            bias = ev_if_bias[j].astype(F32)
            yb = _mlstm(p3, g3, _gates_t(g3, 2 * H_B, MLSTM_L), _pad_lanes(bias.reshape(1, -1)), bias.reshape(-1, 1, 1), ts)
            x2 = _outproj(ya.reshape(m, half), yb.reshape(m, half), ev_w_out[j], x2, tm, 1024)
        else:
            n_main = 6 * half
            h_d = half // HD_D
            w_t = _weight_t(od_w_in[j])
            p, g = _inproj(x2, od_norm[j].reshape(1, d), w_t, _pad_rows(w_t[n_main:]), n_main, tm, 1536)
            p3, g3 = p.reshape(b, s, n_main), g.reshape(b, s, LANES)
            yc = _rglru(p3, od_conv_c_w[j], od_conv_c_b[j].reshape(1, -1), od_gate_w[j].astype(BF16),
                        od_gate_b[j].reshape(1, -1), od_lambda[j].reshape(1, -1))
            al, dt = od_a_log[j].astype(F32), od_dt_bias[j].astype(F32)
            yd = _gdn(p3, g3, _gates_t(g3, 2 * h_d, CHUNK), od_conv_d_w[j], _pad_lanes(al.reshape(1, -1)),
                      _pad_lanes(dt.reshape(1, -1)), al.reshape(-1, 1, 1), dt.reshape(-1, 1, 1),
                      od_onorm[j].reshape(1, HD_D), ts)
            x2 = _outproj(yc.reshape(m, half), yd.reshape(m, half), od_w_out[j], x2, tm, 1024)
    return x2.reshape(b, s, d)
```

```python
import functools

import numpy as np
import jax
import jax.numpy as jnp
from jax import lax
from jax.experimental import pallas as pl
from jax.experimental.pallas import tpu as pltpu

F32 = jnp.float32
BF16 = jnp.bfloat16

CHUNK = 64
EPS = 1e-6
NEG = -1e30
CONV_K = 4
LOG2E = 1.4426950408889634
LANES = 128
HD_A = 128
LOOKBACK = 8
BAND = (LOOKBACK + 1) * CHUNK
REL_MAX = 256
REL_W = 640
ATTN_GROUP = 16
ATTN_HEADS = 2
H_B = 4
MLSTM_L = 256
N_BLK_C = 8
C_RG = 8.0
RG_COLS = 512
HD_D = 128
GDN_GROUP = 4

VMEM_LIMIT = 56 * 1024 * 1024


def _params(n_axes):
    return pltpu.CompilerParams(dimension_semantics=("arbitrary",) * n_axes, vmem_limit_bytes=VMEM_LIMIT)


def _mm(a, b):
    return jnp.dot(a.astype(BF16), b.astype(BF16), preferred_element_type=F32)


def _mm_nt(a, b):
    return lax.dot_general(a.astype(BF16), b.astype(BF16), (((1,), (1,)), ((), ())), preferred_element_type=F32)


def _mm_tn(a, b):
    return lax.dot_general(a.astype(BF16), b.astype(BF16), (((0,), (0,)), ((), ())), preferred_element_type=F32)


def _mm_f32(a, b):
    return jnp.dot(a, b, preferred_element_type=F32, precision=lax.Precision.HIGHEST)


def _log1p(e):
    u = 1.0 + e
    return jnp.where(u == 1.0, e, jnp.log(u) * (e / (u - 1.0)))


def _softplus(x):
    return jnp.maximum(x, 0.0) + _log1p(jnp.exp(-jnp.abs(x)))


def _sigmoid(x):
    return 1.0 / (1.0 + jnp.exp2(x * -LOG2E))


def _silu(x):
    return x * _sigmoid(x)


def _iota2(shape, axis):
    return lax.broadcasted_iota(jnp.int32, shape, axis)


def _chunk_tril(n, length):
    sh = length.bit_length() - 1
    r, c = _iota2((n, n), 0), _iota2((n, n), 1)
    same = (r >> sh) == (c >> sh)
    return jnp.where(same & (c <= r), 1.0, 0.0).astype(F32)


def _triu(n):
    r, c = _iota2((n, n), 0), _iota2((n, n), 1)
    return jnp.where(r <= c, 1.0, 0.0).astype(F32)


def _inproj_body(x_ref, g_ref, w_ref, wg_ref, o_ref, og_ref, xn_ref):
    @pl.when(pl.program_id(1) == 0)
    def _():
        x = x_ref[...]
        ms = jnp.mean(x * x, axis=-1, keepdims=True)
        xn = ((x * lax.rsqrt(ms + EPS)) * g_ref[...]).astype(BF16)
        xn_ref[...] = xn
        og_ref[...] = _mm_nt(xn, wg_ref[...])

    o_ref[...] = _mm_nt(xn_ref[...], w_ref[...]).astype(o_ref.dtype)


def _inproj(x2, g, w_t, wg_t, n, tm, tn):
    m, d = x2.shape
    return pl.pallas_call(
        _inproj_body,
        grid=(m // tm, n // tn),
        in_specs=[
            pl.BlockSpec((tm, d), lambda i, j: (i, 0)),
            pl.BlockSpec((1, d), lambda i, j: (0, 0)),
            pl.BlockSpec((tn, d), lambda i, j: (j, 0)),
            pl.BlockSpec((LANES, d), lambda i, j: (0, 0)),
        ],
        out_specs=[
            pl.BlockSpec((tm, tn), lambda i, j: (i, j)),
            pl.BlockSpec((tm, LANES), lambda i, j: (i, 0)),
        ],
        out_shape=[jax.ShapeDtypeStruct((m, n), BF16), jax.ShapeDtypeStruct((m, LANES), F32)],
        scratch_shapes=[pltpu.VMEM((tm, d), BF16)],
        compiler_params=_params(2),
        name="inproj",
    )(x2, g, w_t, wg_t)


def _outproj_body(ya_ref, yb_ref, wa_ref, wb_ref, x_ref, o_ref, wa_s, wb_s):
    @pl.when(pl.program_id(1) == 0)
    def _():
        wa_s[...] = wa_ref[...].astype(BF16)
        wb_s[...] = wb_ref[...].astype(BF16)

    acc = jnp.dot(ya_ref[...], wa_s[...], preferred_element_type=F32)
    acc = acc + jnp.dot(yb_ref[...], wb_s[...], preferred_element_type=F32)
    o_ref[...] = x_ref[...] + acc


def _outproj(ya, yb, w, x2, tm, tn):
    m, d = x2.shape
    kh = ya.shape[1]
    return pl.pallas_call(
        _outproj_body,
        grid=(d // tn, m // tm),
        in_specs=[
            pl.BlockSpec((tm, kh), lambda j, i: (i, 0)),
            pl.BlockSpec((tm, kh), lambda j, i: (i, 0)),
            pl.BlockSpec((kh, tn), lambda j, i: (0, j)),
            pl.BlockSpec((kh, tn), lambda j, i: (1, j)),
            pl.BlockSpec((tm, tn), lambda j, i: (i, j)),
        ],
        out_specs=pl.BlockSpec((tm, tn), lambda j, i: (i, j)),
        out_shape=jax.ShapeDtypeStruct((m, d), F32),
        scratch_shapes=[pltpu.VMEM((kh, tn), BF16), pltpu.VMEM((kh, tn), BF16)],
        compiler_params=_params(2),
        name="outproj",
    )(ya, yb, w, w, x2)


def _attn_body(q_ref, k_ref, v_ref, z_ref, qg_ref, kg_ref, c_ref, o_ref, qs, ks, vs, bias_s):
    s = q_ref.shape[0]
    pad = LOOKBACK * CHUNK
    col = _iota2((CHUNK, BAND), 1)
    assert ATTN_GROUP >= LOOKBACK
    for hh in range(ATTN_HEADS):
        hc = slice(hh * HD_A, (hh + 1) * HD_A)
        q = q_ref[:, hc].astype(F32)
        q = q * lax.rsqrt(jnp.mean(q * q, axis=-1, keepdims=True) + EPS) * qg_ref[...] * (HD_A ** -0.5 * LOG2E)
        qs[hh] = q.astype(BF16)
        k = k_ref[:, hc].astype(F32)
        k = k * lax.rsqrt(jnp.mean(k * k, axis=-1, keepdims=True) + EPS) * kg_ref[...]
        ks[hh, 0:pad, :] = jnp.zeros((pad, HD_A), BF16)
        ks[hh, pad:pad + s, :] = k.astype(BF16)
        vs[hh, 0:pad, :] = jnp.zeros((pad, HD_A), BF16)
        vs[hh, pad:pad + s, :] = v_ref[:, hc]
        cb = jnp.broadcast_to(c_ref[hh] * LOG2E, (CHUNK, REL_W))
        bias_s[hh] = pltpu.roll(cb, 0, 1, stride=1, stride_axis=0)[:, :BAND]

        def chunks(it, masked, hh=hh, hc=hc):
            gs = range(ATTN_GROUP)
            n = [it * ATTN_GROUP + g for g in gs]
            r0 = [pl.multiple_of(n[g] * CHUNK, CHUNK) for g in gs]
            sc = [_mm_nt(qs[hh, pl.ds(r0[g], CHUNK), :], ks[hh, pl.ds(r0[g], BAND), :]) + bias_s[hh] for g in gs]
            if masked:
                sc = [jnp.where(col >= (LOOKBACK - n[g]) * CHUNK, sc[g], NEG) for g in gs]
            p = [jnp.exp2(sc[g] - jnp.max(sc[g], axis=-1, keepdims=True)) for g in gs]
            pv = [_mm(p[g], vs[hh, pl.ds(r0[g], BAND), :]) for g in gs]
            for g in gs:
                o = pv[g] * (1.0 / jnp.sum(p[g], axis=-1, keepdims=True))
                z = z_ref[pl.ds(r0[g], CHUNK), hc].astype(F32)
                o_ref[pl.ds(r0[g], CHUNK), hc] = (o * _silu(z)).astype(o_ref.dtype)

        chunks(0, True)

        def rest(it, carry, chunks=chunks):
            chunks(it, False)
            return carry

        lax.fori_loop(1, s // (CHUNK * ATTN_GROUP), rest, 0)


def _attn(p3, qg, kg, crel):
    b, s, _ = p3.shape
    h_a = crel.shape[0]
    hp = h_a // ATTN_HEADS
    wh = ATTN_HEADS * HD_A

    def col_spec(sec):
        return pl.BlockSpec((None, s, wh), lambda bi, h, sec=sec: (bi, 0, sec * hp + h))

    return pl.pallas_call(
        _attn_body,
        grid=(b, hp),
        in_specs=[
            col_spec(0), col_spec(1), col_spec(2), col_spec(3),
            pl.BlockSpec((1, HD_A), lambda bi, h: (0, 0)),
            pl.BlockSpec((1, HD_A), lambda bi, h: (0, 0)),
            pl.BlockSpec((ATTN_HEADS, 1, REL_W), lambda bi, h: (h, 0, 0)),
        ],
        out_specs=pl.BlockSpec((None, s, wh), lambda bi, h: (bi, 0, h)),
        out_shape=jax.ShapeDtypeStruct((b, s, h_a * HD_A), BF16),
        scratch_shapes=[
            pltpu.VMEM((ATTN_HEADS, s, HD_A), BF16),
            pltpu.VMEM((ATTN_HEADS, s + LOOKBACK * CHUNK, HD_A), BF16),
            pltpu.VMEM((ATTN_HEADS, s + LOOKBACK * CHUNK, HD_A), BF16),
            pltpu.VMEM((ATTN_HEADS, CHUNK, BAND), F32),
        ],
        compiler_params=_params(2),
        name="chunk_attn",
    )(p3, p3, p3, p3, qg, kg, crel)


def _mlstm_body(q_ref, k_ref, v_ref, o_ref, z_ref, g_ref, gt_ref, bc_ref, br_ref, y_ref,
                c_st, n_st, m_st, col_b, col_i, row_b, row_i):
    ts = q_ref.shape[0]
    ln = MLSTM_L
    nct = ts // ln
    hd = q_ref.shape[1] // H_B
    scale = hd ** -0.5
    j = pl.program_id(1)
    hs = range(H_B)
    hc = [slice(h * hd, (h + 1) * hd) for h in hs]

    @pl.when(j == 0)
    def _():
        c_st[...] = jnp.zeros_like(c_st)
        n_st[...] = jnp.zeros_like(n_st)
        m_st[...] = jnp.zeros_like(m_st)
        gt = gt_ref[...] + br_ref[...]
        for h in hs:
            row_b[h] = _mm_f32(-_softplus(-gt[H_B + h]), _triu(ln)) * LOG2E
            row_i[h] = gt[h] * LOG2E

    g = g_ref[...] + bc_ref[...]
    col_i[...] = g * LOG2E
    col_b[...] = _mm_f32(_chunk_tril(ts, ln), -_softplus(-g)) * LOG2E
    tril = _iota2((ln, ln), 1) <= _iota2((ln, ln), 0)
    log2_scale = float(np.log2(scale))
    m = [m_st[h][0:1, 0:1] for h in hs]
    cm = [c_st[h] for h in hs]
    nm = [n_st[h][0:1, :] for h in hs]
    for c in range(nct):
        rows = slice(c * ln, (c + 1) * ln)
        ci = j * nct + c
        bc_all = col_b[rows, :]
        li_all = col_i[rows, :]
        bc = [bc_all[:, H_B + h:H_B + h + 1] for h in hs]
        lic = [li_all[:, h:h + 1] for h in hs]
        rl = [row_i[h, pl.ds(ci, 1), :] - row_b[h, pl.ds(ci, 1), :] for h in hs]
        qn = [q_ref[rows, hc[h]] for h in hs]
        kn = [k_ref[rows, hc[h]] for h in hs]
        vn = [v_ref[rows, hc[h]] for h in hs]
        qk = [_mm_nt(qn[h], kn[h]) for h in hs]
        qc = [_mm(qn[h], cm[h]) for h in hs]
        dmat = [jnp.where(tril, bc[h] + rl[h], NEG) for h in hs]
        mt = [jnp.maximum(bc[h] + m[h], jnp.max(dmat[h], axis=-1, keepdims=True)) for h in hs]
        cmt = [mt[h] - log2_scale for h in hs]
        w_inter = [jnp.exp2(bc[h] + m[h] - cmt[h]) for h in hs]
        p = [jnp.exp2(dmat[h] - cmt[h]) * qk[h] for h in hs]
        pv = [_mm(p[h], vn[h]) for h in hs]
        b_last = [bc[h][ln - 1:ln, :] for h in hs]
        m_new = [jnp.maximum(b_last[h] + m[h], jnp.max(b_last[h] + rl[h], axis=-1, keepdims=True)) for h in hs]
        wk = [jnp.exp2(b_last[h] - bc[h] + lic[h] - m_new[h]) * kn[h].astype(F32) for h in hs]
        upd = [_mm_tn(wk[h], vn[h]) for h in hs]
        for h in hs:
            num = w_inter[h] * qc[h] + pv[h]
            qn_dot = jnp.sum(qn[h].astype(F32) * nm[h], axis=-1, keepdims=True)
            den = w_inter[h] * qn_dot + jnp.sum(p[h], axis=-1, keepdims=True)
            hout = num * (1.0 / jnp.maximum(jnp.abs(den), jnp.exp2(-mt[h])))
            og = o_ref[rows, hc[h]].astype(F32)
            zg = z_ref[rows, hc[h]].astype(F32)
            gate = zg * (1.0 / ((1.0 + jnp.exp2(og * -LOG2E)) * (1.0 + jnp.exp2(zg * -LOG2E))))
            y_ref[rows, hc[h]] = (hout * gate).astype(y_ref.dtype)
            w_c = jnp.exp2(b_last[h] + m[h] - m_new[h])
            cm[h] = w_c * cm[h] + upd[h]
            nm[h] = w_c * nm[h] + jnp.sum(wk[h], axis=0, keepdims=True)
            m[h] = m_new[h]
    for h in hs:
        c_st[h] = cm[h]
        n_st[h] = jnp.broadcast_to(nm[h], n_st.shape[1:])
        m_st[h] = jnp.broadcast_to(m[h], m_st.shape[1:])


def _mlstm(p3, g3, gt4, bias_col, bias_row, ts):
    b, s, _ = p3.shape
    w_b = 1024
    hd = w_b // H_B
    ncs = s // MLSTM_L

    def col_spec(sec):
        return pl.BlockSpec((None, ts, w_b), lambda bi, j, sec=sec: (bi, j, 4 + sec))

    return pl.pallas_call(
        _mlstm_body,
        grid=(b, s // ts),
        in_specs=[
            col_spec(0), col_spec(1), col_spec(2), col_spec(3), col_spec(4),
            pl.BlockSpec((None, ts, LANES), lambda bi, j: (bi, j, 0)),
            pl.BlockSpec((None, 2 * H_B, ncs, MLSTM_L), lambda bi, j: (bi, 0, 0, 0)),
            pl.BlockSpec((1, LANES), lambda bi, j: (0, 0)),
            pl.BlockSpec((2 * H_B, 1, 1), lambda bi, j: (0, 0, 0)),
        ],
        out_specs=pl.BlockSpec((None, ts, w_b), lambda bi, j: (bi, j, 0)),
        out_shape=jax.ShapeDtypeStruct((b, s, w_b), BF16),
        scratch_shapes=[
            pltpu.VMEM((H_B, hd, hd), F32),
            pltpu.VMEM((H_B, 8, hd), F32),
            pltpu.VMEM((H_B, 8, LANES), F32),
            pltpu.VMEM((ts, LANES), F32),
            pltpu.VMEM((ts, LANES), F32),
            pltpu.VMEM((H_B, ncs, MLSTM_L), F32),
            pltpu.VMEM((H_B, ncs, MLSTM_L), F32),
        ],
        compiler_params=_params(2),
        name="mlstm",
    )(p3, p3, p3, p3, p3, g3, gt4, bias_col, bias_row)


def _rglru_body(x_ref, z_ref, cw_ref, cb_ref, gw_ref, gbr_ref, gbi_ref, lam_ref, y_ref, a_s, b_s):
    s, wc = x_ref.shape
    blk = gw_ref.shape[1]
    row8 = _iota2((8, blk), 0)
    sub = _iota2((s // 8, 8, blk), 1)
    for n in range(wc // blk):
        cs = slice(n * blk, (n + 1) * blk)
        x = x_ref[:, cs].astype(F32)
        cw = cw_ref[:, cs]
        xc = cw[CONV_K - 1:CONV_K, :] * x + cb_ref[:, cs]
        for d in range(1, CONV_K):
            xs = pltpu.roll(x, d, 0)
            xs = jnp.concatenate([jnp.where(row8 >= d, xs[0:8], 0.0), xs[8:]], axis=0)
            xc = xc + cw[CONV_K - 1 - d:CONV_K - d, :] * xs
        gates = _mm(xc, gw_ref[n])
        r = _sigmoid(gates[:, :blk] + gbr_ref[:, cs])
        i = _sigmoid(gates[:, blk:] + gbi_ref[:, cs])
        log_a = -C_RG * r * _softplus(-lam_ref[:, cs])
        a = jnp.exp(log_a)
        var = -jnp.tanh(log_a) * (a * a + 1.0)
        bb = jnp.where(var > 0.0, var * lax.rsqrt(var), 0.0) * (i * xc)
        a = a.reshape(s // 8, 8, blk)
        bb = bb.reshape(s // 8, 8, blk)
        for d in (1, 2, 4):
            keep = sub >= d
            a_sh = jnp.where(keep, pltpu.roll(a, d, 1), 1.0)
            b_sh = jnp.where(keep, pltpu.roll(bb, d, 1), 0.0)
            bb = a * b_sh + bb
            a = a * a_sh
        a_s[:, cs] = a.reshape(s, blk)
        b_s[:, cs] = bb.reshape(s, blk)

    def group(gi, carry):
        rows = pl.ds(pl.multiple_of(gi * 8, 8), 8)
        h = a_s[rows, :] * carry + b_s[rows, :]
        b_s[rows, :] = h
        return jnp.broadcast_to(h[7:8, :], h.shape)

    lax.fori_loop(0, s // 8, group, jnp.zeros((8, wc), F32), unroll=8)
    z = z_ref[...].astype(F32)
    y_ref[...] = (b_s[...] * _silu(z)).astype(y_ref.dtype)


def _rglru(p3, cw, cb, gw, gb, lam):
    b, s, _ = p3.shape
    w_c = cw.shape[1]
    blk = w_c // N_BLK_C
    nsp = w_c // RG_COLS
    return pl.pallas_call(
        _rglru_body,
        grid=(b, nsp),
        in_specs=[
            pl.BlockSpec((None, s, RG_COLS), lambda bi, n: (bi, 0, n)),
            pl.BlockSpec((None, s, RG_COLS), lambda bi, n: (bi, 0, nsp + n)),
            pl.BlockSpec((CONV_K, RG_COLS), lambda bi, n: (0, n)),
            pl.BlockSpec((1, RG_COLS), lambda bi, n: (0, n)),
            pl.BlockSpec((RG_COLS // blk, blk, 2 * blk), lambda bi, n: (n, 0, 0)),
            pl.BlockSpec((1, RG_COLS), lambda bi, n: (0, n)),
            pl.BlockSpec((1, RG_COLS), lambda bi, n: (0, nsp + n)),
            pl.BlockSpec((1, RG_COLS), lambda bi, n: (0, n)),
        ],
        out_specs=pl.BlockSpec((None, s, RG_COLS), lambda bi, n: (bi, 0, n)),
        out_shape=jax.ShapeDtypeStruct((b, s, w_c), BF16),
        scratch_shapes=[pltpu.VMEM((s, RG_COLS), F32), pltpu.VMEM((s, RG_COLS), F32)],
        compiler_params=_params(2),
        name="rglru",
    )(p3, p3, cw, cb, gw, gb, gb, lam)


def _unit_lower_inverse(a_list):
    r, c = _iota2((CHUNK, CHUNK), 0), _iota2((CHUNK, CHUNK), 1)
    eye = jnp.where(r == c, 1.0, 0.0).astype(F32)

    def pair_mask(sh):
        rb, cb_ = r >> sh, c >> sh
        return ((rb & 1) == 1) & (cb_ == rb - 1)

    ts = [eye - jnp.where(pair_mask(0), a, 0.0) for a in a_list]
    for sh in range(1, 6):
        mask = pair_mask(sh)
        x1 = [_mm(t, jnp.where(mask, a, 0.0)) for t, a in zip(ts, a_list)]
        x2 = [_mm(x, t) for x, t in zip(x1, ts)]
        ts = [t - x for t, x in zip(ts, x2)]
    return ts


def _gdn_body(q_ref, k_ref, v_ref, z_ref, g_ref, gt_ref, cw_ref, al_c_ref, dt_c_ref, al_r_ref, dt_r_ref, on_ref,
              y_ref, s_st, tail, qkv, col_d, col_b, row_d, lhs_s, o0_s, n_s, *, nt):
    ts = q_ref.shape[0]
    nct = ts // CHUNK
    w_d = q_ref.shape[1]
    h_d = w_d // HD_D
    scale = HD_D ** -0.5
    t = pl.program_id(0)
    srcs = (q_ref, k_ref, v_ref)

    def conv_block(slot, r0, first, cbk):
        ref = srcs[cbk // h_d]
        ci = slice((cbk % h_d) * HD_D, (cbk % h_d + 1) * HD_D)
        cs = slice(cbk * HD_D, (cbk + 1) * HD_D)
        cur = ref[pl.ds(r0, CHUNK), ci].astype(F32)
        prev = ref[pl.ds(pl.multiple_of(jnp.maximum(r0 - 16, 0), 16), 16), ci].astype(F32)
        xw = jnp.concatenate([jnp.where(first, tail[:, cs], prev), cur], axis=0)
        acc = cw_ref[CONV_K - 1:CONV_K, cs] * cur
        for d in range(1, CONV_K):
            acc = acc + cw_ref[CONV_K - 1 - d:CONV_K - d, cs] * xw[16 - d:16 - d + CHUNK]
        acc = _silu(acc)
        if cbk < 2 * h_d:
            acc = acc * lax.rsqrt(jnp.sum(acc * acc, axis=-1, keepdims=True) + EPS)
        qkv[slot, pl.ds(r0, CHUNK), cs] = acc

    def conv_rows(slot, rb):
        r0 = pl.multiple_of(rb * CHUNK, CHUNK)
        for cbk in range(3 * h_d):
            conv_block(slot, r0, rb == 0, cbk)

    def save_tail():
        keep = lax.rem(t + 1, nt) != 0
        for w, ref in enumerate(srcs):
            tail[:, w * w_d:(w + 1) * w_d] = jnp.where(keep, ref[ts - 16:ts, :].astype(F32), 0.0)

    @pl.when(t == 0)
    def _():
        tail[...] = jnp.zeros_like(tail)

        def rows(rb, carry):
            conv_rows(0, rb)
            return carry

        lax.fori_loop(0, nct, rows, 0)
        save_tail()

    @pl.when(t > 0)
    def _():
        cur_slot = lax.rem(t, 2)
        prv_slot = 1 - cur_slot

        @pl.when(lax.rem(t - 1, nt) == 0)
        def _():
            s_st[...] = jnp.zeros_like(s_st)

        g = g_ref[...]
        col_b[...] = _sigmoid(g)
        col_d[...] = _mm_f32(_chunk_tril(ts, CHUNK), -jnp.exp(al_c_ref[...]) * _softplus(g + dt_c_ref[...]))
        gt = gt_ref[...]
        g_r = -jnp.exp(al_r_ref[...]) * _softplus(gt[0:h_d] + dt_r_ref[...])
        row_d[...] = _mm_f32(g_r.reshape(h_d * nct, CHUNK), _triu(CHUNK)).reshape(h_d, nct, CHUNK)

        r_i, c_i = _iota2((CHUNK, CHUNK), 0), _iota2((CHUNK, CHUNK), 1)
        tril = c_i <= r_i
        strict = c_i < r_i
        hs = range(h_d)

        def group(gi, carry):
            probs = [(cc, h) for cc in range(GDN_GROUP) for h in hs]
            cidx = [gi * GDN_GROUP + cc for cc in range(GDN_GROUP)]
            rows = [pl.ds(pl.multiple_of(ci * CHUNK, CHUNK), CHUNK) for ci in cidx]
            dc_all = [col_d[r, :] for r in rows]
            beta_all = [col_b[r, :] for r in rows]
            dc = [dc_all[cc][:, h:h + 1] for cc, h in probs]
            beta = [beta_all[cc][:, h_d + h:h_d + h + 1] for cc, h in probs]
            q = [qkv[prv_slot, rows[cc], h * HD_D:(h + 1) * HD_D] for cc, h in probs]
            k = [qkv[prv_slot, rows[cc], w_d + h * HD_D:w_d + (h + 1) * HD_D] for cc, h in probs]
            v = [qkv[prv_slot, rows[cc], 2 * w_d + h * HD_D:2 * w_d + (h + 1) * HD_D] for cc, h in probs]
            ps = range(len(probs))
            gam = [jnp.exp(jnp.where(tril, dc[i] - row_d[probs[i][1], pl.ds(cidx[probs[i][0]], 1), :], NEG)) for i in ps]
            kb = [k[i] * beta[i] for i in ps]
            kk = [_mm_nt(kb[i], k[i]) for i in ps]
            t_inv = _unit_lower_inverse([jnp.where(strict, kk[i] * gam[i], 0.0) for i in ps])
            ed = [jnp.exp(dc[i]) for i in ps]
            uw = [_mm(t_inv[i], jnp.concatenate([v[i] * beta[i], kb[i] * ed[i]], axis=1)) for i in ps]
            qs = [q[i] * scale for i in ps]
            qk = [_mm_nt(qs[i], k[i]) * gam[i] for i in ps]
            kd = [k[i] * jnp.exp(dc[i][CHUNK - 1:CHUNK, :] - dc[i]) for i in ps]
            x1 = [_mm(qk[i], uw[i]) for i in ps]
            x2 = [_mm_tn(kd[i], uw[i]) for i in ps]
            for i, (cc, h) in enumerate(probs):
                lhs_s[cc, h] = jnp.concatenate([qs[i] * ed[i] - x1[i][:, HD_D:], x2[i][:, HD_D:]], axis=0).astype(BF16)
                o0_s[cc, h] = x1[i][:, :HD_D]
                n_s[cc, h] = x2[i][:, :HD_D]
            for rr in range(GDN_GROUP):
                conv_rows(cur_slot, gi * GDN_GROUP + rr)

            def step(cc, carry2):
                c = gi * GDN_GROUP + cc
                rws = pl.ds(pl.multiple_of(c * CHUNK, CHUNK), CHUNK)
                d_last = jnp.exp(col_d[pl.ds(c * CHUNK + CHUNK - 1, 1), :])
                st = [s_st[h] for h in hs]
                r = [_mm(lhs_s[cc, h], st[h]) for h in hs]
                for h in hs:
                    o = o0_s[cc, h] + r[h][:CHUNK]
                    s_st[h] = d_last[:, h:h + 1] * st[h] + n_s[cc, h] - r[h][CHUNK:]
                    on = o * lax.rsqrt(jnp.mean(o * o, axis=-1, keepdims=True) + EPS) * on_ref[...]
                    z = z_ref[rws, h * HD_D:(h + 1) * HD_D].astype(F32)
                    y_ref[rws, h * HD_D:(h + 1) * HD_D] = (on * _silu(z)).astype(y_ref.dtype)
                return carry2

            lax.fori_loop(0, GDN_GROUP, step, 0)
            return carry

        lax.fori_loop(0, nct // GDN_GROUP, group, 0)
        save_tail()


def _gdn(p3, g3, gt4, cw, al_c, dt_c, al_r, dt_r, onorm, ts):
    b, s, _ = p3.shape
    w_d = 1024
    h_d = w_d // HD_D
    nct = ts // CHUNK
    nt = s // ts
    n_tiles = b * nt

    def conv_spec(sec):
        def imap(t, sec=sec):
            tc = jnp.minimum(t, n_tiles - 1)
            return (tc // nt, tc % nt, 2 + sec)
        return pl.BlockSpec((None, ts, w_d), imap)

    def prev_map(last):
        def imap(t):
            tp = jnp.maximum(t - 1, 0)
            return (tp // nt, tp % nt, last)
        return imap

    def gt_map(t):
        tp = jnp.maximum(t - 1, 0)
        return (tp // nt, 0, tp % nt, 0)

    return pl.pallas_call(
        functools.partial(_gdn_body, nt=nt),
        grid=(n_tiles + 1,),
        in_specs=[
            conv_spec(0), conv_spec(1), conv_spec(2),
            pl.BlockSpec((None, ts, w_d), prev_map(5)),
            pl.BlockSpec((None, ts, LANES), prev_map(0)),
            pl.BlockSpec((None, 2 * h_d, nct, CHUNK), gt_map),
            pl.BlockSpec((CONV_K, 3 * w_d), lambda t: (0, 0)),
            pl.BlockSpec((1, LANES), lambda t: (0, 0)),
            pl.BlockSpec((1, LANES), lambda t: (0, 0)),
            pl.BlockSpec((h_d, 1, 1), lambda t: (0, 0, 0)),
            pl.BlockSpec((h_d, 1, 1), lambda t: (0, 0, 0)),
            pl.BlockSpec((1, HD_D), lambda t: (0, 0)),
        ],
        out_specs=pl.BlockSpec((None, ts, w_d), prev_map(0)),
        out_shape=jax.ShapeDtypeStruct((b, s, w_d), BF16),
        scratch_shapes=[
            pltpu.VMEM((h_d, HD_D, HD_D), F32),
            pltpu.VMEM((16, 3 * w_d), F32),
            pltpu.VMEM((2, ts, 3 * w_d), F32),
            pltpu.VMEM((ts, LANES), F32),
            pltpu.VMEM((ts, LANES), F32),
            pltpu.VMEM((h_d, nct, CHUNK), F32),
            pltpu.VMEM((GDN_GROUP, h_d, CHUNK + HD_D, HD_D), BF16),
            pltpu.VMEM((GDN_GROUP, h_d, CHUNK, HD_D), F32),
            pltpu.VMEM((GDN_GROUP, h_d, HD_D, HD_D), F32),
        ],
        compiler_params=_params(1),
        name="gated_delta",
    )(p3, p3, p3, p3, g3, gt4, cw, al_c, dt_c, al_r, dt_r, onorm)


def _rel_row(rel_bias):
    t = (np.arange(REL_W) + CHUNK - 1) % REL_W
    idx = np.clip(LOOKBACK * CHUNK + CHUNK - 1 - t, -REL_MAX, REL_MAX) + REL_MAX
    return rel_bias[:, idx].astype(F32)[:, None, :]


def _pad_lanes(v, n=LANES):
    return jnp.pad(v, ((0, 0), (0, n - v.shape[1])))


def _pad_rows(v, n=LANES):
    return jnp.pad(v, ((0, n - v.shape[0]), (0, 0)))


def _weight_t(w):
    return jnp.swapaxes(w, 0, 1).astype(BF16)


def _gates_t(g3, n, length):
    b, s, _ = g3.shape
    return jnp.transpose(g3[:, :, :n], (0, 2, 1)).reshape(b, n, s // length, length)


def kernel(x, ev_norm, ev_w_in, ev_if_bias, ev_qn_gain, ev_kn_gain, ev_rel_bias, ev_w_out, od_norm, od_w_in, od_conv_c_w, od_conv_c_b, od_gate_w, od_gate_b, od_lambda, od_conv_d_w, od_a_log, od_dt_bias, od_onorm, od_w_out):
    b, s, d = x.shape
    m = b * s
    half = d // 2
    tm = min(1024, m)
    ts = min(512, s)
    x2 = x.reshape(m, d)
    depth = ev_norm.shape[0] + od_norm.shape[0]
    for layer in range(depth):
        j = layer // 2
        if layer % 2 == 0:
            n_main = 9 * half
            w_t = _weight_t(ev_w_in[j])
            p, g = _inproj(x2, ev_norm[j].reshape(1, d), w_t, _pad_rows(w_t[n_main:]), n_main, tm, n_main // 4)
            p3, g3 = p.reshape(b, s, n_main), g.reshape(b, s, LANES)
            ya = _attn(p3, ev_qn_gain[j].reshape(1, HD_A), ev_kn_gain[j].reshape(1, HD_A), _rel_row(ev_rel_bias[j]))
            bias = ev_if_bias[j].astype(F32)
            yb = _mlstm(p3, g3, _gates_t(g3, 2 * H_B, MLSTM_L), _pad_lanes(bias.reshape(1, -1)), bias.reshape(-1, 1, 1), ts)
            x2 = _outproj(ya.reshape(m, half), yb.reshape(m, half), ev_w_out[j], x2, tm, 1024)
        else:
            n_main = 6 * half
            h_d = half // HD_D
            w_t = _weight_t(od_w_in[j])
            p, g = _inproj(x2, od_norm[j].reshape(1, d), w_t, _pad_rows(w_t[n_main:]), n_main, tm, n_main // 3)
            p3, g3 = p.reshape(b, s, n_main), g.reshape(b, s, LANES)
            yc = _rglru(p3, od_conv_c_w[j], od_conv_c_b[j].reshape(1, -1), od_gate_w[j].astype(BF16),
                        od_gate_b[j].reshape(1, -1), od_lambda[j].reshape(1, -1))
            al, dt = od_a_log[j].astype(F32), od_dt_bias[j].astype(F32)
            yd = _gdn(p3, g3, _gates_t(g3, 2 * h_d, CHUNK), od_conv_d_w[j], _pad_lanes(al.reshape(1, -1)),
                      _pad_lanes(dt.reshape(1, -1)), al.reshape(-1, 1, 1), dt.reshape(-1, 1, 1),
                      od_onorm[j].reshape(1, HD_D), ts)
            x2 = _outproj(yc.reshape(m, half), yd.reshape(m, half), od_w_out[j], x2, tm, 1024)
    return x2.reshape(b, s, d)
```

```python
import functools

import numpy as np
import jax
import jax.numpy as jnp
from jax import lax
from jax.experimental import pallas as pl
from jax.experimental.pallas import tpu as pltpu

F32 = jnp.float32
BF16 = jnp.bfloat16

CHUNK = 64
EPS = 1e-6
NEG = -1e30
CONV_K = 4
LOG2E = 1.4426950408889634
LANES = 128
HD_A = 128
LOOKBACK = 8
BAND = (LOOKBACK + 1) * CHUNK
REL_MAX = 256
REL_W = 640
ATTN_GROUP = 16
ATTN_HEADS = 2
H_B = 4
MLSTM_L = 256
N_BLK_C = 8
C_RG = 8.0
RG_COLS = 512
HD_D = 128
GDN_GROUP = 4

VMEM_LIMIT = 56 * 1024 * 1024


def _params(n_axes):
    return pltpu.CompilerParams(dimension_semantics=("arbitrary",) * n_axes, vmem_limit_bytes=VMEM_LIMIT)


def _mm(a, b):
    return jnp.dot(a.astype(BF16), b.astype(BF16), preferred_element_type=F32)


def _mm_nt(a, b):
    return lax.dot_general(a.astype(BF16), b.astype(BF16), (((1,), (1,)), ((), ())), preferred_element_type=F32)


def _mm_tn(a, b):
    return lax.dot_general(a.astype(BF16), b.astype(BF16), (((0,), (0,)), ((), ())), preferred_element_type=F32)


def _mm_f32(a, b):
    return jnp.dot(a, b, preferred_element_type=F32, precision=lax.Precision.HIGHEST)


def _log1p(e):
    u = 1.0 + e
    return jnp.where(u == 1.0, e, jnp.log(u) * (e / (u - 1.0)))


def _softplus(x):
    return jnp.maximum(x, 0.0) + _log1p(jnp.exp(-jnp.abs(x)))


def _sigmoid(x):
    return 1.0 / (1.0 + jnp.exp2(x * -LOG2E))


def _silu(x):
    return x * _sigmoid(x)


def _iota2(shape, axis):
    return lax.broadcasted_iota(jnp.int32, shape, axis)


def _chunk_cumsum(x, length):
    pos = _iota2(x.shape, 0) & (length - 1)
    d = 1
    while d < length:
        x = x + jnp.where(pos >= d, pltpu.roll(x, d, 0), 0.0)
        d *= 2
    return x


def _triu(n):
    r, c = _iota2((n, n), 0), _iota2((n, n), 1)
    return jnp.where(r <= c, 1.0, 0.0).astype(F32)


def _inproj_body(x_ref, g_ref, w_ref, wg_ref, o_ref, og_ref, xn_ref):
    @pl.when(pl.program_id(1) == 0)
    def _():
        x = x_ref[...]
        ms = jnp.mean(x * x, axis=-1, keepdims=True)
        xn = ((x * lax.rsqrt(ms + EPS)) * g_ref[...]).astype(BF16)
        xn_ref[...] = xn
        og_ref[...] = _mm_nt(xn, wg_ref[...])

    o_ref[...] = _mm_nt(xn_ref[...], w_ref[...]).astype(o_ref.dtype)


def _inproj(x2, g, w_t, wg_t, n, tm, tn):
    m, d = x2.shape
    return pl.pallas_call(
        _inproj_body,
        grid=(m // tm, n // tn),
        in_specs=[
            pl.BlockSpec((tm, d), lambda i, j: (i, 0)),
            pl.BlockSpec((1, d), lambda i, j: (0, 0)),
            pl.BlockSpec((tn, d), lambda i, j: (j, 0)),
            pl.BlockSpec((LANES, d), lambda i, j: (0, 0)),
        ],
        out_specs=[
            pl.BlockSpec((tm, tn), lambda i, j: (i, j)),
            pl.BlockSpec((tm, LANES), lambda i, j: (i, 0)),
        ],
        out_shape=[jax.ShapeDtypeStruct((m, n), BF16), jax.ShapeDtypeStruct((m, LANES), F32)],
        scratch_shapes=[pltpu.VMEM((tm, d), BF16)],
        compiler_params=_params(2),
        name="inproj",
    )(x2, g, w_t, wg_t)


def _outproj_body(ya_ref, yb_ref, wa_ref, wb_ref, x_ref, o_ref, wa_s, wb_s):
    @pl.when(pl.program_id(1) == 0)
    def _():
        wa_s[...] = wa_ref[...].astype(BF16)
        wb_s[...] = wb_ref[...].astype(BF16)

    acc = jnp.dot(ya_ref[...], wa_s[...], preferred_element_type=F32)
    acc = acc + jnp.dot(yb_ref[...], wb_s[...], preferred_element_type=F32)
    o_ref[...] = x_ref[...] + acc


def _outproj(ya, yb, w, x2, tm, tn):
    m, d = x2.shape
    kh = ya.shape[1]
    return pl.pallas_call(
        _outproj_body,
        grid=(d // tn, m // tm),
        in_specs=[
            pl.BlockSpec((tm, kh), lambda j, i: (i, 0)),
            pl.BlockSpec((tm, kh), lambda j, i: (i, 0)),
            pl.BlockSpec((kh, tn), lambda j, i: (0, j)),
            pl.BlockSpec((kh, tn), lambda j, i: (1, j)),
            pl.BlockSpec((tm, tn), lambda j, i: (i, j)),
        ],
        out_specs=pl.BlockSpec((tm, tn), lambda j, i: (i, j)),
        out_shape=jax.ShapeDtypeStruct((m, d), F32),
        scratch_shapes=[pltpu.VMEM((kh, tn), BF16), pltpu.VMEM((kh, tn), BF16)],
        compiler_params=_params(2),
        name="outproj",
    )(ya, yb, w, w, x2)


def _attn_body(q_ref, k_ref, v_ref, z_ref, qg_ref, kg_ref, c_ref, o_ref, qs, ks, vs, bias_s):
    s = q_ref.shape[0]
    pad = LOOKBACK * CHUNK
    col = _iota2((CHUNK, BAND), 1)
    assert ATTN_GROUP >= LOOKBACK
    for hh in range(ATTN_HEADS):
        hc = slice(hh * HD_A, (hh + 1) * HD_A)
        q = q_ref[:, hc].astype(F32)
        q = q * lax.rsqrt(jnp.mean(q * q, axis=-1, keepdims=True) + EPS) * qg_ref[...] * (HD_A ** -0.5 * LOG2E)
        qs[hh] = q.astype(BF16)
        k = k_ref[:, hc].astype(F32)
        k = k * lax.rsqrt(jnp.mean(k * k, axis=-1, keepdims=True) + EPS) * kg_ref[...]
        ks[hh, 0:pad, :] = jnp.zeros((pad, HD_A), BF16)
        ks[hh, pad:pad + s, :] = k.astype(BF16)
        vs[hh, 0:pad, :] = jnp.zeros((pad, HD_A), BF16)
        vs[hh, pad:pad + s, :] = v_ref[:, hc]
        cb = jnp.broadcast_to(c_ref[hh] * LOG2E, (CHUNK, REL_W))
        bias_s[hh] = pltpu.roll(cb, 0, 1, stride=1, stride_axis=0)[:, :BAND]

        def chunks(it, masked, hh=hh, hc=hc):
            gs = range(ATTN_GROUP)
            n = [it * ATTN_GROUP + g for g in gs]
            r0 = [pl.multiple_of(n[g] * CHUNK, CHUNK) for g in gs]
            sc = [_mm_nt(qs[hh, pl.ds(r0[g], CHUNK), :], ks[hh, pl.ds(r0[g], BAND), :]) + bias_s[hh] for g in gs]
            if masked:
                sc = [jnp.where(col >= (LOOKBACK - n[g]) * CHUNK, sc[g], NEG) for g in gs]
            p = [jnp.exp2(sc[g] - jnp.max(sc[g], axis=-1, keepdims=True)) for g in gs]
            pv = [_mm(p[g], vs[hh, pl.ds(r0[g], BAND), :]) for g in gs]
            for g in gs:
                o = pv[g] * (1.0 / jnp.sum(p[g], axis=-1, keepdims=True))
                z = z_ref[pl.ds(r0[g], CHUNK), hc].astype(F32)
                o_ref[pl.ds(r0[g], CHUNK), hc] = (o * _silu(z)).astype(o_ref.dtype)

        chunks(0, True)

        def rest(it, carry, chunks=chunks):
            chunks(it, False)
            return carry

        lax.fori_loop(1, s // (CHUNK * ATTN_GROUP), rest, 0)


def _attn(p3, qg, kg, crel):
    b, s, _ = p3.shape
    h_a = crel.shape[0]
    hp = h_a // ATTN_HEADS
    wh = ATTN_HEADS * HD_A

    def col_spec(sec):
        return pl.BlockSpec((None, s, wh), lambda bi, h, sec=sec: (bi, 0, sec * hp + h))

    return pl.pallas_call(
        _attn_body,
        grid=(b, hp),
        in_specs=[
            col_spec(0), col_spec(1), col_spec(2), col_spec(3),
            pl.BlockSpec((1, HD_A), lambda bi, h: (0, 0)),
            pl.BlockSpec((1, HD_A), lambda bi, h: (0, 0)),
            pl.BlockSpec((ATTN_HEADS, 1, REL_W), lambda bi, h: (h, 0, 0)),
        ],
        out_specs=pl.BlockSpec((None, s, wh), lambda bi, h: (bi, 0, h)),
        out_shape=jax.ShapeDtypeStruct((b, s, h_a * HD_A), BF16),
        scratch_shapes=[
            pltpu.VMEM((ATTN_HEADS, s, HD_A), BF16),
            pltpu.VMEM((ATTN_HEADS, s + LOOKBACK * CHUNK, HD_A), BF16),
            pltpu.VMEM((ATTN_HEADS, s + LOOKBACK * CHUNK, HD_A), BF16),
            pltpu.VMEM((ATTN_HEADS, CHUNK, BAND), F32),
        ],
        compiler_params=_params(2),
        name="chunk_attn",
    )(p3, p3, p3, p3, qg, kg, crel)


def _mlstm_body(q_ref, k_ref, v_ref, o_ref, z_ref, g_ref, gt_ref, bc_ref, br_ref, y_ref,
                c_st, n_st, m_st, col_b, col_i, row_b, row_i):
    ts = q_ref.shape[0]
    ln = MLSTM_L
    nct = ts // ln
    hd = q_ref.shape[1] // H_B
    scale = hd ** -0.5
    j = pl.program_id(1)
    hs = range(H_B)
    hc = [slice(h * hd, (h + 1) * hd) for h in hs]

    @pl.when(j == 0)
    def _():
        c_st[...] = jnp.zeros_like(c_st)
        n_st[...] = jnp.zeros_like(n_st)
        m_st[...] = jnp.zeros_like(m_st)
        gt = gt_ref[...] + br_ref[...]
        for h in hs:
            row_b[h] = _mm_f32(-_softplus(-gt[H_B + h]), _triu(ln)) * LOG2E
            row_i[h] = gt[h] * LOG2E

    g = g_ref[...] + bc_ref[...]
    col_i[...] = g * LOG2E
    col_b[...] = _chunk_cumsum(-_softplus(-g), ln) * LOG2E
    tril = _iota2((ln, ln), 1) <= _iota2((ln, ln), 0)
    log2_scale = float(np.log2(scale))
    m = [m_st[h][0:1, 0:1] for h in hs]
    cm = [c_st[h] for h in hs]
    nm = [n_st[h][0:1, :] for h in hs]
    chs = [(c, h) for c in range(nct) for h in hs]
    rows = [slice(c * ln, (c + 1) * ln) for c in range(nct)]
    bc_all = [col_b[r, :] for r in rows]
    li_all = [col_i[r, :] for r in rows]
    bc = {(c, h): bc_all[c][:, H_B + h:H_B + h + 1] for c, h in chs}
    lic = {(c, h): li_all[c][:, h:h + 1] for c, h in chs}
    rl = {(c, h): row_i[h, pl.ds(j * nct + c, 1), :] - row_b[h, pl.ds(j * nct + c, 1), :] for c, h in chs}
    b_last = {ch: bc[ch][ln - 1:ln, :] for ch in chs}
    m_in, m_new = {}, {}
    for c, h in chs:
        m_in[c, h] = m[h]
        m_new[c, h] = jnp.maximum(b_last[c, h] + m[h], jnp.max(b_last[c, h] + rl[c, h], axis=-1, keepdims=True))
        m[h] = m_new[c, h]
    qn = {(c, h): q_ref[rows[c], hc[h]] for c, h in chs}
    kn = {(c, h): k_ref[rows[c], hc[h]] for c, h in chs}
    vn = {(c, h): v_ref[rows[c], hc[h]] for c, h in chs}
    qk = {ch: _mm_nt(qn[ch], kn[ch]) for ch in chs}
    dmat = {ch: jnp.where(tril, bc[ch] + rl[ch], NEG) for ch in chs}
    mt = {ch: jnp.maximum(bc[ch] + m_in[ch], jnp.max(dmat[ch], axis=-1, keepdims=True)) for ch in chs}
    cmt = {ch: mt[ch] - log2_scale for ch in chs}
    w_inter = {ch: jnp.exp2(bc[ch] + m_in[ch] - cmt[ch]) for ch in chs}
    p = {ch: jnp.exp2(dmat[ch] - cmt[ch]) * qk[ch] for ch in chs}
    pv = {ch: _mm(p[ch], vn[ch]) for ch in chs}
    ws_c = {ch: jnp.exp2(b_last[ch] - bc[ch] + lic[ch] - m_new[ch]).astype(BF16) for ch in chs}
    ws_r = {ch: jnp.exp2(b_last[ch] + rl[ch] - m_new[ch]) for ch in chs}
    upd = {ch: _mm_tn(kn[ch] * ws_c[ch], vn[ch]) for ch in chs}
    n_add = {ch: _mm(jnp.broadcast_to(ws_r[ch], (8, ln)), kn[ch])[0:1, :] for ch in chs}
    psum = {ch: jnp.sum(p[ch], axis=-1, keepdims=True) for ch in chs}
    gate = {}
    for c, h in chs:
        og = o_ref[rows[c], hc[h]].astype(F32)
        zg = z_ref[rows[c], hc[h]].astype(F32)
        gate[c, h] = zg * (1.0 / ((1.0 + jnp.exp2(og * -LOG2E)) * (1.0 + jnp.exp2(zg * -LOG2E))))
    for c in range(nct):
        qc = {h: _mm(qn[c, h], cm[h]) for h in hs}
        qn_dot = {h: _mm_nt(qn[c, h], jnp.broadcast_to(nm[h], (8, hd)))[:, 0:1] for h in hs}
        for h in hs:
            num = w_inter[c, h] * qc[h] + pv[c, h]
            den = w_inter[c, h] * qn_dot[h] + psum[c, h]
            hout = num * (1.0 / jnp.maximum(jnp.abs(den), jnp.exp2(-mt[c, h])))
            y_ref[rows[c], hc[h]] = (hout * gate[c, h]).astype(y_ref.dtype)
            w_c = jnp.exp2(b_last[c, h] + m_in[c, h] - m_new[c, h])
            cm[h] = w_c * cm[h] + upd[c, h]
            nm[h] = w_c * nm[h] + n_add[c, h]
    for h in hs:
        c_st[h] = cm[h]
        n_st[h] = jnp.broadcast_to(nm[h], n_st.shape[1:])
        m_st[h] = jnp.broadcast_to(m[h], m_st.shape[1:])


def _mlstm(p3, g3, gt4, bias_col, bias_row, ts):
    b, s, _ = p3.shape
    w_b = 1024
    hd = w_b // H_B
    ncs = s // MLSTM_L

    def col_spec(sec):
        return pl.BlockSpec((None, ts, w_b), lambda bi, j, sec=sec: (bi, j, 4 + sec))

    return pl.pallas_call(
        _mlstm_body,
        grid=(b, s // ts),
        in_specs=[
            col_spec(0), col_spec(1), col_spec(2), col_spec(3), col_spec(4),
            pl.BlockSpec((None, ts, LANES), lambda bi, j: (bi, j, 0)),
            pl.BlockSpec((None, 2 * H_B, ncs, MLSTM_L), lambda bi, j: (bi, 0, 0, 0)),
            pl.BlockSpec((1, LANES), lambda bi, j: (0, 0)),
            pl.BlockSpec((2 * H_B, 1, 1), lambda bi, j: (0, 0, 0)),
        ],
        out_specs=pl.BlockSpec((None, ts, w_b), lambda bi, j: (bi, j, 0)),
        out_shape=jax.ShapeDtypeStruct((b, s, w_b), BF16),
        scratch_shapes=[
            pltpu.VMEM((H_B, hd, hd), F32),
            pltpu.VMEM((H_B, 8, hd), F32),
            pltpu.VMEM((H_B, 8, LANES), F32),
            pltpu.VMEM((ts, LANES), F32),
            pltpu.VMEM((ts, LANES), F32),
            pltpu.VMEM((H_B, ncs, MLSTM_L), F32),
            pltpu.VMEM((H_B, ncs, MLSTM_L), F32),
        ],
        compiler_params=_params(2),
        name="mlstm",
    )(p3, p3, p3, p3, p3, g3, gt4, bias_col, bias_row)


def _rglru_body(x_ref, z_ref, cw_ref, cb_ref, gw_ref, gbr_ref, gbi_ref, lam_ref, y_ref, a_s, b_s):
    s, wc = x_ref.shape
    blk = gw_ref.shape[1]
    row8 = _iota2((8, blk), 0)
    sub = _iota2((s // 8, 8, blk), 1)
    for n in range(wc // blk):
        cs = slice(n * blk, (n + 1) * blk)
        x = x_ref[:, cs].astype(F32)
        cw = cw_ref[:, cs]
        xc = cw[CONV_K - 1:CONV_K, :] * x + cb_ref[:, cs]
        for d in range(1, CONV_K):
            xs = pltpu.roll(x, d, 0)
            xs = jnp.concatenate([jnp.where(row8 >= d, xs[0:8], 0.0), xs[8:]], axis=0)
            xc = xc + cw[CONV_K - 1 - d:CONV_K - d, :] * xs
        gates = _mm(xc, gw_ref[n])
        r = _sigmoid(gates[:, :blk] + gbr_ref[:, cs])
        i = _sigmoid(gates[:, blk:] + gbi_ref[:, cs])
        log_a = -C_RG * r * _softplus(-lam_ref[:, cs])
        a = jnp.exp(log_a)
        var = -jnp.tanh(log_a) * (a * a + 1.0)
        bb = jnp.where(var > 0.0, var * lax.rsqrt(var), 0.0) * (i * xc)
        a = a.reshape(s // 8, 8, blk)
        bb = bb.reshape(s // 8, 8, blk)
        for d in (1, 2, 4):
            keep = sub >= d
            a_sh = jnp.where(keep, pltpu.roll(a, d, 1), 1.0)
            b_sh = jnp.where(keep, pltpu.roll(bb, d, 1), 0.0)
            bb = a * b_sh + bb
            a = a * a_sh
        a_s[:, cs] = a.reshape(s, blk)
        b_s[:, cs] = bb.reshape(s, blk)

    def group(gi, carry):
        rows = pl.ds(pl.multiple_of(gi * 8, 8), 8)
        h = a_s[rows, :] * carry + b_s[rows, :]
        b_s[rows, :] = h
        return jnp.broadcast_to(h[7:8, :], h.shape)

    lax.fori_loop(0, s // 8, group, jnp.zeros((8, wc), F32), unroll=8)
    z = z_ref[...].astype(F32)
    y_ref[...] = (b_s[...] * _silu(z)).astype(y_ref.dtype)


def _rglru(p3, cw, cb, gw, gb, lam):
    b, s, _ = p3.shape
    w_c = cw.shape[1]
    blk = w_c // N_BLK_C
    nsp = w_c // RG_COLS
    return pl.pallas_call(
        _rglru_body,
        grid=(b, nsp),
        in_specs=[
            pl.BlockSpec((None, s, RG_COLS), lambda bi, n: (bi, 0, n)),
            pl.BlockSpec((None, s, RG_COLS), lambda bi, n: (bi, 0, nsp + n)),
            pl.BlockSpec((CONV_K, RG_COLS), lambda bi, n: (0, n)),
            pl.BlockSpec((1, RG_COLS), lambda bi, n: (0, n)),
            pl.BlockSpec((RG_COLS // blk, blk, 2 * blk), lambda bi, n: (n, 0, 0)),
            pl.BlockSpec((1, RG_COLS), lambda bi, n: (0, n)),
            pl.BlockSpec((1, RG_COLS), lambda bi, n: (0, nsp + n)),
            pl.BlockSpec((1, RG_COLS), lambda bi, n: (0, n)),
        ],
        out_specs=pl.BlockSpec((None, s, RG_COLS), lambda bi, n: (bi, 0, n)),
        out_shape=jax.ShapeDtypeStruct((b, s, w_c), BF16),
        scratch_shapes=[pltpu.VMEM((s, RG_COLS), F32), pltpu.VMEM((s, RG_COLS), F32)],
        compiler_params=_params(2),
        name="rglru",
    )(p3, p3, cw, cb, gw, gb, gb, lam)


def _unit_lower_inverse(a_list):
    r, c = _iota2((CHUNK, CHUNK), 0), _iota2((CHUNK, CHUNK), 1)
    eye = jnp.where(r == c, 1.0, 0.0).astype(F32)

    def pair_mask(sh):
        rb, cb_ = r >> sh, c >> sh
        return ((rb & 1) == 1) & (cb_ == rb - 1)

    ts = [eye - jnp.where(pair_mask(0), a, 0.0) for a in a_list]
    for sh in range(1, 6):
        mask = pair_mask(sh)
        x1 = [_mm(t, jnp.where(mask, a, 0.0)) for t, a in zip(ts, a_list)]
        x2 = [_mm(x, t) for x, t in zip(x1, ts)]
        ts = [t - x for t, x in zip(ts, x2)]
    return ts


def _gdn_body(q_ref, k_ref, v_ref, z_ref, g_ref, gt_ref, cw_ref, al_c_ref, dt_c_ref, al_r_ref, dt_r_ref, on_ref,
              y_ref, s_st, tail, qkv, col_d, col_b, row_d, lhs_s, o0_s, n_s, *, nt):
    ts = q_ref.shape[0]
    nct = ts // CHUNK
    w_d = q_ref.shape[1]
    h_d = w_d // HD_D
    scale = HD_D ** -0.5
    t = pl.program_id(0)
    srcs = (q_ref, k_ref, v_ref)

    def conv_block(slot, r0, first, cbk):
        ref = srcs[cbk // h_d]
        ci = slice((cbk % h_d) * HD_D, (cbk % h_d + 1) * HD_D)
        cs = slice(cbk * HD_D, (cbk + 1) * HD_D)
        cur = ref[pl.ds(r0, CHUNK), ci].astype(F32)
        prev = ref[pl.ds(pl.multiple_of(jnp.maximum(r0 - 16, 0), 16), 16), ci].astype(F32)
        xw = jnp.concatenate([jnp.where(first, tail[:, cs], prev), cur], axis=0)
        acc = cw_ref[CONV_K - 1:CONV_K, cs] * cur
        for d in range(1, CONV_K):
            acc = acc + cw_ref[CONV_K - 1 - d:CONV_K - d, cs] * xw[16 - d:16 - d + CHUNK]
        acc = _silu(acc)
        if cbk < 2 * h_d:
            acc = acc * lax.rsqrt(jnp.sum(acc * acc, axis=-1, keepdims=True) + EPS)
        qkv[slot, pl.ds(r0, CHUNK), cs] = acc

    def conv_rows(slot, rb):
        r0 = pl.multiple_of(rb * CHUNK, CHUNK)
        for cbk in range(3 * h_d):
            conv_block(slot, r0, rb == 0, cbk)

    def save_tail():
        keep = lax.rem(t + 1, nt) != 0
        for w, ref in enumerate(srcs):
            tail[:, w * w_d:(w + 1) * w_d] = jnp.where(keep, ref[ts - 16:ts, :].astype(F32), 0.0)

    @pl.when(t == 0)
    def _():
        tail[...] = jnp.zeros_like(tail)

        def rows(rb, carry):
            conv_rows(0, rb)
            return carry

        lax.fori_loop(0, nct, rows, 0)
        save_tail()

    @pl.when(t > 0)
    def _():
        cur_slot = lax.rem(t, 2)
        prv_slot = 1 - cur_slot

        @pl.when(lax.rem(t - 1, nt) == 0)
        def _():
            s_st[...] = jnp.zeros_like(s_st)

        g = g_ref[...]
        col_b[...] = _sigmoid(g)
        col_d[...] = _chunk_cumsum(-jnp.exp(al_c_ref[...]) * _softplus(g + dt_c_ref[...]), CHUNK)
        gt = gt_ref[...]
        g_r = -jnp.exp(al_r_ref[...]) * _softplus(gt[0:h_d] + dt_r_ref[...])
        row_d[...] = _mm_f32(g_r.reshape(h_d * nct, CHUNK), _triu(CHUNK)).reshape(h_d, nct, CHUNK)

        r_i, c_i = _iota2((CHUNK, CHUNK), 0), _iota2((CHUNK, CHUNK), 1)
        tril = c_i <= r_i
        strict = c_i < r_i
        hs = range(h_d)

        def group(gi, carry):
            probs = [(cc, h) for cc in range(GDN_GROUP) for h in hs]
            cidx = [gi * GDN_GROUP + cc for cc in range(GDN_GROUP)]
            rows = [pl.ds(pl.multiple_of(ci * CHUNK, CHUNK), CHUNK) for ci in cidx]
            dc_all = [col_d[r, :] for r in rows]
            beta_all = [col_b[r, :] for r in rows]
            dc = [dc_all[cc][:, h:h + 1] for cc, h in probs]
            beta = [beta_all[cc][:, h_d + h:h_d + h + 1] for cc, h in probs]
            q = [qkv[prv_slot, rows[cc], h * HD_D:(h + 1) * HD_D] for cc, h in probs]
            k = [qkv[prv_slot, rows[cc], w_d + h * HD_D:w_d + (h + 1) * HD_D] for cc, h in probs]
            v = [qkv[prv_slot, rows[cc], 2 * w_d + h * HD_D:2 * w_d + (h + 1) * HD_D] for cc, h in probs]
            ps = range(len(probs))
            gam = [jnp.exp(jnp.where(tril, dc[i] - row_d[probs[i][1], pl.ds(cidx[probs[i][0]], 1), :], NEG)) for i in ps]
            kb = [k[i] * beta[i] for i in ps]
            kk = [_mm_nt(kb[i], k[i]) for i in ps]
            t_inv = _unit_lower_inverse([jnp.where(strict, kk[i] * gam[i], 0.0) for i in ps])
            ed = [jnp.exp(dc[i]) for i in ps]
            uw = [_mm(t_inv[i], jnp.concatenate([v[i] * beta[i], kb[i] * ed[i]], axis=1)) for i in ps]
            qs = [q[i] * scale for i in ps]
            qk = [_mm_nt(qs[i], k[i]) * gam[i] for i in ps]
            kd = [k[i] * jnp.exp(dc[i][CHUNK - 1:CHUNK, :] - dc[i]) for i in ps]
            x1 = [_mm(qk[i], uw[i]) for i in ps]
            x2 = [_mm_tn(kd[i], uw[i]) for i in ps]
            for i, (cc, h) in enumerate(probs):
                lhs_s[cc, h] = jnp.concatenate([qs[i] * ed[i] - x1[i][:, HD_D:], x2[i][:, HD_D:]], axis=0).astype(BF16)
                o0_s[cc, h] = x1[i][:, :HD_D]
                n_s[cc, h] = x2[i][:, :HD_D]
            for rr in range(GDN_GROUP):
                conv_rows(cur_slot, gi * GDN_GROUP + rr)

            def step(cc, carry2):
                c = gi * GDN_GROUP + cc
                rws = pl.ds(pl.multiple_of(c * CHUNK, CHUNK), CHUNK)
                d_last = jnp.exp(col_d[pl.ds(c * CHUNK + CHUNK - 1, 1), :])
                st = [s_st[h] for h in hs]
                r = [_mm(lhs_s[cc, h], st[h]) for h in hs]
                for h in hs:
                    o = o0_s[cc, h] + r[h][:CHUNK]
                    s_st[h] = d_last[:, h:h + 1] * st[h] + n_s[cc, h] - r[h][CHUNK:]
                    on = o * lax.rsqrt(jnp.mean(o * o, axis=-1, keepdims=True) + EPS) * on_ref[...]
                    z = z_ref[rws, h * HD_D:(h + 1) * HD_D].astype(F32)
                    y_ref[rws, h * HD_D:(h + 1) * HD_D] = (on * _silu(z)).astype(y_ref.dtype)
                return carry2

            lax.fori_loop(0, GDN_GROUP, step, 0)
            return carry

        lax.fori_loop(0, nct // GDN_GROUP, group, 0)
        save_tail()


def _gdn(p3, g3, gt4, cw, al_c, dt_c, al_r, dt_r, onorm, ts):
    b, s, _ = p3.shape
    w_d = 1024
    h_d = w_d // HD_D
    nct = ts // CHUNK
    nt = s // ts
    n_tiles = b * nt

    def conv_spec(sec):
        def imap(t, sec=sec):
            tc = jnp.minimum(t, n_tiles - 1)
            return (tc // nt, tc % nt, 2 + sec)
        return pl.BlockSpec((None, ts, w_d), imap)

    def prev_map(last):
        def imap(t):
            tp = jnp.maximum(t - 1, 0)
            return (tp // nt, tp % nt, last)
        return imap

    def gt_map(t):
        tp = jnp.maximum(t - 1, 0)
        return (tp // nt, 0, tp % nt, 0)

    return pl.pallas_call(
        functools.partial(_gdn_body, nt=nt),
        grid=(n_tiles + 1,),
        in_specs=[
            conv_spec(0), conv_spec(1), conv_spec(2),
            pl.BlockSpec((None, ts, w_d), prev_map(5)),
            pl.BlockSpec((None, ts, LANES), prev_map(0)),
            pl.BlockSpec((None, 2 * h_d, nct, CHUNK), gt_map),
            pl.BlockSpec((CONV_K, 3 * w_d), lambda t: (0, 0)),
            pl.BlockSpec((1, LANES), lambda t: (0, 0)),
            pl.BlockSpec((1, LANES), lambda t: (0, 0)),
            pl.BlockSpec((h_d, 1, 1), lambda t: (0, 0, 0)),
            pl.BlockSpec((h_d, 1, 1), lambda t: (0, 0, 0)),
            pl.BlockSpec((1, HD_D), lambda t: (0, 0)),
        ],
        out_specs=pl.BlockSpec((None, ts, w_d), prev_map(0)),
        out_shape=jax.ShapeDtypeStruct((b, s, w_d), BF16),
        scratch_shapes=[
            pltpu.VMEM((h_d, HD_D, HD_D), F32),
            pltpu.VMEM((16, 3 * w_d), F32),
            pltpu.VMEM((2, ts, 3 * w_d), F32),
            pltpu.VMEM((ts, LANES), F32),
            pltpu.VMEM((ts, LANES), F32),
            pltpu.VMEM((h_d, nct, CHUNK), F32),
            pltpu.VMEM((GDN_GROUP, h_d, CHUNK + HD_D, HD_D), BF16),
            pltpu.VMEM((GDN_GROUP, h_d, CHUNK, HD_D), F32),
            pltpu.VMEM((GDN_GROUP, h_d, HD_D, HD_D), F32),
        ],
        compiler_params=_params(1),
        name="gated_delta",
    )(p3, p3, p3, p3, g3, gt4, cw, al_c, dt_c, al_r, dt_r, onorm)


def _rel_row(rel_bias):
    t = (np.arange(REL_W) + CHUNK - 1) % REL_W
    idx = np.clip(LOOKBACK * CHUNK + CHUNK - 1 - t, -REL_MAX, REL_MAX) + REL_MAX
    return rel_bias[:, idx].astype(F32)[:, None, :]


def _pad_lanes(v, n=LANES):
    return jnp.pad(v, ((0, 0), (0, n - v.shape[1])))


def _pad_rows(v, n=LANES):
    return jnp.pad(v, ((0, n - v.shape[0]), (0, 0)))


def _weight_t(w):
    return jnp.swapaxes(w, 0, 1).astype(BF16)


def _gates_t(g3, n, length):
    b, s, _ = g3.shape
    return jnp.transpose(g3[:, :, :n], (0, 2, 1)).reshape(b, n, s // length, length)


def kernel(x, ev_norm, ev_w_in, ev_if_bias, ev_qn_gain, ev_kn_gain, ev_rel_bias, ev_w_out, od_norm, od_w_in, od_conv_c_w, od_conv_c_b, od_gate_w, od_gate_b, od_lambda, od_conv_d_w, od_a_log, od_dt_bias, od_onorm, od_w_out):
    b, s, d = x.shape
    m = b * s
    half = d // 2
    tm = min(1024, m)
    ts = min(512, s)
    x2 = x.reshape(m, d)
    depth = ev_norm.shape[0] + od_norm.shape[0]
    for layer in range(depth):
        j = layer // 2
        if layer % 2 == 0:
            n_main = 9 * half
            w_t = _weight_t(ev_w_in[j])
            p, g = _inproj(x2, ev_norm[j].reshape(1, d), w_t, _pad_rows(w_t[n_main:]), n_main, tm, n_main // 4)
            p3, g3 = p.reshape(b, s, n_main), g.reshape(b, s, LANES)
            ya = _attn(p3, ev_qn_gain[j].reshape(1, HD_A), ev_kn_gain[j].reshape(1, HD_A), _rel_row(ev_rel_bias[j]))
            bias = ev_if_bias[j].astype(F32)
            yb = _mlstm(p3, g3, _gates_t(g3, 2 * H_B, MLSTM_L), _pad_lanes(bias.reshape(1, -1)), bias.reshape(-1, 1, 1), ts)
            x2 = _outproj(ya.reshape(m, half), yb.reshape(m, half), ev_w_out[j], x2, tm, 1024)
        else:
            n_main = 6 * half
            h_d = half // HD_D
            w_t = _weight_t(od_w_in[j])
            p, g = _inproj(x2, od_norm[j].reshape(1, d), w_t, _pad_rows(w_t[n_main:]), n_main, tm, n_main // 3)
            p3, g3 = p.reshape(b, s, n_main), g.reshape(b, s, LANES)
            yc = _rglru(p3, od_conv_c_w[j], od_conv_c_b[j].reshape(1, -1), od_gate_w[j].astype(BF16),
                        od_gate_b[j].reshape(1, -1), od_lambda[j].reshape(1, -1))
            al, dt = od_a_log[j].astype(F32), od_dt_bias[j].astype(F32)
            yd = _gdn(p3, g3, _gates_t(g3, 2 * h_d, CHUNK), od_conv_d_w[j], _pad_lanes(al.reshape(1, -1)),
                      _pad_lanes(dt.reshape(1, -1)), al.reshape(-1, 1, 1), dt.reshape(-1, 1, 1),
                      od_onorm[j].reshape(1, HD_D), ts)
            x2 = _outproj(yc.reshape(m, half), yd.reshape(m, half), od_w_out[j], x2, tm, 1024)
    return x2.reshape(b, s, d)
```

```python
import functools

import numpy as np
import jax
import jax.numpy as jnp
from jax import lax
from jax.experimental import pallas as pl
from jax.experimental.pallas import tpu as pltpu

F32 = jnp.float32
BF16 = jnp.bfloat16

CHUNK = 64
EPS = 1e-6
NEG = -1e30
CONV_K = 4
LOG2E = 1.4426950408889634
LANES = 128
HD_A = 128
LOOKBACK = 8
BAND = (LOOKBACK + 1) * CHUNK
REL_MAX = 256
REL_W = 640
ATTN_GROUP = 16
ATTN_HEADS = 2
H_B = 4
MLSTM_L = 256
N_BLK_C = 8
C_RG = 8.0
RG_COLS = 512
HD_D = 128
GDN_GROUP = 4

V7X_VMEM_BYTES = 64 * 1024 * 1024
VMEM_LIMIT = V7X_VMEM_BYTES - 8 * 1024 * 1024


def _params(n_axes):
    return pltpu.CompilerParams(dimension_semantics=("arbitrary",) * n_axes, vmem_limit_bytes=VMEM_LIMIT)


def _mm(a, b):
    return jnp.dot(a.astype(BF16), b.astype(BF16), preferred_element_type=F32)


def _mm_nt(a, b):
    return lax.dot_general(a.astype(BF16), b.astype(BF16), (((1,), (1,)), ((), ())), preferred_element_type=F32)


def _mm_tn(a, b):
    return lax.dot_general(a.astype(BF16), b.astype(BF16), (((0,), (0,)), ((), ())), preferred_element_type=F32)


def _mm_f32(a, b):
    return jnp.dot(a, b, preferred_element_type=F32, precision=lax.Precision.HIGHEST)


def _log1p(e):
    u = 1.0 + e
    return jnp.where(u == 1.0, e, jnp.log(u) * (e / (u - 1.0)))


def _softplus(x):
    return jnp.maximum(x, 0.0) + _log1p(jnp.exp(-jnp.abs(x)))


def _sigmoid(x):
    return 1.0 / (1.0 + jnp.exp2(x * -LOG2E))


def _silu(x):
    return x * _sigmoid(x)


def _iota2(shape, axis):
    return lax.broadcasted_iota(jnp.int32, shape, axis)


def _chunk_cumsum(x, length):
    pos = _iota2(x.shape, 0) & (length - 1)
    d = 1
    while d < length:
        x = x + jnp.where(pos >= d, pltpu.roll(x, d, 0), 0.0)
        d *= 2
    return x


def _triu(n):
    r, c = _iota2((n, n), 0), _iota2((n, n), 1)
    return jnp.where(r <= c, 1.0, 0.0).astype(F32)


def _inproj_body(x_ref, g_ref, w_ref, wg_ref, o_ref, og_ref, xn_ref):
    @pl.when(pl.program_id(1) == 0)
    def _():
        x = x_ref[...]
        ms = jnp.mean(x * x, axis=-1, keepdims=True)
        xn = ((x * lax.rsqrt(ms + EPS)) * g_ref[...]).astype(BF16)
        xn_ref[...] = xn
        og_ref[...] = _mm_nt(xn, wg_ref[...])

    o_ref[...] = _mm_nt(xn_ref[...], w_ref[...]).astype(o_ref.dtype)


def _inproj(x2, g, w_t, wg_t, n, tm, tn):
    m, d = x2.shape
    return pl.pallas_call(
        _inproj_body,
        grid=(m // tm, n // tn),
        in_specs=[
            pl.BlockSpec((tm, d), lambda i, j: (i, 0)),
            pl.BlockSpec((1, d), lambda i, j: (0, 0)),
            pl.BlockSpec((tn, d), lambda i, j: (j, 0)),
            pl.BlockSpec((LANES, d), lambda i, j: (0, 0)),
        ],
        out_specs=[
            pl.BlockSpec((tm, tn), lambda i, j: (i, j)),
            pl.BlockSpec((tm, LANES), lambda i, j: (i, 0)),
        ],
        out_shape=[jax.ShapeDtypeStruct((m, n), BF16), jax.ShapeDtypeStruct((m, LANES), F32)],
        scratch_shapes=[pltpu.VMEM((tm, d), BF16)],
        compiler_params=_params(2),
        name="inproj",
    )(x2, g, w_t, wg_t)


def _outproj_body(ya_ref, yb_ref, wa_ref, wb_ref, x_ref, o_ref, wa_s, wb_s):
    @pl.when(pl.program_id(1) == 0)
    def _():
        wa_s[...] = wa_ref[...].astype(BF16)
        wb_s[...] = wb_ref[...].astype(BF16)

    acc = jnp.dot(ya_ref[...], wa_s[...], preferred_element_type=F32)
    acc = acc + jnp.dot(yb_ref[...], wb_s[...], preferred_element_type=F32)
    o_ref[...] = x_ref[...] + acc


def _outproj(ya, yb, w, x2, tm, tn):
    m, d = x2.shape
    kh = ya.shape[1]
    return pl.pallas_call(
        _outproj_body,
        grid=(d // tn, m // tm),
        in_specs=[
            pl.BlockSpec((tm, kh), lambda j, i: (i, 0)),
            pl.BlockSpec((tm, kh), lambda j, i: (i, 0)),
            pl.BlockSpec((kh, tn), lambda j, i: (0, j)),
            pl.BlockSpec((kh, tn), lambda j, i: (1, j)),
            pl.BlockSpec((tm, tn), lambda j, i: (i, j)),
        ],
        out_specs=pl.BlockSpec((tm, tn), lambda j, i: (i, j)),
        out_shape=jax.ShapeDtypeStruct((m, d), F32),
        scratch_shapes=[pltpu.VMEM((kh, tn), BF16), pltpu.VMEM((kh, tn), BF16)],
        compiler_params=_params(2),
        name="outproj",
    )(ya, yb, w, w, x2)


def _attn_body(q_ref, k_ref, v_ref, z_ref, qg_ref, kg_ref, c_ref, o_ref, qs, ks, vs, bias_s):
    s = q_ref.shape[0]
    pad = LOOKBACK * CHUNK
    col = _iota2((CHUNK, BAND), 1)
    assert ATTN_GROUP >= LOOKBACK
    for hh in range(ATTN_HEADS):
        hc = slice(hh * HD_A, (hh + 1) * HD_A)
        q = q_ref[:, hc].astype(F32)
        q = q * lax.rsqrt(jnp.mean(q * q, axis=-1, keepdims=True) + EPS) * qg_ref[...] * (HD_A ** -0.5 * LOG2E)
        qs[hh] = q.astype(BF16)
        k = k_ref[:, hc].astype(F32)
        k = k * lax.rsqrt(jnp.mean(k * k, axis=-1, keepdims=True) + EPS) * kg_ref[...]
        ks[hh, 0:pad, :] = jnp.zeros((pad, HD_A), BF16)
        ks[hh, pad:pad + s, :] = k.astype(BF16)
        vs[hh, 0:pad, :] = jnp.zeros((pad, HD_A), BF16)
        vs[hh, pad:pad + s, :] = v_ref[:, hc]
        cb = jnp.broadcast_to(c_ref[hh] * LOG2E, (CHUNK, REL_W))
        bias_s[hh] = pltpu.roll(cb, 0, 1, stride=1, stride_axis=0)[:, :BAND]

        def chunks(it, masked, hh=hh, hc=hc):
            gs = range(ATTN_GROUP)
            n = [it * ATTN_GROUP + g for g in gs]
            r0 = [pl.multiple_of(n[g] * CHUNK, CHUNK) for g in gs]
            sc = [_mm_nt(qs[hh, pl.ds(r0[g], CHUNK), :], ks[hh, pl.ds(r0[g], BAND), :]) + bias_s[hh] for g in gs]
            if masked:
                sc = [jnp.where(col >= (LOOKBACK - n[g]) * CHUNK, sc[g], NEG) for g in gs]
            p = [jnp.exp2(sc[g] - jnp.max(sc[g], axis=-1, keepdims=True)) for g in gs]
            pv = [_mm(p[g], vs[hh, pl.ds(r0[g], BAND), :]) for g in gs]
            for g in gs:
                o = pv[g] * (1.0 / jnp.sum(p[g], axis=-1, keepdims=True))
                z = z_ref[pl.ds(r0[g], CHUNK), hc].astype(F32)
                o_ref[pl.ds(r0[g], CHUNK), hc] = (o * _silu(z)).astype(o_ref.dtype)

        chunks(0, True)

        def rest(it, carry, chunks=chunks):
            chunks(it, False)
            return carry

        lax.fori_loop(1, s // (CHUNK * ATTN_GROUP), rest, 0)


def _attn(p3, qg, kg, crel):
    b, s, _ = p3.shape
    h_a = crel.shape[0]
    hp = h_a // ATTN_HEADS
    wh = ATTN_HEADS * HD_A

    def col_spec(sec):
        return pl.BlockSpec((None, s, wh), lambda bi, h, sec=sec: (bi, 0, sec * hp + h))

    return pl.pallas_call(
        _attn_body,
        grid=(b, hp),
        in_specs=[
            col_spec(0), col_spec(1), col_spec(2), col_spec(3),
            pl.BlockSpec((1, HD_A), lambda bi, h: (0, 0)),
            pl.BlockSpec((1, HD_A), lambda bi, h: (0, 0)),
            pl.BlockSpec((ATTN_HEADS, 1, REL_W), lambda bi, h: (h, 0, 0)),
        ],
        out_specs=pl.BlockSpec((None, s, wh), lambda bi, h: (bi, 0, h)),
        out_shape=jax.ShapeDtypeStruct((b, s, h_a * HD_A), BF16),
        scratch_shapes=[
            pltpu.VMEM((ATTN_HEADS, s, HD_A), BF16),
            pltpu.VMEM((ATTN_HEADS, s + LOOKBACK * CHUNK, HD_A), BF16),
            pltpu.VMEM((ATTN_HEADS, s + LOOKBACK * CHUNK, HD_A), BF16),
            pltpu.VMEM((ATTN_HEADS, CHUNK, BAND), F32),
        ],
        compiler_params=_params(2),
        name="chunk_attn",
    )(p3, p3, p3, p3, qg, kg, crel)


def _mlstm_body(q_ref, k_ref, v_ref, o_ref, z_ref, g_ref, gt_ref, bc_ref, br_ref, y_ref,
                c_st, n_st, m_st, col_b, col_i, row_b, row_i):
    ts = q_ref.shape[0]
    ln = MLSTM_L
    nct = ts // ln
    hd = q_ref.shape[1] // H_B
    scale = hd ** -0.5
    j = pl.program_id(1)
    hs = range(H_B)
    hc = [slice(h * hd, (h + 1) * hd) for h in hs]

    @pl.when(j == 0)
    def _():
        c_st[...] = jnp.zeros_like(c_st)
        n_st[...] = jnp.zeros_like(n_st)
        m_st[...] = jnp.zeros_like(m_st)
        gt = gt_ref[...] + br_ref[...]
        for h in hs:
            row_b[h] = _mm_f32(-_softplus(-gt[H_B + h]), _triu(ln)) * LOG2E
            row_i[h] = gt[h] * LOG2E

    g = g_ref[...] + bc_ref[...]
    col_i[...] = g * LOG2E
    col_b[...] = _chunk_cumsum(-_softplus(-g), ln) * LOG2E
    tril = _iota2((ln, ln), 1) <= _iota2((ln, ln), 0)
    log2_scale = float(np.log2(scale))
    m = [m_st[h][0:1, 0:1] for h in hs]
    cm = [c_st[h] for h in hs]
    nm = [n_st[h][0:1, :] for h in hs]
    chs = [(c, h) for c in range(nct) for h in hs]
    rows = [slice(c * ln, (c + 1) * ln) for c in range(nct)]
    bc_all = [col_b[r, :] for r in rows]
    li_all = [col_i[r, :] for r in rows]
    bc = {(c, h): bc_all[c][:, H_B + h:H_B + h + 1] for c, h in chs}
    lic = {(c, h): li_all[c][:, h:h + 1] for c, h in chs}
    rl = {(c, h): row_i[h, pl.ds(j * nct + c, 1), :] - row_b[h, pl.ds(j * nct + c, 1), :] for c, h in chs}
    b_last = {ch: bc[ch][ln - 1:ln, :] for ch in chs}
    m_in, m_new = {}, {}
    for c, h in chs:
        m_in[c, h] = m[h]
        m_new[c, h] = jnp.maximum(b_last[c, h] + m[h], jnp.max(b_last[c, h] + rl[c, h], axis=-1, keepdims=True))
        m[h] = m_new[c, h]
    qn = {(c, h): q_ref[rows[c], hc[h]] for c, h in chs}
    kn = {(c, h): k_ref[rows[c], hc[h]] for c, h in chs}
    vn = {(c, h): v_ref[rows[c], hc[h]] for c, h in chs}
    qk = {ch: _mm_nt(qn[ch], kn[ch]) for ch in chs}
    dmat = {ch: jnp.where(tril, bc[ch] + rl[ch], NEG) for ch in chs}
    mt = {ch: jnp.maximum(bc[ch] + m_in[ch], jnp.max(dmat[ch], axis=-1, keepdims=True)) for ch in chs}
    cmt = {ch: mt[ch] - log2_scale for ch in chs}
    w_inter = {ch: jnp.exp2(bc[ch] + m_in[ch] - cmt[ch]) for ch in chs}
    p = {ch: jnp.exp2(dmat[ch] - cmt[ch]) * qk[ch] for ch in chs}
    pv = {ch: _mm(p[ch], vn[ch]) for ch in chs}
    ws_c = {ch: jnp.exp2(b_last[ch] - bc[ch] + lic[ch] - m_new[ch]).astype(BF16) for ch in chs}
    ws_r = {ch: jnp.exp2(b_last[ch] + rl[ch] - m_new[ch]) for ch in chs}
    upd = {ch: _mm_tn(kn[ch] * ws_c[ch], vn[ch]) for ch in chs}
    n_add = {ch: _mm(jnp.broadcast_to(ws_r[ch], (8, ln)), kn[ch])[0:1, :] for ch in chs}
    psum = {ch: jnp.sum(p[ch], axis=-1, keepdims=True) for ch in chs}
    gate = {}
    for c, h in chs:
        og = o_ref[rows[c], hc[h]].astype(F32)
        zg = z_ref[rows[c], hc[h]].astype(F32)
        gate[c, h] = zg * (1.0 / ((1.0 + jnp.exp2(og * -LOG2E)) * (1.0 + jnp.exp2(zg * -LOG2E))))
    for c in range(nct):
        qc = {h: _mm(qn[c, h], cm[h]) for h in hs}
        qn_dot = {h: _mm_nt(qn[c, h], jnp.broadcast_to(nm[h], (8, hd)))[:, 0:1] for h in hs}
        for h in hs:
            num = w_inter[c, h] * qc[h] + pv[c, h]
            den = w_inter[c, h] * qn_dot[h] + psum[c, h]
            hout = num * (1.0 / jnp.maximum(jnp.abs(den), jnp.exp2(-mt[c, h])))
            y_ref[rows[c], hc[h]] = (hout * gate[c, h]).astype(y_ref.dtype)
            w_c = jnp.exp2(b_last[c, h] + m_in[c, h] - m_new[c, h])
            cm[h] = w_c * cm[h] + upd[c, h]
            nm[h] = w_c * nm[h] + n_add[c, h]
    for h in hs:
        c_st[h] = cm[h]
        n_st[h] = jnp.broadcast_to(nm[h], n_st.shape[1:])
        m_st[h] = jnp.broadcast_to(m[h], m_st.shape[1:])


def _mlstm(p3, g3, gt4, bias_col, bias_row, ts):
    b, s, _ = p3.shape
    w_b = 1024
    hd = w_b // H_B
    ncs = s // MLSTM_L

    def col_spec(sec):
        return pl.BlockSpec((None, ts, w_b), lambda bi, j, sec=sec: (bi, j, 4 + sec))

    return pl.pallas_call(
        _mlstm_body,
        grid=(b, s // ts),
        in_specs=[
            col_spec(0), col_spec(1), col_spec(2), col_spec(3), col_spec(4),
            pl.BlockSpec((None, ts, LANES), lambda bi, j: (bi, j, 0)),
            pl.BlockSpec((None, 2 * H_B, ncs, MLSTM_L), lambda bi, j: (bi, 0, 0, 0)),
            pl.BlockSpec((1, LANES), lambda bi, j: (0, 0)),
            pl.BlockSpec((2 * H_B, 1, 1), lambda bi, j: (0, 0, 0)),
        ],
        out_specs=pl.BlockSpec((None, ts, w_b), lambda bi, j: (bi, j, 0)),
        out_shape=jax.ShapeDtypeStruct((b, s, w_b), BF16),
        scratch_shapes=[
            pltpu.VMEM((H_B, hd, hd), F32),
            pltpu.VMEM((H_B, 8, hd), F32),
            pltpu.VMEM((H_B, 8, LANES), F32),
            pltpu.VMEM((ts, LANES), F32),
            pltpu.VMEM((ts, LANES), F32),
            pltpu.VMEM((H_B, ncs, MLSTM_L), F32),
            pltpu.VMEM((H_B, ncs, MLSTM_L), F32),
        ],
        compiler_params=_params(2),
        name="mlstm",
    )(p3, p3, p3, p3, p3, g3, gt4, bias_col, bias_row)


def _rglru_body(x_ref, z_ref, cw_ref, cb_ref, gw_ref, gbr_ref, gbi_ref, lam_ref, y_ref, a_s, b_s):
    s, wc = x_ref.shape
    blk = gw_ref.shape[1]
    row8 = _iota2((8, blk), 0)
    sub = _iota2((s // 8, 8, blk), 1)
    for n in range(wc // blk):
        cs = slice(n * blk, (n + 1) * blk)
        x = x_ref[:, cs].astype(F32)
        cw = cw_ref[:, cs]
        xc = cw[CONV_K - 1:CONV_K, :] * x + cb_ref[:, cs]
        for d in range(1, CONV_K):
            xs = pltpu.roll(x, d, 0)
            xs = jnp.concatenate([jnp.where(row8 >= d, xs[0:8], 0.0), xs[8:]], axis=0)
            xc = xc + cw[CONV_K - 1 - d:CONV_K - d, :] * xs
        gates = _mm(xc, gw_ref[n])
        r = _sigmoid(gates[:, :blk] + gbr_ref[:, cs])
        i = _sigmoid(gates[:, blk:] + gbi_ref[:, cs])
        nla = r * (C_RG * _softplus(-lam_ref[:, cs]))
        a = jnp.exp2(nla * -LOG2E)
        var = jnp.tanh(nla) * (a * a + 1.0)
        bb = jnp.where(var > 0.0, var * lax.rsqrt(var), 0.0) * (i * xc)
        a = a.reshape(s // 8, 8, blk)
        bb = bb.reshape(s // 8, 8, blk)
        for d in (1, 2, 4):
            keep = sub >= d
            a_sh = jnp.where(keep, pltpu.roll(a, d, 1), 1.0)
            b_sh = jnp.where(keep, pltpu.roll(bb, d, 1), 0.0)
            bb = a * b_sh + bb
            a = a * a_sh
        a_s[:, cs] = a.reshape(s, blk)
        b_s[:, cs] = bb.reshape(s, blk)

    def group(gi, carry):
        rows = pl.ds(pl.multiple_of(gi * 8, 8), 8)
        h = a_s[rows, :] * carry + b_s[rows, :]
        b_s[rows, :] = h
        return jnp.broadcast_to(h[7:8, :], h.shape)

    lax.fori_loop(0, s // 8, group, jnp.zeros((8, wc), F32), unroll=8)
    z = z_ref[...].astype(F32)
    y_ref[...] = (b_s[...] * _silu(z)).astype(y_ref.dtype)


def _rglru(p3, cw, cb, gw, gb, lam):
    b, s, _ = p3.shape
    w_c = cw.shape[1]
    blk = w_c // N_BLK_C
    nsp = w_c // RG_COLS
    return pl.pallas_call(
        _rglru_body,
        grid=(b, nsp),
        in_specs=[
            pl.BlockSpec((None, s, RG_COLS), lambda bi, n: (bi, 0, n)),
            pl.BlockSpec((None, s, RG_COLS), lambda bi, n: (bi, 0, nsp + n)),
            pl.BlockSpec((CONV_K, RG_COLS), lambda bi, n: (0, n)),
            pl.BlockSpec((1, RG_COLS), lambda bi, n: (0, n)),
            pl.BlockSpec((RG_COLS // blk, blk, 2 * blk), lambda bi, n: (n, 0, 0)),
            pl.BlockSpec((1, RG_COLS), lambda bi, n: (0, n)),
            pl.BlockSpec((1, RG_COLS), lambda bi, n: (0, nsp + n)),
            pl.BlockSpec((1, RG_COLS), lambda bi, n: (0, n)),
        ],
        out_specs=pl.BlockSpec((None, s, RG_COLS), lambda bi, n: (bi, 0, n)),
        out_shape=jax.ShapeDtypeStruct((b, s, w_c), BF16),
        scratch_shapes=[pltpu.VMEM((s, RG_COLS), F32), pltpu.VMEM((s, RG_COLS), F32)],
        compiler_params=_params(2),
        name="rglru",
    )(p3, p3, cw, cb, gw, gb, gb, lam)


def _unit_lower_inverse(a_list):
    r, c = _iota2((CHUNK, CHUNK), 0), _iota2((CHUNK, CHUNK), 1)
    eye = jnp.where(r == c, 1.0, 0.0).astype(F32)

    def pair_mask(sh):
        rb, cb_ = r >> sh, c >> sh
        return ((rb & 1) == 1) & (cb_ == rb - 1)

    ts = [eye - jnp.where(pair_mask(0), a, 0.0) for a in a_list]
    for sh in range(1, 6):
        mask = pair_mask(sh)
        x1 = [_mm(t, jnp.where(mask, a, 0.0)) for t, a in zip(ts, a_list)]
        x2 = [_mm(x, t) for x, t in zip(x1, ts)]
        ts = [t - x for t, x in zip(ts, x2)]
    return ts


def _gdn_body(q_ref, k_ref, v_ref, z_ref, g_ref, gt_ref, cw_ref, al_c_ref, dt_c_ref, al_r_ref, dt_r_ref, on_ref,
              y_ref, s_st, tail, qkv, col_d, col_b, row_d, lhs_s, o0_s, n_s, *, nt):
    ts = q_ref.shape[0]
    nct = ts // CHUNK
    w_d = q_ref.shape[1]
    h_d = w_d // HD_D
    scale = HD_D ** -0.5
    t = pl.program_id(0)
    srcs = (q_ref, k_ref, v_ref)

    def conv_block(slot, r0, first, cbk):
        ref = srcs[cbk // h_d]
        ci = slice((cbk % h_d) * HD_D, (cbk % h_d + 1) * HD_D)
        cs = slice(cbk * HD_D, (cbk + 1) * HD_D)
        cur = ref[pl.ds(r0, CHUNK), ci].astype(F32)
        prev = ref[pl.ds(pl.multiple_of(jnp.maximum(r0 - 16, 0), 16), 16), ci].astype(F32)
        xw = jnp.concatenate([jnp.where(first, tail[:, cs], prev), cur], axis=0)
        acc = cw_ref[CONV_K - 1:CONV_K, cs] * cur
        for d in range(1, CONV_K):
            acc = acc + cw_ref[CONV_K - 1 - d:CONV_K - d, cs] * xw[16 - d:16 - d + CHUNK]
        acc = _silu(acc)
        if cbk < 2 * h_d:
            acc = acc * lax.rsqrt(jnp.sum(acc * acc, axis=-1, keepdims=True) + EPS)
        qkv[slot, pl.ds(r0, CHUNK), cs] = acc

    def conv_rows(slot, rb):
        r0 = pl.multiple_of(rb * CHUNK, CHUNK)
        for cbk in range(3 * h_d):
            conv_block(slot, r0, rb == 0, cbk)

    def save_tail():
        keep = lax.rem(t + 1, nt) != 0
        for w, ref in enumerate(srcs):
            tail[:, w * w_d:(w + 1) * w_d] = jnp.where(keep, ref[ts - 16:ts, :].astype(F32), 0.0)

    @pl.when(t == 0)
    def _():
        tail[...] = jnp.zeros_like(tail)

        def rows(rb, carry):
            conv_rows(0, rb)
            return carry

        lax.fori_loop(0, nct, rows, 0)
        save_tail()

    @pl.when(t > 0)
    def _():
        cur_slot = lax.rem(t, 2)
        prv_slot = 1 - cur_slot

        @pl.when(lax.rem(t - 1, nt) == 0)
        def _():
            s_st[...] = jnp.zeros_like(s_st)

        g = g_ref[...]
        col_b[...] = _sigmoid(g)
        col_d[...] = _chunk_cumsum(-jnp.exp(al_c_ref[...]) * _softplus(g + dt_c_ref[...]), CHUNK) * LOG2E
        gt = gt_ref[...]
        g_r = -jnp.exp(al_r_ref[...]) * _softplus(gt[0:h_d] + dt_r_ref[...])
        row_d[...] = _mm_f32(g_r.reshape(h_d * nct, CHUNK), _triu(CHUNK)).reshape(h_d, nct, CHUNK) * LOG2E

        r_i, c_i = _iota2((CHUNK, CHUNK), 0), _iota2((CHUNK, CHUNK), 1)
        tril = c_i <= r_i
        strict = c_i < r_i
        hs = range(h_d)

        def group(gi, carry):
            probs = [(cc, h) for cc in range(GDN_GROUP) for h in hs]
            cidx = [gi * GDN_GROUP + cc for cc in range(GDN_GROUP)]
            rows = [pl.ds(pl.multiple_of(ci * CHUNK, CHUNK), CHUNK) for ci in cidx]
            dc_all = [col_d[r, :] for r in rows]
            beta_all = [col_b[r, :] for r in rows]
            dc = [dc_all[cc][:, h:h + 1] for cc, h in probs]
            beta = [beta_all[cc][:, h_d + h:h_d + h + 1] for cc, h in probs]
            q = [qkv[prv_slot, rows[cc], h * HD_D:(h + 1) * HD_D] for cc, h in probs]
            k = [qkv[prv_slot, rows[cc], w_d + h * HD_D:w_d + (h + 1) * HD_D] for cc, h in probs]
            v = [qkv[prv_slot, rows[cc], 2 * w_d + h * HD_D:2 * w_d + (h + 1) * HD_D] for cc, h in probs]
            ps = range(len(probs))
            gam = [jnp.exp2(jnp.where(tril, dc[i] - row_d[probs[i][1], pl.ds(cidx[probs[i][0]], 1), :], NEG)) for i in ps]
            kb = [k[i] * beta[i] for i in ps]
            kk = [_mm_nt(kb[i], k[i]) for i in ps]
            t_inv = _unit_lower_inverse([jnp.where(strict, kk[i] * gam[i], 0.0) for i in ps])
            ed = [jnp.exp2(dc[i]) for i in ps]
            uw = [_mm(t_inv[i], jnp.concatenate([v[i] * beta[i], kb[i] * ed[i]], axis=1)) for i in ps]
            qs = [q[i] * scale for i in ps]
            qk = [_mm_nt(qs[i], k[i]) * gam[i] for i in ps]
            kd = [k[i] * jnp.exp2(dc[i][CHUNK - 1:CHUNK, :] - dc[i]) for i in ps]
            x1 = [_mm(qk[i], uw[i]) for i in ps]
            x2 = [_mm_tn(kd[i], uw[i]) for i in ps]
            for i, (cc, h) in enumerate(probs):
                lhs_s[cc, h] = jnp.concatenate([qs[i] * ed[i] - x1[i][:, HD_D:], x2[i][:, HD_D:]], axis=0).astype(BF16)
                o0_s[cc, h] = x1[i][:, :HD_D]
                n_s[cc, h] = x2[i][:, :HD_D]
            for rr in range(GDN_GROUP):
                conv_rows(cur_slot, gi * GDN_GROUP + rr)

            def step(cc, carry2):
                c = gi * GDN_GROUP + cc
                rws = pl.ds(pl.multiple_of(c * CHUNK, CHUNK), CHUNK)
                d_last = jnp.exp2(col_d[pl.ds(c * CHUNK + CHUNK - 1, 1), :])
                st = [s_st[h] for h in hs]
                r = [_mm(lhs_s[cc, h], st[h]) for h in hs]
                for h in hs:
                    o = o0_s[cc, h] + r[h][:CHUNK]
                    s_st[h] = d_last[:, h:h + 1] * st[h] + n_s[cc, h] - r[h][CHUNK:]
                    on = o * lax.rsqrt(jnp.mean(o * o, axis=-1, keepdims=True) + EPS) * on_ref[...]
                    z = z_ref[rws, h * HD_D:(h + 1) * HD_D].astype(F32)
                    y_ref[rws, h * HD_D:(h + 1) * HD_D] = (on * _silu(z)).astype(y_ref.dtype)
                return carry2

            lax.fori_loop(0, GDN_GROUP, step, 0)
            return carry

        lax.fori_loop(0, nct // GDN_GROUP, group, 0)
        save_tail()


def _gdn(p3, g3, gt4, cw, al_c, dt_c, al_r, dt_r, onorm, ts):
    b, s, _ = p3.shape
    w_d = 1024
    h_d = w_d // HD_D
    nct = ts // CHUNK
    nt = s // ts
    n_tiles = b * nt

    def conv_spec(sec):
        def imap(t, sec=sec):
            tc = jnp.minimum(t, n_tiles - 1)
            return (tc // nt, tc % nt, 2 + sec)
        return pl.BlockSpec((None, ts, w_d), imap)

    def prev_map(last):
        def imap(t):
            tp = jnp.maximum(t - 1, 0)
            return (tp // nt, tp % nt, last)
        return imap

    def gt_map(t):
        tp = jnp.maximum(t - 1, 0)
        return (tp // nt, 0, tp % nt, 0)

    return pl.pallas_call(
        functools.partial(_gdn_body, nt=nt),
        grid=(n_tiles + 1,),
        in_specs=[
            conv_spec(0), conv_spec(1), conv_spec(2),
            pl.BlockSpec((None, ts, w_d), prev_map(5)),
            pl.BlockSpec((None, ts, LANES), prev_map(0)),
            pl.BlockSpec((None, 2 * h_d, nct, CHUNK), gt_map),
            pl.BlockSpec((CONV_K, 3 * w_d), lambda t: (0, 0)),
            pl.BlockSpec((1, LANES), lambda t: (0, 0)),
            pl.BlockSpec((1, LANES), lambda t: (0, 0)),
            pl.BlockSpec((h_d, 1, 1), lambda t: (0, 0, 0)),
            pl.BlockSpec((h_d, 1, 1), lambda t: (0, 0, 0)),
            pl.BlockSpec((1, HD_D), lambda t: (0, 0)),
        ],
        out_specs=pl.BlockSpec((None, ts, w_d), prev_map(0)),
        out_shape=jax.ShapeDtypeStruct((b, s, w_d), BF16),
        scratch_shapes=[
            pltpu.VMEM((h_d, HD_D, HD_D), F32),
            pltpu.VMEM((16, 3 * w_d), F32),
            pltpu.VMEM((2, ts, 3 * w_d), F32),
            pltpu.VMEM((ts, LANES), F32),
            pltpu.VMEM((ts, LANES), F32),
            pltpu.VMEM((h_d, nct, CHUNK), F32),
            pltpu.VMEM((GDN_GROUP, h_d, CHUNK + HD_D, HD_D), BF16),
            pltpu.VMEM((GDN_GROUP, h_d, CHUNK, HD_D), F32),
            pltpu.VMEM((GDN_GROUP, h_d, HD_D, HD_D), F32),
        ],
        compiler_params=_params(1),
        name="gated_delta",
    )(p3, p3, p3, p3, g3, gt4, cw, al_c, dt_c, al_r, dt_r, onorm)


def _rel_row(rel_bias):
    t = (np.arange(REL_W) + CHUNK - 1) % REL_W
    idx = np.clip(LOOKBACK * CHUNK + CHUNK - 1 - t, -REL_MAX, REL_MAX) + REL_MAX
    return rel_bias[:, idx].astype(F32)[:, None, :]


def _pad_lanes(v, n=LANES):
    return jnp.pad(v, ((0, 0), (0, n - v.shape[1])))


def _pad_rows(v, n=LANES):
    return jnp.pad(v, ((0, n - v.shape[0]), (0, 0)))


def _weight_t(w):
    return jnp.swapaxes(w, 0, 1).astype(BF16)


def _gates_t(g3, n, length):
    b, s, _ = g3.shape
    return jnp.transpose(g3[:, :, :n], (0, 2, 1)).reshape(b, n, s // length, length)


def kernel(x, ev_norm, ev_w_in, ev_if_bias, ev_qn_gain, ev_kn_gain, ev_rel_bias, ev_w_out, od_norm, od_w_in, od_conv_c_w, od_conv_c_b, od_gate_w, od_gate_b, od_lambda, od_conv_d_w, od_a_log, od_dt_bias, od_onorm, od_w_out):
    b, s, d = x.shape
    m = b * s
    half = d // 2
    tm = min(1024, m)
    ts = min(512, s)
    assert half == 1024 and m % tm == 0 and s % ts == 0 and s % (CHUNK * ATTN_GROUP) == 0 and ts % MLSTM_L == 0
    x2 = x.reshape(m, d)
    depth = ev_norm.shape[0] + od_norm.shape[0]
    for layer in range(depth):
        j = layer // 2
        if layer % 2 == 0:
            n_main = 9 * half
            w_t = _weight_t(ev_w_in[j])
            p, g = _inproj(x2, ev_norm[j].reshape(1, d), w_t, _pad_rows(w_t[n_main:]), n_main, tm, n_main // 4)
            p3, g3 = p.reshape(b, s, n_main), g.reshape(b, s, LANES)
            ya = _attn(p3, ev_qn_gain[j].reshape(1, HD_A), ev_kn_gain[j].reshape(1, HD_A), _rel_row(ev_rel_bias[j]))
            bias = ev_if_bias[j].astype(F32)
            yb = _mlstm(p3, g3, _gates_t(g3, 2 * H_B, MLSTM_L), _pad_lanes(bias.reshape(1, -1)), bias.reshape(-1, 1, 1), ts)
            x2 = _outproj(ya.reshape(m, half), yb.reshape(m, half), ev_w_out[j], x2, tm, 1024)
        else:
            n_main = 6 * half
            h_d = half // HD_D
            w_t = _weight_t(od_w_in[j])
            p, g = _inproj(x2, od_norm[j].reshape(1, d), w_t, _pad_rows(w_t[n_main:]), n_main, tm, n_main // 3)
            p3, g3 = p.reshape(b, s, n_main), g.reshape(b, s, LANES)
            yc = _rglru(p3, od_conv_c_w[j], od_conv_c_b[j].reshape(1, -1), od_gate_w[j].astype(BF16),
                        od_gate_b[j].reshape(1, -1), od_lambda[j].reshape(1, -1))
            al, dt = od_a_log[j].astype(F32), od_dt_bias[j].astype(F32)
            yd = _gdn(p3, g3, _gates_t(g3, 2 * h_d, CHUNK), od_conv_d_w[j], _pad_lanes(al.reshape(1, -1)),
                      _pad_lanes(dt.reshape(1, -1)), al.reshape(-1, 1, 1), dt.reshape(-1, 1, 1),
                      od_onorm[j].reshape(1, HD_D), ts)
            x2 = _outproj(yc.reshape(m, half), yd.reshape(m, half), od_w_out[j], x2, tm, 1024)
    return x2.reshape(b, s, d)
```

```python
import functools

import numpy as np
import jax
import jax.numpy as jnp
from jax import lax
from jax.experimental import pallas as pl
from jax.experimental.pallas import tpu as pltpu

F32 = jnp.float32
BF16 = jnp.bfloat16

CHUNK = 64
EPS = 1e-6
NEG = -1e30
CONV_K = 4
LOG2E = 1.4426950408889634
LANES = 128
HD_A = 128
LOOKBACK = 8
BAND = (LOOKBACK + 1) * CHUNK
REL_MAX = 256
REL_W = 640
ATTN_GROUP = 16
ATTN_HEADS = 2
H_B = 4
MLSTM_L = 256
N_BLK_C = 8
C_RG = 8.0
RG_COLS = 512
HD_D = 128
GDN_GROUP = 4

V7X_VMEM_BYTES = 64 * 1024 * 1024
VMEM_LIMIT = V7X_VMEM_BYTES - 8 * 1024 * 1024


def _params(n_axes):
    return pltpu.CompilerParams(dimension_semantics=("arbitrary",) * n_axes, vmem_limit_bytes=VMEM_LIMIT)


def _mm(a, b):
    return jnp.dot(a.astype(BF16), b.astype(BF16), preferred_element_type=F32)


def _mm_nt(a, b):
    return lax.dot_general(a.astype(BF16), b.astype(BF16), (((1,), (1,)), ((), ())), preferred_element_type=F32)


def _mm_tn(a, b):
    return lax.dot_general(a.astype(BF16), b.astype(BF16), (((0,), (0,)), ((), ())), preferred_element_type=F32)


def _mm_f32(a, b):
    return jnp.dot(a, b, preferred_element_type=F32, precision=lax.Precision.HIGHEST)


def _log1p(e):
    u = 1.0 + e
    return jnp.where(u == 1.0, e, jnp.log(u) * (e / (u - 1.0)))


def _softplus(x):
    return jnp.maximum(x, 0.0) + _log1p(jnp.exp(-jnp.abs(x)))


def _sigmoid(x):
    return 1.0 / (1.0 + jnp.exp2(x * -LOG2E))


def _silu(x):
    return x * _sigmoid(x)


def _iota2(shape, axis):
    return lax.broadcasted_iota(jnp.int32, shape, axis)


def _chunk_cumsum(x, length):
    pos = _iota2(x.shape, 0) & (length - 1)
    d = 1
    while d < length:
        x = x + jnp.where(pos >= d, pltpu.roll(x, d, 0), 0.0)
        d *= 2
    return x


def _triu(n):
    r, c = _iota2((n, n), 0), _iota2((n, n), 1)
    return jnp.where(r <= c, 1.0, 0.0).astype(F32)


def _inproj_body(x_ref, g_ref, w_ref, wg_ref, o_ref, og_ref, xn_ref):
    @pl.when(pl.program_id(1) == 0)
    def _():
        x = x_ref[...]
        ms = jnp.mean(x * x, axis=-1, keepdims=True)
        xn = ((x * lax.rsqrt(ms + EPS)) * g_ref[...]).astype(BF16)
        xn_ref[...] = xn
        og_ref[...] = _mm_nt(xn, wg_ref[...])

    o_ref[...] = _mm_nt(xn_ref[...], w_ref[...]).astype(o_ref.dtype)


def _inproj(x2, g, w_t, wg_t, n, tm, tn):
    m, d = x2.shape
    return pl.pallas_call(
        _inproj_body,
        grid=(m // tm, n // tn),
        in_specs=[
            pl.BlockSpec((tm, d), lambda i, j: (i, 0)),
            pl.BlockSpec((1, d), lambda i, j: (0, 0)),
            pl.BlockSpec((tn, d), lambda i, j: (j, 0)),
            pl.BlockSpec((LANES, d), lambda i, j: (0, 0)),
        ],
        out_specs=[
            pl.BlockSpec((tm, tn), lambda i, j: (i, j)),
            pl.BlockSpec((tm, LANES), lambda i, j: (i, 0)),
        ],
        out_shape=[jax.ShapeDtypeStruct((m, n), BF16), jax.ShapeDtypeStruct((m, LANES), F32)],
        scratch_shapes=[pltpu.VMEM((tm, d), BF16)],
        compiler_params=_params(2),
        name="inproj",
    )(x2, g, w_t, wg_t)


def _outproj_body(ya_ref, yb_ref, wa_ref, wb_ref, x_ref, o_ref, wa_s, wb_s):
    @pl.when(pl.program_id(1) == 0)
    def _():
        wa_s[...] = wa_ref[...].astype(BF16)
        wb_s[...] = wb_ref[...].astype(BF16)

    acc = jnp.dot(ya_ref[...], wa_s[...], preferred_element_type=F32)
    acc = acc + jnp.dot(yb_ref[...], wb_s[...], preferred_element_type=F32)
    o_ref[...] = x_ref[...] + acc


def _outproj(ya, yb, w, x2, tm, tn):
    m, d = x2.shape
    kh = ya.shape[1]
    return pl.pallas_call(
        _outproj_body,
        grid=(d // tn, m // tm),
        in_specs=[
            pl.BlockSpec((tm, kh), lambda j, i: (i, 0)),
            pl.BlockSpec((tm, kh), lambda j, i: (i, 0)),
            pl.BlockSpec((kh, tn), lambda j, i: (0, j)),
            pl.BlockSpec((kh, tn), lambda j, i: (1, j)),
            pl.BlockSpec((tm, tn), lambda j, i: (i, j)),
        ],
        out_specs=pl.BlockSpec((tm, tn), lambda j, i: (i, j)),
        out_shape=jax.ShapeDtypeStruct((m, d), F32),
        scratch_shapes=[pltpu.VMEM((kh, tn), BF16), pltpu.VMEM((kh, tn), BF16)],
        compiler_params=_params(2),
        name="outproj",
    )(ya, yb, w, w, x2)


def _attn_body(q_ref, k_ref, v_ref, z_ref, qg_ref, kg_ref, c_ref, o_ref, qs, ks, vs, bias_s):
    s = q_ref.shape[0]
    pad = LOOKBACK * CHUNK
    col = _iota2((CHUNK, BAND), 1)
    assert ATTN_GROUP >= LOOKBACK
    for hh in range(ATTN_HEADS):
        hc = slice(hh * HD_A, (hh + 1) * HD_A)
        q = q_ref[:, hc].astype(F32)
        q = q * lax.rsqrt(jnp.mean(q * q, axis=-1, keepdims=True) + EPS) * qg_ref[...] * (HD_A ** -0.5 * LOG2E)
        qs[hh] = q.astype(BF16)
        k = k_ref[:, hc].astype(F32)
        k = k * lax.rsqrt(jnp.mean(k * k, axis=-1, keepdims=True) + EPS) * kg_ref[...]
        ks[hh, 0:pad, :] = jnp.zeros((pad, HD_A), BF16)
        ks[hh, pad:pad + s, :] = k.astype(BF16)
        vs[hh, 0:pad, :] = jnp.zeros((pad, HD_A), BF16)
        vs[hh, pad:pad + s, :] = v_ref[:, hc]
        cb = jnp.broadcast_to(c_ref[hh] * LOG2E, (CHUNK, REL_W))
        bias_s[hh] = pltpu.roll(cb, 0, 1, stride=1, stride_axis=0)[:, :BAND]

        def chunks(it, masked, hh=hh, hc=hc):
            gs = range(ATTN_GROUP)
            n = [it * ATTN_GROUP + g for g in gs]
            r0 = [pl.multiple_of(n[g] * CHUNK, CHUNK) for g in gs]
            sc = [_mm_nt(qs[hh, pl.ds(r0[g], CHUNK), :], ks[hh, pl.ds(r0[g], BAND), :]) + bias_s[hh] for g in gs]
            if masked:
                sc = [jnp.where(col >= (LOOKBACK - n[g]) * CHUNK, sc[g], NEG) for g in gs]
            p = [jnp.exp2(sc[g] - jnp.max(sc[g], axis=-1, keepdims=True)) for g in gs]
            pv = [_mm(p[g], vs[hh, pl.ds(r0[g], BAND), :]) for g in gs]
            for g in gs:
                o = pv[g] * (1.0 / jnp.sum(p[g], axis=-1, keepdims=True))
                z = z_ref[pl.ds(r0[g], CHUNK), hc].astype(F32)
                o_ref[pl.ds(r0[g], CHUNK), hc] = (o * _silu(z)).astype(o_ref.dtype)

        chunks(0, True)

        def rest(it, carry, chunks=chunks):
            chunks(it, False)
            return carry

        lax.fori_loop(1, s // (CHUNK * ATTN_GROUP), rest, 0)


def _attn(p3, qg, kg, crel):
    b, s, _ = p3.shape
    h_a = crel.shape[0]
    hp = h_a // ATTN_HEADS
    wh = ATTN_HEADS * HD_A

    def col_spec(sec):
        return pl.BlockSpec((None, s, wh), lambda bi, h, sec=sec: (bi, 0, sec * hp + h))

    return pl.pallas_call(
        _attn_body,
        grid=(b, hp),
        in_specs=[
            col_spec(0), col_spec(1), col_spec(2), col_spec(3),
            pl.BlockSpec((1, HD_A), lambda bi, h: (0, 0)),
            pl.BlockSpec((1, HD_A), lambda bi, h: (0, 0)),
            pl.BlockSpec((ATTN_HEADS, 1, REL_W), lambda bi, h: (h, 0, 0)),
        ],
        out_specs=pl.BlockSpec((None, s, wh), lambda bi, h: (bi, 0, h)),
        out_shape=jax.ShapeDtypeStruct((b, s, h_a * HD_A), BF16),
        scratch_shapes=[
            pltpu.VMEM((ATTN_HEADS, s, HD_A), BF16),
            pltpu.VMEM((ATTN_HEADS, s + LOOKBACK * CHUNK, HD_A), BF16),
            pltpu.VMEM((ATTN_HEADS, s + LOOKBACK * CHUNK, HD_A), BF16),
            pltpu.VMEM((ATTN_HEADS, CHUNK, BAND), F32),
        ],
        compiler_params=_params(2),
        name="chunk_attn",
    )(p3, p3, p3, p3, qg, kg, crel)


def _mlstm_body(q_ref, k_ref, v_ref, o_ref, z_ref, gt_ref, br_ref, y_ref, c_st, n_st, m_st, row_b, row_i):
    ts = q_ref.shape[0]
    ln = MLSTM_L
    nct = ts // ln
    hd = q_ref.shape[1] // H_B
    scale = hd ** -0.5
    j = pl.program_id(1)
    hs = range(H_B)
    hc = [slice(h * hd, (h + 1) * hd) for h in hs]

    @pl.when(j == 0)
    def _():
        c_st[...] = jnp.zeros_like(c_st)
        n_st[...] = jnp.zeros_like(n_st)
        m_st[...] = jnp.zeros_like(m_st)
        gt = gt_ref[...] + br_ref[...]
        for h in hs:
            row_b[h] = _mm_f32(-_softplus(-gt[H_B + h]), _triu(ln)) * LOG2E
            row_i[h] = gt[h] * LOG2E

    tril = _iota2((ln, ln), 1) <= _iota2((ln, ln), 0)
    log2_scale = float(np.log2(scale))
    m = [m_st[h][0:1, 0:1] for h in hs]
    cm = [c_st[h] for h in hs]
    nm = [n_st[h][0:1, :] for h in hs]
    chs = [(c, h) for c in range(nct) for h in hs]
    rows = [slice(c * ln, (c + 1) * ln) for c in range(nct)]
    li_r = {(c, h): row_i[h, pl.ds(j * nct + c, 1), :] for c, h in chs}
    bc_r = {(c, h): row_b[h, pl.ds(j * nct + c, 1), :] for c, h in chs}
    pad = jnp.zeros((LANES - 2 * H_B, ln), F32)
    cols = [jnp.concatenate([li_r[c, h] for h in hs] + [bc_r[c, h] for h in hs] + [pad], axis=0).T
            for c in range(nct)]
    lic = {(c, h): cols[c][:, h:h + 1] for c, h in chs}
    bc = {(c, h): cols[c][:, H_B + h:H_B + h + 1] for c, h in chs}
    rl = {ch: li_r[ch] - bc_r[ch] for ch in chs}
    b_last = {ch: bc[ch][ln - 1:ln, :] for ch in chs}
    m_in, m_new = {}, {}
    for c, h in chs:
        m_in[c, h] = m[h]
        m_new[c, h] = jnp.maximum(b_last[c, h] + m[h], jnp.max(b_last[c, h] + rl[c, h], axis=-1, keepdims=True))
        m[h] = m_new[c, h]
    qn = {(c, h): q_ref[rows[c], hc[h]] for c, h in chs}
    kn = {(c, h): k_ref[rows[c], hc[h]] for c, h in chs}
    vn = {(c, h): v_ref[rows[c], hc[h]] for c, h in chs}
    qk = {ch: _mm_nt(qn[ch], kn[ch]) for ch in chs}
    dmat = {ch: jnp.where(tril, bc[ch] + rl[ch], NEG) for ch in chs}
    mt = {ch: jnp.maximum(bc[ch] + m_in[ch], jnp.max(dmat[ch], axis=-1, keepdims=True)) for ch in chs}
    cmt = {ch: mt[ch] - log2_scale for ch in chs}
    w_inter = {ch: jnp.exp2(bc[ch] + m_in[ch] - cmt[ch]) for ch in chs}
    p = {ch: jnp.exp2(dmat[ch] - cmt[ch]) * qk[ch] for ch in chs}
    pv = {ch: _mm(p[ch], vn[ch]) for ch in chs}
    ws_c = {ch: jnp.exp2(b_last[ch] - bc[ch] + lic[ch] - m_new[ch]).astype(BF16) for ch in chs}
    ws_r = {ch: jnp.exp2(b_last[ch] + rl[ch] - m_new[ch]) for ch in chs}
    upd = {ch: _mm_tn(kn[ch] * ws_c[ch], vn[ch]) for ch in chs}
    n_add = {ch: _mm(jnp.broadcast_to(ws_r[ch], (8, ln)), kn[ch])[0:1, :] for ch in chs}
    psum = {ch: jnp.sum(p[ch], axis=-1, keepdims=True) for ch in chs}
    gate = {}
    for c, h in chs:
        og = o_ref[rows[c], hc[h]].astype(F32)
        zg = z_ref[rows[c], hc[h]].astype(F32)
        gate[c, h] = zg * (1.0 / ((1.0 + jnp.exp2(og * -LOG2E)) * (1.0 + jnp.exp2(zg * -LOG2E))))
    for c in range(nct):
        qc = {h: _mm(qn[c, h], cm[h]) for h in hs}
        qn_dot = {h: _mm_nt(qn[c, h], jnp.broadcast_to(nm[h], (8, hd)))[:, 0:1] for h in hs}
        for h in hs:
            num = w_inter[c, h] * qc[h] + pv[c, h]
            den = w_inter[c, h] * qn_dot[h] + psum[c, h]
            hout = num * (1.0 / jnp.maximum(jnp.abs(den), jnp.exp2(-mt[c, h])))
            y_ref[rows[c], hc[h]] = (hout * gate[c, h]).astype(y_ref.dtype)
            w_c = jnp.exp2(b_last[c, h] + m_in[c, h] - m_new[c, h])
            cm[h] = w_c * cm[h] + upd[c, h]
            nm[h] = w_c * nm[h] + n_add[c, h]
    for h in hs:
        c_st[h] = cm[h]
        n_st[h] = jnp.broadcast_to(nm[h], n_st.shape[1:])
        m_st[h] = jnp.broadcast_to(m[h], m_st.shape[1:])


def _mlstm(p3, gt4, bias_row, ts):
    b, s, _ = p3.shape
    w_b = 1024
    hd = w_b // H_B
    ncs = s // MLSTM_L

    def col_spec(sec):
        return pl.BlockSpec((None, ts, w_b), lambda bi, j, sec=sec: (bi, j, 4 + sec))

    return pl.pallas_call(
        _mlstm_body,
        grid=(b, s // ts),
        in_specs=[
            col_spec(0), col_spec(1), col_spec(2), col_spec(3), col_spec(4),
            pl.BlockSpec((None, 2 * H_B, ncs, MLSTM_L), lambda bi, j: (bi, 0, 0, 0)),
            pl.BlockSpec((2 * H_B, 1, 1), lambda bi, j: (0, 0, 0)),
        ],
        out_specs=pl.BlockSpec((None, ts, w_b), lambda bi, j: (bi, j, 0)),
        out_shape=jax.ShapeDtypeStruct((b, s, w_b), BF16),
        scratch_shapes=[
            pltpu.VMEM((H_B, hd, hd), F32),
            pltpu.VMEM((H_B, 8, hd), F32),
            pltpu.VMEM((H_B, 8, LANES), F32),
            pltpu.VMEM((H_B, ncs, MLSTM_L), F32),
            pltpu.VMEM((H_B, ncs, MLSTM_L), F32),
        ],
        compiler_params=_params(2),
        name="mlstm",
    )(p3, p3, p3, p3, p3, gt4, bias_row)


def _rglru_body(x_ref, z_ref, cw_ref, cb_ref, gw_ref, gbr_ref, gbi_ref, lam_ref, y_ref, a_s, b_s):
    s, wc = x_ref.shape
    blk = gw_ref.shape[1]
    row8 = _iota2((8, blk), 0)
    sub = _iota2((s // 8, 8, blk), 1)
    for n in range(wc // blk):
        cs = slice(n * blk, (n + 1) * blk)
        x = x_ref[:, cs].astype(F32)
        cw = cw_ref[:, cs]
        xc = cw[CONV_K - 1:CONV_K, :] * x + cb_ref[:, cs]
        for d in range(1, CONV_K):
            xs = pltpu.roll(x, d, 0)
            xs = jnp.concatenate([jnp.where(row8 >= d, xs[0:8], 0.0), xs[8:]], axis=0)
            xc = xc + cw[CONV_K - 1 - d:CONV_K - d, :] * xs
        gates = _mm(xc, gw_ref[n])
        r = _sigmoid(gates[:, :blk] + gbr_ref[:, cs])
        i = _sigmoid(gates[:, blk:] + gbi_ref[:, cs])
        nla = r * (C_RG * _softplus(-lam_ref[:, cs]))
        a = jnp.exp2(nla * -LOG2E)
        var = jnp.tanh(nla) * (a * a + 1.0)
        bb = jnp.where(var > 0.0, var * lax.rsqrt(var), 0.0) * (i * xc)
        a = a.reshape(s // 8, 8, blk)
        bb = bb.reshape(s // 8, 8, blk)
        for d in (1, 2, 4):
            keep = sub >= d
            a_sh = jnp.where(keep, pltpu.roll(a, d, 1), 1.0)
            b_sh = jnp.where(keep, pltpu.roll(bb, d, 1), 0.0)
            bb = a * b_sh + bb
            a = a * a_sh
        a_s[:, cs] = a.reshape(s, blk)
        b_s[:, cs] = bb.reshape(s, blk)

    def group(gi, carry):
        rows = pl.ds(pl.multiple_of(gi * 8, 8), 8)
        h = a_s[rows, :] * carry + b_s[rows, :]
        b_s[rows, :] = h
        return jnp.broadcast_to(h[7:8, :], h.shape)

    lax.fori_loop(0, s // 8, group, jnp.zeros((8, wc), F32), unroll=8)
    z = z_ref[...].astype(F32)
    y_ref[...] = (b_s[...] * _silu(z)).astype(y_ref.dtype)


def _rglru(p3, cw, cb, gw, gb, lam):
    b, s, _ = p3.shape
    w_c = cw.shape[1]
    blk = w_c // N_BLK_C
    nsp = w_c // RG_COLS
    return pl.pallas_call(
        _rglru_body,
        grid=(b, nsp),
        in_specs=[
            pl.BlockSpec((None, s, RG_COLS), lambda bi, n: (bi, 0, n)),
            pl.BlockSpec((None, s, RG_COLS), lambda bi, n: (bi, 0, nsp + n)),
            pl.BlockSpec((CONV_K, RG_COLS), lambda bi, n: (0, n)),
            pl.BlockSpec((1, RG_COLS), lambda bi, n: (0, n)),
            pl.BlockSpec((RG_COLS // blk, blk, 2 * blk), lambda bi, n: (n, 0, 0)),
            pl.BlockSpec((1, RG_COLS), lambda bi, n: (0, n)),
            pl.BlockSpec((1, RG_COLS), lambda bi, n: (0, nsp + n)),
            pl.BlockSpec((1, RG_COLS), lambda bi, n: (0, n)),
        ],
        out_specs=pl.BlockSpec((None, s, RG_COLS), lambda bi, n: (bi, 0, n)),
        out_shape=jax.ShapeDtypeStruct((b, s, w_c), BF16),
        scratch_shapes=[pltpu.VMEM((s, RG_COLS), F32), pltpu.VMEM((s, RG_COLS), F32)],
        compiler_params=_params(2),
        name="rglru",
    )(p3, p3, cw, cb, gw, gb, gb, lam)


def _unit_lower_inverse(a_list):
    r, c = _iota2((CHUNK, CHUNK), 0), _iota2((CHUNK, CHUNK), 1)
    eye = jnp.where(r == c, 1.0, 0.0).astype(F32)

    def pair_mask(sh):
        rb, cb_ = r >> sh, c >> sh
        return ((rb & 1) == 1) & (cb_ == rb - 1)

    ts = [eye - jnp.where(pair_mask(0), a, 0.0) for a in a_list]
    for sh in range(1, 6):
        mask = pair_mask(sh)
        x1 = [_mm(t, jnp.where(mask, a, 0.0)) for t, a in zip(ts, a_list)]
        x2 = [_mm(x, t) for x, t in zip(x1, ts)]
        ts = [t - x for t, x in zip(ts, x2)]
    return ts


def _gdn_body(q_ref, k_ref, v_ref, z_ref, g_ref, gt_ref, cw_ref, al_c_ref, dt_c_ref, al_r_ref, dt_r_ref, on_ref,
              y_ref, s_st, tail, qkv, col_d, col_b, row_d, lhs_s, o0_s, n_s, *, nt):
    ts = q_ref.shape[0]
    nct = ts // CHUNK
    w_d = q_ref.shape[1]
    h_d = w_d // HD_D
    scale = HD_D ** -0.5
    t = pl.program_id(0)
    srcs = (q_ref, k_ref, v_ref)

    def conv_block(slot, r0, first, cbk):
        ref = srcs[cbk // h_d]
        ci = slice((cbk % h_d) * HD_D, (cbk % h_d + 1) * HD_D)
        cs = slice(cbk * HD_D, (cbk + 1) * HD_D)
        cur = ref[pl.ds(r0, CHUNK), ci].astype(F32)
        prev = ref[pl.ds(pl.multiple_of(jnp.maximum(r0 - 16, 0), 16), 16), ci].astype(F32)
        xw = jnp.concatenate([jnp.where(first, tail[:, cs], prev), cur], axis=0)
        acc = cw_ref[CONV_K - 1:CONV_K, cs] * cur
        for d in range(1, CONV_K):
            acc = acc + cw_ref[CONV_K - 1 - d:CONV_K - d, cs] * xw[16 - d:16 - d + CHUNK]
        acc = _silu(acc)
        if cbk < 2 * h_d:
            acc = acc * lax.rsqrt(jnp.sum(acc * acc, axis=-1, keepdims=True) + EPS)
        qkv[slot, pl.ds(r0, CHUNK), cs] = acc

    def conv_rows(slot, rb):
        r0 = pl.multiple_of(rb * CHUNK, CHUNK)
        for cbk in range(3 * h_d):
            conv_block(slot, r0, rb == 0, cbk)

    def save_tail():
        keep = lax.rem(t + 1, nt) != 0
        for w, ref in enumerate(srcs):
            tail[:, w * w_d:(w + 1) * w_d] = jnp.where(keep, ref[ts - 16:ts, :].astype(F32), 0.0)

    @pl.when(t == 0)
    def _():
        tail[...] = jnp.zeros_like(tail)

        def rows(rb, carry):
            conv_rows(0, rb)
            return carry

        lax.fori_loop(0, nct, rows, 0)
        save_tail()

    @pl.when(t > 0)
    def _():
        cur_slot = lax.rem(t, 2)
        prv_slot = 1 - cur_slot

        @pl.when(lax.rem(t - 1, nt) == 0)
        def _():
            s_st[...] = jnp.zeros_like(s_st)

        g = g_ref[...]
        col_b[...] = _sigmoid(g)
        col_d[...] = _chunk_cumsum(-jnp.exp(al_c_ref[...]) * _softplus(g + dt_c_ref[...]), CHUNK) * LOG2E
        gt = gt_ref[...]
        g_r = -jnp.exp(al_r_ref[...]) * _softplus(gt[0:h_d] + dt_r_ref[...])
        row_d[...] = _mm_f32(g_r.reshape(h_d * nct, CHUNK), _triu(CHUNK)).reshape(h_d, nct, CHUNK) * LOG2E

        r_i, c_i = _iota2((CHUNK, CHUNK), 0), _iota2((CHUNK, CHUNK), 1)
        tril = c_i <= r_i
        strict = c_i < r_i
        hs = range(h_d)

        def group(gi, carry):
            probs = [(cc, h) for cc in range(GDN_GROUP) for h in hs]
            cidx = [gi * GDN_GROUP + cc for cc in range(GDN_GROUP)]
            rows = [pl.ds(pl.multiple_of(ci * CHUNK, CHUNK), CHUNK) for ci in cidx]
            dc_all = [col_d[r, :] for r in rows]
            beta_all = [col_b[r, :] for r in rows]
            dc = [dc_all[cc][:, h:h + 1] for cc, h in probs]
            beta = [beta_all[cc][:, h_d + h:h_d + h + 1] for cc, h in probs]
            q = [qkv[prv_slot, rows[cc], h * HD_D:(h + 1) * HD_D] for cc, h in probs]
            k = [qkv[prv_slot, rows[cc], w_d + h * HD_D:w_d + (h + 1) * HD_D] for cc, h in probs]
            v = [qkv[prv_slot, rows[cc], 2 * w_d + h * HD_D:2 * w_d + (h + 1) * HD_D] for cc, h in probs]
            ps = range(len(probs))
            gam = [jnp.exp2(jnp.where(tril, dc[i] - row_d[probs[i][1], pl.ds(cidx[probs[i][0]], 1), :], NEG)) for i in ps]
            kb = [k[i] * beta[i] for i in ps]
            kk = [_mm_nt(kb[i], k[i]) for i in ps]
            t_inv = _unit_lower_inverse([jnp.where(strict, kk[i] * gam[i], 0.0) for i in ps])
            ed = [jnp.exp2(dc[i]) for i in ps]
            uw = [_mm(t_inv[i], jnp.concatenate([v[i] * beta[i], kb[i] * ed[i]], axis=1)) for i in ps]
            qs = [q[i] * scale for i in ps]
            qk = [_mm_nt(qs[i], k[i]) * gam[i] for i in ps]
            kd = [k[i] * jnp.exp2(dc[i][CHUNK - 1:CHUNK, :] - dc[i]) for i in ps]
            x1 = [_mm(qk[i], uw[i]) for i in ps]
            x2 = [_mm_tn(kd[i], uw[i]) for i in ps]
            for i, (cc, h) in enumerate(probs):
                lhs_s[cc, h] = jnp.concatenate([qs[i] * ed[i] - x1[i][:, HD_D:], x2[i][:, HD_D:]], axis=0).astype(BF16)
                o0_s[cc, h] = x1[i][:, :HD_D]
                n_s[cc, h] = x2[i][:, :HD_D]
            for rr in range(GDN_GROUP):
                conv_rows(cur_slot, gi * GDN_GROUP + rr)

            def step(cc, carry2):
                c = gi * GDN_GROUP + cc
                rws = pl.ds(pl.multiple_of(c * CHUNK, CHUNK), CHUNK)
                d_last = jnp.exp2(col_d[pl.ds(c * CHUNK + CHUNK - 1, 1), :])
                st = [s_st[h] for h in hs]
                r = [_mm(lhs_s[cc, h], st[h]) for h in hs]
                for h in hs:
                    o = o0_s[cc, h] + r[h][:CHUNK]
                    s_st[h] = d_last[:, h:h + 1] * st[h] + n_s[cc, h] - r[h][CHUNK:]
                    on = o * lax.rsqrt(jnp.mean(o * o, axis=-1, keepdims=True) + EPS) * on_ref[...]
                    z = z_ref[rws, h * HD_D:(h + 1) * HD_D].astype(F32)
                    y_ref[rws, h * HD_D:(h + 1) * HD_D] = (on * _silu(z)).astype(y_ref.dtype)
                return carry2

            lax.fori_loop(0, GDN_GROUP, step, 0)
            return carry

        lax.fori_loop(0, nct // GDN_GROUP, group, 0)
        save_tail()


def _gdn(p3, g3, gt4, cw, al_c, dt_c, al_r, dt_r, onorm, ts):
    b, s, _ = p3.shape
    w_d = 1024
    h_d = w_d // HD_D
    nct = ts // CHUNK
    nt = s // ts
    n_tiles = b * nt

    def conv_spec(sec):
        def imap(t, sec=sec):
            tc = jnp.minimum(t, n_tiles - 1)
            return (tc // nt, tc % nt, 2 + sec)
        return pl.BlockSpec((None, ts, w_d), imap)

    def prev_map(last):
        def imap(t):
            tp = jnp.maximum(t - 1, 0)
            return (tp // nt, tp % nt, last)
        return imap

    def gt_map(t):
        tp = jnp.maximum(t - 1, 0)
        return (tp // nt, 0, tp % nt, 0)

    return pl.pallas_call(
        functools.partial(_gdn_body, nt=nt),
        grid=(n_tiles + 1,),
        in_specs=[
            conv_spec(0), conv_spec(1), conv_spec(2),
            pl.BlockSpec((None, ts, w_d), prev_map(5)),
            pl.BlockSpec((None, ts, LANES), prev_map(0)),
            pl.BlockSpec((None, 2 * h_d, nct, CHUNK), gt_map),
            pl.BlockSpec((CONV_K, 3 * w_d), lambda t: (0, 0)),
            pl.BlockSpec((1, LANES), lambda t: (0, 0)),
            pl.BlockSpec((1, LANES), lambda t: (0, 0)),
            pl.BlockSpec((h_d, 1, 1), lambda t: (0, 0, 0)),
            pl.BlockSpec((h_d, 1, 1), lambda t: (0, 0, 0)),
            pl.BlockSpec((1, HD_D), lambda t: (0, 0)),
        ],
        out_specs=pl.BlockSpec((None, ts, w_d), prev_map(0)),
        out_shape=jax.ShapeDtypeStruct((b, s, w_d), BF16),
        scratch_shapes=[
            pltpu.VMEM((h_d, HD_D, HD_D), F32),
            pltpu.VMEM((16, 3 * w_d), F32),
            pltpu.VMEM((2, ts, 3 * w_d), F32),
            pltpu.VMEM((ts, LANES), F32),
            pltpu.VMEM((ts, LANES), F32),
            pltpu.VMEM((h_d, nct, CHUNK), F32),
            pltpu.VMEM((GDN_GROUP, h_d, CHUNK + HD_D, HD_D), BF16),
            pltpu.VMEM((GDN_GROUP, h_d, CHUNK, HD_D), F32),
            pltpu.VMEM((GDN_GROUP, h_d, HD_D, HD_D), F32),
        ],
        compiler_params=_params(1),
        name="gated_delta",
    )(p3, p3, p3, p3, g3, gt4, cw, al_c, dt_c, al_r, dt_r, onorm)


def _rel_row(rel_bias):
    t = (np.arange(REL_W) + CHUNK - 1) % REL_W
    idx = np.clip(LOOKBACK * CHUNK + CHUNK - 1 - t, -REL_MAX, REL_MAX) + REL_MAX
    return rel_bias[:, idx].astype(F32)[:, None, :]


def _pad_lanes(v, n=LANES):
    return jnp.pad(v, ((0, 0), (0, n - v.shape[1])))


def _pad_rows(v, n=LANES):
    return jnp.pad(v, ((0, n - v.shape[0]), (0, 0)))


def _weight_t(w):
    return jnp.swapaxes(w, 0, 1).astype(BF16)


def _gates_t(g3, n, length):
    b, s, _ = g3.shape
    return jnp.transpose(g3[:, :, :n], (0, 2, 1)).reshape(b, n, s // length, length)


def kernel(x, ev_norm, ev_w_in, ev_if_bias, ev_qn_gain, ev_kn_gain, ev_rel_bias, ev_w_out, od_norm, od_w_in, od_conv_c_w, od_conv_c_b, od_gate_w, od_gate_b, od_lambda, od_conv_d_w, od_a_log, od_dt_bias, od_onorm, od_w_out):
    b, s, d = x.shape
    m = b * s
    half = d // 2
    tm = min(1024, m)
    ts = min(512, s)
    assert half == 1024 and m % tm == 0 and s % ts == 0 and s % (CHUNK * ATTN_GROUP) == 0 and ts % MLSTM_L == 0
    x2 = x.reshape(m, d)
    depth = ev_norm.shape[0] + od_norm.shape[0]
    for layer in range(depth):
        j = layer // 2
        if layer % 2 == 0:
            n_main = 9 * half
            w_t = _weight_t(ev_w_in[j])
            p, g = _inproj(x2, ev_norm[j].reshape(1, d), w_t, _pad_rows(w_t[n_main:]), n_main, tm, n_main // 4)
            p3, g3 = p.reshape(b, s, n_main), g.reshape(b, s, LANES)
            ya = _attn(p3, ev_qn_gain[j].reshape(1, HD_A), ev_kn_gain[j].reshape(1, HD_A), _rel_row(ev_rel_bias[j]))
            bias = ev_if_bias[j].astype(F32)
            yb = _mlstm(p3, _gates_t(g3, 2 * H_B, MLSTM_L), bias.reshape(-1, 1, 1), ts)
            x2 = _outproj(ya.reshape(m, half), yb.reshape(m, half), ev_w_out[j], x2, tm, 1024)
        else:
            n_main = 6 * half
            h_d = half // HD_D
            w_t = _weight_t(od_w_in[j])
            p, g = _inproj(x2, od_norm[j].reshape(1, d), w_t, _pad_rows(w_t[n_main:]), n_main, tm, n_main // 3)
            p3, g3 = p.reshape(b, s, n_main), g.reshape(b, s, LANES)
            yc = _rglru(p3, od_conv_c_w[j], od_conv_c_b[j].reshape(1, -1), od_gate_w[j].astype(BF16),
                        od_gate_b[j].reshape(1, -1), od_lambda[j].reshape(1, -1))
            al, dt = od_a_log[j].astype(F32), od_dt_bias[j].astype(F32)
            yd = _gdn(p3, g3, _gates_t(g3, 2 * h_d, CHUNK), od_conv_d_w[j], _pad_lanes(al.reshape(1, -1)),
                      _pad_lanes(dt.reshape(1, -1)), al.reshape(-1, 1, 1), dt.reshape(-1, 1, 1),
                      od_onorm[j].reshape(1, HD_D), ts)
            x2 = _outproj(yc.reshape(m, half), yd.reshape(m, half), od_w_out[j], x2, tm, 1024)
    return x2.reshape(b, s, d)
```

```python
import functools

import numpy as np
import jax
import jax.numpy as jnp
from jax import lax
from jax.experimental import pallas as pl
from jax.experimental.pallas import tpu as pltpu

F32 = jnp.float32
BF16 = jnp.bfloat16

CHUNK = 64
EPS = 1e-6
NEG = -1e30
CONV_K = 4
LOG2E = 1.4426950408889634
LANES = 128
HD_A = 128
LOOKBACK = 8
BAND = (LOOKBACK + 1) * CHUNK
REL_MAX = 256
REL_W = 640
ATTN_GROUP = 16
ATTN_HEADS = 4
H_B = 4
MLSTM_L = 256
N_BLK_C = 8
C_RG = 8.0
RG_COLS = 1024
HD_D = 128
GDN_GROUP = 4

V7X_VMEM_BYTES = 64 * 1024 * 1024
VMEM_LIMIT = V7X_VMEM_BYTES - 8 * 1024 * 1024


def _params(n_axes):
    return pltpu.CompilerParams(dimension_semantics=("arbitrary",) * n_axes, vmem_limit_bytes=VMEM_LIMIT)


def _mm(a, b):
    return jnp.dot(a.astype(BF16), b.astype(BF16), preferred_element_type=F32)


def _mm_nt(a, b):
    return lax.dot_general(a.astype(BF16), b.astype(BF16), (((1,), (1,)), ((), ())), preferred_element_type=F32)


def _mm_tn(a, b):
    return lax.dot_general(a.astype(BF16), b.astype(BF16), (((0,), (0,)), ((), ())), preferred_element_type=F32)


def _mm_f32(a, b):
    return jnp.dot(a, b, preferred_element_type=F32, precision=lax.Precision.HIGHEST)


def _log1p(e):
    u = 1.0 + e
    return jnp.where(u == 1.0, e, jnp.log(u) * (e / (u - 1.0)))


def _softplus(x):
    return jnp.maximum(x, 0.0) + _log1p(jnp.exp(-jnp.abs(x)))


def _sigmoid(x):
    return 1.0 / (1.0 + jnp.exp2(x * -LOG2E))


def _silu(x):
    return x * _sigmoid(x)


def _iota2(shape, axis):
    return lax.broadcasted_iota(jnp.int32, shape, axis)


def _chunk_cumsum(x, length):
    pos = _iota2(x.shape, 0) & (length - 1)
    d = 1
    while d < length:
        x = x + jnp.where(pos >= d, pltpu.roll(x, d, 0), 0.0)
        d *= 2
    return x


def _triu(n):
    r, c = _iota2((n, n), 0), _iota2((n, n), 1)
    return jnp.where(r <= c, 1.0, 0.0).astype(F32)


def _inproj_body(x_ref, g_ref, w_ref, wg_ref, o_ref, og_ref, xn_ref):
    @pl.when(pl.program_id(1) == 0)
    def _():
        x = x_ref[...]
        ms = jnp.mean(x * x, axis=-1, keepdims=True)
        xn = ((x * lax.rsqrt(ms + EPS)) * g_ref[...]).astype(BF16)
        xn_ref[...] = xn
        og_ref[...] = _mm_nt(xn, wg_ref[...])

    o_ref[...] = _mm_nt(xn_ref[...], w_ref[...]).astype(o_ref.dtype)


def _inproj(x2, g, w_t, wg_t, n, tm, tn):
    m, d = x2.shape
    return pl.pallas_call(
        _inproj_body,
        grid=(m // tm, n // tn),
        in_specs=[
            pl.BlockSpec((tm, d), lambda i, j: (i, 0)),
            pl.BlockSpec((1, d), lambda i, j: (0, 0)),
            pl.BlockSpec((tn, d), lambda i, j: (j, 0)),
            pl.BlockSpec((LANES, d), lambda i, j: (0, 0)),
        ],
        out_specs=[
            pl.BlockSpec((tm, tn), lambda i, j: (i, j)),
            pl.BlockSpec((tm, LANES), lambda i, j: (i, 0)),
        ],
        out_shape=[jax.ShapeDtypeStruct((m, n), BF16), jax.ShapeDtypeStruct((m, LANES), F32)],
        scratch_shapes=[pltpu.VMEM((tm, d), BF16)],
        compiler_params=_params(2),
        name="inproj",
    )(x2, g, w_t, wg_t)


def _outproj_body(ya_ref, yb_ref, wa_ref, wb_ref, x_ref, o_ref, wa_s, wb_s):
    @pl.when(pl.program_id(1) == 0)
    def _():
        wa_s[...] = wa_ref[...].astype(BF16)
        wb_s[...] = wb_ref[...].astype(BF16)

    acc = jnp.dot(ya_ref[...], wa_s[...], preferred_element_type=F32)
    acc = acc + jnp.dot(yb_ref[...], wb_s[...], preferred_element_type=F32)
    o_ref[...] = x_ref[...] + acc


def _outproj(ya, yb, w, x2, tm, tn):
    m, d = x2.shape
    kh = ya.shape[1]
    return pl.pallas_call(
        _outproj_body,
        grid=(d // tn, m // tm),
        in_specs=[
            pl.BlockSpec((tm, kh), lambda j, i: (i, 0)),
            pl.BlockSpec((tm, kh), lambda j, i: (i, 0)),
            pl.BlockSpec((kh, tn), lambda j, i: (0, j)),
            pl.BlockSpec((kh, tn), lambda j, i: (1, j)),
            pl.BlockSpec((tm, tn), lambda j, i: (i, j)),
        ],
        out_specs=pl.BlockSpec((tm, tn), lambda j, i: (i, j)),
        out_shape=jax.ShapeDtypeStruct((m, d), F32),
        scratch_shapes=[pltpu.VMEM((kh, tn), BF16), pltpu.VMEM((kh, tn), BF16)],
        compiler_params=_params(2),
        name="outproj",
    )(ya, yb, w, w, x2)


def _attn_body(q_ref, k_ref, v_ref, z_ref, qg_ref, kg_ref, c_ref, o_ref, qs, ks, vs, bias_s):
    s = q_ref.shape[0]
    pad = LOOKBACK * CHUNK
    col = _iota2((CHUNK, BAND), 1)
    assert ATTN_GROUP >= LOOKBACK
    for hh in range(ATTN_HEADS):
        hc = slice(hh * HD_A, (hh + 1) * HD_A)
        q = q_ref[:, hc].astype(F32)
        q = q * lax.rsqrt(jnp.mean(q * q, axis=-1, keepdims=True) + EPS) * qg_ref[...] * (HD_A ** -0.5 * LOG2E)
        qs[hh] = q.astype(BF16)
        k = k_ref[:, hc].astype(F32)
        k = k * lax.rsqrt(jnp.mean(k * k, axis=-1, keepdims=True) + EPS) * kg_ref[...]
        ks[hh, 0:pad, :] = jnp.zeros((pad, HD_A), BF16)
        ks[hh, pad:pad + s, :] = k.astype(BF16)
        vs[hh, 0:pad, :] = jnp.zeros((pad, HD_A), BF16)
        vs[hh, pad:pad + s, :] = v_ref[:, hc]
        cb = jnp.broadcast_to(c_ref[hh] * LOG2E, (CHUNK, REL_W))
        bias_s[hh] = pltpu.roll(cb, 0, 1, stride=1, stride_axis=0)[:, :BAND]

        def chunks(it, masked, hh=hh, hc=hc):
            gs = range(ATTN_GROUP)
            n = [it * ATTN_GROUP + g for g in gs]
            r0 = [pl.multiple_of(n[g] * CHUNK, CHUNK) for g in gs]
            sc = [_mm_nt(qs[hh, pl.ds(r0[g], CHUNK), :], ks[hh, pl.ds(r0[g], BAND), :]) + bias_s[hh] for g in gs]
            if masked:
                sc = [jnp.where(col >= (LOOKBACK - n[g]) * CHUNK, sc[g], NEG) for g in gs]
            p = [jnp.exp2(sc[g] - jnp.max(sc[g], axis=-1, keepdims=True)) for g in gs]
            pv = [_mm(p[g], vs[hh, pl.ds(r0[g], BAND), :]) for g in gs]
            for g in gs:
                o = pv[g] * (1.0 / jnp.sum(p[g], axis=-1, keepdims=True))
                z = z_ref[pl.ds(r0[g], CHUNK), hc].astype(F32)
                o_ref[pl.ds(r0[g], CHUNK), hc] = (o * _silu(z)).astype(o_ref.dtype)

        chunks(0, True)

        def rest(it, carry, chunks=chunks):
            chunks(it, False)
            return carry

        lax.fori_loop(1, s // (CHUNK * ATTN_GROUP), rest, 0)


def _attn(p3, qg, kg, crel):
    b, s, _ = p3.shape
    h_a = crel.shape[0]
    hp = h_a // ATTN_HEADS
    wh = ATTN_HEADS * HD_A

    def col_spec(sec):
        return pl.BlockSpec((None, s, wh), lambda bi, h, sec=sec: (bi, 0, sec * hp + h))

    return pl.pallas_call(
        _attn_body,
        grid=(b, hp),
        in_specs=[
            col_spec(0), col_spec(1), col_spec(2), col_spec(3),
            pl.BlockSpec((1, HD_A), lambda bi, h: (0, 0)),
            pl.BlockSpec((1, HD_A), lambda bi, h: (0, 0)),
            pl.BlockSpec((ATTN_HEADS, 1, REL_W), lambda bi, h: (h, 0, 0)),
        ],
        out_specs=pl.BlockSpec((None, s, wh), lambda bi, h: (bi, 0, h)),
        out_shape=jax.ShapeDtypeStruct((b, s, h_a * HD_A), BF16),
        scratch_shapes=[
            pltpu.VMEM((ATTN_HEADS, s, HD_A), BF16),
            pltpu.VMEM((ATTN_HEADS, s + LOOKBACK * CHUNK, HD_A), BF16),
            pltpu.VMEM((ATTN_HEADS, s + LOOKBACK * CHUNK, HD_A), BF16),
            pltpu.VMEM((ATTN_HEADS, CHUNK, BAND), F32),
        ],
        compiler_params=_params(2),
        name="chunk_attn",
    )(p3, p3, p3, p3, qg, kg, crel)


def _mlstm_body(q_ref, k_ref, v_ref, o_ref, z_ref, gt_ref, br_ref, y_ref, c_st, n_st, m_st, row_b, row_i):
    ts = q_ref.shape[0]
    ln = MLSTM_L
    nct = ts // ln
    hd = q_ref.shape[1] // H_B
    scale = hd ** -0.5
    j = pl.program_id(1)
    hs = range(H_B)
    hc = [slice(h * hd, (h + 1) * hd) for h in hs]

    @pl.when(j == 0)
    def _():
        c_st[...] = jnp.zeros_like(c_st)
        n_st[...] = jnp.zeros_like(n_st)
        m_st[...] = jnp.zeros_like(m_st)
        gt = gt_ref[...] + br_ref[...]
        for h in hs:
            row_b[h] = _mm_f32(-_softplus(-gt[H_B + h]), _triu(ln)) * LOG2E
            row_i[h] = gt[h] * LOG2E

    tril = _iota2((ln, ln), 1) <= _iota2((ln, ln), 0)
    log2_scale = float(np.log2(scale))
    m = [m_st[h][0:1, 0:1] for h in hs]
    cm = [c_st[h] for h in hs]
    nm = [n_st[h][0:1, :] for h in hs]
    chs = [(c, h) for c in range(nct) for h in hs]
    rows = [slice(c * ln, (c + 1) * ln) for c in range(nct)]
    li_r = {(c, h): row_i[h, pl.ds(j * nct + c, 1), :] for c, h in chs}
    bc_r = {(c, h): row_b[h, pl.ds(j * nct + c, 1), :] for c, h in chs}
    pad = jnp.zeros((LANES - 2 * H_B, ln), F32)
    cols = [jnp.concatenate([li_r[c, h] for h in hs] + [bc_r[c, h] for h in hs] + [pad], axis=0).T
            for c in range(nct)]
    lic = {(c, h): cols[c][:, h:h + 1] for c, h in chs}
    bc = {(c, h): cols[c][:, H_B + h:H_B + h + 1] for c, h in chs}
    rl = {ch: li_r[ch] - bc_r[ch] for ch in chs}
    b_last = {ch: bc[ch][ln - 1:ln, :] for ch in chs}
    m_in, m_new = {}, {}
    for c, h in chs:
        m_in[c, h] = m[h]
        m_new[c, h] = jnp.maximum(b_last[c, h] + m[h], jnp.max(b_last[c, h] + rl[c, h], axis=-1, keepdims=True))
        m[h] = m_new[c, h]
    qn = {(c, h): q_ref[rows[c], hc[h]] for c, h in chs}
    kn = {(c, h): k_ref[rows[c], hc[h]] for c, h in chs}
    vn = {(c, h): v_ref[rows[c], hc[h]] for c, h in chs}
    qk = {ch: _mm_nt(qn[ch], kn[ch]) for ch in chs}
    dmat = {ch: jnp.where(tril, bc[ch] + rl[ch], NEG) for ch in chs}
    mt = {ch: jnp.maximum(bc[ch] + m_in[ch], jnp.max(dmat[ch], axis=-1, keepdims=True)) for ch in chs}
    cmt = {ch: mt[ch] - log2_scale for ch in chs}
    w_inter = {ch: jnp.exp2(bc[ch] + m_in[ch] - cmt[ch]) for ch in chs}
    p = {ch: jnp.exp2(dmat[ch] - cmt[ch]) * qk[ch] for ch in chs}
    pv = {ch: _mm(p[ch], vn[ch]) for ch in chs}
    ws_c = {ch: jnp.exp2(b_last[ch] - bc[ch] + lic[ch] - m_new[ch]).astype(BF16) for ch in chs}
    ws_r = {ch: jnp.exp2(b_last[ch] + rl[ch] - m_new[ch]) for ch in chs}
    upd = {ch: _mm_tn(kn[ch] * ws_c[ch], vn[ch]) for ch in chs}
    n_add = {ch: _mm(jnp.broadcast_to(ws_r[ch], (8, ln)), kn[ch])[0:1, :] for ch in chs}
    psum = {ch: jnp.sum(p[ch], axis=-1, keepdims=True) for ch in chs}
    gate = {}
    for c, h in chs:
        og = o_ref[rows[c], hc[h]].astype(F32)
        zg = z_ref[rows[c], hc[h]].astype(F32)
        gate[c, h] = zg * (1.0 / ((1.0 + jnp.exp2(og * -LOG2E)) * (1.0 + jnp.exp2(zg * -LOG2E))))
    for c in range(nct):
        qc = {h: _mm(qn[c, h], cm[h]) for h in hs}
        qn_dot = {h: _mm_nt(qn[c, h], jnp.broadcast_to(nm[h], (8, hd)))[:, 0:1] for h in hs}
        for h in hs:
            num = w_inter[c, h] * qc[h] + pv[c, h]
            den = w_inter[c, h] * qn_dot[h] + psum[c, h]
            hout = num * (1.0 / jnp.maximum(jnp.abs(den), jnp.exp2(-mt[c, h])))
            y_ref[rows[c], hc[h]] = (hout * gate[c, h]).astype(y_ref.dtype)
            w_c = jnp.exp2(b_last[c, h] + m_in[c, h] - m_new[c, h])
            cm[h] = w_c * cm[h] + upd[c, h]
            nm[h] = w_c * nm[h] + n_add[c, h]
    for h in hs:
        c_st[h] = cm[h]
        n_st[h] = jnp.broadcast_to(nm[h], n_st.shape[1:])
        m_st[h] = jnp.broadcast_to(m[h], m_st.shape[1:])


def _mlstm(p3, gt4, bias_row, ts):
    b, s, _ = p3.shape
    w_b = 1024
    hd = w_b // H_B
    ncs = s // MLSTM_L

    def col_spec(sec):
        return pl.BlockSpec((None, ts, w_b), lambda bi, j, sec=sec: (bi, j, 4 + sec))

    return pl.pallas_call(
        _mlstm_body,
        grid=(b, s // ts),
        in_specs=[
            col_spec(0), col_spec(1), col_spec(2), col_spec(3), col_spec(4),
            pl.BlockSpec((None, 2 * H_B, ncs, MLSTM_L), lambda bi, j: (bi, 0, 0, 0)),
            pl.BlockSpec((2 * H_B, 1, 1), lambda bi, j: (0, 0, 0)),
        ],
        out_specs=pl.BlockSpec((None, ts, w_b), lambda bi, j: (bi, j, 0)),
        out_shape=jax.ShapeDtypeStruct((b, s, w_b), BF16),
        scratch_shapes=[
            pltpu.VMEM((H_B, hd, hd), F32),
            pltpu.VMEM((H_B, 8, hd), F32),
            pltpu.VMEM((H_B, 8, LANES), F32),
            pltpu.VMEM((H_B, ncs, MLSTM_L), F32),
            pltpu.VMEM((H_B, ncs, MLSTM_L), F32),
        ],
        compiler_params=_params(2),
        name="mlstm",
    )(p3, p3, p3, p3, p3, gt4, bias_row)


def _rglru_body(x_ref, z_ref, cw_ref, cb_ref, gw_ref, gbr_ref, gbi_ref, lam_ref, y_ref, a_s, b_s):
    s, wc = x_ref.shape
    blk = gw_ref.shape[1]
    row8 = _iota2((8, blk), 0)
    sub = _iota2((s // 8, 8, blk), 1)
    for n in range(wc // blk):
        cs = slice(n * blk, (n + 1) * blk)
        x = x_ref[:, cs].astype(F32)
        cw = cw_ref[:, cs]
        xc = cw[CONV_K - 1:CONV_K, :] * x + cb_ref[:, cs]
        for d in range(1, CONV_K):
            xs = pltpu.roll(x, d, 0)
            xs = jnp.concatenate([jnp.where(row8 >= d, xs[0:8], 0.0), xs[8:]], axis=0)
            xc = xc + cw[CONV_K - 1 - d:CONV_K - d, :] * xs
        gates = _mm(xc, gw_ref[n])
        r = _sigmoid(gates[:, :blk] + gbr_ref[:, cs])
        i = _sigmoid(gates[:, blk:] + gbi_ref[:, cs])
        nla = r * (C_RG * _softplus(-lam_ref[:, cs]))
        a = jnp.exp2(nla * -LOG2E)
        var = jnp.tanh(nla) * (a * a + 1.0)
        bb = jnp.where(var > 0.0, var * lax.rsqrt(var), 0.0) * (i * xc)
        a = a.reshape(s // 8, 8, blk)
        bb = bb.reshape(s // 8, 8, blk)
        for d in (1, 2, 4):
            keep = sub >= d
            a_sh = jnp.where(keep, pltpu.roll(a, d, 1), 1.0)
            b_sh = jnp.where(keep, pltpu.roll(bb, d, 1), 0.0)
            bb = a * b_sh + bb
            a = a * a_sh
        a_s[:, cs] = a.reshape(s, blk)
        b_s[:, cs] = bb.reshape(s, blk)

    def group(gi, carry):
        rows = pl.ds(pl.multiple_of(gi * 8, 8), 8)
        h = a_s[rows, :] * carry + b_s[rows, :]
        b_s[rows, :] = h
        return jnp.broadcast_to(h[7:8, :], h.shape)

    lax.fori_loop(0, s // 8, group, jnp.zeros((8, wc), F32), unroll=8)
    z = z_ref[...].astype(F32)
    y_ref[...] = (b_s[...] * _silu(z)).astype(y_ref.dtype)


def _rglru(p3, cw, cb, gw, gb, lam):
    b, s, _ = p3.shape
    w_c = cw.shape[1]
    blk = w_c // N_BLK_C
    nsp = w_c // RG_COLS
    return pl.pallas_call(
        _rglru_body,
        grid=(b, nsp),
        in_specs=[
            pl.BlockSpec((None, s, RG_COLS), lambda bi, n: (bi, 0, n)),
            pl.BlockSpec((None, s, RG_COLS), lambda bi, n: (bi, 0, nsp + n)),
            pl.BlockSpec((CONV_K, RG_COLS), lambda bi, n: (0, n)),
            pl.BlockSpec((1, RG_COLS), lambda bi, n: (0, n)),
            pl.BlockSpec((RG_COLS // blk, blk, 2 * blk), lambda bi, n: (n, 0, 0)),
            pl.BlockSpec((1, RG_COLS), lambda bi, n: (0, n)),
            pl.BlockSpec((1, RG_COLS), lambda bi, n: (0, nsp + n)),
            pl.BlockSpec((1, RG_COLS), lambda bi, n: (0, n)),
        ],
        out_specs=pl.BlockSpec((None, s, RG_COLS), lambda bi, n: (bi, 0, n)),
        out_shape=jax.ShapeDtypeStruct((b, s, w_c), BF16),
        scratch_shapes=[pltpu.VMEM((s, RG_COLS), F32), pltpu.VMEM((s, RG_COLS), F32)],
        compiler_params=_params(2),
        name="rglru",
    )(p3, p3, cw, cb, gw, gb, gb, lam)


def _unit_lower_inverse(a_list):
    r, c = _iota2((CHUNK, CHUNK), 0), _iota2((CHUNK, CHUNK), 1)
    eye = jnp.where(r == c, 1.0, 0.0).astype(F32)

    def pair_mask(sh):
        rb, cb_ = r >> sh, c >> sh
        return ((rb & 1) == 1) & (cb_ == rb - 1)

    ts = [eye - jnp.where(pair_mask(0), a, 0.0) for a in a_list]
    for sh in range(1, 6):
        mask = pair_mask(sh)
        x1 = [_mm(t, jnp.where(mask, a, 0.0)) for t, a in zip(ts, a_list)]
        x2 = [_mm(x, t) for x, t in zip(x1, ts)]
        ts = [t - x for t, x in zip(ts, x2)]
    return ts


def _gdn_body(q_ref, k_ref, v_ref, z_ref, g_ref, gt_ref, cw_ref, al_c_ref, dt_c_ref, al_r_ref, dt_r_ref, on_ref,
              y_ref, s_st, tail, qkv, col_d, col_b, row_d, lhs_s, o0_s, n_s, *, nt):
    ts = q_ref.shape[0]
    nct = ts // CHUNK
    w_d = q_ref.shape[1]
    h_d = w_d // HD_D
    scale = HD_D ** -0.5
    t = pl.program_id(0)
    srcs = (q_ref, k_ref, v_ref)

    def conv_block(slot, r0, first, cbk):
        ref = srcs[cbk // h_d]
        ci = slice((cbk % h_d) * HD_D, (cbk % h_d + 1) * HD_D)
        cs = slice(cbk * HD_D, (cbk + 1) * HD_D)
        cur = ref[pl.ds(r0, CHUNK), ci].astype(F32)
        prev = ref[pl.ds(pl.multiple_of(jnp.maximum(r0 - 16, 0), 16), 16), ci].astype(F32)
        xw = jnp.concatenate([jnp.where(first, tail[:, cs], prev), cur], axis=0)
        acc = cw_ref[CONV_K - 1:CONV_K, cs] * cur
        for d in range(1, CONV_K):
            acc = acc + cw_ref[CONV_K - 1 - d:CONV_K - d, cs] * xw[16 - d:16 - d + CHUNK]
        acc = _silu(acc)
        if cbk < 2 * h_d:
            acc = acc * lax.rsqrt(jnp.sum(acc * acc, axis=-1, keepdims=True) + EPS)
        qkv[slot, pl.ds(r0, CHUNK), cs] = acc

    def conv_rows(slot, rb):
        r0 = pl.multiple_of(rb * CHUNK, CHUNK)
        for cbk in range(3 * h_d):
            conv_block(slot, r0, rb == 0, cbk)

    def save_tail():
        keep = lax.rem(t + 1, nt) != 0
        for w, ref in enumerate(srcs):
            tail[:, w * w_d:(w + 1) * w_d] = jnp.where(keep, ref[ts - 16:ts, :].astype(F32), 0.0)

    @pl.when(t == 0)
    def _():
        tail[...] = jnp.zeros_like(tail)

        def rows(rb, carry):
            conv_rows(0, rb)
            return carry

        lax.fori_loop(0, nct, rows, 0)
        save_tail()

    @pl.when(t > 0)
    def _():
        cur_slot = lax.rem(t, 2)
        prv_slot = 1 - cur_slot

        @pl.when(lax.rem(t - 1, nt) == 0)
        def _():
            s_st[...] = jnp.zeros_like(s_st)

        g = g_ref[...]
        col_b[...] = _sigmoid(g)
        col_d[...] = _chunk_cumsum(-jnp.exp(al_c_ref[...]) * _softplus(g + dt_c_ref[...]), CHUNK) * LOG2E
        gt = gt_ref[...]
        g_r = -jnp.exp(al_r_ref[...]) * _softplus(gt[0:h_d] + dt_r_ref[...])
        row_d[...] = _mm_f32(g_r.reshape(h_d * nct, CHUNK), _triu(CHUNK)).reshape(h_d, nct, CHUNK) * LOG2E

        r_i, c_i = _iota2((CHUNK, CHUNK), 0), _iota2((CHUNK, CHUNK), 1)
        tril = c_i <= r_i
        strict = c_i < r_i
        hs = range(h_d)

        def group(gi, carry):
            probs = [(cc, h) for cc in range(GDN_GROUP) for h in hs]
            cidx = [gi * GDN_GROUP + cc for cc in range(GDN_GROUP)]
            rows = [pl.ds(pl.multiple_of(ci * CHUNK, CHUNK), CHUNK) for ci in cidx]
            dc_all = [col_d[r, :] for r in rows]
            beta_all = [col_b[r, :] for r in rows]
            dc = [dc_all[cc][:, h:h + 1] for cc, h in probs]
            beta = [beta_all[cc][:, h_d + h:h_d + h + 1] for cc, h in probs]
            q = [qkv[prv_slot, rows[cc], h * HD_D:(h + 1) * HD_D] for cc, h in probs]
            k = [qkv[prv_slot, rows[cc], w_d + h * HD_D:w_d + (h + 1) * HD_D] for cc, h in probs]
            v = [qkv[prv_slot, rows[cc], 2 * w_d + h * HD_D:2 * w_d + (h + 1) * HD_D] for cc, h in probs]
            ps = range(len(probs))
            gam = [jnp.exp2(jnp.where(tril, dc[i] - row_d[probs[i][1], pl.ds(cidx[probs[i][0]], 1), :], NEG)) for i in ps]
            kb = [k[i] * beta[i] for i in ps]
            kk = [_mm_nt(kb[i], k[i]) for i in ps]
            t_inv = _unit_lower_inverse([jnp.where(strict, kk[i] * gam[i], 0.0) for i in ps])
            ed = [jnp.exp2(dc[i]) for i in ps]
            uw = [_mm(t_inv[i], jnp.concatenate([v[i] * beta[i], kb[i] * ed[i]], axis=1)) for i in ps]
            qs = [q[i] * scale for i in ps]
            qk = [_mm_nt(qs[i], k[i]) * gam[i] for i in ps]
            kd = [k[i] * jnp.exp2(dc[i][CHUNK - 1:CHUNK, :] - dc[i]) for i in ps]
            x1 = [_mm(qk[i], uw[i]) for i in ps]
            x2 = [_mm_tn(kd[i], uw[i]) for i in ps]
            for i, (cc, h) in enumerate(probs):
                lhs_s[cc, h] = jnp.concatenate([qs[i] * ed[i] - x1[i][:, HD_D:], x2[i][:, HD_D:]], axis=0).astype(BF16)
                o0_s[cc, h] = x1[i][:, :HD_D]
                n_s[cc, h] = x2[i][:, :HD_D]
            for rr in range(GDN_GROUP):
                conv_rows(cur_slot, gi * GDN_GROUP + rr)

            def step(cc, carry2):
                c = gi * GDN_GROUP + cc
                rws = pl.ds(pl.multiple_of(c * CHUNK, CHUNK), CHUNK)
                d_last = jnp.exp2(col_d[pl.ds(c * CHUNK + CHUNK - 1, 1), :])
                st = [s_st[h] for h in hs]
                r = [_mm(lhs_s[cc, h], st[h]) for h in hs]
                for h in hs:
                    o = o0_s[cc, h] + r[h][:CHUNK]
                    s_st[h] = d_last[:, h:h + 1] * st[h] + n_s[cc, h] - r[h][CHUNK:]
                    on = o * lax.rsqrt(jnp.mean(o * o, axis=-1, keepdims=True) + EPS) * on_ref[...]
                    z = z_ref[rws, h * HD_D:(h + 1) * HD_D].astype(F32)
                    y_ref[rws, h * HD_D:(h + 1) * HD_D] = (on * _silu(z)).astype(y_ref.dtype)
                return carry2

            lax.fori_loop(0, GDN_GROUP, step, 0)
            return carry

        lax.fori_loop(0, nct // GDN_GROUP, group, 0)
        save_tail()


def _gdn(p3, g3, gt4, cw, al_c, dt_c, al_r, dt_r, onorm, ts):
    b, s, _ = p3.shape
    w_d = 1024
    h_d = w_d // HD_D
    nct = ts // CHUNK
    nt = s // ts
    n_tiles = b * nt

    def conv_spec(sec):
        def imap(t, sec=sec):
            tc = jnp.minimum(t, n_tiles - 1)
            return (tc // nt, tc % nt, 2 + sec)
        return pl.BlockSpec((None, ts, w_d), imap)

    def prev_map(last):
        def imap(t):
            tp = jnp.maximum(t - 1, 0)
            return (tp // nt, tp % nt, last)
        return imap

    def gt_map(t):
        tp = jnp.maximum(t - 1, 0)
        return (tp // nt, 0, tp % nt, 0)

    return pl.pallas_call(
        functools.partial(_gdn_body, nt=nt),
        grid=(n_tiles + 1,),
        in_specs=[
            conv_spec(0), conv_spec(1), conv_spec(2),
            pl.BlockSpec((None, ts, w_d), prev_map(5)),
            pl.BlockSpec((None, ts, LANES), prev_map(0)),
            pl.BlockSpec((None, 2 * h_d, nct, CHUNK), gt_map),
            pl.BlockSpec((CONV_K, 3 * w_d), lambda t: (0, 0)),
            pl.BlockSpec((1, LANES), lambda t: (0, 0)),
            pl.BlockSpec((1, LANES), lambda t: (0, 0)),
            pl.BlockSpec((h_d, 1, 1), lambda t: (0, 0, 0)),
            pl.BlockSpec((h_d, 1, 1), lambda t: (0, 0, 0)),
            pl.BlockSpec((1, HD_D), lambda t: (0, 0)),
        ],
        out_specs=pl.BlockSpec((None, ts, w_d), prev_map(0)),
        out_shape=jax.ShapeDtypeStruct((b, s, w_d), BF16),
        scratch_shapes=[
            pltpu.VMEM((h_d, HD_D, HD_D), F32),
            pltpu.VMEM((16, 3 * w_d), F32),
            pltpu.VMEM((2, ts, 3 * w_d), F32),
            pltpu.VMEM((ts, LANES), F32),
            pltpu.VMEM((ts, LANES), F32),
            pltpu.VMEM((h_d, nct, CHUNK), F32),
            pltpu.VMEM((GDN_GROUP, h_d, CHUNK + HD_D, HD_D), BF16),
            pltpu.VMEM((GDN_GROUP, h_d, CHUNK, HD_D), F32),
            pltpu.VMEM((GDN_GROUP, h_d, HD_D, HD_D), F32),
        ],
        compiler_params=_params(1),
        name="gated_delta",
    )(p3, p3, p3, p3, g3, gt4, cw, al_c, dt_c, al_r, dt_r, onorm)


def _rel_row(rel_bias):
    t = (np.arange(REL_W) + CHUNK - 1) % REL_W
    idx = np.clip(LOOKBACK * CHUNK + CHUNK - 1 - t, -REL_MAX, REL_MAX) + REL_MAX
    return rel_bias[:, idx].astype(F32)[:, None, :]


def _pad_lanes(v, n=LANES):
    return jnp.pad(v, ((0, 0), (0, n - v.shape[1])))


def _pad_rows(v, n=LANES):
    return jnp.pad(v, ((0, n - v.shape[0]), (0, 0)))


def _weight_t(w):
    return jnp.swapaxes(w, 0, 1).astype(BF16)


def _gates_t(g3, n, length):
    b, s, _ = g3.shape
    return jnp.transpose(g3[:, :, :n], (0, 2, 1)).reshape(b, n, s // length, length)


def kernel(x, ev_norm, ev_w_in, ev_if_bias, ev_qn_gain, ev_kn_gain, ev_rel_bias, ev_w_out, od_norm, od_w_in, od_conv_c_w, od_conv_c_b, od_gate_w, od_gate_b, od_lambda, od_conv_d_w, od_a_log, od_dt_bias, od_onorm, od_w_out):
    b, s, d = x.shape
    m = b * s
    half = d // 2
    tm = min(1024, m)
    ts = min(512, s)
    assert half == 1024 and m % tm == 0 and s % ts == 0 and s % (CHUNK * ATTN_GROUP) == 0 and ts % MLSTM_L == 0
    x2 = x.reshape(m, d)
    depth = ev_norm.shape[0] + od_norm.shape[0]
    for layer in range(depth):
        j = layer // 2
        if layer % 2 == 0:
            n_main = 9 * half
            w_t = _weight_t(ev_w_in[j])
            p, g = _inproj(x2, ev_norm[j].reshape(1, d), w_t, _pad_rows(w_t[n_main:]), n_main, tm, n_main // 4)
            p3, g3 = p.reshape(b, s, n_main), g.reshape(b, s, LANES)
            ya = _attn(p3, ev_qn_gain[j].reshape(1, HD_A), ev_kn_gain[j].reshape(1, HD_A), _rel_row(ev_rel_bias[j]))
            bias = ev_if_bias[j].astype(F32)
            yb = _mlstm(p3, _gates_t(g3, 2 * H_B, MLSTM_L), bias.reshape(-1, 1, 1), ts)
            x2 = _outproj(ya.reshape(m, half), yb.reshape(m, half), ev_w_out[j], x2, tm, 1024)
        else:
            n_main = 6 * half
            h_d = half // HD_D
            w_t = _weight_t(od_w_in[j])
            p, g = _inproj(x2, od_norm[j].reshape(1, d), w_t, _pad_rows(w_t[n_main:]), n_main, tm, n_main // 3)
            p3, g3 = p.reshape(b, s, n_main), g.reshape(b, s, LANES)
            yc = _rglru(p3, od_conv_c_w[j], od_conv_c_b[j].reshape(1, -1), od_gate_w[j].astype(BF16),
                        od_gate_b[j].reshape(1, -1), od_lambda[j].reshape(1, -1))
            al, dt = od_a_log[j].astype(F32), od_dt_bias[j].astype(F32)
            yd = _gdn(p3, g3, _gates_t(g3, 2 * h_d, CHUNK), od_conv_d_w[j], _pad_lanes(al.reshape(1, -1)),
                      _pad_lanes(dt.reshape(1, -1)), al.reshape(-1, 1, 1), dt.reshape(-1, 1, 1),
                      od_onorm[j].reshape(1, HD_D), ts)
            x2 = _outproj(yc.reshape(m, half), yd.reshape(m, half), od_w_out[j], x2, tm, 1024)
    return x2.reshape(b, s, d)
```

```python
import functools

import numpy as np
import jax
import jax.numpy as jnp
from jax import lax
from jax.experimental import pallas as pl
from jax.experimental.pallas import tpu as pltpu

F32 = jnp.float32
BF16 = jnp.bfloat16

CHUNK = 64
EPS = 1e-6
NEG = -1e30
CONV_K = 4
LOG2E = 1.4426950408889634
LANES = 128
HD_A = 128
LOOKBACK = 8
BAND = (LOOKBACK + 1) * CHUNK
REL_MAX = 256
REL_W = 640
ATTN_GROUP = 16
ATTN_HEADS = 2
H_B = 4
MLSTM_L = 256
N_BLK_C = 8
C_RG = 8.0
RG_COLS = 1024
HD_D = 128
GDN_GROUP = 4

V7X_VMEM_BYTES = 64 * 1024 * 1024
VMEM_LIMIT = V7X_VMEM_BYTES - 8 * 1024 * 1024


def _params(n_axes):
    return pltpu.CompilerParams(dimension_semantics=("arbitrary",) * n_axes, vmem_limit_bytes=VMEM_LIMIT)


def _mm(a, b):
    return jnp.dot(a.astype(BF16), b.astype(BF16), preferred_element_type=F32)


def _mm_nt(a, b):
    return lax.dot_general(a.astype(BF16), b.astype(BF16), (((1,), (1,)), ((), ())), preferred_element_type=F32)


def _mm_tn(a, b):
    return lax.dot_general(a.astype(BF16), b.astype(BF16), (((0,), (0,)), ((), ())), preferred_element_type=F32)


def _mm_f32(a, b):
    return jnp.dot(a, b, preferred_element_type=F32, precision=lax.Precision.HIGHEST)


def _log1p(e):
    u = 1.0 + e
    return jnp.where(u == 1.0, e, jnp.log(u) * (e / (u - 1.0)))


def _softplus(x):
    return jnp.maximum(x, 0.0) + _log1p(jnp.exp(-jnp.abs(x)))


def _sigmoid(x):
    return 1.0 / (1.0 + jnp.exp2(x * -LOG2E))


def _silu(x):
    return x * _sigmoid(x)


def _iota2(shape, axis):
    return lax.broadcasted_iota(jnp.int32, shape, axis)


def _chunk_cumsum(x, length):
    pos = _iota2(x.shape, 0) & (length - 1)
    d = 1
    while d < length:
        x = x + jnp.where(pos >= d, pltpu.roll(x, d, 0), 0.0)
        d *= 2
    return x


def _triu(n):
    r, c = _iota2((n, n), 0), _iota2((n, n), 1)
    return jnp.where(r <= c, 1.0, 0.0).astype(F32)


def _inproj_body(x_ref, g_ref, w_ref, wg_ref, o_ref, og_ref, xn_ref):
    @pl.when(pl.program_id(1) == 0)
    def _():
        x = x_ref[...]
        ms = jnp.mean(x * x, axis=-1, keepdims=True)
        xn = ((x * lax.rsqrt(ms + EPS)) * g_ref[...]).astype(BF16)
        xn_ref[...] = xn
        og_ref[...] = _mm_nt(xn, wg_ref[...])

    o_ref[...] = _mm_nt(xn_ref[...], w_ref[...]).astype(o_ref.dtype)


def _inproj(x2, g, w_t, wg_t, n, tm, tn):
    m, d = x2.shape
    return pl.pallas_call(
        _inproj_body,
        grid=(m // tm, n // tn),
        in_specs=[
            pl.BlockSpec((tm, d), lambda i, j: (i, 0)),
            pl.BlockSpec((1, d), lambda i, j: (0, 0)),
            pl.BlockSpec((tn, d), lambda i, j: (j, 0)),
            pl.BlockSpec((LANES, d), lambda i, j: (0, 0)),
        ],
        out_specs=[
            pl.BlockSpec((tm, tn), lambda i, j: (i, j)),
            pl.BlockSpec((tm, LANES), lambda i, j: (i, 0)),
        ],
        out_shape=[jax.ShapeDtypeStruct((m, n), BF16), jax.ShapeDtypeStruct((m, LANES), F32)],
        scratch_shapes=[pltpu.VMEM((tm, d), BF16)],
        compiler_params=_params(2),
        name="inproj",
    )(x2, g, w_t, wg_t)


def _outproj_body(ya_ref, yb_ref, wa_ref, wb_ref, x_ref, o_ref, wa_s, wb_s):
    @pl.when(pl.program_id(1) == 0)
    def _():
        wa_s[...] = wa_ref[...].astype(BF16)
        wb_s[...] = wb_ref[...].astype(BF16)

    acc = jnp.dot(ya_ref[...], wa_s[...], preferred_element_type=F32)
    acc = acc + jnp.dot(yb_ref[...], wb_s[...], preferred_element_type=F32)
    o_ref[...] = x_ref[...] + acc


def _outproj(ya, yb, w, x2, tm, tn):
    m, d = x2.shape
    kh = ya.shape[1]
    return pl.pallas_call(
        _outproj_body,
        grid=(d // tn, m // tm),
        in_specs=[
            pl.BlockSpec((tm, kh), lambda j, i: (i, 0)),
            pl.BlockSpec((tm, kh), lambda j, i: (i, 0)),
            pl.BlockSpec((kh, tn), lambda j, i: (0, j)),
            pl.BlockSpec((kh, tn), lambda j, i: (1, j)),
            pl.BlockSpec((tm, tn), lambda j, i: (i, j)),
        ],
        out_specs=pl.BlockSpec((tm, tn), lambda j, i: (i, j)),
        out_shape=jax.ShapeDtypeStruct((m, d), F32),
        scratch_shapes=[pltpu.VMEM((kh, tn), BF16), pltpu.VMEM((kh, tn), BF16)],
        compiler_params=_params(2),
        name="outproj",
    )(ya, yb, w, w, x2)


def _attn_body(q_ref, k_ref, v_ref, z_ref, qg_ref, kg_ref, c_ref, o_ref, qs, ks, vs, bias_s):
    s = q_ref.shape[0]
    pad = LOOKBACK * CHUNK
    col = _iota2((CHUNK, BAND), 1)
    assert ATTN_GROUP >= LOOKBACK
    for hh in range(ATTN_HEADS):
        hc = slice(hh * HD_A, (hh + 1) * HD_A)
        q = q_ref[:, hc].astype(F32)
        q = q * lax.rsqrt(jnp.mean(q * q, axis=-1, keepdims=True) + EPS) * (qg_ref[...] * (HD_A ** -0.5 * LOG2E))
        qs[hh] = q.astype(BF16)
        k = k_ref[:, hc].astype(F32)
        k = k * lax.rsqrt(jnp.mean(k * k, axis=-1, keepdims=True) + EPS) * kg_ref[...]
        ks[hh, 0:pad, :] = jnp.zeros((pad, HD_A), BF16)
        ks[hh, pad:pad + s, :] = k.astype(BF16)
        vs[hh, 0:pad, :] = jnp.zeros((pad, HD_A), BF16)
        vs[hh, pad:pad + s, :] = v_ref[:, hc]
        cb = jnp.broadcast_to(c_ref[hh] * LOG2E, (CHUNK, REL_W))
        bias_s[hh] = pltpu.roll(cb, 0, 1, stride=1, stride_axis=0)[:, :BAND]

        def chunks(it, masked, hh=hh, hc=hc):
            gs = range(ATTN_GROUP)
            n = [it * ATTN_GROUP + g for g in gs]
            r0 = [pl.multiple_of(n[g] * CHUNK, CHUNK) for g in gs]
            sc = [_mm_nt(qs[hh, pl.ds(r0[g], CHUNK), :], ks[hh, pl.ds(r0[g], BAND), :]) + bias_s[hh] for g in gs]
            if masked:
                sc = [jnp.where(col >= (LOOKBACK - n[g]) * CHUNK, sc[g], NEG) for g in gs]
            p = [jnp.exp2(sc[g] - jnp.max(sc[g], axis=-1, keepdims=True)) for g in gs]
            pv = [_mm(p[g], vs[hh, pl.ds(r0[g], BAND), :]) for g in gs]
            for g in gs:
                o = pv[g] * (1.0 / jnp.sum(p[g], axis=-1, keepdims=True))
                z = z_ref[pl.ds(r0[g], CHUNK), hc].astype(F32)
                o_ref[pl.ds(r0[g], CHUNK), hc] = (o * _silu(z)).astype(o_ref.dtype)

        chunks(0, True)

        def rest(it, carry, chunks=chunks):
            chunks(it, False)
            return carry

        lax.fori_loop(1, s // (CHUNK * ATTN_GROUP), rest, 0)


def _attn(p3, qg, kg, crel):
    b, s, _ = p3.shape
    h_a = crel.shape[0]
    hp = h_a // ATTN_HEADS
    wh = ATTN_HEADS * HD_A

    def col_spec(sec):
        return pl.BlockSpec((None, s, wh), lambda bi, h, sec=sec: (bi, 0, sec * hp + h))

    return pl.pallas_call(
        _attn_body,
        grid=(b, hp),
        in_specs=[
            col_spec(0), col_spec(1), col_spec(2), col_spec(3),
            pl.BlockSpec((1, HD_A), lambda bi, h: (0, 0)),
            pl.BlockSpec((1, HD_A), lambda bi, h: (0, 0)),
            pl.BlockSpec((ATTN_HEADS, 1, REL_W), lambda bi, h: (h, 0, 0)),
        ],
        out_specs=pl.BlockSpec((None, s, wh), lambda bi, h: (bi, 0, h)),
        out_shape=jax.ShapeDtypeStruct((b, s, h_a * HD_A), BF16),
        scratch_shapes=[
            pltpu.VMEM((ATTN_HEADS, s, HD_A), BF16),
            pltpu.VMEM((ATTN_HEADS, s + LOOKBACK * CHUNK, HD_A), BF16),
            pltpu.VMEM((ATTN_HEADS, s + LOOKBACK * CHUNK, HD_A), BF16),
            pltpu.VMEM((ATTN_HEADS, CHUNK, BAND), F32),
        ],
        compiler_params=_params(2),
        name="chunk_attn",
    )(p3, p3, p3, p3, qg, kg, crel)


def _mlstm_body(q_ref, k_ref, v_ref, o_ref, z_ref, gt_ref, br_ref, y_ref, c_st, n_st, m_st, row_b, row_i):
    ts = q_ref.shape[0]
    ln = MLSTM_L
    nct = ts // ln
    hd = q_ref.shape[1] // H_B
    scale = hd ** -0.5
    j = pl.program_id(1)
    hs = range(H_B)
    hc = [slice(h * hd, (h + 1) * hd) for h in hs]

    @pl.when(j == 0)
    def _():
        c_st[...] = jnp.zeros_like(c_st)
        n_st[...] = jnp.zeros_like(n_st)
        m_st[...] = jnp.zeros_like(m_st)
        gt = gt_ref[...] + br_ref[...]
        for h in hs:
            row_b[h] = _mm_f32(-_softplus(-gt[H_B + h]), _triu(ln)) * LOG2E
            row_i[h] = gt[h] * LOG2E

    tril = _iota2((ln, ln), 1) <= _iota2((ln, ln), 0)
    log2_scale = float(np.log2(scale))
    m = [m_st[h][0:1, 0:1] for h in hs]
    cm = [c_st[h] for h in hs]
    nm = [n_st[h][0:1, :] for h in hs]
    chs = [(c, h) for c in range(nct) for h in hs]
    rows = [slice(c * ln, (c + 1) * ln) for c in range(nct)]
    li_r = {(c, h): row_i[h, pl.ds(j * nct + c, 1), :] for c, h in chs}
    bc_r = {(c, h): row_b[h, pl.ds(j * nct + c, 1), :] for c, h in chs}
    pad = jnp.zeros((LANES - 2 * H_B, ln), F32)
    cols = [jnp.concatenate([li_r[c, h] for h in hs] + [bc_r[c, h] for h in hs] + [pad], axis=0).T
            for c in range(nct)]
    lic = {(c, h): cols[c][:, h:h + 1] for c, h in chs}
    bc = {(c, h): cols[c][:, H_B + h:H_B + h + 1] for c, h in chs}
    rl = {ch: li_r[ch] - bc_r[ch] for ch in chs}
    b_last = {ch: bc[ch][ln - 1:ln, :] for ch in chs}
    m_in, m_new = {}, {}
    for c, h in chs:
        m_in[c, h] = m[h]
        m_new[c, h] = jnp.maximum(b_last[c, h] + m[h], jnp.max(b_last[c, h] + rl[c, h], axis=-1, keepdims=True))
        m[h] = m_new[c, h]
    qn = {(c, h): q_ref[rows[c], hc[h]] for c, h in chs}
    kn = {(c, h): k_ref[rows[c], hc[h]] for c, h in chs}
    vn = {(c, h): v_ref[rows[c], hc[h]] for c, h in chs}
    qk = {ch: _mm_nt(qn[ch], kn[ch]) for ch in chs}
    dmat = {ch: jnp.where(tril, bc[ch] + rl[ch], NEG) for ch in chs}
    mt = {ch: jnp.maximum(bc[ch] + m_in[ch], jnp.max(dmat[ch], axis=-1, keepdims=True)) for ch in chs}
    cmt = {ch: mt[ch] - log2_scale for ch in chs}
    w_inter = {ch: jnp.exp2(bc[ch] + m_in[ch] - cmt[ch]) for ch in chs}
    p = {ch: jnp.exp2(dmat[ch] - cmt[ch]) * qk[ch] for ch in chs}
    pv = {ch: _mm(p[ch], vn[ch]) for ch in chs}
    ws_c = {ch: jnp.exp2(b_last[ch] - bc[ch] + lic[ch] - m_new[ch]).astype(BF16) for ch in chs}
    ws_r = {ch: jnp.exp2(b_last[ch] + rl[ch] - m_new[ch]) for ch in chs}
    upd = {ch: _mm_tn(kn[ch] * ws_c[ch], vn[ch]) for ch in chs}
    n_add = {ch: _mm(jnp.broadcast_to(ws_r[ch], (8, ln)), kn[ch])[0:1, :] for ch in chs}
    psum = {ch: jnp.sum(p[ch], axis=-1, keepdims=True) for ch in chs}
    gate = {}
    for c, h in chs:
        og = o_ref[rows[c], hc[h]].astype(F32)
        zg = z_ref[rows[c], hc[h]].astype(F32)
        gate[c, h] = zg * (1.0 / ((1.0 + jnp.exp2(og * -LOG2E)) * (1.0 + jnp.exp2(zg * -LOG2E))))
    for c in range(nct):
        qc = {h: _mm(qn[c, h], cm[h]) for h in hs}
        qn_dot = {h: _mm_nt(qn[c, h], jnp.broadcast_to(nm[h], (8, hd)))[:, 0:1] for h in hs}
        for h in hs:
            num = w_inter[c, h] * qc[h] + pv[c, h]
            den = w_inter[c, h] * qn_dot[h] + psum[c, h]
            hout = num * (1.0 / jnp.maximum(jnp.abs(den), jnp.exp2(-mt[c, h])))
            y_ref[rows[c], hc[h]] = (hout * gate[c, h]).astype(y_ref.dtype)
            w_c = jnp.exp2(b_last[c, h] + m_in[c, h] - m_new[c, h])
            cm[h] = w_c * cm[h] + upd[c, h]
            nm[h] = w_c * nm[h] + n_add[c, h]
    for h in hs:
        c_st[h] = cm[h]
        n_st[h] = jnp.broadcast_to(nm[h], n_st.shape[1:])
        m_st[h] = jnp.broadcast_to(m[h], m_st.shape[1:])


def _mlstm(p3, gt4, bias_row, ts):
    b, s, _ = p3.shape
    w_b = 1024
    hd = w_b // H_B
    ncs = s // MLSTM_L

    def col_spec(sec):
        return pl.BlockSpec((None, ts, w_b), lambda bi, j, sec=sec: (bi, j, 4 + sec))

    return pl.pallas_call(
        _mlstm_body,
        grid=(b, s // ts),
        in_specs=[
            col_spec(0), col_spec(1), col_spec(2), col_spec(3), col_spec(4),
            pl.BlockSpec((None, 2 * H_B, ncs, MLSTM_L), lambda bi, j: (bi, 0, 0, 0)),
            pl.BlockSpec((2 * H_B, 1, 1), lambda bi, j: (0, 0, 0)),
        ],
        out_specs=pl.BlockSpec((None, ts, w_b), lambda bi, j: (bi, j, 0)),
        out_shape=jax.ShapeDtypeStruct((b, s, w_b), BF16),
        scratch_shapes=[
            pltpu.VMEM((H_B, hd, hd), F32),
            pltpu.VMEM((H_B, 8, hd), F32),
            pltpu.VMEM((H_B, 8, LANES), F32),
            pltpu.VMEM((H_B, ncs, MLSTM_L), F32),
            pltpu.VMEM((H_B, ncs, MLSTM_L), F32),
        ],
        compiler_params=_params(2),
        name="mlstm",
    )(p3, p3, p3, p3, p3, gt4, bias_row)


def _rglru_body(x_ref, z_ref, cw_ref, cb_ref, gw_ref, gbr_ref, gbi_ref, lam_ref, y_ref, a_s, b_s):
    s, wc = x_ref.shape
    blk = gw_ref.shape[1]
    row8 = _iota2((8, blk), 0)
    sub = _iota2((s // 8, 8, blk), 1)
    for n in range(wc // blk):
        cs = slice(n * blk, (n + 1) * blk)
        x = x_ref[:, cs].astype(F32)
        cw = cw_ref[:, cs]
        xc = cw[CONV_K - 1:CONV_K, :] * x + cb_ref[:, cs]
        for d in range(1, CONV_K):
            xs = pltpu.roll(x, d, 0)
            xs = jnp.concatenate([jnp.where(row8 >= d, xs[0:8], 0.0), xs[8:]], axis=0)
            xc = xc + cw[CONV_K - 1 - d:CONV_K - d, :] * xs
        gates = _mm(xc, gw_ref[n])
        r = _sigmoid(gates[:, :blk] + gbr_ref[:, cs])
        i = _sigmoid(gates[:, blk:] + gbi_ref[:, cs])
        nla = r * (C_RG * _softplus(-lam_ref[:, cs]))
        a = jnp.exp2(nla * -LOG2E)
        var = jnp.tanh(nla) * (a * a + 1.0)
        bb = jnp.where(var > 0.0, var * lax.rsqrt(var), 0.0) * (i * xc)
        a = a.reshape(s // 8, 8, blk)
        bb = bb.reshape(s // 8, 8, blk)
        for d in (1, 2, 4):
            keep = sub >= d
            a_sh = jnp.where(keep, pltpu.roll(a, d, 1), 1.0)
            b_sh = jnp.where(keep, pltpu.roll(bb, d, 1), 0.0)
            bb = a * b_sh + bb
            a = a * a_sh
        a_s[:, cs] = a.reshape(s, blk)
        b_s[:, cs] = bb.reshape(s, blk)

    def group(gi, carry):
        rows = pl.ds(pl.multiple_of(gi * 8, 8), 8)
        h = a_s[rows, :] * carry + b_s[rows, :]
        b_s[rows, :] = h
        return jnp.broadcast_to(h[7:8, :], h.shape)

    lax.fori_loop(0, s // 8, group, jnp.zeros((8, wc), F32), unroll=8)
    z = z_ref[...].astype(F32)
    y_ref[...] = (b_s[...] * _silu(z)).astype(y_ref.dtype)


def _rglru(p3, cw, cb, gw, gb, lam):
    b, s, _ = p3.shape
    w_c = cw.shape[1]
    blk = w_c // N_BLK_C
    nsp = w_c // RG_COLS
    return pl.pallas_call(
        _rglru_body,
        grid=(b, nsp),
        in_specs=[
            pl.BlockSpec((None, s, RG_COLS), lambda bi, n: (bi, 0, n)),
            pl.BlockSpec((None, s, RG_COLS), lambda bi, n: (bi, 0, nsp + n)),
            pl.BlockSpec((CONV_K, RG_COLS), lambda bi, n: (0, n)),
            pl.BlockSpec((1, RG_COLS), lambda bi, n: (0, n)),
            pl.BlockSpec((RG_COLS // blk, blk, 2 * blk), lambda bi, n: (n, 0, 0)),
            pl.BlockSpec((1, RG_COLS), lambda bi, n: (0, n)),
            pl.BlockSpec((1, RG_COLS), lambda bi, n: (0, nsp + n)),
            pl.BlockSpec((1, RG_COLS), lambda bi, n: (0, n)),
        ],
        out_specs=pl.BlockSpec((None, s, RG_COLS), lambda bi, n: (bi, 0, n)),
        out_shape=jax.ShapeDtypeStruct((b, s, w_c), BF16),
        scratch_shapes=[pltpu.VMEM((s, RG_COLS), F32), pltpu.VMEM((s, RG_COLS), F32)],
        compiler_params=_params(2),
        name="rglru",
    )(p3, p3, cw, cb, gw, gb, gb, lam)


def _unit_lower_inverse(a_list):
    r, c = _iota2((CHUNK, CHUNK), 0), _iota2((CHUNK, CHUNK), 1)
    eye = jnp.where(r == c, 1.0, 0.0).astype(F32)

    def pair_mask(sh):
        rb, cb_ = r >> sh, c >> sh
        return ((rb & 1) == 1) & (cb_ == rb - 1)

    ts = [eye - jnp.where(pair_mask(0), a, 0.0) for a in a_list]
    for sh in range(1, 6):
        mask = pair_mask(sh)
        x1 = [_mm(t, jnp.where(mask, a, 0.0)) for t, a in zip(ts, a_list)]
        x2 = [_mm(x, t) for x, t in zip(x1, ts)]
        ts = [t - x for t, x in zip(ts, x2)]
    return ts


def _gdn_body(q_ref, k_ref, v_ref, z_ref, g_ref, gt_ref, cw_ref, al_c_ref, dt_c_ref, al_r_ref, dt_r_ref, on_ref,
              y_ref, s_st, tail, qkv, col_d, col_b, row_d, lhs_s, o0_s, n_s, *, nt):
    ts = q_ref.shape[0]
    nct = ts // CHUNK
    w_d = q_ref.shape[1]
    h_d = w_d // HD_D
    scale = HD_D ** -0.5
    t = pl.program_id(0)
    srcs = (q_ref, k_ref, v_ref)

    def conv_block(slot, r0, first, cbk):
        ref = srcs[cbk // h_d]
        ci = slice((cbk % h_d) * HD_D, (cbk % h_d + 1) * HD_D)
        cs = slice(cbk * HD_D, (cbk + 1) * HD_D)
        cur = ref[pl.ds(r0, CHUNK), ci].astype(F32)
        prev = ref[pl.ds(pl.multiple_of(jnp.maximum(r0 - 16, 0), 16), 16), ci].astype(F32)
        xw = jnp.concatenate([jnp.where(first, tail[:, cs], prev), cur], axis=0)
        acc = cw_ref[CONV_K - 1:CONV_K, cs] * cur
        for d in range(1, CONV_K):
            acc = acc + cw_ref[CONV_K - 1 - d:CONV_K - d, cs] * xw[16 - d:16 - d + CHUNK]
        acc = _silu(acc)
        if cbk < 2 * h_d:
            acc = acc * lax.rsqrt(jnp.sum(acc * acc, axis=-1, keepdims=True) + EPS)
        qkv[slot, pl.ds(r0, CHUNK), cs] = acc

    def conv_rows(slot, rb):
        r0 = pl.multiple_of(rb * CHUNK, CHUNK)
        for cbk in range(3 * h_d):
            conv_block(slot, r0, rb == 0, cbk)

    def save_tail():
        keep = lax.rem(t + 1, nt) != 0
        for w, ref in enumerate(srcs):
            tail[:, w * w_d:(w + 1) * w_d] = jnp.where(keep, ref[ts - 16:ts, :].astype(F32), 0.0)

    @pl.when(t == 0)
    def _():
        tail[...] = jnp.zeros_like(tail)

        def rows(rb, carry):
            conv_rows(0, rb)
            return carry

        lax.fori_loop(0, nct, rows, 0)
        save_tail()

    @pl.when(t > 0)
    def _():
        cur_slot = lax.rem(t, 2)
        prv_slot = 1 - cur_slot

        @pl.when(lax.rem(t - 1, nt) == 0)
        def _():
            s_st[...] = jnp.zeros_like(s_st)

        g = g_ref[...]
        col_b[...] = _sigmoid(g)
        col_d[...] = _chunk_cumsum(-jnp.exp(al_c_ref[...]) * _softplus(g + dt_c_ref[...]), CHUNK) * LOG2E
        gt = gt_ref[...]
        g_r = -jnp.exp(al_r_ref[...]) * _softplus(gt[0:h_d] + dt_r_ref[...])
        row_d[...] = _mm_f32(g_r.reshape(h_d * nct, CHUNK), _triu(CHUNK)).reshape(h_d, nct, CHUNK) * LOG2E

        r_i, c_i = _iota2((CHUNK, CHUNK), 0), _iota2((CHUNK, CHUNK), 1)
        tril = c_i <= r_i
        strict = c_i < r_i
        hs = range(h_d)

        def group(gi, carry):
            probs = [(cc, h) for cc in range(GDN_GROUP) for h in hs]
            cidx = [gi * GDN_GROUP + cc for cc in range(GDN_GROUP)]
            rows = [pl.ds(pl.multiple_of(ci * CHUNK, CHUNK), CHUNK) for ci in cidx]
            dc_all = [col_d[r, :] for r in rows]
            beta_all = [col_b[r, :] for r in rows]
            dc = [dc_all[cc][:, h:h + 1] for cc, h in probs]
            beta = [beta_all[cc][:, h_d + h:h_d + h + 1] for cc, h in probs]
            q = [qkv[prv_slot, rows[cc], h * HD_D:(h + 1) * HD_D] for cc, h in probs]
            k = [qkv[prv_slot, rows[cc], w_d + h * HD_D:w_d + (h + 1) * HD_D] for cc, h in probs]
            v = [qkv[prv_slot, rows[cc], 2 * w_d + h * HD_D:2 * w_d + (h + 1) * HD_D] for cc, h in probs]
            ps = range(len(probs))
            gam = [jnp.exp2(jnp.where(tril, dc[i] - row_d[probs[i][1], pl.ds(cidx[probs[i][0]], 1), :], NEG)) for i in ps]
            kb = [k[i] * beta[i] for i in ps]
            kk = [_mm_nt(kb[i], k[i]) for i in ps]
            t_inv = _unit_lower_inverse([jnp.where(strict, kk[i] * gam[i], 0.0) for i in ps])
            ed = [jnp.exp2(dc[i]) for i in ps]
            uw = [_mm(t_inv[i], jnp.concatenate([v[i] * beta[i], kb[i] * ed[i]], axis=1)) for i in ps]
            qs = [q[i] * scale for i in ps]
            qk = [_mm_nt(qs[i], k[i]) * gam[i] for i in ps]
            kd = [k[i] * jnp.exp2(dc[i][CHUNK - 1:CHUNK, :] - dc[i]) for i in ps]
            x1 = [_mm(qk[i], uw[i]) for i in ps]
            x2 = [_mm_tn(kd[i], uw[i]) for i in ps]
            for i, (cc, h) in enumerate(probs):
                lhs_s[cc, h] = jnp.concatenate([qs[i] * ed[i] - x1[i][:, HD_D:], x2[i][:, HD_D:]], axis=0).astype(BF16)
                o0_s[cc, h] = x1[i][:, :HD_D]
                n_s[cc, h] = x2[i][:, :HD_D]
            for rr in range(GDN_GROUP):
                conv_rows(cur_slot, gi * GDN_GROUP + rr)

            def step(cc, carry2):
                c = gi * GDN_GROUP + cc
                rws = pl.ds(pl.multiple_of(c * CHUNK, CHUNK), CHUNK)
                d_last = jnp.exp2(col_d[pl.ds(c * CHUNK + CHUNK - 1, 1), :])
                st = [s_st[h] for h in hs]
                r = [_mm(lhs_s[cc, h], st[h]) for h in hs]
                for h in hs:
                    o = o0_s[cc, h] + r[h][:CHUNK]
                    s_st[h] = d_last[:, h:h + 1] * st[h] + n_s[cc, h] - r[h][CHUNK:]
                    on = o * lax.rsqrt(jnp.mean(o * o, axis=-1, keepdims=True) + EPS) * on_ref[...]
                    z = z_ref[rws, h * HD_D:(h + 1) * HD_D].astype(F32)
                    y_ref[rws, h * HD_D:(h + 1) * HD_D] = (on * _silu(z)).astype(y_ref.dtype)
                return carry2

            lax.fori_loop(0, GDN_GROUP, step, 0)
            return carry

        lax.fori_loop(0, nct // GDN_GROUP, group, 0)
        save_tail()


def _gdn(p3, g3, gt4, cw, al_c, dt_c, al_r, dt_r, onorm, ts):
    b, s, _ = p3.shape
    w_d = 1024
    h_d = w_d // HD_D
    nct = ts // CHUNK
    nt = s // ts
    n_tiles = b * nt

    def conv_spec(sec):
        def imap(t, sec=sec):
            tc = jnp.minimum(t, n_tiles - 1)
            return (tc // nt, tc % nt, 2 + sec)
        return pl.BlockSpec((None, ts, w_d), imap)

    def prev_map(last):
        def imap(t):
            tp = jnp.maximum(t - 1, 0)
            return (tp // nt, tp % nt, last)
        return imap

    def gt_map(t):
        tp = jnp.maximum(t - 1, 0)
        return (tp // nt, 0, tp % nt, 0)

    return pl.pallas_call(
        functools.partial(_gdn_body, nt=nt),
        grid=(n_tiles + 1,),
        in_specs=[
            conv_spec(0), conv_spec(1), conv_spec(2),
            pl.BlockSpec((None, ts, w_d), prev_map(5)),
            pl.BlockSpec((None, ts, LANES), prev_map(0)),
            pl.BlockSpec((None, 2 * h_d, nct, CHUNK), gt_map),
            pl.BlockSpec((CONV_K, 3 * w_d), lambda t: (0, 0)),
            pl.BlockSpec((1, LANES), lambda t: (0, 0)),
            pl.BlockSpec((1, LANES), lambda t: (0, 0)),
            pl.BlockSpec((h_d, 1, 1), lambda t: (0, 0, 0)),
            pl.BlockSpec((h_d, 1, 1), lambda t: (0, 0, 0)),
            pl.BlockSpec((1, HD_D), lambda t: (0, 0)),
        ],
        out_specs=pl.BlockSpec((None, ts, w_d), prev_map(0)),
        out_shape=jax.ShapeDtypeStruct((b, s, w_d), BF16),
        scratch_shapes=[
            pltpu.VMEM((h_d, HD_D, HD_D), F32),
            pltpu.VMEM((16, 3 * w_d), F32),
            pltpu.VMEM((2, ts, 3 * w_d), F32),
            pltpu.VMEM((ts, LANES), F32),
            pltpu.VMEM((ts, LANES), F32),
            pltpu.VMEM((h_d, nct, CHUNK), F32),
            pltpu.VMEM((GDN_GROUP, h_d, CHUNK + HD_D, HD_D), BF16),
            pltpu.VMEM((GDN_GROUP, h_d, CHUNK, HD_D), F32),
            pltpu.VMEM((GDN_GROUP, h_d, HD_D, HD_D), F32),
        ],
        compiler_params=_params(1),
        name="gated_delta",
    )(p3, p3, p3, p3, g3, gt4, cw, al_c, dt_c, al_r, dt_r, onorm)


def _rel_row(rel_bias):
    t = (np.arange(REL_W) + CHUNK - 1) % REL_W
    idx = np.clip(LOOKBACK * CHUNK + CHUNK - 1 - t, -REL_MAX, REL_MAX) + REL_MAX
    return rel_bias[:, idx].astype(F32)[:, None, :]


def _pad_lanes(v, n=LANES):
    return jnp.pad(v, ((0, 0), (0, n - v.shape[1])))


def _pad_rows(v, n=LANES):
    return jnp.pad(v, ((0, n - v.shape[0]), (0, 0)))


def _weight_t(w):
    return jnp.swapaxes(w, 0, 1).astype(BF16)


def _gates_t(g3, n, length):
    b, s, _ = g3.shape
    return jnp.transpose(g3[:, :, :n], (0, 2, 1)).reshape(b, n, s // length, length)


def kernel(x, ev_norm, ev_w_in, ev_if_bias, ev_qn_gain, ev_kn_gain, ev_rel_bias, ev_w_out, od_norm, od_w_in, od_conv_c_w, od_conv_c_b, od_gate_w, od_gate_b, od_lambda, od_conv_d_w, od_a_log, od_dt_bias, od_onorm, od_w_out):
    b, s, d = x.shape
    m = b * s
    half = d // 2
    tm = min(1024, m)
    ts = min(512, s)
    assert half == 1024 and m % tm == 0 and s % ts == 0 and s % (CHUNK * ATTN_GROUP) == 0 and ts % MLSTM_L == 0
    x2 = x.reshape(m, d)
    depth = ev_norm.shape[0] + od_norm.shape[0]
    for layer in range(depth):
        j = layer // 2
        if layer % 2 == 0:
            n_main = 9 * half
            w_t = _weight_t(ev_w_in[j])
            p, g = _inproj(x2, ev_norm[j].reshape(1, d), w_t, _pad_rows(w_t[n_main:]), n_main, tm, n_main // 4)
            p3, g3 = p.reshape(b, s, n_main), g.reshape(b, s, LANES)
            ya = _attn(p3, ev_qn_gain[j].reshape(1, HD_A), ev_kn_gain[j].reshape(1, HD_A), _rel_row(ev_rel_bias[j]))
            bias = ev_if_bias[j].astype(F32)
            yb = _mlstm(p3, _gates_t(g3, 2 * H_B, MLSTM_L), bias.reshape(-1, 1, 1), ts)
            x2 = _outproj(ya.reshape(m, half), yb.reshape(m, half), ev_w_out[j], x2, tm, 1024)
        else:
            n_main = 6 * half
            h_d = half // HD_D
            w_t = _weight_t(od_w_in[j])
            p, g = _inproj(x2, od_norm[j].reshape(1, d), w_t, _pad_rows(w_t[n_main:]), n_main, tm, n_main // 3)
            p3, g3 = p.reshape(b, s, n_main), g.reshape(b, s, LANES)
            yc = _rglru(p3, od_conv_c_w[j], od_conv_c_b[j].reshape(1, -1), od_gate_w[j].astype(BF16),
                        od_gate_b[j].reshape(1, -1), od_lambda[j].reshape(1, -1))
            al, dt = od_a_log[j].astype(F32), od_dt_bias[j].astype(F32)
            yd = _gdn(p3, g3, _gates_t(g3, 2 * h_d, CHUNK), od_conv_d_w[j], _pad_lanes(al.reshape(1, -1)),
                      _pad_lanes(dt.reshape(1, -1)), al.reshape(-1, 1, 1), dt.reshape(-1, 1, 1),
                      od_onorm[j].reshape(1, HD_D), ts)
            x2 = _outproj(yc.reshape(m, half), yd.reshape(m, half), od_w_out[j], x2, tm, 1024)
    return x2.reshape(b, s, d)
```

```python
import functools

import numpy as np
import jax
import jax.numpy as jnp
from jax import lax
from jax.experimental import pallas as pl
from jax.experimental.pallas import tpu as pltpu

F32 = jnp.float32
BF16 = jnp.bfloat16

CHUNK = 64
EPS = 1e-6
NEG = -1e30
CONV_K = 4
LOG2E = 1.4426950408889634
LANES = 128
HD_A = 128
LOOKBACK = 8
BAND = (LOOKBACK + 1) * CHUNK
REL_MAX = 256
REL_W = 640
ATTN_GROUP = 16
ATTN_HEADS = 2
H_B = 4
MLSTM_L = 256
N_BLK_C = 8
C_RG = 8.0
RG_COLS = 1024
HD_D = 128
GDN_GROUP = 4

V7X_VMEM_BYTES = 64 * 1024 * 1024
VMEM_LIMIT = V7X_VMEM_BYTES - 8 * 1024 * 1024


def _params(n_axes):
    return pltpu.CompilerParams(dimension_semantics=("arbitrary",) * n_axes, vmem_limit_bytes=VMEM_LIMIT)


def _mm(a, b):
    return jnp.dot(a.astype(BF16), b.astype(BF16), preferred_element_type=F32)


def _mm_nt(a, b):
    return lax.dot_general(a.astype(BF16), b.astype(BF16), (((1,), (1,)), ((), ())), preferred_element_type=F32)


def _mm_tn(a, b):
    return lax.dot_general(a.astype(BF16), b.astype(BF16), (((0,), (0,)), ((), ())), preferred_element_type=F32)


def _mm_f32(a, b):
    return jnp.dot(a, b, preferred_element_type=F32, precision=lax.Precision.HIGHEST)


def _log1p(e):
    u = 1.0 + e
    return jnp.where(u == 1.0, e, jnp.log(u) * (e / (u - 1.0)))


def _softplus(x):
    return jnp.maximum(x, 0.0) + _log1p(jnp.exp(-jnp.abs(x)))


def _sigmoid(x):
    return 1.0 / (1.0 + jnp.exp2(x * -LOG2E))


def _silu(x):
    return x * _sigmoid(x)


def _iota2(shape, axis):
    return lax.broadcasted_iota(jnp.int32, shape, axis)


def _chunk_cumsum(x, length):
    pos = _iota2(x.shape, 0) & (length - 1)
    d = 1
    while d < length:
        x = x + jnp.where(pos >= d, pltpu.roll(x, d, 0), 0.0)
        d *= 2
    return x


def _triu(n):
    r, c = _iota2((n, n), 0), _iota2((n, n), 1)
    return jnp.where(r <= c, 1.0, 0.0).astype(F32)


def _inproj_body(x_ref, g_ref, w_ref, wg_ref, o_ref, og_ref, xn_ref):
    @pl.when(pl.program_id(1) == 0)
    def _():
        x = x_ref[...]
        ms = jnp.mean(x * x, axis=-1, keepdims=True)
        xn = ((x * lax.rsqrt(ms + EPS)) * g_ref[...]).astype(BF16)
        xn_ref[...] = xn
        og_ref[...] = _mm_nt(xn, wg_ref[...])

    o_ref[...] = _mm_nt(xn_ref[...], w_ref[...]).astype(o_ref.dtype)


def _inproj(x2, g, w_t, wg_t, n, tm, tn):
    m, d = x2.shape
    return pl.pallas_call(
        _inproj_body,
        grid=(m // tm, n // tn),
        in_specs=[
            pl.BlockSpec((tm, d), lambda i, j: (i, 0)),
            pl.BlockSpec((1, d), lambda i, j: (0, 0)),
            pl.BlockSpec((tn, d), lambda i, j: (j, 0)),
            pl.BlockSpec((LANES, d), lambda i, j: (0, 0)),
        ],
        out_specs=[
            pl.BlockSpec((tm, tn), lambda i, j: (i, j)),
            pl.BlockSpec((tm, LANES), lambda i, j: (i, 0)),
        ],
        out_shape=[jax.ShapeDtypeStruct((m, n), BF16), jax.ShapeDtypeStruct((m, LANES), F32)],
        scratch_shapes=[pltpu.VMEM((tm, d), BF16)],
        compiler_params=_params(2),
        name="inproj",
    )(x2, g, w_t, wg_t)


def _outproj_body(ya_ref, yb_ref, wa_ref, wb_ref, x_ref, o_ref, wa_s, wb_s):
    @pl.when(pl.program_id(1) == 0)
    def _():
        wa_s[...] = wa_ref[...].astype(BF16)
        wb_s[...] = wb_ref[...].astype(BF16)

    acc = jnp.dot(ya_ref[...], wa_s[...], preferred_element_type=F32)
    acc = acc + jnp.dot(yb_ref[...], wb_s[...], preferred_element_type=F32)
    o_ref[...] = x_ref[...] + acc


def _outproj(ya, yb, w, x2, tm, tn):
    m, d = x2.shape
    kh = ya.shape[1]
    return pl.pallas_call(
        _outproj_body,
        grid=(d // tn, m // tm),
        in_specs=[
            pl.BlockSpec((tm, kh), lambda j, i: (i, 0)),
            pl.BlockSpec((tm, kh), lambda j, i: (i, 0)),
            pl.BlockSpec((kh, tn), lambda j, i: (0, j)),
            pl.BlockSpec((kh, tn), lambda j, i: (1, j)),
            pl.BlockSpec((tm, tn), lambda j, i: (i, j)),
        ],
        out_specs=pl.BlockSpec((tm, tn), lambda j, i: (i, j)),
        out_shape=jax.ShapeDtypeStruct((m, d), F32),
        scratch_shapes=[pltpu.VMEM((kh, tn), BF16), pltpu.VMEM((kh, tn), BF16)],
        compiler_params=_params(2),
        name="outproj",
    )(ya, yb, w, w, x2)


def _attn_body(q_ref, k_ref, v_ref, z_ref, qg_ref, kg_ref, c_ref, o_ref, qs, ks, vs, bias_s):
    s = q_ref.shape[0]
    pad = LOOKBACK * CHUNK
    col = _iota2((CHUNK, BAND), 1)
    assert ATTN_GROUP >= LOOKBACK
    for hh in range(ATTN_HEADS):
        hc = slice(hh * HD_A, (hh + 1) * HD_A)
        q = q_ref[:, hc].astype(F32)
        q = q * lax.rsqrt(jnp.mean(q * q, axis=-1, keepdims=True) + EPS) * (qg_ref[...] * (HD_A ** -0.5 * LOG2E))
        qs[hh] = q.astype(BF16)
        k = k_ref[:, hc].astype(F32)
        k = k * lax.rsqrt(jnp.mean(k * k, axis=-1, keepdims=True) + EPS) * kg_ref[...]
        ks[hh, 0:pad, :] = jnp.zeros((pad, HD_A), BF16)
        ks[hh, pad:pad + s, :] = k.astype(BF16)
        vs[hh, 0:pad, :] = jnp.zeros((pad, HD_A), BF16)
        vs[hh, pad:pad + s, :] = v_ref[:, hc]
        cb = jnp.broadcast_to(c_ref[hh] * LOG2E, (CHUNK, REL_W))
        bias_s[hh] = pltpu.roll(cb, 0, 1, stride=1, stride_axis=0)[:, :BAND]

        def chunks(it, masked, hh=hh, hc=hc):
            gs = range(ATTN_GROUP)
            n = [it * ATTN_GROUP + g for g in gs]
            r0 = [pl.multiple_of(n[g] * CHUNK, CHUNK) for g in gs]
            sc = [_mm_nt(qs[hh, pl.ds(r0[g], CHUNK), :], ks[hh, pl.ds(r0[g], BAND), :]) + bias_s[hh] for g in gs]
            if masked:
                sc = [jnp.where(col >= (LOOKBACK - n[g]) * CHUNK, sc[g], NEG) for g in gs]
            p = [jnp.exp2(sc[g] - jnp.max(sc[g], axis=-1, keepdims=True)) for g in gs]
            pv = [_mm(p[g], vs[hh, pl.ds(r0[g], BAND), :]) for g in gs]
            for g in gs:
                o = pv[g] * (1.0 / jnp.sum(p[g], axis=-1, keepdims=True))
                z = z_ref[pl.ds(r0[g], CHUNK), hc].astype(F32)
                o_ref[pl.ds(r0[g], CHUNK), hc] = (o * _silu(z)).astype(o_ref.dtype)

        chunks(0, True)

        def rest(it, carry, chunks=chunks):
            chunks(it, False)
            return carry

        lax.fori_loop(1, s // (CHUNK * ATTN_GROUP), rest, 0)


def _attn(p3, qg, kg, crel):
    b, s, _ = p3.shape
    h_a = crel.shape[0]
    hp = h_a // ATTN_HEADS
    wh = ATTN_HEADS * HD_A

    def col_spec(sec):
        return pl.BlockSpec((None, s, wh), lambda bi, h, sec=sec: (bi, 0, sec * hp + h))

    return pl.pallas_call(
        _attn_body,
        grid=(b, hp),
        in_specs=[
            col_spec(0), col_spec(1), col_spec(2), col_spec(3),
            pl.BlockSpec((1, HD_A), lambda bi, h: (0, 0)),
            pl.BlockSpec((1, HD_A), lambda bi, h: (0, 0)),
            pl.BlockSpec((ATTN_HEADS, 1, REL_W), lambda bi, h: (h, 0, 0)),
        ],
        out_specs=pl.BlockSpec((None, s, wh), lambda bi, h: (bi, 0, h)),
        out_shape=jax.ShapeDtypeStruct((b, s, h_a * HD_A), BF16),
        scratch_shapes=[
            pltpu.VMEM((ATTN_HEADS, s, HD_A), BF16),
            pltpu.VMEM((ATTN_HEADS, s + LOOKBACK * CHUNK, HD_A), BF16),
            pltpu.VMEM((ATTN_HEADS, s + LOOKBACK * CHUNK, HD_A), BF16),
            pltpu.VMEM((ATTN_HEADS, CHUNK, BAND), F32),
        ],
        compiler_params=_params(2),
        name="chunk_attn",
    )(p3, p3, p3, p3, qg, kg, crel)


def _mlstm_body(q_ref, k_ref, v_ref, o_ref, z_ref, gt_ref, br_ref, y_ref, c_st, n_st, m_st, row_b, row_i):
    ts = q_ref.shape[0]
    ln = MLSTM_L
    nct = ts // ln
    hd = q_ref.shape[1] // H_B
    scale = hd ** -0.5
    j = pl.program_id(1)
    hs = range(H_B)
    hc = [slice(h * hd, (h + 1) * hd) for h in hs]

    @pl.when(j == 0)
    def _():
        c_st[...] = jnp.zeros_like(c_st)
        n_st[...] = jnp.zeros_like(n_st)
        m_st[...] = jnp.zeros_like(m_st)
        gt = gt_ref[...] + br_ref[...]
        for h in hs:
            row_b[h] = _mm_f32(-_softplus(-gt[H_B + h]), _triu(ln)) * LOG2E
            row_i[h] = gt[h] * LOG2E

    tril = _iota2((ln, ln), 1) <= _iota2((ln, ln), 0)
    log2_scale = float(np.log2(scale))
    m = [m_st[h][0:1, 0:1] for h in hs]
    cm = [c_st[h] for h in hs]
    nm = [n_st[h][0:1, :] for h in hs]
    chs = [(c, h) for c in range(nct) for h in hs]
    rows = [slice(c * ln, (c + 1) * ln) for c in range(nct)]
    li_r = {(c, h): row_i[h, pl.ds(j * nct + c, 1), :] for c, h in chs}
    bc_r = {(c, h): row_b[h, pl.ds(j * nct + c, 1), :] for c, h in chs}
    pad = jnp.zeros((LANES - 2 * H_B, ln), F32)
    cols = [jnp.concatenate([li_r[c, h] for h in hs] + [bc_r[c, h] for h in hs] + [pad], axis=0).T
            for c in range(nct)]
    lic = {(c, h): cols[c][:, h:h + 1] for c, h in chs}
    bc = {(c, h): cols[c][:, H_B + h:H_B + h + 1] for c, h in chs}
    rl = {ch: li_r[ch] - bc_r[ch] for ch in chs}
    b_last = {ch: bc[ch][ln - 1:ln, :] for ch in chs}
    m_in, m_new = {}, {}
    for c, h in chs:
        m_in[c, h] = m[h]
        m_new[c, h] = jnp.maximum(b_last[c, h] + m[h], jnp.max(b_last[c, h] + rl[c, h], axis=-1, keepdims=True))
        m[h] = m_new[c, h]
    qn = {(c, h): q_ref[rows[c], hc[h]] for c, h in chs}
    kn = {(c, h): k_ref[rows[c], hc[h]] for c, h in chs}
    vn = {(c, h): v_ref[rows[c], hc[h]] for c, h in chs}
    qk = {ch: _mm_nt(qn[ch], kn[ch]) for ch in chs}
    dmat = {ch: jnp.where(tril, bc[ch] + rl[ch], NEG) for ch in chs}
    mt = {ch: jnp.maximum(bc[ch] + m_in[ch], jnp.max(dmat[ch], axis=-1, keepdims=True)) for ch in chs}
    cmt = {ch: mt[ch] - log2_scale for ch in chs}
    w_inter = {ch: jnp.exp2(bc[ch] + m_in[ch] - cmt[ch]) for ch in chs}
    p = {ch: jnp.exp2(dmat[ch] - cmt[ch]) * qk[ch] for ch in chs}
    pv = {ch: _mm(p[ch], vn[ch]) for ch in chs}
    ws_c = {ch: jnp.exp2(b_last[ch] - bc[ch] + lic[ch] - m_new[ch]).astype(BF16) for ch in chs}
    ws_r = {ch: jnp.exp2(b_last[ch] + rl[ch] - m_new[ch]) for ch in chs}
    upd = {ch: _mm_tn(kn[ch] * ws_c[ch], vn[ch]) for ch in chs}
    n_add = {ch: _mm(jnp.broadcast_to(ws_r[ch], (8, ln)), kn[ch])[0:1, :] for ch in chs}
    psum = {ch: jnp.sum(p[ch], axis=-1, keepdims=True) for ch in chs}
    gate = {}
    for c, h in chs:
        og = o_ref[rows[c], hc[h]].astype(F32)
        zg = z_ref[rows[c], hc[h]].astype(F32)
        gate[c, h] = zg * (1.0 / ((1.0 + jnp.exp2(og * -LOG2E)) * (1.0 + jnp.exp2(zg * -LOG2E))))
    for c in range(nct):
        qc = {h: _mm(qn[c, h], cm[h]) for h in hs}
        qn_dot = {h: _mm_nt(qn[c, h], jnp.broadcast_to(nm[h], (8, hd)))[:, 0:1] for h in hs}
        for h in hs:
            num = w_inter[c, h] * qc[h] + pv[c, h]
            den = w_inter[c, h] * qn_dot[h] + psum[c, h]
            hout = num * (1.0 / jnp.maximum(jnp.abs(den), jnp.exp2(-mt[c, h])))
            y_ref[rows[c], hc[h]] = (hout * gate[c, h]).astype(y_ref.dtype)
            w_c = jnp.exp2(b_last[c, h] + m_in[c, h] - m_new[c, h])
            cm[h] = w_c * cm[h] + upd[c, h]
            nm[h] = w_c * nm[h] + n_add[c, h]
    for h in hs:
        c_st[h] = cm[h]
        n_st[h] = jnp.broadcast_to(nm[h], n_st.shape[1:])
        m_st[h] = jnp.broadcast_to(m[h], m_st.shape[1:])


def _mlstm(p3, gt4, bias_row, ts):
    b, s, _ = p3.shape
    w_b = 1024
    hd = w_b // H_B
    ncs = s // MLSTM_L

    def col_spec(sec):
        return pl.BlockSpec((None, ts, w_b), lambda bi, j, sec=sec: (bi, j, 4 + sec))

    return pl.pallas_call(
        _mlstm_body,
        grid=(b, s // ts),
        in_specs=[
            col_spec(0), col_spec(1), col_spec(2), col_spec(3), col_spec(4),
            pl.BlockSpec((None, 2 * H_B, ncs, MLSTM_L), lambda bi, j: (bi, 0, 0, 0)),
            pl.BlockSpec((2 * H_B, 1, 1), lambda bi, j: (0, 0, 0)),
        ],
        out_specs=pl.BlockSpec((None, ts, w_b), lambda bi, j: (bi, j, 0)),
        out_shape=jax.ShapeDtypeStruct((b, s, w_b), BF16),
        scratch_shapes=[
            pltpu.VMEM((H_B, hd, hd), F32),
            pltpu.VMEM((H_B, 8, hd), F32),
            pltpu.VMEM((H_B, 8, LANES), F32),
            pltpu.VMEM((H_B, ncs, MLSTM_L), F32),
            pltpu.VMEM((H_B, ncs, MLSTM_L), F32),
        ],
        compiler_params=_params(2),
        name="mlstm",
    )(p3, p3, p3, p3, p3, gt4, bias_row)


def _rglru_body(x_ref, z_ref, cw_ref, cb_ref, gw_ref, gbr_ref, gbi_ref, lam_ref, y_ref, a_s, b_s):
    s, wc = x_ref.shape
    blk = gw_ref.shape[1]
    row8 = _iota2((8, blk), 0)
    sub = _iota2((s // 8, 8, blk), 1)
    for n in range(wc // blk):
        cs = slice(n * blk, (n + 1) * blk)
        x = x_ref[:, cs].astype(F32)
        cw = cw_ref[:, cs]
        xc = cw[CONV_K - 1:CONV_K, :] * x + cb_ref[:, cs]
        for d in range(1, CONV_K):
            xs = pltpu.roll(x, d, 0)
            xs = jnp.concatenate([jnp.where(row8 >= d, xs[0:8], 0.0), xs[8:]], axis=0)
            xc = xc + cw[CONV_K - 1 - d:CONV_K - d, :] * xs
        gates = _mm(xc, gw_ref[n])
        r = _sigmoid(gates[:, :blk] + gbr_ref[:, cs])
        i = _sigmoid(gates[:, blk:] + gbi_ref[:, cs])
        nla = r * (C_RG * _softplus(-lam_ref[:, cs]))
        a = jnp.exp2(nla * -LOG2E)
        var = jnp.tanh(nla) * (a * a + 1.0)
        bb = jnp.where(var > 0.0, var * lax.rsqrt(var), 0.0) * (i * xc)
        a = a.reshape(s // 8, 8, blk)
        bb = bb.reshape(s // 8, 8, blk)
        for d in (1, 2, 4):
            keep = sub >= d
            a_sh = jnp.where(keep, pltpu.roll(a, d, 1), 1.0)
            b_sh = jnp.where(keep, pltpu.roll(bb, d, 1), 0.0)
            bb = a * b_sh + bb
            a = a * a_sh
        a_s[:, cs] = a.reshape(s, blk)
        b_s[:, cs] = bb.reshape(s, blk)

    def group(gi, carry):
        rows = pl.ds(pl.multiple_of(gi * 8, 8), 8)
        h = a_s[rows, :] * carry + b_s[rows, :]
        b_s[rows, :] = h
        return jnp.broadcast_to(h[7:8, :], h.shape)

    lax.fori_loop(0, s // 8, group, jnp.zeros((8, wc), F32), unroll=8)
    z = z_ref[...].astype(F32)
    y_ref[...] = (b_s[...] * _silu(z)).astype(y_ref.dtype)


def _rglru(p3, cw, cb, gw, gb, lam):
    b, s, _ = p3.shape
    w_c = cw.shape[1]
    blk = w_c // N_BLK_C
    nsp = w_c // RG_COLS
    return pl.pallas_call(
        _rglru_body,
        grid=(b, nsp),
        in_specs=[
            pl.BlockSpec((None, s, RG_COLS), lambda bi, n: (bi, 0, n)),
            pl.BlockSpec((None, s, RG_COLS), lambda bi, n: (bi, 0, nsp + n)),
            pl.BlockSpec((CONV_K, RG_COLS), lambda bi, n: (0, n)),
            pl.BlockSpec((1, RG_COLS), lambda bi, n: (0, n)),
            pl.BlockSpec((RG_COLS // blk, blk, 2 * blk), lambda bi, n: (n, 0, 0)),
            pl.BlockSpec((1, RG_COLS), lambda bi, n: (0, n)),
            pl.BlockSpec((1, RG_COLS), lambda bi, n: (0, nsp + n)),
            pl.BlockSpec((1, RG_COLS), lambda bi, n: (0, n)),
        ],
        out_specs=pl.BlockSpec((None, s, RG_COLS), lambda bi, n: (bi, 0, n)),
        out_shape=jax.ShapeDtypeStruct((b, s, w_c), BF16),
        scratch_shapes=[pltpu.VMEM((s, RG_COLS), F32), pltpu.VMEM((s, RG_COLS), F32)],
        compiler_params=_params(2),
        name="rglru",
    )(p3, p3, cw, cb, gw, gb, gb, lam)


def _unit_lower_inverse(a_list):
    r, c = _iota2((CHUNK, CHUNK), 0), _iota2((CHUNK, CHUNK), 1)
    eye = jnp.where(r == c, 1.0, 0.0).astype(F32)

    def pair_mask(sh):
        rb, cb_ = r >> sh, c >> sh
        return ((rb & 1) == 1) & (cb_ == rb - 1)

    ts = [eye - jnp.where(pair_mask(0), a, 0.0) for a in a_list]
    for sh in range(1, 6):
        mask = pair_mask(sh)
        x1 = [_mm(t, jnp.where(mask, a, 0.0)) for t, a in zip(ts, a_list)]
        x2 = [_mm(x, t) for x, t in zip(x1, ts)]
        ts = [t - x for t, x in zip(ts, x2)]
    return ts


def _gdn_body(q_ref, k_ref, v_ref, z_ref, g_ref, gt_ref, cw_ref, al_c_ref, dt_c_ref, al_r_ref, dt_r_ref, on_ref,
              y_ref, s_st, tail, qkv, col_d, col_b, row_d, lhs_s, o0_s, n_s, *, nt):
    ts = q_ref.shape[0]
    nct = ts // CHUNK
    w_d = q_ref.shape[1]
    h_d = w_d // HD_D
    scale = HD_D ** -0.5
    t = pl.program_id(0)
    srcs = (q_ref, k_ref, v_ref)

    def conv_block(slot, r0, first, cbk):
        ref = srcs[cbk // h_d]
        ci = slice((cbk % h_d) * HD_D, (cbk % h_d + 1) * HD_D)
        cs = slice(cbk * HD_D, (cbk + 1) * HD_D)
        cur = ref[pl.ds(r0, CHUNK), ci].astype(F32)
        prev = ref[pl.ds(pl.multiple_of(jnp.maximum(r0 - 16, 0), 16), 16), ci].astype(F32)
        xw = jnp.concatenate([jnp.where(first, tail[:, cs], prev), cur], axis=0)
        acc = cw_ref[CONV_K - 1:CONV_K, cs] * cur
        for d in range(1, CONV_K):
            acc = acc + cw_ref[CONV_K - 1 - d:CONV_K - d, cs] * xw[16 - d:16 - d + CHUNK]
        acc = _silu(acc)
        if cbk < 2 * h_d:
            acc = acc * lax.rsqrt(jnp.sum(acc * acc, axis=-1, keepdims=True) + EPS)
        qkv[slot, pl.ds(r0, CHUNK), cs] = acc

    def conv_rows(slot, rb):
        r0 = pl.multiple_of(rb * CHUNK, CHUNK)
        for cbk in range(3 * h_d):
            conv_block(slot, r0, rb == 0, cbk)

    def save_tail():
        keep = lax.rem(t + 1, nt) != 0
        for w, ref in enumerate(srcs):
            tail[:, w * w_d:(w + 1) * w_d] = jnp.where(keep, ref[ts - 16:ts, :].astype(F32), 0.0)

    @pl.when(t == 0)
    def _():
        tail[...] = jnp.zeros_like(tail)

        def rows(rb, carry):
            conv_rows(0, rb)
            return carry

        lax.fori_loop(0, nct, rows, 0)
        save_tail()

    @pl.when(t > 0)
    def _():
        cur_slot = lax.rem(t, 2)
        prv_slot = 1 - cur_slot

        @pl.when(lax.rem(t - 1, nt) == 0)
        def _():
            s_st[...] = jnp.zeros_like(s_st)

        g = g_ref[...]
        col_b[...] = _sigmoid(g)
        col_d[...] = _chunk_cumsum(-jnp.exp(al_c_ref[...]) * _softplus(g + dt_c_ref[...]), CHUNK) * LOG2E
        gt = gt_ref[...]
        g_r = -jnp.exp(al_r_ref[...]) * _softplus(gt[0:h_d] + dt_r_ref[...])
        row_d[...] = _mm_f32(g_r.reshape(h_d * nct, CHUNK), _triu(CHUNK)).reshape(h_d, nct, CHUNK) * LOG2E

        r_i, c_i = _iota2((CHUNK, CHUNK), 0), _iota2((CHUNK, CHUNK), 1)
        tril = c_i <= r_i
        strict = c_i < r_i
        hs = range(h_d)

        def group(gi, carry):
            probs = [(cc, h) for cc in range(GDN_GROUP) for h in hs]
            cidx = [gi * GDN_GROUP + cc for cc in range(GDN_GROUP)]
            rows = [pl.ds(pl.multiple_of(ci * CHUNK, CHUNK), CHUNK) for ci in cidx]
            dc_all = [col_d[r, :] for r in rows]
            beta_all = [col_b[r, :] for r in rows]
            dc = [dc_all[cc][:, h:h + 1] for cc, h in probs]
            beta = [beta_all[cc][:, h_d + h:h_d + h + 1] for cc, h in probs]
            q = [qkv[prv_slot, rows[cc], h * HD_D:(h + 1) * HD_D] for cc, h in probs]
            k = [qkv[prv_slot, rows[cc], w_d + h * HD_D:w_d + (h + 1) * HD_D] for cc, h in probs]
            v = [qkv[prv_slot, rows[cc], 2 * w_d + h * HD_D:2 * w_d + (h + 1) * HD_D] for cc, h in probs]
            ps = range(len(probs))
            gam = [jnp.exp2(jnp.where(tril, dc[i] - row_d[probs[i][1], pl.ds(cidx[probs[i][0]], 1), :], NEG)) for i in ps]
            kb = [k[i] * beta[i] for i in ps]
            kk = [_mm_nt(kb[i], k[i]) for i in ps]
            t_inv = _unit_lower_inverse([jnp.where(strict, kk[i] * gam[i], 0.0) for i in ps])
            ed = [jnp.exp2(dc[i]) for i in ps]
            uw = [_mm(t_inv[i], jnp.concatenate([v[i] * beta[i], kb[i] * ed[i]], axis=1)) for i in ps]
            qs = [q[i] * scale for i in ps]
            qk = [_mm_nt(qs[i], k[i]) * gam[i] for i in ps]
            kd = [k[i] * jnp.exp2(dc[i][CHUNK - 1:CHUNK, :] - dc[i]) for i in ps]
            x1 = [_mm(qk[i], uw[i]) for i in ps]
            x2 = [_mm_tn(kd[i], uw[i]) for i in ps]
            for i, (cc, h) in enumerate(probs):
                lhs_s[cc, h] = jnp.concatenate([qs[i] * ed[i] - x1[i][:, HD_D:], x2[i][:, HD_D:]], axis=0).astype(BF16)
                o0_s[cc, h] = x1[i][:, :HD_D]
                n_s[cc, h] = x2[i][:, :HD_D]
            for rr in range(GDN_GROUP):
                conv_rows(cur_slot, gi * GDN_GROUP + rr)

            def step(cc, carry2):
                c = gi * GDN_GROUP + cc
                rws = pl.ds(pl.multiple_of(c * CHUNK, CHUNK), CHUNK)
                d_last = jnp.exp2(col_d[pl.ds(c * CHUNK + CHUNK - 1, 1), :])
                st = [s_st[h] for h in hs]
                r = [_mm(lhs_s[cc, h], st[h]) for h in hs]
                for h in hs:
                    o = o0_s[cc, h] + r[h][:CHUNK]
                    s_st[h] = d_last[:, h:h + 1] * st[h] + n_s[cc, h] - r[h][CHUNK:]
                    on = o * lax.rsqrt(jnp.mean(o * o, axis=-1, keepdims=True) + EPS) * on_ref[...]
                    z = z_ref[rws, h * HD_D:(h + 1) * HD_D].astype(F32)
                    y_ref[rws, h * HD_D:(h + 1) * HD_D] = (on * _silu(z)).astype(y_ref.dtype)
                return carry2

            for cc in range(GDN_GROUP):
                step(cc, 0)
            return carry

        lax.fori_loop(0, nct // GDN_GROUP, group, 0)
        save_tail()


def _gdn(p3, g3, gt4, cw, al_c, dt_c, al_r, dt_r, onorm, ts):
    b, s, _ = p3.shape
    w_d = 1024
    h_d = w_d // HD_D
    nct = ts // CHUNK
    nt = s // ts
    n_tiles = b * nt

    def conv_spec(sec):
        def imap(t, sec=sec):
            tc = jnp.minimum(t, n_tiles - 1)
            return (tc // nt, tc % nt, 2 + sec)
        return pl.BlockSpec((None, ts, w_d), imap)

    def prev_map(last):
        def imap(t):
            tp = jnp.maximum(t - 1, 0)
            return (tp // nt, tp % nt, last)
        return imap

    def gt_map(t):
        tp = jnp.maximum(t - 1, 0)
        return (tp // nt, 0, tp % nt, 0)

    return pl.pallas_call(
        functools.partial(_gdn_body, nt=nt),
        grid=(n_tiles + 1,),
        in_specs=[
            conv_spec(0), conv_spec(1), conv_spec(2),
            pl.BlockSpec((None, ts, w_d), prev_map(5)),
            pl.BlockSpec((None, ts, LANES), prev_map(0)),
            pl.BlockSpec((None, 2 * h_d, nct, CHUNK), gt_map),
            pl.BlockSpec((CONV_K, 3 * w_d), lambda t: (0, 0)),
            pl.BlockSpec((1, LANES), lambda t: (0, 0)),
            pl.BlockSpec((1, LANES), lambda t: (0, 0)),
            pl.BlockSpec((h_d, 1, 1), lambda t: (0, 0, 0)),
            pl.BlockSpec((h_d, 1, 1), lambda t: (0, 0, 0)),
            pl.BlockSpec((1, HD_D), lambda t: (0, 0)),
        ],
        out_specs=pl.BlockSpec((None, ts, w_d), prev_map(0)),
        out_shape=jax.ShapeDtypeStruct((b, s, w_d), BF16),
        scratch_shapes=[
            pltpu.VMEM((h_d, HD_D, HD_D), F32),
            pltpu.VMEM((16, 3 * w_d), F32),
            pltpu.VMEM((2, ts, 3 * w_d), F32),
            pltpu.VMEM((ts, LANES), F32),
            pltpu.VMEM((ts, LANES), F32),
            pltpu.VMEM((h_d, nct, CHUNK), F32),
            pltpu.VMEM((GDN_GROUP, h_d, CHUNK + HD_D, HD_D), BF16),
            pltpu.VMEM((GDN_GROUP, h_d, CHUNK, HD_D), F32),
            pltpu.VMEM((GDN_GROUP, h_d, HD_D, HD_D), F32),
        ],
        compiler_params=_params(1),
        name="gated_delta",
    )(p3, p3, p3, p3, g3, gt4, cw, al_c, dt_c, al_r, dt_r, onorm)


def _rel_row(rel_bias):
    t = (np.arange(REL_W) + CHUNK - 1) % REL_W
    idx = np.clip(LOOKBACK * CHUNK + CHUNK - 1 - t, -REL_MAX, REL_MAX) + REL_MAX
    return rel_bias[:, idx].astype(F32)[:, None, :]


def _pad_lanes(v, n=LANES):
    return jnp.pad(v, ((0, 0), (0, n - v.shape[1])))


def _pad_rows(v, n=LANES):
    return jnp.pad(v, ((0, n - v.shape[0]), (0, 0)))


def _weight_t(w):
    return jnp.swapaxes(w, 0, 1).astype(BF16)


def _gates_t(g3, n, length):
    b, s, _ = g3.shape
    return jnp.transpose(g3[:, :, :n], (0, 2, 1)).reshape(b, n, s // length, length)


def kernel(x, ev_norm, ev_w_in, ev_if_bias, ev_qn_gain, ev_kn_gain, ev_rel_bias, ev_w_out, od_norm, od_w_in, od_conv_c_w, od_conv_c_b, od_gate_w, od_gate_b, od_lambda, od_conv_d_w, od_a_log, od_dt_bias, od_onorm, od_w_out):
    b, s, d = x.shape
    m = b * s
    half = d // 2
    tm = min(1024, m)
    ts = min(512, s)
    assert half == 1024 and m % tm == 0 and s % ts == 0 and s % (CHUNK * ATTN_GROUP) == 0 and ts % MLSTM_L == 0
    x2 = x.reshape(m, d)
    depth = ev_norm.shape[0] + od_norm.shape[0]
    for layer in range(depth):
        j = layer // 2
        if layer % 2 == 0:
            n_main = 9 * half
            w_t = _weight_t(ev_w_in[j])
            p, g = _inproj(x2, ev_norm[j].reshape(1, d), w_t, _pad_rows(w_t[n_main:]), n_main, tm, n_main // 4)
            p3, g3 = p.reshape(b, s, n_main), g.reshape(b, s, LANES)
            ya = _attn(p3, ev_qn_gain[j].reshape(1, HD_A), ev_kn_gain[j].reshape(1, HD_A), _rel_row(ev_rel_bias[j]))
            bias = ev_if_bias[j].astype(F32)
            yb = _mlstm(p3, _gates_t(g3, 2 * H_B, MLSTM_L), bias.reshape(-1, 1, 1), ts)
            x2 = _outproj(ya.reshape(m, half), yb.reshape(m, half), ev_w_out[j], x2, tm, 1024)
        else:
            n_main = 6 * half
            h_d = half // HD_D
            w_t = _weight_t(od_w_in[j])
            p, g = _inproj(x2, od_norm[j].reshape(1, d), w_t, _pad_rows(w_t[n_main:]), n_main, tm, n_main // 3)
            p3, g3 = p.reshape(b, s, n_main), g.reshape(b, s, LANES)
            yc = _rglru(p3, od_conv_c_w[j], od_conv_c_b[j].reshape(1, -1), od_gate_w[j].astype(BF16),
                        od_gate_b[j].reshape(1, -1), od_lambda[j].reshape(1, -1))
            al, dt = od_a_log[j].astype(F32), od_dt_bias[j].astype(F32)
            yd = _gdn(p3, g3, _gates_t(g3, 2 * h_d, CHUNK), od_conv_d_w[j], _pad_lanes(al.reshape(1, -1)),
                      _pad_lanes(dt.reshape(1, -1)), al.reshape(-1, 1, 1), dt.reshape(-1, 1, 1),
                      od_onorm[j].reshape(1, HD_D), ts)
            x2 = _outproj(yc.reshape(m, half), yd.reshape(m, half), od_w_out[j], x2, tm, 1024)
    return x2.reshape(b, s, d)
```

```python
import functools

import numpy as np
import jax
import jax.numpy as jnp
from jax import lax
from jax.experimental import pallas as pl
from jax.experimental.pallas import tpu as pltpu

F32 = jnp.float32
BF16 = jnp.bfloat16

CHUNK = 64
EPS = 1e-6
NEG = -1e30
CONV_K = 4
LOG2E = 1.4426950408889634
LANES = 128
HD_A = 128
LOOKBACK = 8
BAND = (LOOKBACK + 1) * CHUNK
REL_MAX = 256
REL_W = 640
ATTN_GROUP = 16
ATTN_HEADS = 2
H_B = 4
MLSTM_L = 256
N_BLK_C = 8
C_RG = 8.0
RG_COLS = 1024
HD_D = 128
GDN_GROUP = 4

V7X_VMEM_BYTES = 64 * 1024 * 1024
VMEM_LIMIT = V7X_VMEM_BYTES - 8 * 1024 * 1024


def _params(n_axes):
    return pltpu.CompilerParams(dimension_semantics=("arbitrary",) * n_axes, vmem_limit_bytes=VMEM_LIMIT)


def _mm(a, b):
    return jnp.dot(a.astype(BF16), b.astype(BF16), preferred_element_type=F32)


def _mm_nt(a, b):
    return lax.dot_general(a.astype(BF16), b.astype(BF16), (((1,), (1,)), ((), ())), preferred_element_type=F32)


def _mm_tn(a, b):
    return lax.dot_general(a.astype(BF16), b.astype(BF16), (((0,), (0,)), ((), ())), preferred_element_type=F32)


def _mm_f32(a, b):
    return jnp.dot(a, b, preferred_element_type=F32, precision=lax.Precision.HIGHEST)


def _log1p(e):
    u = 1.0 + e
    return jnp.where(u == 1.0, e, jnp.log(u) * (e / (u - 1.0)))


def _softplus(x):
    return jnp.maximum(x, 0.0) + _log1p(jnp.exp(-jnp.abs(x)))


def _sigmoid(x):
    return 1.0 / (1.0 + jnp.exp2(x * -LOG2E))


def _silu(x):
    return x * _sigmoid(x)


def _iota2(shape, axis):
    return lax.broadcasted_iota(jnp.int32, shape, axis)


def _chunk_cumsum(x, length):
    pos = _iota2(x.shape, 0) & (length - 1)
    d = 1
    while d < length:
        x = x + jnp.where(pos >= d, pltpu.roll(x, d, 0), 0.0)
        d *= 2
    return x


def _triu(n):
    r, c = _iota2((n, n), 0), _iota2((n, n), 1)
    return jnp.where(r <= c, 1.0, 0.0).astype(F32)


def _inproj_body(x_ref, g_ref, w_ref, wg_ref, o_ref, og_ref, xn_ref):
    @pl.when(pl.program_id(1) == 0)
    def _():
        x = x_ref[...]
        ms = jnp.mean(x * x, axis=-1, keepdims=True)
        xn = ((x * lax.rsqrt(ms + EPS)) * g_ref[...]).astype(BF16)
        xn_ref[...] = xn
        og_ref[...] = _mm_nt(xn, wg_ref[...])

    o_ref[...] = _mm_nt(xn_ref[...], w_ref[...]).astype(o_ref.dtype)


def _inproj(x2, g, w_t, wg_t, n, tm, tn):
    m, d = x2.shape
    return pl.pallas_call(
        _inproj_body,
        grid=(m // tm, n // tn),
        in_specs=[
            pl.BlockSpec((tm, d), lambda i, j: (i, 0)),
            pl.BlockSpec((1, d), lambda i, j: (0, 0)),
            pl.BlockSpec((tn, d), lambda i, j: (j, 0)),
            pl.BlockSpec((LANES, d), lambda i, j: (0, 0)),
        ],
        out_specs=[
            pl.BlockSpec((tm, tn), lambda i, j: (i, j)),
            pl.BlockSpec((tm, LANES), lambda i, j: (i, 0)),
        ],
        out_shape=[jax.ShapeDtypeStruct((m, n), BF16), jax.ShapeDtypeStruct((m, LANES), F32)],
        scratch_shapes=[pltpu.VMEM((tm, d), BF16)],
        compiler_params=_params(2),
        name="inproj",
    )(x2, g, w_t, wg_t)


def _outproj_body(ya_ref, yb_ref, wa_ref, wb_ref, x_ref, o_ref, wa_s, wb_s):
    @pl.when(pl.program_id(1) == 0)
    def _():
        wa_s[...] = wa_ref[...].astype(BF16)
        wb_s[...] = wb_ref[...].astype(BF16)

    acc = jnp.dot(ya_ref[...], wa_s[...], preferred_element_type=F32)
    acc = acc + jnp.dot(yb_ref[...], wb_s[...], preferred_element_type=F32)
    o_ref[...] = x_ref[...] + acc


def _outproj(ya, yb, w, x2, tm, tn):
    m, d = x2.shape
    kh = ya.shape[1]
    return pl.pallas_call(
        _outproj_body,
        grid=(d // tn, m // tm),
        in_specs=[
            pl.BlockSpec((tm, kh), lambda j, i: (i, 0)),
            pl.BlockSpec((tm, kh), lambda j, i: (i, 0)),
            pl.BlockSpec((kh, tn), lambda j, i: (0, j)),
            pl.BlockSpec((kh, tn), lambda j, i: (1, j)),
            pl.BlockSpec((tm, tn), lambda j, i: (i, j)),
        ],
        out_specs=pl.BlockSpec((tm, tn), lambda j, i: (i, j)),
        out_shape=jax.ShapeDtypeStruct((m, d), F32),
        scratch_shapes=[pltpu.VMEM((kh, tn), BF16), pltpu.VMEM((kh, tn), BF16)],
        compiler_params=_params(2),
        name="outproj",
    )(ya, yb, w, w, x2)


def _attn_body(q_ref, k_ref, v_ref, z_ref, qg_ref, kg_ref, c_ref, o_ref, qs, ks, vs, bias_s):
    s = q_ref.shape[0]
    pad = LOOKBACK * CHUNK
    col = _iota2((CHUNK, BAND), 1)
    assert ATTN_GROUP >= LOOKBACK
    for hh in range(ATTN_HEADS):
        hc = slice(hh * HD_A, (hh + 1) * HD_A)
        q = q_ref[:, hc].astype(F32)
        q = q * lax.rsqrt(jnp.mean(q * q, axis=-1, keepdims=True) + EPS) * (qg_ref[...] * (HD_A ** -0.5 * LOG2E))
        qs[hh] = q.astype(BF16)
        k = k_ref[:, hc].astype(F32)
        k = k * lax.rsqrt(jnp.mean(k * k, axis=-1, keepdims=True) + EPS) * kg_ref[...]
        ks[hh, 0:pad, :] = jnp.zeros((pad, HD_A), BF16)
        ks[hh, pad:pad + s, :] = k.astype(BF16)
        vs[hh, 0:pad, :] = jnp.zeros((pad, HD_A), BF16)
        vs[hh, pad:pad + s, :] = v_ref[:, hc]
        cb = jnp.broadcast_to(c_ref[hh] * LOG2E, (CHUNK, REL_W))
        bias_s[hh] = pltpu.roll(cb, 0, 1, stride=1, stride_axis=0)[:, :BAND]

        def chunks(it, masked, hh=hh, hc=hc):
            gs = range(ATTN_GROUP)
            n = [it * ATTN_GROUP + g for g in gs]
            r0 = [pl.multiple_of(n[g] * CHUNK, CHUNK) for g in gs]
            sc = [_mm_nt(qs[hh, pl.ds(r0[g], CHUNK), :], ks[hh, pl.ds(r0[g], BAND), :]) + bias_s[hh] for g in gs]
            if masked:
                sc = [jnp.where(col >= (LOOKBACK - n[g]) * CHUNK, sc[g], NEG) for g in gs]
            p = [jnp.exp2(sc[g] - jnp.max(sc[g], axis=-1, keepdims=True)) for g in gs]
            pv = [_mm(p[g], vs[hh, pl.ds(r0[g], BAND), :]) for g in gs]
            for g in gs:
                o = pv[g] * (1.0 / jnp.sum(p[g], axis=-1, keepdims=True))
                z = z_ref[pl.ds(r0[g], CHUNK), hc].astype(F32)
                o_ref[pl.ds(r0[g], CHUNK), hc] = (o * _silu(z)).astype(o_ref.dtype)

        chunks(0, True)

        def rest(it, carry, chunks=chunks):
            chunks(it, False)
            return carry

        lax.fori_loop(1, s // (CHUNK * ATTN_GROUP), rest, 0)


def _attn(p3, qg, kg, crel):
    b, s, _ = p3.shape
    h_a = crel.shape[0]
    hp = h_a // ATTN_HEADS
    wh = ATTN_HEADS * HD_A

    def col_spec(sec):
        return pl.BlockSpec((None, s, wh), lambda bi, h, sec=sec: (bi, 0, sec * hp + h))

    return pl.pallas_call(
        _attn_body,
        grid=(b, hp),
        in_specs=[
            col_spec(0), col_spec(1), col_spec(2), col_spec(3),
            pl.BlockSpec((1, HD_A), lambda bi, h: (0, 0)),
            pl.BlockSpec((1, HD_A), lambda bi, h: (0, 0)),
            pl.BlockSpec((ATTN_HEADS, 1, REL_W), lambda bi, h: (h, 0, 0)),
        ],
        out_specs=pl.BlockSpec((None, s, wh), lambda bi, h: (bi, 0, h)),
        out_shape=jax.ShapeDtypeStruct((b, s, h_a * HD_A), BF16),
        scratch_shapes=[
            pltpu.VMEM((ATTN_HEADS, s, HD_A), BF16),
            pltpu.VMEM((ATTN_HEADS, s + LOOKBACK * CHUNK, HD_A), BF16),
            pltpu.VMEM((ATTN_HEADS, s + LOOKBACK * CHUNK, HD_A), BF16),
            pltpu.VMEM((ATTN_HEADS, CHUNK, BAND), F32),
        ],
        compiler_params=_params(2),
        name="chunk_attn",
    )(p3, p3, p3, p3, qg, kg, crel)


def _mlstm_body(q_ref, k_ref, v_ref, o_ref, z_ref, gt_ref, br_ref, y_ref, c_st, n_st, m_st, row_b, row_i):
    ts = q_ref.shape[0]
    ln = MLSTM_L
    nct = ts // ln
    hd = q_ref.shape[1] // H_B
    scale = hd ** -0.5
    j = pl.program_id(1)
    hs = range(H_B)
    hc = [slice(h * hd, (h + 1) * hd) for h in hs]

    @pl.when(j == 0)
    def _():
        c_st[...] = jnp.zeros_like(c_st)
        n_st[...] = jnp.zeros_like(n_st)
        m_st[...] = jnp.zeros_like(m_st)
        gt = gt_ref[...] + br_ref[...]
        for h in hs:
            row_b[h] = _mm_f32(-_softplus(-gt[H_B + h]), _triu(ln)) * LOG2E
            row_i[h] = gt[h] * LOG2E

    tril = _iota2((ln, ln), 1) <= _iota2((ln, ln), 0)
    log2_scale = float(np.log2(scale))
    m = [m_st[h][0:1, 0:1] for h in hs]
    cm = [c_st[h] for h in hs]
    nm = [n_st[h][0:1, :] for h in hs]
    chs = [(c, h) for c in range(nct) for h in hs]
    rows = [slice(c * ln, (c + 1) * ln) for c in range(nct)]
    li_r = {(c, h): row_i[h, pl.ds(j * nct + c, 1), :] for c, h in chs}
    bc_r = {(c, h): row_b[h, pl.ds(j * nct + c, 1), :] for c, h in chs}
    pad = jnp.zeros((LANES - 2 * H_B, ln), F32)
    cols = [jnp.concatenate([li_r[c, h] for h in hs] + [bc_r[c, h] for h in hs] + [pad], axis=0).T
            for c in range(nct)]
    lic = {(c, h): cols[c][:, h:h + 1] for c, h in chs}
    bc = {(c, h): cols[c][:, H_B + h:H_B + h + 1] for c, h in chs}
    rl = {ch: li_r[ch] - bc_r[ch] for ch in chs}
    b_last = {ch: bc[ch][ln - 1:ln, :] for ch in chs}
    m_in, m_new = {}, {}
    for c, h in chs:
        m_in[c, h] = m[h]
        m_new[c, h] = jnp.maximum(b_last[c, h] + m[h], jnp.max(b_last[c, h] + rl[c, h], axis=-1, keepdims=True))
        m[h] = m_new[c, h]
    qn = {(c, h): q_ref[rows[c], hc[h]] for c, h in chs}
    kn = {(c, h): k_ref[rows[c], hc[h]] for c, h in chs}
    vn = {(c, h): v_ref[rows[c], hc[h]] for c, h in chs}
    qk = {ch: _mm_nt(qn[ch], kn[ch]) for ch in chs}
    dmat = {ch: jnp.where(tril, bc[ch] + rl[ch], NEG) for ch in chs}
    mt = {ch: jnp.maximum(bc[ch] + m_in[ch], jnp.max(dmat[ch], axis=-1, keepdims=True)) for ch in chs}
    cmt = {ch: mt[ch] - log2_scale for ch in chs}
    w_inter = {ch: jnp.exp2(bc[ch] + m_in[ch] - cmt[ch]) for ch in chs}
    p = {ch: jnp.exp2(dmat[ch] - cmt[ch]) * qk[ch] for ch in chs}
    pv = {ch: _mm(p[ch], vn[ch]) for ch in chs}
    ws_c = {ch: jnp.exp2(b_last[ch] - bc[ch] + lic[ch] - m_new[ch]).astype(BF16) for ch in chs}
    ws_r = {ch: jnp.exp2(b_last[ch] + rl[ch] - m_new[ch]) for ch in chs}
    upd = {ch: _mm_tn(kn[ch] * ws_c[ch], vn[ch]) for ch in chs}
    n_add = {ch: _mm(jnp.broadcast_to(ws_r[ch], (8, ln)), kn[ch])[0:1, :] for ch in chs}
    psum = {ch: jnp.sum(p[ch], axis=-1, keepdims=True) for ch in chs}
    gate = {}
    for c, h in chs:
        og = o_ref[rows[c], hc[h]].astype(F32)
        zg = z_ref[rows[c], hc[h]].astype(F32)
        gate[c, h] = zg * (1.0 / ((1.0 + jnp.exp2(og * -LOG2E)) * (1.0 + jnp.exp2(zg * -LOG2E))))
    for c in range(nct):
        qc = {h: _mm(qn[c, h], cm[h]) for h in hs}
        qn_dot = {h: _mm_nt(qn[c, h], jnp.broadcast_to(nm[h], (8, hd)))[:, 0:1] for h in hs}
        for h in hs:
            num = w_inter[c, h] * qc[h] + pv[c, h]
            den = w_inter[c, h] * qn_dot[h] + psum[c, h]
            hout = num * (1.0 / jnp.maximum(jnp.abs(den), jnp.exp2(-mt[c, h])))
            y_ref[rows[c], hc[h]] = (hout * gate[c, h]).astype(y_ref.dtype)
            w_c = jnp.exp2(b_last[c, h] + m_in[c, h] - m_new[c, h])
            cm[h] = w_c * cm[h] + upd[c, h]
            nm[h] = w_c * nm[h] + n_add[c, h]
    for h in hs:
        c_st[h] = cm[h]
        n_st[h] = jnp.broadcast_to(nm[h], n_st.shape[1:])
        m_st[h] = jnp.broadcast_to(m[h], m_st.shape[1:])


def _mlstm(p3, gt4, bias_row, ts):
    b, s, _ = p3.shape
    w_b = 1024
    hd = w_b // H_B
    ncs = s // MLSTM_L

    def col_spec(sec):
        return pl.BlockSpec((None, ts, w_b), lambda bi, j, sec=sec: (bi, j, 4 + sec))

    return pl.pallas_call(
        _mlstm_body,
        grid=(b, s // ts),
        in_specs=[
            col_spec(0), col_spec(1), col_spec(2), col_spec(3), col_spec(4),
            pl.BlockSpec((None, 2 * H_B, ncs, MLSTM_L), lambda bi, j: (bi, 0, 0, 0)),
            pl.BlockSpec((2 * H_B, 1, 1), lambda bi, j: (0, 0, 0)),
        ],
        out_specs=pl.BlockSpec((None, ts, w_b), lambda bi, j: (bi, j, 0)),
        out_shape=jax.ShapeDtypeStruct((b, s, w_b), BF16),
        scratch_shapes=[
            pltpu.VMEM((H_B, hd, hd), F32),
            pltpu.VMEM((H_B, 8, hd), F32),
            pltpu.VMEM((H_B, 8, LANES), F32),
            pltpu.VMEM((H_B, ncs, MLSTM_L), F32),
            pltpu.VMEM((H_B, ncs, MLSTM_L), F32),
        ],
        compiler_params=_params(2),
        name="mlstm",
    )(p3, p3, p3, p3, p3, gt4, bias_row)


def _rglru_body(x_ref, z_ref, cw_ref, cb_ref, gw_ref, gbr_ref, gbi_ref, lam_ref, y_ref, a_s, b_s):
    s, wc = x_ref.shape
    blk = gw_ref.shape[1]
    row8 = _iota2((8, blk), 0)
    sub = _iota2((s // 8, 8, blk), 1)
    for n in range(wc // blk):
        cs = slice(n * blk, (n + 1) * blk)
        x = x_ref[:, cs].astype(F32)
        cw = cw_ref[:, cs]
        xc = cw[CONV_K - 1:CONV_K, :] * x + cb_ref[:, cs]
        for d in range(1, CONV_K):
            xs = pltpu.roll(x, d, 0)
            xs = jnp.concatenate([jnp.where(row8 >= d, xs[0:8], 0.0), xs[8:]], axis=0)
            xc = xc + cw[CONV_K - 1 - d:CONV_K - d, :] * xs
        gates = _mm(xc, gw_ref[n])
        r = _sigmoid(gates[:, :blk] + gbr_ref[:, cs])
        i = _sigmoid(gates[:, blk:] + gbi_ref[:, cs])
        nla = r * (C_RG * _softplus(-lam_ref[:, cs]))
        a = jnp.exp2(nla * -LOG2E)
        var = jnp.tanh(nla) * (a * a + 1.0)
        bb = jnp.where(var > 0.0, var * lax.rsqrt(var), 0.0) * (i * xc)
        a = a.reshape(s // 8, 8, blk)
        bb = bb.reshape(s // 8, 8, blk)
        for d in (1, 2, 4):
            keep = sub >= d
            a_sh = jnp.where(keep, pltpu.roll(a, d, 1), 1.0)
            b_sh = jnp.where(keep, pltpu.roll(bb, d, 1), 0.0)
            bb = a * b_sh + bb
            a = a * a_sh
        a_s[:, cs] = a.reshape(s, blk)
        b_s[:, cs] = bb.reshape(s, blk)

    def group(gi, carry):
        rows = pl.ds(pl.multiple_of(gi * 8, 8), 8)
        h = a_s[rows, :] * carry + b_s[rows, :]
        b_s[rows, :] = h
        return jnp.broadcast_to(h[7:8, :], h.shape)

    lax.fori_loop(0, s // 8, group, jnp.zeros((8, wc), F32), unroll=8)
    z = z_ref[...].astype(F32)
    y_ref[...] = (b_s[...] * _silu(z)).astype(y_ref.dtype)


def _rglru(p3, cw, cb, gw, gb, lam):
    b, s, _ = p3.shape
    w_c = cw.shape[1]
    blk = w_c // N_BLK_C
    nsp = w_c // RG_COLS
    return pl.pallas_call(
        _rglru_body,
        grid=(b, nsp),
        in_specs=[
            pl.BlockSpec((None, s, RG_COLS), lambda bi, n: (bi, 0, n)),
            pl.BlockSpec((None, s, RG_COLS), lambda bi, n: (bi, 0, nsp + n)),
            pl.BlockSpec((CONV_K, RG_COLS), lambda bi, n: (0, n)),
            pl.BlockSpec((1, RG_COLS), lambda bi, n: (0, n)),
            pl.BlockSpec((RG_COLS // blk, blk, 2 * blk), lambda bi, n: (n, 0, 0)),
            pl.BlockSpec((1, RG_COLS), lambda bi, n: (0, n)),
            pl.BlockSpec((1, RG_COLS), lambda bi, n: (0, nsp + n)),
            pl.BlockSpec((1, RG_COLS), lambda bi, n: (0, n)),
        ],
        out_specs=pl.BlockSpec((None, s, RG_COLS), lambda bi, n: (bi, 0, n)),
        out_shape=jax.ShapeDtypeStruct((b, s, w_c), BF16),
        scratch_shapes=[pltpu.VMEM((s, RG_COLS), F32), pltpu.VMEM((s, RG_COLS), F32)],
        compiler_params=_params(2),
        name="rglru",
    )(p3, p3, cw, cb, gw, gb, gb, lam)


def _unit_lower_inverse(a_list):
    r, c = _iota2((CHUNK, CHUNK), 0), _iota2((CHUNK, CHUNK), 1)
    eye = jnp.where(r == c, 1.0, 0.0).astype(F32)

    def pair_mask(sh):
        rb, cb_ = r >> sh, c >> sh
        return ((rb & 1) == 1) & (cb_ == rb - 1)

    ts = [eye - jnp.where(pair_mask(0), a, 0.0) for a in a_list]
    for sh in range(1, 6):
        mask = pair_mask(sh)
        x1 = [_mm(t, jnp.where(mask, a, 0.0)) for t, a in zip(ts, a_list)]
        x2 = [_mm(x, t) for x, t in zip(x1, ts)]
        ts = [t - x for t, x in zip(ts, x2)]
    return ts


def _gdn_body(q_ref, k_ref, v_ref, z_ref, g_ref, gt_ref, cw_ref, al_c_ref, dt_c_ref, al_r_ref, dt_r_ref, on_ref,
              y_ref, s_st, tail, qkv, col_d, col_b, row_d, lhs_s, o0_s, n_s, *, nt):
    ts = q_ref.shape[0]
    nct = ts // CHUNK
    w_d = q_ref.shape[1]
    h_d = w_d // HD_D
    scale = HD_D ** -0.5
    t = pl.program_id(0)
    srcs = (q_ref, k_ref, v_ref)

    def conv_block(slot, r0, first, cbk):
        ref = srcs[cbk // h_d]
        ci = slice((cbk % h_d) * HD_D, (cbk % h_d + 1) * HD_D)
        cs = slice(cbk * HD_D, (cbk + 1) * HD_D)
        cur = ref[pl.ds(r0, CHUNK), ci].astype(F32)
        if isinstance(first, bool):
            halo = tail[:, cs] if first else ref[pl.ds(r0 - 16, 16), ci].astype(F32)
        else:
            prev = ref[pl.ds(pl.multiple_of(jnp.maximum(r0 - 16, 0), 16), 16), ci].astype(F32)
            halo = jnp.where(first, tail[:, cs], prev)
        xw = jnp.concatenate([halo, cur], axis=0)
        acc = cw_ref[CONV_K - 1:CONV_K, cs] * cur
        for d in range(1, CONV_K):
            acc = acc + cw_ref[CONV_K - 1 - d:CONV_K - d, cs] * xw[16 - d:16 - d + CHUNK]
        acc = _silu(acc)
        if cbk < 2 * h_d:
            acc = acc * lax.rsqrt(jnp.sum(acc * acc, axis=-1, keepdims=True) + EPS)
        qkv[slot, pl.ds(r0, CHUNK), cs] = acc

    def conv_rows(slot, rb):
        r0 = rb * CHUNK if isinstance(rb, int) else pl.multiple_of(rb * CHUNK, CHUNK)
        for cbk in range(3 * h_d):
            conv_block(slot, r0, rb == 0, cbk)

    def save_tail():
        keep = lax.rem(t + 1, nt) != 0
        for w, ref in enumerate(srcs):
            tail[:, w * w_d:(w + 1) * w_d] = jnp.where(keep, ref[ts - 16:ts, :].astype(F32), 0.0)

    @pl.when(t == 0)
    def _():
        tail[...] = jnp.zeros_like(tail)

        def rows(rb, carry):
            conv_rows(0, rb)
            return carry

        lax.fori_loop(0, nct, rows, 0)
        save_tail()

    @pl.when(t > 0)
    def _():
        cur_slot = lax.rem(t, 2)
        prv_slot = 1 - cur_slot

        @pl.when(lax.rem(t - 1, nt) == 0)
        def _():
            s_st[...] = jnp.zeros_like(s_st)

        g = g_ref[...]
        col_b[...] = _sigmoid(g)
        col_d[...] = _chunk_cumsum(-jnp.exp(al_c_ref[...]) * _softplus(g + dt_c_ref[...]), CHUNK) * LOG2E
        gt = gt_ref[...]
        g_r = -jnp.exp(al_r_ref[...]) * _softplus(gt[0:h_d] + dt_r_ref[...])
        row_d[...] = _mm_f32(g_r.reshape(h_d * nct, CHUNK), _triu(CHUNK)).reshape(h_d, nct, CHUNK) * LOG2E

        r_i, c_i = _iota2((CHUNK, CHUNK), 0), _iota2((CHUNK, CHUNK), 1)
        tril = c_i <= r_i
        strict = c_i < r_i
        hs = range(h_d)

        def group(gi):
            probs = [(cc, h) for cc in range(GDN_GROUP) for h in hs]
            cidx = [gi * GDN_GROUP + cc for cc in range(GDN_GROUP)]
            rows = [pl.ds(ci * CHUNK, CHUNK) for ci in cidx]
            dc_all = [col_d[r, :] for r in rows]
            beta_all = [col_b[r, :] for r in rows]
            dc = [dc_all[cc][:, h:h + 1] for cc, h in probs]
            beta = [beta_all[cc][:, h_d + h:h_d + h + 1] for cc, h in probs]
            q = [qkv[prv_slot, rows[cc], h * HD_D:(h + 1) * HD_D] for cc, h in probs]
            k = [qkv[prv_slot, rows[cc], w_d + h * HD_D:w_d + (h + 1) * HD_D] for cc, h in probs]
            v = [qkv[prv_slot, rows[cc], 2 * w_d + h * HD_D:2 * w_d + (h + 1) * HD_D] for cc, h in probs]
            ps = range(len(probs))
            gam = [jnp.exp2(jnp.where(tril, dc[i] - row_d[probs[i][1], pl.ds(cidx[probs[i][0]], 1), :], NEG)) for i in ps]
            kb = [k[i] * beta[i] for i in ps]
            kk = [_mm_nt(kb[i], k[i]) for i in ps]
            t_inv = _unit_lower_inverse([jnp.where(strict, kk[i] * gam[i], 0.0) for i in ps])
            ed = [jnp.exp2(dc[i]) for i in ps]
            uw = [_mm(t_inv[i], jnp.concatenate([v[i] * beta[i], kb[i] * ed[i]], axis=1)) for i in ps]
            qs = [q[i] * scale for i in ps]
            qk = [_mm_nt(qs[i], k[i]) * gam[i] for i in ps]
            kd = [k[i] * jnp.exp2(dc[i][CHUNK - 1:CHUNK, :] - dc[i]) for i in ps]
            x1 = [_mm(qk[i], uw[i]) for i in ps]
            x2 = [_mm_tn(kd[i], uw[i]) for i in ps]
            for i, (cc, h) in enumerate(probs):
                lhs_s[cc, h] = jnp.concatenate([qs[i] * ed[i] - x1[i][:, HD_D:], x2[i][:, HD_D:]], axis=0).astype(BF16)
                o0_s[cc, h] = x1[i][:, :HD_D]
                n_s[cc, h] = x2[i][:, :HD_D]
            for rr in range(GDN_GROUP):
                conv_rows(cur_slot, gi * GDN_GROUP + rr)

            def step(cc):
                c = gi * GDN_GROUP + cc
                rws = pl.ds(c * CHUNK, CHUNK)
                d_last = jnp.exp2(col_d[pl.ds(c * CHUNK + CHUNK - 1, 1), :])
                st = [s_st[h] for h in hs]
                r = [_mm(lhs_s[cc, h], st[h]) for h in hs]
                for h in hs:
                    o = o0_s[cc, h] + r[h][:CHUNK]
                    s_st[h] = d_last[:, h:h + 1] * st[h] + n_s[cc, h] - r[h][CHUNK:]
                    on = o * lax.rsqrt(jnp.mean(o * o, axis=-1, keepdims=True) + EPS) * on_ref[...]
                    z = z_ref[rws, h * HD_D:(h + 1) * HD_D].astype(F32)
                    y_ref[rws, h * HD_D:(h + 1) * HD_D] = (on * _silu(z)).astype(y_ref.dtype)

            for cc in range(GDN_GROUP):
                step(cc)

        for gi in range(nct // GDN_GROUP):
            group(gi)
        save_tail()


def _gdn(p3, g3, gt4, cw, al_c, dt_c, al_r, dt_r, onorm, ts):
    b, s, _ = p3.shape
    w_d = 1024
    h_d = w_d // HD_D
    nct = ts // CHUNK
    nt = s // ts
    n_tiles = b * nt

    def conv_spec(sec):
        def imap(t, sec=sec):
            tc = jnp.minimum(t, n_tiles - 1)
            return (tc // nt, tc % nt, 2 + sec)
        return pl.BlockSpec((None, ts, w_d), imap)

    def prev_map(last):
        def imap(t):
            tp = jnp.maximum(t - 1, 0)
            return (tp // nt, tp % nt, last)
        return imap

    def gt_map(t):
        tp = jnp.maximum(t - 1, 0)
        return (tp // nt, 0, tp % nt, 0)

    return pl.pallas_call(
        functools.partial(_gdn_body, nt=nt),
        grid=(n_tiles + 1,),
        in_specs=[
            conv_spec(0), conv_spec(1), conv_spec(2),
            pl.BlockSpec((None, ts, w_d), prev_map(5)),
            pl.BlockSpec((None, ts, LANES), prev_map(0)),
            pl.BlockSpec((None, 2 * h_d, nct, CHUNK), gt_map),
            pl.BlockSpec((CONV_K, 3 * w_d), lambda t: (0, 0)),
            pl.BlockSpec((1, LANES), lambda t: (0, 0)),
            pl.BlockSpec((1, LANES), lambda t: (0, 0)),
            pl.BlockSpec((h_d, 1, 1), lambda t: (0, 0, 0)),
            pl.BlockSpec((h_d, 1, 1), lambda t: (0, 0, 0)),
            pl.BlockSpec((1, HD_D), lambda t: (0, 0)),
        ],
        out_specs=pl.BlockSpec((None, ts, w_d), prev_map(0)),
        out_shape=jax.ShapeDtypeStruct((b, s, w_d), BF16),
        scratch_shapes=[
            pltpu.VMEM((h_d, HD_D, HD_D), F32),
            pltpu.VMEM((16, 3 * w_d), F32),
            pltpu.VMEM((2, ts, 3 * w_d), F32),
            pltpu.VMEM((ts, LANES), F32),
            pltpu.VMEM((ts, LANES), F32),
            pltpu.VMEM((h_d, nct, CHUNK), F32),
            pltpu.VMEM((GDN_GROUP, h_d, CHUNK + HD_D, HD_D), BF16),
            pltpu.VMEM((GDN_GROUP, h_d, CHUNK, HD_D), F32),
            pltpu.VMEM((GDN_GROUP, h_d, HD_D, HD_D), F32),
        ],
        compiler_params=_params(1),
        name="gated_delta",
    )(p3, p3, p3, p3, g3, gt4, cw, al_c, dt_c, al_r, dt_r, onorm)


def _rel_row(rel_bias):
    t = (np.arange(REL_W) + CHUNK - 1) % REL_W
    idx = np.clip(LOOKBACK * CHUNK + CHUNK - 1 - t, -REL_MAX, REL_MAX) + REL_MAX
    return rel_bias[:, idx].astype(F32)[:, None, :]


def _pad_lanes(v, n=LANES):
    return jnp.pad(v, ((0, 0), (0, n - v.shape[1])))


def _pad_rows(v, n=LANES):
    return jnp.pad(v, ((0, n - v.shape[0]), (0, 0)))


def _weight_t(w):
    return jnp.swapaxes(w, 0, 1).astype(BF16)


def _gates_t(g3, n, length):
    b, s, _ = g3.shape
    return jnp.transpose(g3[:, :, :n], (0, 2, 1)).reshape(b, n, s // length, length)


def kernel(x, ev_norm, ev_w_in, ev_if_bias, ev_qn_gain, ev_kn_gain, ev_rel_bias, ev_w_out, od_norm, od_w_in, od_conv_c_w, od_conv_c_b, od_gate_w, od_gate_b, od_lambda, od_conv_d_w, od_a_log, od_dt_bias, od_onorm, od_w_out):
    b, s, d = x.shape
    m = b * s
    half = d // 2
    tm = min(1024, m)
    ts = min(512, s)
    assert half == 1024 and m % tm == 0 and s % ts == 0 and s % (CHUNK * ATTN_GROUP) == 0 and ts % MLSTM_L == 0
    x2 = x.reshape(m, d)
    depth = ev_norm.shape[0] + od_norm.shape[0]
    for layer in range(depth):
        j = layer // 2
        if layer % 2 == 0:
            n_main = 9 * half
            w_t = _weight_t(ev_w_in[j])
            p, g = _inproj(x2, ev_norm[j].reshape(1, d), w_t, _pad_rows(w_t[n_main:]), n_main, tm, n_main // 4)
            p3, g3 = p.reshape(b, s, n_main), g.reshape(b, s, LANES)
            ya = _attn(p3, ev_qn_gain[j].reshape(1, HD_A), ev_kn_gain[j].reshape(1, HD_A), _rel_row(ev_rel_bias[j]))
            bias = ev_if_bias[j].astype(F32)
            yb = _mlstm(p3, _gates_t(g3, 2 * H_B, MLSTM_L), bias.reshape(-1, 1, 1), ts)
            x2 = _outproj(ya.reshape(m, half), yb.reshape(m, half), ev_w_out[j], x2, tm, 1024)
        else:
            n_main = 6 * half
            h_d = half // HD_D
            w_t = _weight_t(od_w_in[j])
            p, g = _inproj(x2, od_norm[j].reshape(1, d), w_t, _pad_rows(w_t[n_main:]), n_main, tm, n_main // 3)
            p3, g3 = p.reshape(b, s, n_main), g.reshape(b, s, LANES)
            yc = _rglru(p3, od_conv_c_w[j], od_conv_c_b[j].reshape(1, -1), od_gate_w[j].astype(BF16),
                        od_gate_b[j].reshape(1, -1), od_lambda[j].reshape(1, -1))
            al, dt = od_a_log[j].astype(F32), od_dt_bias[j].astype(F32)
            yd = _gdn(p3, g3, _gates_t(g3, 2 * h_d, CHUNK), od_conv_d_w[j], _pad_lanes(al.reshape(1, -1)),
                      _pad_lanes(dt.reshape(1, -1)), al.reshape(-1, 1, 1), dt.reshape(-1, 1, 1),
                      od_onorm[j].reshape(1, HD_D), ts)
            x2 = _outproj(yc.reshape(m, half), yd.reshape(m, half), od_w_out[j], x2, tm, 1024)
    return x2.reshape(b, s, d)
```

```python
import functools

import numpy as np
import jax
import jax.numpy as jnp
from jax import lax
from jax.experimental import pallas as pl
from jax.experimental.pallas import tpu as pltpu

F32 = jnp.float32
BF16 = jnp.bfloat16

CHUNK = 64
EPS = 1e-6
NEG = -1e30
CONV_K = 4
LOG2E = 1.4426950408889634
LANES = 128
HD_A = 128
LOOKBACK = 8
BAND = (LOOKBACK + 1) * CHUNK
REL_MAX = 256
REL_W = 640
ATTN_GROUP = 16
ATTN_HEADS = 2
H_B = 4
MLSTM_L = 256
N_BLK_C = 8
C_RG = 8.0
RG_COLS = 1024
HD_D = 128
GDN_GROUP = 4

V7X_VMEM_BYTES = 64 * 1024 * 1024
VMEM_LIMIT = V7X_VMEM_BYTES - 8 * 1024 * 1024


def _params(n_axes):
    return pltpu.CompilerParams(dimension_semantics=("arbitrary",) * n_axes, vmem_limit_bytes=VMEM_LIMIT)


def _mm(a, b):
    return jnp.dot(a.astype(BF16), b.astype(BF16), preferred_element_type=F32)


def _mm_nt(a, b):
    return lax.dot_general(a.astype(BF16), b.astype(BF16), (((1,), (1,)), ((), ())), preferred_element_type=F32)


def _mm_tn(a, b):
    return lax.dot_general(a.astype(BF16), b.astype(BF16), (((0,), (0,)), ((), ())), preferred_element_type=F32)


def _mm_f32(a, b):
    return jnp.dot(a, b, preferred_element_type=F32, precision=lax.Precision.HIGHEST)


def _log1p(e):
    u = 1.0 + e
    return jnp.where(u == 1.0, e, jnp.log(u) * (e / (u - 1.0)))


def _softplus(x):
    return jnp.maximum(x, 0.0) + _log1p(jnp.exp(-jnp.abs(x)))


def _sigmoid(x):
    return 1.0 / (1.0 + jnp.exp2(x * -LOG2E))


def _silu(x):
    return x * _sigmoid(x)


def _iota2(shape, axis):
    return lax.broadcasted_iota(jnp.int32, shape, axis)


def _chunk_cumsum(x, length):
    pos = _iota2(x.shape, 0) & (length - 1)
    d = 1
    while d < length:
        x = x + jnp.where(pos >= d, pltpu.roll(x, d, 0), 0.0)
        d *= 2
    return x


def _triu(n):
    r, c = _iota2((n, n), 0), _iota2((n, n), 1)
    return jnp.where(r <= c, 1.0, 0.0).astype(F32)


def _inproj_body(x_ref, g_ref, w_ref, wg_ref, o_ref, og_ref, xn_ref):
    @pl.when(pl.program_id(1) == 0)
    def _():
        x = x_ref[...]
        ms = jnp.mean(x * x, axis=-1, keepdims=True)
        xn = ((x * lax.rsqrt(ms + EPS)) * g_ref[...]).astype(BF16)
        xn_ref[...] = xn
        og_ref[...] = _mm_nt(xn, wg_ref[...])

    o_ref[...] = _mm_nt(xn_ref[...], w_ref[...]).astype(o_ref.dtype)


def _inproj(x2, g, w_t, wg_t, n, tm, tn):
    m, d = x2.shape
    return pl.pallas_call(
        _inproj_body,
        grid=(m // tm, n // tn),
        in_specs=[
            pl.BlockSpec((tm, d), lambda i, j: (i, 0)),
            pl.BlockSpec((1, d), lambda i, j: (0, 0)),
            pl.BlockSpec((tn, d), lambda i, j: (j, 0)),
            pl.BlockSpec((LANES, d), lambda i, j: (0, 0)),
        ],
        out_specs=[
            pl.BlockSpec((tm, tn), lambda i, j: (i, j)),
            pl.BlockSpec((tm, LANES), lambda i, j: (i, 0)),
        ],
        out_shape=[jax.ShapeDtypeStruct((m, n), BF16), jax.ShapeDtypeStruct((m, LANES), F32)],
        scratch_shapes=[pltpu.VMEM((tm, d), BF16)],
        compiler_params=_params(2),
        name="inproj",
    )(x2, g, w_t, wg_t)


def _outproj_body(ya_ref, yb_ref, wa_ref, wb_ref, x_ref, o_ref, wa_s, wb_s):
    @pl.when(pl.program_id(1) == 0)
    def _():
        wa_s[...] = wa_ref[...].astype(BF16)
        wb_s[...] = wb_ref[...].astype(BF16)

    acc = jnp.dot(ya_ref[...], wa_s[...], preferred_element_type=F32)
    acc = acc + jnp.dot(yb_ref[...], wb_s[...], preferred_element_type=F32)
    o_ref[...] = x_ref[...] + acc


def _outproj(ya, yb, w, x2, tm, tn):
    m, d = x2.shape
    kh = ya.shape[1]
    return pl.pallas_call(
        _outproj_body,
        grid=(d // tn, m // tm),
        in_specs=[
            pl.BlockSpec((tm, kh), lambda j, i: (i, 0)),
            pl.BlockSpec((tm, kh), lambda j, i: (i, 0)),
            pl.BlockSpec((kh, tn), lambda j, i: (0, j)),
            pl.BlockSpec((kh, tn), lambda j, i: (1, j)),
            pl.BlockSpec((tm, tn), lambda j, i: (i, j)),
        ],
        out_specs=pl.BlockSpec((tm, tn), lambda j, i: (i, j)),
        out_shape=jax.ShapeDtypeStruct((m, d), F32),
        scratch_shapes=[pltpu.VMEM((kh, tn), BF16), pltpu.VMEM((kh, tn), BF16)],
        compiler_params=_params(2),
        name="outproj",
    )(ya, yb, w, w, x2)


def _attn_body(q_ref, k_ref, v_ref, z_ref, qg_ref, kg_ref, c_ref, o_ref, qs, ks, vs, bias_s):
    s = q_ref.shape[0]
    pad = LOOKBACK * CHUNK
    col = _iota2((CHUNK, BAND), 1)
    for hh in range(ATTN_HEADS):
        hc = slice(hh * HD_A, (hh + 1) * HD_A)
        q = q_ref[:, hc].astype(F32)
        q = q * lax.rsqrt(jnp.mean(q * q, axis=-1, keepdims=True) + EPS) * (qg_ref[...] * (HD_A ** -0.5 * LOG2E))
        qs[hh] = q.astype(BF16)
        k = k_ref[:, hc].astype(F32)
        k = k * lax.rsqrt(jnp.mean(k * k, axis=-1, keepdims=True) + EPS) * kg_ref[...]
        ks[hh, 0:pad, :] = jnp.zeros((pad, HD_A), BF16)
        ks[hh, pad:pad + s, :] = k.astype(BF16)
        vs[hh, 0:pad, :] = jnp.zeros((pad, HD_A), BF16)
        vs[hh, pad:pad + s, :] = v_ref[:, hc]
        cb = jnp.broadcast_to(c_ref[hh] * LOG2E, (CHUNK, REL_W))
        bias_s[hh] = pltpu.roll(cb, 0, 1, stride=1, stride_axis=0)[:, :BAND]

        for it in range(s // (CHUNK * ATTN_GROUP)):
            gs = range(ATTN_GROUP)
            n = [it * ATTN_GROUP + g for g in gs]
            r0 = [n[g] * CHUNK for g in gs]
            sc = [_mm_nt(qs[hh, pl.ds(r0[g], CHUNK), :], ks[hh, pl.ds(r0[g], BAND), :]) + bias_s[hh] for g in gs]
            sc = [jnp.where(col >= (LOOKBACK - n[g]) * CHUNK, sc[g], NEG) if n[g] < LOOKBACK else sc[g] for g in gs]
            p = [jnp.exp2(sc[g] - jnp.max(sc[g], axis=-1, keepdims=True)) for g in gs]
            pv = [_mm(p[g], vs[hh, pl.ds(r0[g], BAND), :]) for g in gs]
            for g in gs:
                o = pv[g] * (1.0 / jnp.sum(p[g], axis=-1, keepdims=True))
                z = z_ref[pl.ds(r0[g], CHUNK), hc].astype(F32)
                o_ref[pl.ds(r0[g], CHUNK), hc] = (o * _silu(z)).astype(o_ref.dtype)


def _attn(p3, qg, kg, crel):
    b, s, _ = p3.shape
    h_a = crel.shape[0]
    hp = h_a // ATTN_HEADS
    wh = ATTN_HEADS * HD_A

    def col_spec(sec):
        return pl.BlockSpec((None, s, wh), lambda bi, h, sec=sec: (bi, 0, sec * hp + h))

    return pl.pallas_call(
        _attn_body,
        grid=(b, hp),
        in_specs=[
            col_spec(0), col_spec(1), col_spec(2), col_spec(3),
            pl.BlockSpec((1, HD_A), lambda bi, h: (0, 0)),
            pl.BlockSpec((1, HD_A), lambda bi, h: (0, 0)),
            pl.BlockSpec((ATTN_HEADS, 1, REL_W), lambda bi, h: (h, 0, 0)),
        ],
        out_specs=pl.BlockSpec((None, s, wh), lambda bi, h: (bi, 0, h)),
        out_shape=jax.ShapeDtypeStruct((b, s, h_a * HD_A), BF16),
        scratch_shapes=[
            pltpu.VMEM((ATTN_HEADS, s, HD_A), BF16),
            pltpu.VMEM((ATTN_HEADS, s + LOOKBACK * CHUNK, HD_A), BF16),
            pltpu.VMEM((ATTN_HEADS, s + LOOKBACK * CHUNK, HD_A), BF16),
            pltpu.VMEM((ATTN_HEADS, CHUNK, BAND), F32),
        ],
        compiler_params=_params(2),
        name="chunk_attn",
    )(p3, p3, p3, p3, qg, kg, crel)


def _mlstm_body(q_ref, k_ref, v_ref, o_ref, z_ref, gt_ref, br_ref, y_ref, c_st, n_st, m_st, row_b, row_i):
    ts = q_ref.shape[0]
    ln = MLSTM_L
    nct = ts // ln
    hd = q_ref.shape[1] // H_B
    scale = hd ** -0.5
    j = pl.program_id(1)
    hs = range(H_B)
    hc = [slice(h * hd, (h + 1) * hd) for h in hs]

    @pl.when(j == 0)
    def _():
        c_st[...] = jnp.zeros_like(c_st)
        n_st[...] = jnp.zeros_like(n_st)
        m_st[...] = jnp.zeros_like(m_st)
        gt = gt_ref[...] + br_ref[...]
        for h in hs:
            row_b[h] = _mm_f32(-_softplus(-gt[H_B + h]), _triu(ln)) * LOG2E
            row_i[h] = gt[h] * LOG2E

    tril = _iota2((ln, ln), 1) <= _iota2((ln, ln), 0)
    log2_scale = float(np.log2(scale))
    m = [m_st[h][0:1, 0:1] for h in hs]
    cm = [c_st[h] for h in hs]
    nm = [n_st[h][0:1, :] for h in hs]
    chs = [(c, h) for c in range(nct) for h in hs]
    rows = [slice(c * ln, (c + 1) * ln) for c in range(nct)]
    li_r = {(c, h): row_i[h, pl.ds(j * nct + c, 1), :] for c, h in chs}
    bc_r = {(c, h): row_b[h, pl.ds(j * nct + c, 1), :] for c, h in chs}
    pad = jnp.zeros((LANES - 2 * H_B, ln), F32)
    cols = [jnp.concatenate([li_r[c, h] for h in hs] + [bc_r[c, h] for h in hs] + [pad], axis=0).T
            for c in range(nct)]
    lic = {(c, h): cols[c][:, h:h + 1] for c, h in chs}
    bc = {(c, h): cols[c][:, H_B + h:H_B + h + 1] for c, h in chs}
    rl = {ch: li_r[ch] - bc_r[ch] for ch in chs}
    b_last = {ch: bc[ch][ln - 1:ln, :] for ch in chs}
    m_in, m_new = {}, {}
    for c, h in chs:
        m_in[c, h] = m[h]
        m_new[c, h] = jnp.maximum(b_last[c, h] + m[h], jnp.max(b_last[c, h] + rl[c, h], axis=-1, keepdims=True))
        m[h] = m_new[c, h]
    qn = {(c, h): q_ref[rows[c], hc[h]] for c, h in chs}
    kn = {(c, h): k_ref[rows[c], hc[h]] for c, h in chs}
    vn = {(c, h): v_ref[rows[c], hc[h]] for c, h in chs}
    qk = {ch: _mm_nt(qn[ch], kn[ch]) for ch in chs}
    dmat = {ch: jnp.where(tril, bc[ch] + rl[ch], NEG) for ch in chs}
    mt = {ch: jnp.maximum(bc[ch] + m_in[ch], jnp.max(dmat[ch], axis=-1, keepdims=True)) for ch in chs}
    cmt = {ch: mt[ch] - log2_scale for ch in chs}
    w_inter = {ch: jnp.exp2(bc[ch] + m_in[ch] - cmt[ch]) for ch in chs}
    p = {ch: jnp.exp2(dmat[ch] - cmt[ch]) * qk[ch] for ch in chs}
    pv = {ch: _mm(p[ch], vn[ch]) for ch in chs}
    ws_c = {ch: jnp.exp2(b_last[ch] - bc[ch] + lic[ch] - m_new[ch]).astype(BF16) for ch in chs}
    ws_r = {ch: jnp.exp2(b_last[ch] + rl[ch] - m_new[ch]) for ch in chs}
    upd = {ch: _mm_tn(kn[ch] * ws_c[ch], vn[ch]) for ch in chs}
    n_add = {ch: _mm(jnp.broadcast_to(ws_r[ch], (8, ln)), kn[ch])[0:1, :] for ch in chs}
    psum = {ch: jnp.sum(p[ch], axis=-1, keepdims=True) for ch in chs}
    gate = {}
    for c, h in chs:
        og = o_ref[rows[c], hc[h]].astype(F32)
        zg = z_ref[rows[c], hc[h]].astype(F32)
        gate[c, h] = zg * (1.0 / ((1.0 + jnp.exp2(og * -LOG2E)) * (1.0 + jnp.exp2(zg * -LOG2E))))
    for c in range(nct):
        qc = {h: _mm(qn[c, h], cm[h]) for h in hs}
        qn_dot = {h: _mm_nt(qn[c, h], jnp.broadcast_to(nm[h], (8, hd)))[:, 0:1] for h in hs}
        for h in hs:
            num = w_inter[c, h] * qc[h] + pv[c, h]
            den = w_inter[c, h] * qn_dot[h] + psum[c, h]
            hout = num * (1.0 / jnp.maximum(jnp.abs(den), jnp.exp2(-mt[c, h])))
            y_ref[rows[c], hc[h]] = (hout * gate[c, h]).astype(y_ref.dtype)
            w_c = jnp.exp2(b_last[c, h] + m_in[c, h] - m_new[c, h])
            cm[h] = w_c * cm[h] + upd[c, h]
            nm[h] = w_c * nm[h] + n_add[c, h]
    for h in hs:
        c_st[h] = cm[h]
        n_st[h] = jnp.broadcast_to(nm[h], n_st.shape[1:])
        m_st[h] = jnp.broadcast_to(m[h], m_st.shape[1:])


def _mlstm(p3, gt4, bias_row, ts):
    b, s, _ = p3.shape
    w_b = 1024
    hd = w_b // H_B
    ncs = s // MLSTM_L

    def col_spec(sec):
        return pl.BlockSpec((None, ts, w_b), lambda bi, j, sec=sec: (bi, j, 4 + sec))

    return pl.pallas_call(
        _mlstm_body,
        grid=(b, s // ts),
        in_specs=[
            col_spec(0), col_spec(1), col_spec(2), col_spec(3), col_spec(4),
            pl.BlockSpec((None, 2 * H_B, ncs, MLSTM_L), lambda bi, j: (bi, 0, 0, 0)),
            pl.BlockSpec((2 * H_B, 1, 1), lambda bi, j: (0, 0, 0)),
        ],
        out_specs=pl.BlockSpec((None, ts, w_b), lambda bi, j: (bi, j, 0)),
        out_shape=jax.ShapeDtypeStruct((b, s, w_b), BF16),
        scratch_shapes=[
            pltpu.VMEM((H_B, hd, hd), F32),
            pltpu.VMEM((H_B, 8, hd), F32),
            pltpu.VMEM((H_B, 8, LANES), F32),
            pltpu.VMEM((H_B, ncs, MLSTM_L), F32),
            pltpu.VMEM((H_B, ncs, MLSTM_L), F32),
        ],
        compiler_params=_params(2),
        name="mlstm",
    )(p3, p3, p3, p3, p3, gt4, bias_row)


def _rglru_body(x_ref, z_ref, cw_ref, cb_ref, gw_ref, gbr_ref, gbi_ref, lam_ref, y_ref, a_s, b_s):
    s, wc = x_ref.shape
    blk = gw_ref.shape[1]
    row8 = _iota2((8, blk), 0)
    sub = _iota2((s // 8, 8, blk), 1)
    for n in range(wc // blk):
        cs = slice(n * blk, (n + 1) * blk)
        x = x_ref[:, cs].astype(F32)
        cw = cw_ref[:, cs]
        xc = cw[CONV_K - 1:CONV_K, :] * x + cb_ref[:, cs]
        for d in range(1, CONV_K):
            xs = pltpu.roll(x, d, 0)
            xs = jnp.concatenate([jnp.where(row8 >= d, xs[0:8], 0.0), xs[8:]], axis=0)
            xc = xc + cw[CONV_K - 1 - d:CONV_K - d, :] * xs
        gates = _mm(xc, gw_ref[n])
        r = _sigmoid(gates[:, :blk] + gbr_ref[:, cs])
        i = _sigmoid(gates[:, blk:] + gbi_ref[:, cs])
        nla = r * (C_RG * _softplus(-lam_ref[:, cs]))
        a = jnp.exp2(nla * -LOG2E)
        var = jnp.tanh(nla) * (a * a + 1.0)
        bb = jnp.where(var > 0.0, var * lax.rsqrt(var), 0.0) * (i * xc)
        a = a.reshape(s // 8, 8, blk)
        bb = bb.reshape(s // 8, 8, blk)
        for d in (1, 2, 4):
            keep = sub >= d
            a_sh = jnp.where(keep, pltpu.roll(a, d, 1), 1.0)
            b_sh = jnp.where(keep, pltpu.roll(bb, d, 1), 0.0)
            bb = a * b_sh + bb
            a = a * a_sh
        a_s[:, cs] = a.reshape(s, blk)
        b_s[:, cs] = bb.reshape(s, blk)

    def group(gi, carry):
        rows = pl.ds(pl.multiple_of(gi * 8, 8), 8)
        h = a_s[rows, :] * carry + b_s[rows, :]
        b_s[rows, :] = h
        return jnp.broadcast_to(h[7:8, :], h.shape)

    lax.fori_loop(0, s // 8, group, jnp.zeros((8, wc), F32), unroll=8)
    z = z_ref[...].astype(F32)
    y_ref[...] = (b_s[...] * _silu(z)).astype(y_ref.dtype)


def _rglru(p3, cw, cb, gw, gb, lam):
    b, s, _ = p3.shape
    w_c = cw.shape[1]
    blk = w_c // N_BLK_C
    nsp = w_c // RG_COLS
    return pl.pallas_call(
        _rglru_body,
        grid=(b, nsp),
        in_specs=[
            pl.BlockSpec((None, s, RG_COLS), lambda bi, n: (bi, 0, n)),
            pl.BlockSpec((None, s, RG_COLS), lambda bi, n: (bi, 0, nsp + n)),
            pl.BlockSpec((CONV_K, RG_COLS), lambda bi, n: (0, n)),
            pl.BlockSpec((1, RG_COLS), lambda bi, n: (0, n)),
            pl.BlockSpec((RG_COLS // blk, blk, 2 * blk), lambda bi, n: (n, 0, 0)),
            pl.BlockSpec((1, RG_COLS), lambda bi, n: (0, n)),
            pl.BlockSpec((1, RG_COLS), lambda bi, n: (0, nsp + n)),
            pl.BlockSpec((1, RG_COLS), lambda bi, n: (0, n)),
        ],
        out_specs=pl.BlockSpec((None, s, RG_COLS), lambda bi, n: (bi, 0, n)),
        out_shape=jax.ShapeDtypeStruct((b, s, w_c), BF16),
        scratch_shapes=[pltpu.VMEM((s, RG_COLS), F32), pltpu.VMEM((s, RG_COLS), F32)],
        compiler_params=_params(2),
        name="rglru",
    )(p3, p3, cw, cb, gw, gb, gb, lam)


def _unit_lower_inverse(a_list):
    r, c = _iota2((CHUNK, CHUNK), 0), _iota2((CHUNK, CHUNK), 1)
    eye = jnp.where(r == c, 1.0, 0.0).astype(F32)

    def pair_mask(sh):
        rb, cb_ = r >> sh, c >> sh
        return ((rb & 1) == 1) & (cb_ == rb - 1)

    ts = [eye - jnp.where(pair_mask(0), a, 0.0) for a in a_list]
    for sh in range(1, 6):
        mask = pair_mask(sh)
        x1 = [_mm(t, jnp.where(mask, a, 0.0)) for t, a in zip(ts, a_list)]
        x2 = [_mm(x, t) for x, t in zip(x1, ts)]
        ts = [t - x for t, x in zip(ts, x2)]
    return ts


def _gdn_body(q_ref, k_ref, v_ref, z_ref, g_ref, gt_ref, cw_ref, al_c_ref, dt_c_ref, al_r_ref, dt_r_ref, on_ref,
              y_ref, s_st, tail, qkv, col_d, col_b, row_d, lhs_s, o0_s, n_s, *, nt):
    ts = q_ref.shape[0]
    nct = ts // CHUNK
    w_d = q_ref.shape[1]
    h_d = w_d // HD_D
    scale = HD_D ** -0.5
    t = pl.program_id(0)
    srcs = (q_ref, k_ref, v_ref)
    sh_r, sh_c = _iota2((CHUNK, 16 + CHUNK), 0), _iota2((CHUNK, 16 + CHUNK), 1)
    shift = jnp.concatenate([sh_c == 16 + sh_r - d for d in range(CONV_K)], axis=0).astype(BF16)

    def conv_block(slot, r0, first, cbk):
        ref = srcs[cbk // h_d]
        ci = slice((cbk % h_d) * HD_D, (cbk % h_d + 1) * HD_D)
        cs = slice(cbk * HD_D, (cbk + 1) * HD_D)
        cur = ref[pl.ds(r0, CHUNK), ci]
        if isinstance(first, bool):
            halo = tail[:, cs] if first else ref[pl.ds(r0 - 16, 16), ci]
        else:
            prev = ref[pl.ds(pl.multiple_of(jnp.maximum(r0 - 16, 0), 16), 16), ci]
            halo = jnp.where(first, tail[:, cs], prev)
        xw = jnp.concatenate([halo, cur], axis=0)
        sh = jnp.dot(shift, xw, preferred_element_type=F32)
        acc = cw_ref[CONV_K - 1:CONV_K, cs] * sh[0:CHUNK]
        for d in range(1, CONV_K):
            acc = acc + cw_ref[CONV_K - 1 - d:CONV_K - d, cs] * sh[d * CHUNK:(d + 1) * CHUNK]
        acc = _silu(acc)
        if cbk < 2 * h_d:
            acc = acc * lax.rsqrt(jnp.sum(acc * acc, axis=-1, keepdims=True) + EPS)
        qkv[slot, pl.ds(r0, CHUNK), cs] = acc

    def conv_rows(slot, rb):
        r0 = rb * CHUNK if isinstance(rb, int) else pl.multiple_of(rb * CHUNK, CHUNK)
        for cbk in range(3 * h_d):
            conv_block(slot, r0, rb == 0, cbk)

    def save_tail():
        keep = lax.rem(t + 1, nt) != 0
        for w, ref in enumerate(srcs):
            tail[:, w * w_d:(w + 1) * w_d] = jnp.where(keep, ref[ts - 16:ts, :], 0.0)

    @pl.when(t == 0)
    def _():
        tail[...] = jnp.zeros_like(tail)

        def rows(rb, carry):
            conv_rows(0, rb)
            return carry

        lax.fori_loop(0, nct, rows, 0)
        save_tail()

    @pl.when(t > 0)
    def _():
        cur_slot = lax.rem(t, 2)
        prv_slot = 1 - cur_slot

        @pl.when(lax.rem(t - 1, nt) == 0)
        def _():
            s_st[...] = jnp.zeros_like(s_st)

        g = g_ref[...]
        col_b[...] = _sigmoid(g)
        col_d[...] = _chunk_cumsum(-jnp.exp(al_c_ref[...]) * _softplus(g + dt_c_ref[...]), CHUNK) * LOG2E
        gt = gt_ref[...]
        g_r = -jnp.exp(al_r_ref[...]) * _softplus(gt[0:h_d] + dt_r_ref[...])
        row_d[...] = _mm_f32(g_r.reshape(h_d * nct, CHUNK), _triu(CHUNK)).reshape(h_d, nct, CHUNK) * LOG2E

        r_i, c_i = _iota2((CHUNK, CHUNK), 0), _iota2((CHUNK, CHUNK), 1)
        tril = c_i <= r_i
        strict = c_i < r_i
        hs = range(h_d)

        def group(gi):
            probs = [(cc, h) for cc in range(GDN_GROUP) for h in hs]
            cidx = [gi * GDN_GROUP + cc for cc in range(GDN_GROUP)]
            rows = [pl.ds(ci * CHUNK, CHUNK) for ci in cidx]
            dc_all = [col_d[r, :] for r in rows]
            beta_all = [col_b[r, :] for r in rows]
            dc = [dc_all[cc][:, h:h + 1] for cc, h in probs]
            beta = [beta_all[cc][:, h_d + h:h_d + h + 1] for cc, h in probs]
            q = [qkv[prv_slot, rows[cc], h * HD_D:(h + 1) * HD_D] for cc, h in probs]
            k = [qkv[prv_slot, rows[cc], w_d + h * HD_D:w_d + (h + 1) * HD_D] for cc, h in probs]
            v = [qkv[prv_slot, rows[cc], 2 * w_d + h * HD_D:2 * w_d + (h + 1) * HD_D] for cc, h in probs]
            ps = range(len(probs))
            gam = [jnp.exp2(jnp.where(tril, dc[i] - row_d[probs[i][1], pl.ds(cidx[probs[i][0]], 1), :], NEG)) for i in ps]
            kb = [k[i] * beta[i] for i in ps]
            kk = [_mm_nt(kb[i], k[i]) for i in ps]
            t_inv = _unit_lower_inverse([jnp.where(strict, kk[i] * gam[i], 0.0) for i in ps])
            ed = [jnp.exp2(dc[i]) for i in ps]
            uw = [_mm(t_inv[i], jnp.concatenate([v[i] * beta[i], kb[i] * ed[i]], axis=1)) for i in ps]
            qs = [q[i] * scale for i in ps]
            qk = [_mm_nt(qs[i], k[i]) * gam[i] for i in ps]
            kd = [k[i] * jnp.exp2(dc[i][CHUNK - 1:CHUNK, :] - dc[i]) for i in ps]
            x1 = [_mm(qk[i], uw[i]) for i in ps]
            x2 = [_mm_tn(kd[i], uw[i]) for i in ps]
            for i, (cc, h) in enumerate(probs):
                lhs_s[cc, h] = jnp.concatenate([qs[i] * ed[i] - x1[i][:, HD_D:], x2[i][:, HD_D:]], axis=0).astype(BF16)
                o0_s[cc, h] = x1[i][:, :HD_D]
                n_s[cc, h] = x2[i][:, :HD_D]
            for rr in range(GDN_GROUP):
                conv_rows(cur_slot, gi * GDN_GROUP + rr)

            def step(cc):
                c = gi * GDN_GROUP + cc
                rws = pl.ds(c * CHUNK, CHUNK)
                d_last = jnp.exp2(col_d[pl.ds(c * CHUNK + CHUNK - 1, 1), :])
                st = [s_st[h] for h in hs]
                r = [_mm(lhs_s[cc, h], st[h]) for h in hs]
                for h in hs:
                    o = o0_s[cc, h] + r[h][:CHUNK]
                    s_st[h] = d_last[:, h:h + 1] * st[h] + n_s[cc, h] - r[h][CHUNK:]
                    on = o * lax.rsqrt(jnp.mean(o * o, axis=-1, keepdims=True) + EPS) * on_ref[...]
                    z = z_ref[rws, h * HD_D:(h + 1) * HD_D].astype(F32)
                    y_ref[rws, h * HD_D:(h + 1) * HD_D] = (on * _silu(z)).astype(y_ref.dtype)

            for cc in range(GDN_GROUP):
                step(cc)

        for gi in range(nct // GDN_GROUP):
            group(gi)
        save_tail()


def _gdn(p3, g3, gt4, cw, al_c, dt_c, al_r, dt_r, onorm, ts):
    b, s, _ = p3.shape
    w_d = 1024
    h_d = w_d // HD_D
    nct = ts // CHUNK
    nt = s // ts
    n_tiles = b * nt

    def conv_spec(sec):
        def imap(t, sec=sec):
            tc = jnp.minimum(t, n_tiles - 1)
            return (tc // nt, tc % nt, 2 + sec)
        return pl.BlockSpec((None, ts, w_d), imap)

    def prev_map(last):
        def imap(t):
            tp = jnp.maximum(t - 1, 0)
            return (tp // nt, tp % nt, last)
        return imap

    def gt_map(t):
        tp = jnp.maximum(t - 1, 0)
        return (tp // nt, 0, tp % nt, 0)

    return pl.pallas_call(
        functools.partial(_gdn_body, nt=nt),
        grid=(n_tiles + 1,),
        in_specs=[
            conv_spec(0), conv_spec(1), conv_spec(2),
            pl.BlockSpec((None, ts, w_d), prev_map(5)),
            pl.BlockSpec((None, ts, LANES), prev_map(0)),
            pl.BlockSpec((None, 2 * h_d, nct, CHUNK), gt_map),
            pl.BlockSpec((CONV_K, 3 * w_d), lambda t: (0, 0)),
            pl.BlockSpec((1, LANES), lambda t: (0, 0)),
            pl.BlockSpec((1, LANES), lambda t: (0, 0)),
            pl.BlockSpec((h_d, 1, 1), lambda t: (0, 0, 0)),
            pl.BlockSpec((h_d, 1, 1), lambda t: (0, 0, 0)),
            pl.BlockSpec((1, HD_D), lambda t: (0, 0)),
        ],
        out_specs=pl.BlockSpec((None, ts, w_d), prev_map(0)),
        out_shape=jax.ShapeDtypeStruct((b, s, w_d), BF16),
        scratch_shapes=[
            pltpu.VMEM((h_d, HD_D, HD_D), F32),
            pltpu.VMEM((16, 3 * w_d), BF16),
            pltpu.VMEM((2, ts, 3 * w_d), F32),
            pltpu.VMEM((ts, LANES), F32),
            pltpu.VMEM((ts, LANES), F32),
            pltpu.VMEM((h_d, nct, CHUNK), F32),
            pltpu.VMEM((GDN_GROUP, h_d, CHUNK + HD_D, HD_D), BF16),
            pltpu.VMEM((GDN_GROUP, h_d, CHUNK, HD_D), F32),
            pltpu.VMEM((GDN_GROUP, h_d, HD_D, HD_D), F32),
        ],
        compiler_params=_params(1),
        name="gated_delta",
    )(p3, p3, p3, p3, g3, gt4, cw, al_c, dt_c, al_r, dt_r, onorm)


def _rel_row(rel_bias):
    t = (np.arange(REL_W) + CHUNK - 1) % REL_W
    idx = np.clip(LOOKBACK * CHUNK + CHUNK - 1 - t, -REL_MAX, REL_MAX) + REL_MAX
    return rel_bias[:, idx].astype(F32)[:, None, :]


def _pad_lanes(v, n=LANES):
    return jnp.pad(v, ((0, 0), (0, n - v.shape[1])))


def _pad_rows(v, n=LANES):
    return jnp.pad(v, ((0, n - v.shape[0]), (0, 0)))


def _weight_t(w):
    return jnp.swapaxes(w, 0, 1).astype(BF16)


def _gates_t(g3, n, length):
    b, s, _ = g3.shape
    return jnp.transpose(g3[:, :, :n], (0, 2, 1)).reshape(b, n, s // length, length)


def kernel(x, ev_norm, ev_w_in, ev_if_bias, ev_qn_gain, ev_kn_gain, ev_rel_bias, ev_w_out, od_norm, od_w_in, od_conv_c_w, od_conv_c_b, od_gate_w, od_gate_b, od_lambda, od_conv_d_w, od_a_log, od_dt_bias, od_onorm, od_w_out):
    b, s, d = x.shape
    m = b * s
    half = d // 2
    tm = min(1024, m)
    ts = min(512, s)
    assert half == 1024 and m % tm == 0 and s % ts == 0 and s % (CHUNK * ATTN_GROUP) == 0 and ts % MLSTM_L == 0
    x2 = x.reshape(m, d)
    depth = ev_norm.shape[0] + od_norm.shape[0]
    for layer in range(depth):
        j = layer // 2
        if layer % 2 == 0:
            n_main = 9 * half
            w_t = _weight_t(ev_w_in[j])
            p, g = _inproj(x2, ev_norm[j].reshape(1, d), w_t, _pad_rows(w_t[n_main:]), n_main, tm, n_main // 4)
            p3, g3 = p.reshape(b, s, n_main), g.reshape(b, s, LANES)
            ya = _attn(p3, ev_qn_gain[j].reshape(1, HD_A), ev_kn_gain[j].reshape(1, HD_A), _rel_row(ev_rel_bias[j]))
            bias = ev_if_bias[j].astype(F32)
            yb = _mlstm(p3, _gates_t(g3, 2 * H_B, MLSTM_L), bias.reshape(-1, 1, 1), ts)
            x2 = _outproj(ya.reshape(m, half), yb.reshape(m, half), ev_w_out[j], x2, tm, 1024)
        else:
            n_main = 6 * half
            h_d = half // HD_D
            w_t = _weight_t(od_w_in[j])
            p, g = _inproj(x2, od_norm[j].reshape(1, d), w_t, _pad_rows(w_t[n_main:]), n_main, tm, n_main // 3)
            p3, g3 = p.reshape(b, s, n_main), g.reshape(b, s, LANES)
            yc = _rglru(p3, od_conv_c_w[j], od_conv_c_b[j].reshape(1, -1), od_gate_w[j].astype(BF16),
                        od_gate_b[j].reshape(1, -1), od_lambda[j].reshape(1, -1))
            al, dt = od_a_log[j].astype(F32), od_dt_bias[j].astype(F32)
            yd = _gdn(p3, g3, _gates_t(g3, 2 * h_d, CHUNK), od_conv_d_w[j], _pad_lanes(al.reshape(1, -1)),
                      _pad_lanes(dt.reshape(1, -1)), al.reshape(-1, 1, 1), dt.reshape(-1, 1, 1),
                      od_onorm[j].reshape(1, HD_D), ts)
            x2 = _outproj(yc.reshape(m, half), yd.reshape(m, half), od_w_out[j], x2, tm, 1024)
    return x2.reshape(b, s, d)
```

```python
import functools

import numpy as np
import jax
import jax.numpy as jnp
from jax import lax
from jax.experimental import pallas as pl
from jax.experimental.pallas import tpu as pltpu

F32 = jnp.float32
BF16 = jnp.bfloat16

CHUNK = 64
EPS = 1e-6
NEG = -1e30
CONV_K = 4
LOG2E = 1.4426950408889634
LANES = 128
HD_A = 128
LOOKBACK = 8
BAND = (LOOKBACK + 1) * CHUNK
REL_MAX = 256
REL_W = 640
ATTN_GROUP = 16
ATTN_HEADS = 2
H_B = 4
MLSTM_L = 256
N_BLK_C = 8
C_RG = 8.0
RG_COLS = 1024
HD_D = 128
GDN_GROUP = 4

V7X_VMEM_BYTES = 64 * 1024 * 1024
VMEM_LIMIT = V7X_VMEM_BYTES - 8 * 1024 * 1024


def _params(n_axes):
    return pltpu.CompilerParams(dimension_semantics=("arbitrary",) * n_axes, vmem_limit_bytes=VMEM_LIMIT)


def _mm(a, b):
    return jnp.dot(a.astype(BF16), b.astype(BF16), preferred_element_type=F32)


def _mm_nt(a, b):
    return lax.dot_general(a.astype(BF16), b.astype(BF16), (((1,), (1,)), ((), ())), preferred_element_type=F32)


def _mm_tn(a, b):
    return lax.dot_general(a.astype(BF16), b.astype(BF16), (((0,), (0,)), ((), ())), preferred_element_type=F32)


def _mm_f32(a, b):
    return jnp.dot(a, b, preferred_element_type=F32, precision=lax.Precision.HIGHEST)


def _log1p(e):
    u = 1.0 + e
    return jnp.where(u == 1.0, e, jnp.log(u) * (e / (u - 1.0)))


def _softplus(x):
    return jnp.maximum(x, 0.0) + _log1p(jnp.exp(-jnp.abs(x)))


def _sigmoid(x):
    return 1.0 / (1.0 + jnp.exp2(x * -LOG2E))


def _silu(x):
    return x * _sigmoid(x)


def _iota2(shape, axis):
    return lax.broadcasted_iota(jnp.int32, shape, axis)


def _chunk_cumsum(x, length):
    pos = _iota2(x.shape, 0) & (length - 1)
    d = 1
    while d < length:
        x = x + jnp.where(pos >= d, pltpu.roll(x, d, 0), 0.0)
        d *= 2
    return x


def _triu(n):
    r, c = _iota2((n, n), 0), _iota2((n, n), 1)
    return jnp.where(r <= c, 1.0, 0.0).astype(F32)


def _inproj_body(x_ref, g_ref, w_ref, wg_ref, o_ref, og_ref, xn_ref):
    @pl.when(pl.program_id(1) == 0)
    def _():
        x = x_ref[...]
        ms = jnp.mean(x * x, axis=-1, keepdims=True)
        xn = ((x * lax.rsqrt(ms + EPS)) * g_ref[...]).astype(BF16)
        xn_ref[...] = xn
        og_ref[...] = _mm_nt(xn, wg_ref[...])

    o_ref[...] = _mm_nt(xn_ref[...], w_ref[...]).astype(o_ref.dtype)


def _inproj(x2, g, w_t, wg_t, n, tm, tn):
    m, d = x2.shape
    return pl.pallas_call(
        _inproj_body,
        grid=(m // tm, n // tn),
        in_specs=[
            pl.BlockSpec((tm, d), lambda i, j: (i, 0)),
            pl.BlockSpec((1, d), lambda i, j: (0, 0)),
            pl.BlockSpec((tn, d), lambda i, j: (j, 0)),
            pl.BlockSpec((LANES, d), lambda i, j: (0, 0)),
        ],
        out_specs=[
            pl.BlockSpec((tm, tn), lambda i, j: (i, j)),
            pl.BlockSpec((tm, LANES), lambda i, j: (i, 0)),
        ],
        out_shape=[jax.ShapeDtypeStruct((m, n), BF16), jax.ShapeDtypeStruct((m, LANES), F32)],
        scratch_shapes=[pltpu.VMEM((tm, d), BF16)],
        compiler_params=_params(2),
        name="inproj",
    )(x2, g, w_t, wg_t)


def _outproj_body(ya_ref, yb_ref, wa_ref, wb_ref, x_ref, o_ref, wa_s, wb_s):
    @pl.when(pl.program_id(1) == 0)
    def _():
        wa_s[...] = wa_ref[...].astype(BF16)
        wb_s[...] = wb_ref[...].astype(BF16)

    acc = jnp.dot(ya_ref[...], wa_s[...], preferred_element_type=F32)
    acc = acc + jnp.dot(yb_ref[...], wb_s[...], preferred_element_type=F32)
    o_ref[...] = x_ref[...] + acc


def _outproj(ya, yb, w, x2, tm, tn):
    m, d = x2.shape
    kh = ya.shape[1]
    return pl.pallas_call(
        _outproj_body,
        grid=(d // tn, m // tm),
        in_specs=[
            pl.BlockSpec((tm, kh), lambda j, i: (i, 0)),
            pl.BlockSpec((tm, kh), lambda j, i: (i, 0)),
            pl.BlockSpec((kh, tn), lambda j, i: (0, j)),
            pl.BlockSpec((kh, tn), lambda j, i: (1, j)),
            pl.BlockSpec((tm, tn), lambda j, i: (i, j)),
        ],
        out_specs=pl.BlockSpec((tm, tn), lambda j, i: (i, j)),
        out_shape=jax.ShapeDtypeStruct((m, d), F32),
        scratch_shapes=[pltpu.VMEM((kh, tn), BF16), pltpu.VMEM((kh, tn), BF16)],
        compiler_params=_params(2),
        name="outproj",
    )(ya, yb, w, w, x2)


def _attn_body(q_ref, k_ref, v_ref, z_ref, qg_ref, kg_ref, c_ref, o_ref, qs, ks, vs, bias_s):
    s = q_ref.shape[0]
    pad = LOOKBACK * CHUNK
    col = _iota2((CHUNK, BAND), 1)
    for hh in range(ATTN_HEADS):
        hc = slice(hh * HD_A, (hh + 1) * HD_A)
        q = q_ref[:, hc].astype(F32)
        q = q * lax.rsqrt(jnp.mean(q * q, axis=-1, keepdims=True) + EPS) * (qg_ref[...] * (HD_A ** -0.5 * LOG2E))
        qs[hh] = q.astype(BF16)
        k = k_ref[:, hc].astype(F32)
        k = k * lax.rsqrt(jnp.mean(k * k, axis=-1, keepdims=True) + EPS) * kg_ref[...]
        ks[hh, 0:pad, :] = jnp.zeros((pad, HD_A), BF16)
        ks[hh, pad:pad + s, :] = k.astype(BF16)
        vs[hh, 0:pad, :] = jnp.zeros((pad, HD_A), BF16)
        vs[hh, pad:pad + s, :] = v_ref[:, hc]
        cb = jnp.broadcast_to(c_ref[hh] * LOG2E, (CHUNK, REL_W))
        bias_s[hh] = pltpu.roll(cb, 0, 1, stride=1, stride_axis=0)[:, :BAND]

        for it in range(s // (CHUNK * ATTN_GROUP)):
            gs = range(ATTN_GROUP)
            n = [it * ATTN_GROUP + g for g in gs]
            r0 = [n[g] * CHUNK for g in gs]
            sc = [_mm_nt(qs[hh, pl.ds(r0[g], CHUNK), :], ks[hh, pl.ds(r0[g], BAND), :]) + bias_s[hh] for g in gs]
            sc = [jnp.where(col >= (LOOKBACK - n[g]) * CHUNK, sc[g], NEG) if n[g] < LOOKBACK else sc[g] for g in gs]
            p = [jnp.exp2(sc[g] - jnp.max(sc[g], axis=-1, keepdims=True)) for g in gs]
            pv = [_mm(p[g], vs[hh, pl.ds(r0[g], BAND), :]) for g in gs]
            for g in gs:
                o = pv[g] * (1.0 / jnp.sum(p[g], axis=-1, keepdims=True))
                z = z_ref[pl.ds(r0[g], CHUNK), hc].astype(F32)
                o_ref[pl.ds(r0[g], CHUNK), hc] = (o * _silu(z)).astype(o_ref.dtype)


def _attn(p3, qg, kg, crel):
    b, s, _ = p3.shape
    h_a = crel.shape[0]
    hp = h_a // ATTN_HEADS
    wh = ATTN_HEADS * HD_A

    def col_spec(sec):
        return pl.BlockSpec((None, s, wh), lambda bi, h, sec=sec: (bi, 0, sec * hp + h))

    return pl.pallas_call(
        _attn_body,
        grid=(b, hp),
        in_specs=[
            col_spec(0), col_spec(1), col_spec(2), col_spec(3),
            pl.BlockSpec((1, HD_A), lambda bi, h: (0, 0)),
            pl.BlockSpec((1, HD_A), lambda bi, h: (0, 0)),
            pl.BlockSpec((ATTN_HEADS, 1, REL_W), lambda bi, h: (h, 0, 0)),
        ],
        out_specs=pl.BlockSpec((None, s, wh), lambda bi, h: (bi, 0, h)),
        out_shape=jax.ShapeDtypeStruct((b, s, h_a * HD_A), BF16),
        scratch_shapes=[
            pltpu.VMEM((ATTN_HEADS, s, HD_A), BF16),
            pltpu.VMEM((ATTN_HEADS, s + LOOKBACK * CHUNK, HD_A), BF16),
            pltpu.VMEM((ATTN_HEADS, s + LOOKBACK * CHUNK, HD_A), BF16),
            pltpu.VMEM((ATTN_HEADS, CHUNK, BAND), F32),
        ],
        compiler_params=_params(2),
        name="chunk_attn",
    )(p3, p3, p3, p3, qg, kg, crel)


def _mlstm_body(q_ref, k_ref, v_ref, o_ref, z_ref, gt_ref, br_ref, y_ref, c_st, n_st, m_st, row_b, row_i):
    ts = q_ref.shape[0]
    ln = MLSTM_L
    nct = ts // ln
    hd = q_ref.shape[1] // H_B
    scale = hd ** -0.5
    j = pl.program_id(1)
    hs = range(H_B)
    hc = [slice(h * hd, (h + 1) * hd) for h in hs]

    @pl.when(j == 0)
    def _():
        c_st[...] = jnp.zeros_like(c_st)
        n_st[...] = jnp.zeros_like(n_st)
        m_st[...] = jnp.zeros_like(m_st)
        gt = gt_ref[...] + br_ref[...]
        for h in hs:
            row_b[h] = _mm_f32(-_softplus(-gt[H_B + h]), _triu(ln)) * LOG2E
            row_i[h] = gt[h] * LOG2E

    tril = _iota2((ln, ln), 1) <= _iota2((ln, ln), 0)
    log2_scale = float(np.log2(scale))
    m = [m_st[h][0:1, 0:1] for h in hs]
    cm = [c_st[h] for h in hs]
    nm = [n_st[h][0:1, :] for h in hs]
    chs = [(c, h) for c in range(nct) for h in hs]
    rows = [slice(c * ln, (c + 1) * ln) for c in range(nct)]
    li_r = {(c, h): row_i[h, pl.ds(j * nct + c, 1), :] for c, h in chs}
    bc_r = {(c, h): row_b[h, pl.ds(j * nct + c, 1), :] for c, h in chs}
    pad = jnp.zeros((LANES - 2 * H_B, ln), F32)
    cols = [jnp.concatenate([li_r[c, h] for h in hs] + [bc_r[c, h] for h in hs] + [pad], axis=0).T
            for c in range(nct)]
    lic = {(c, h): cols[c][:, h:h + 1] for c, h in chs}
    bc = {(c, h): cols[c][:, H_B + h:H_B + h + 1] for c, h in chs}
    rl = {ch: li_r[ch] - bc_r[ch] for ch in chs}
    b_last = {ch: bc[ch][ln - 1:ln, :] for ch in chs}
    m_in, m_new = {}, {}
    for c, h in chs:
        m_in[c, h] = m[h]
        m_new[c, h] = jnp.maximum(b_last[c, h] + m[h], jnp.max(b_last[c, h] + rl[c, h], axis=-1, keepdims=True))
        m[h] = m_new[c, h]
    qn = {(c, h): q_ref[rows[c], hc[h]] for c, h in chs}
    kn = {(c, h): k_ref[rows[c], hc[h]] for c, h in chs}
    vn = {(c, h): v_ref[rows[c], hc[h]] for c, h in chs}
    qk = {ch: _mm_nt(qn[ch], kn[ch]) for ch in chs}
    dmat = {ch: jnp.where(tril, bc[ch] + rl[ch], NEG) for ch in chs}
    mt = {ch: jnp.maximum(bc[ch] + m_in[ch], jnp.max(dmat[ch], axis=-1, keepdims=True)) for ch in chs}
    cmt = {ch: mt[ch] - log2_scale for ch in chs}
    w_inter = {ch: jnp.exp2(bc[ch] + m_in[ch] - cmt[ch]) for ch in chs}
    p = {ch: jnp.exp2(dmat[ch] - cmt[ch]) * qk[ch] for ch in chs}
    pv = {ch: _mm(p[ch], vn[ch]) for ch in chs}
    ws_c = {ch: jnp.exp2(b_last[ch] - bc[ch] + lic[ch] - m_new[ch]).astype(BF16) for ch in chs}
    ws_r = {ch: jnp.exp2(b_last[ch] + rl[ch] - m_new[ch]) for ch in chs}
    upd = {ch: _mm_tn(kn[ch] * ws_c[ch], vn[ch]) for ch in chs}
    n_add = {ch: _mm(jnp.broadcast_to(ws_r[ch], (8, ln)), kn[ch])[0:1, :] for ch in chs}
    psum = {ch: jnp.sum(p[ch], axis=-1, keepdims=True) for ch in chs}
    gate = {}
    for c, h in chs:
        og = o_ref[rows[c], hc[h]].astype(F32)
        zg = z_ref[rows[c], hc[h]].astype(F32)
        gate[c, h] = zg * (1.0 / ((1.0 + jnp.exp2(og * -LOG2E)) * (1.0 + jnp.exp2(zg * -LOG2E))))
    for c in range(nct):
        qc = {h: _mm(qn[c, h], cm[h]) for h in hs}
        qn_dot = {h: _mm_nt(qn[c, h], jnp.broadcast_to(nm[h], (8, hd)))[:, 0:1] for h in hs}
        for h in hs:
            num = w_inter[c, h] * qc[h] + pv[c, h]
            den = w_inter[c, h] * qn_dot[h] + psum[c, h]
            hout = num * (1.0 / jnp.maximum(jnp.abs(den), jnp.exp2(-mt[c, h])))
            y_ref[rows[c], hc[h]] = (hout * gate[c, h]).astype(y_ref.dtype)
            w_c = jnp.exp2(b_last[c, h] + m_in[c, h] - m_new[c, h])
            cm[h] = w_c * cm[h] + upd[c, h]
            nm[h] = w_c * nm[h] + n_add[c, h]
    for h in hs:
        c_st[h] = cm[h]
        n_st[h] = jnp.broadcast_to(nm[h], n_st.shape[1:])
        m_st[h] = jnp.broadcast_to(m[h], m_st.shape[1:])


def _mlstm(p3, gt4, bias_row, ts):
    b, s, _ = p3.shape
    w_b = 1024
    hd = w_b // H_B
    ncs = s // MLSTM_L

    def col_spec(sec):
        return pl.BlockSpec((None, ts, w_b), lambda bi, j, sec=sec: (bi, j, 4 + sec))

    return pl.pallas_call(
        _mlstm_body,
        grid=(b, s // ts),
        in_specs=[
            col_spec(0), col_spec(1), col_spec(2), col_spec(3), col_spec(4),
            pl.BlockSpec((None, 2 * H_B, ncs, MLSTM_L), lambda bi, j: (bi, 0, 0, 0)),
            pl.BlockSpec((2 * H_B, 1, 1), lambda bi, j: (0, 0, 0)),
        ],
        out_specs=pl.BlockSpec((None, ts, w_b), lambda bi, j: (bi, j, 0)),
        out_shape=jax.ShapeDtypeStruct((b, s, w_b), BF16),
        scratch_shapes=[
            pltpu.VMEM((H_B, hd, hd), F32),
            pltpu.VMEM((H_B, 8, hd), F32),
            pltpu.VMEM((H_B, 8, LANES), F32),
            pltpu.VMEM((H_B, ncs, MLSTM_L), F32),
            pltpu.VMEM((H_B, ncs, MLSTM_L), F32),
        ],
        compiler_params=_params(2),
        name="mlstm",
    )(p3, p3, p3, p3, p3, gt4, bias_row)


def _rglru_body(x_ref, z_ref, cw_ref, cb_ref, gw_ref, gbr_ref, gbi_ref, lam_ref, y_ref, a_s, b_s):
    s, wc = x_ref.shape
    blk = gw_ref.shape[1]
    row8 = _iota2((8, blk), 0)
    for n in range(wc // blk):
        cs = slice(n * blk, (n + 1) * blk)
        x = x_ref[:, cs].astype(F32)
        cw = cw_ref[:, cs]
        xc = cw[CONV_K - 1:CONV_K, :] * x + cb_ref[:, cs]
        for d in range(1, CONV_K):
            xs = pltpu.roll(x, d, 0)
            xs = jnp.concatenate([jnp.where(row8 >= d, xs[0:8], 0.0), xs[8:]], axis=0)
            xc = xc + cw[CONV_K - 1 - d:CONV_K - d, :] * xs
        gates = _mm(xc, gw_ref[n])
        r = _sigmoid(gates[:, :blk] + gbr_ref[:, cs])
        i = _sigmoid(gates[:, blk:] + gbi_ref[:, cs])
        nla = r * (C_RG * _softplus(-lam_ref[:, cs]))
        a = jnp.exp2(nla * -LOG2E)
        var = jnp.tanh(nla) * (a * a + 1.0)
        bb = jnp.where(var > 0.0, var * lax.rsqrt(var), 0.0) * (i * xc)
        a_s[n] = a
        b_s[n] = bb

    seg = s // 8
    nb = wc // blk
    h = [jnp.zeros((8, blk), F32)] * nb
    p = [jnp.ones((8, blk), F32)] * nb
    for i in range(seg):
        rows = pl.ds(i, 8, stride=seg)
        for n in range(nb):
            a = a_s[n, rows, :]
            h[n] = a * h[n] + b_s[n, rows, :]
            p[n] = a * p[n]
            b_s[n, rows, :] = h[n]
            a_s[n, rows, :] = p[n]
    for n in range(nb):
        cs = slice(n * blk, (n + 1) * blk)
        start = jnp.zeros((1, blk), F32)
        for j in range(8):
            rows = slice(j * seg, (j + 1) * seg)
            hj = b_s[n, rows, :] + a_s[n, rows, :] * start
            z = z_ref[rows, cs].astype(F32)
            y_ref[rows, cs] = (hj * _silu(z)).astype(y_ref.dtype)
            start = h[n][j:j + 1, :] + p[n][j:j + 1, :] * start


def _rglru(p3, cw, cb, gw, gb, lam):
    b, s, _ = p3.shape
    w_c = cw.shape[1]
    blk = w_c // N_BLK_C
    nsp = w_c // RG_COLS
    return pl.pallas_call(
        _rglru_body,
        grid=(b, nsp),
        in_specs=[
            pl.BlockSpec((None, s, RG_COLS), lambda bi, n: (bi, 0, n)),
            pl.BlockSpec((None, s, RG_COLS), lambda bi, n: (bi, 0, nsp + n)),
            pl.BlockSpec((CONV_K, RG_COLS), lambda bi, n: (0, n)),
            pl.BlockSpec((1, RG_COLS), lambda bi, n: (0, n)),
            pl.BlockSpec((RG_COLS // blk, blk, 2 * blk), lambda bi, n: (n, 0, 0)),
            pl.BlockSpec((1, RG_COLS), lambda bi, n: (0, n)),
            pl.BlockSpec((1, RG_COLS), lambda bi, n: (0, nsp + n)),
            pl.BlockSpec((1, RG_COLS), lambda bi, n: (0, n)),
        ],
        out_specs=pl.BlockSpec((None, s, RG_COLS), lambda bi, n: (bi, 0, n)),
        out_shape=jax.ShapeDtypeStruct((b, s, w_c), BF16),
        scratch_shapes=[pltpu.VMEM((RG_COLS // blk, s, blk), F32), pltpu.VMEM((RG_COLS // blk, s, blk), F32)],
        compiler_params=_params(2),
        name="rglru",
    )(p3, p3, cw, cb, gw, gb, gb, lam)


def _unit_lower_inverse(a_list):
    r, c = _iota2((CHUNK, CHUNK), 0), _iota2((CHUNK, CHUNK), 1)
    eye = jnp.where(r == c, 1.0, 0.0).astype(F32)

    def pair_mask(sh):
        rb, cb_ = r >> sh, c >> sh
        return ((rb & 1) == 1) & (cb_ == rb - 1)

    ts = [eye - jnp.where(pair_mask(0), a, 0.0) for a in a_list]
    for sh in range(1, 6):
        mask = pair_mask(sh)
        x1 = [_mm(t, jnp.where(mask, a, 0.0)) for t, a in zip(ts, a_list)]
        x2 = [_mm(x, t) for x, t in zip(x1, ts)]
        ts = [t - x for t, x in zip(ts, x2)]
    return ts


def _gdn_body(q_ref, k_ref, v_ref, z_ref, g_ref, gt_ref, cw_ref, al_c_ref, dt_c_ref, al_r_ref, dt_r_ref, on_ref,
              y_ref, s_st, tail, qkv, col_d, col_b, row_d, lhs_s, o0_s, n_s, *, nt):
    ts = q_ref.shape[0]
    nct = ts // CHUNK
    w_d = q_ref.shape[1]
    h_d = w_d // HD_D
    scale = HD_D ** -0.5
    t = pl.program_id(0)
    srcs = (q_ref, k_ref, v_ref)
    sh_r, sh_c = _iota2((CHUNK, 16 + CHUNK), 0), _iota2((CHUNK, 16 + CHUNK), 1)
    shift = jnp.concatenate([sh_c == 16 + sh_r - d for d in range(CONV_K)], axis=0).astype(BF16)

    def conv_block(slot, r0, first, cbk):
        ref = srcs[cbk // h_d]
        ci = slice((cbk % h_d) * HD_D, (cbk % h_d + 1) * HD_D)
        cs = slice(cbk * HD_D, (cbk + 1) * HD_D)
        cur = ref[pl.ds(r0, CHUNK), ci]
        if isinstance(first, bool):
            halo = tail[:, cs] if first else ref[pl.ds(r0 - 16, 16), ci]
        else:
            prev = ref[pl.ds(pl.multiple_of(jnp.maximum(r0 - 16, 0), 16), 16), ci]
            halo = jnp.where(first, tail[:, cs], prev)
        xw = jnp.concatenate([halo, cur], axis=0)
        sh = jnp.dot(shift, xw, preferred_element_type=F32)
        acc = cw_ref[CONV_K - 1:CONV_K, cs] * sh[0:CHUNK]
        for d in range(1, CONV_K):
            acc = acc + cw_ref[CONV_K - 1 - d:CONV_K - d, cs] * sh[d * CHUNK:(d + 1) * CHUNK]
        acc = _silu(acc)
        if cbk < 2 * h_d:
            acc = acc * lax.rsqrt(jnp.sum(acc * acc, axis=-1, keepdims=True) + EPS)
        qkv[slot, pl.ds(r0, CHUNK), cs] = acc

    def conv_rows(slot, rb):
        r0 = rb * CHUNK if isinstance(rb, int) else pl.multiple_of(rb * CHUNK, CHUNK)
        for cbk in range(3 * h_d):
            conv_block(slot, r0, rb == 0, cbk)

    def save_tail():
        keep = lax.rem(t + 1, nt) != 0
        for w, ref in enumerate(srcs):
            tail[:, w * w_d:(w + 1) * w_d] = jnp.where(keep, ref[ts - 16:ts, :], 0.0)

    @pl.when(t == 0)
    def _():
        tail[...] = jnp.zeros_like(tail)

        def rows(rb, carry):
            conv_rows(0, rb)
            return carry

        lax.fori_loop(0, nct, rows, 0)
        save_tail()

    @pl.when(t > 0)
    def _():
        cur_slot = lax.rem(t, 2)
        prv_slot = 1 - cur_slot

        @pl.when(lax.rem(t - 1, nt) == 0)
        def _():
            s_st[...] = jnp.zeros_like(s_st)

        g = g_ref[...]
        col_b[...] = _sigmoid(g)
        col_d[...] = _chunk_cumsum(-jnp.exp(al_c_ref[...]) * _softplus(g + dt_c_ref[...]), CHUNK) * LOG2E
        gt = gt_ref[...]
        g_r = -jnp.exp(al_r_ref[...]) * _softplus(gt[0:h_d] + dt_r_ref[...])
        row_d[...] = _mm_f32(g_r.reshape(h_d * nct, CHUNK), _triu(CHUNK)).reshape(h_d, nct, CHUNK) * LOG2E

        r_i, c_i = _iota2((CHUNK, CHUNK), 0), _iota2((CHUNK, CHUNK), 1)
        tril = c_i <= r_i
        strict = c_i < r_i
        hs = range(h_d)

        def group(gi):
            probs = [(cc, h) for cc in range(GDN_GROUP) for h in hs]
            cidx = [gi * GDN_GROUP + cc for cc in range(GDN_GROUP)]
            rows = [pl.ds(ci * CHUNK, CHUNK) for ci in cidx]
            dc_all = [col_d[r, :] for r in rows]
            beta_all = [col_b[r, :] for r in rows]
            dc = [dc_all[cc][:, h:h + 1] for cc, h in probs]
            beta = [beta_all[cc][:, h_d + h:h_d + h + 1] for cc, h in probs]
            q = [qkv[prv_slot, rows[cc], h * HD_D:(h + 1) * HD_D] for cc, h in probs]
            k = [qkv[prv_slot, rows[cc], w_d + h * HD_D:w_d + (h + 1) * HD_D] for cc, h in probs]
            v = [qkv[prv_slot, rows[cc], 2 * w_d + h * HD_D:2 * w_d + (h + 1) * HD_D] for cc, h in probs]
            ps = range(len(probs))
            gam = [jnp.exp2(jnp.where(tril, dc[i] - row_d[probs[i][1], pl.ds(cidx[probs[i][0]], 1), :], NEG)) for i in ps]
            kb = [k[i] * beta[i] for i in ps]
            kk = [_mm_nt(kb[i], k[i]) for i in ps]
            t_inv = _unit_lower_inverse([jnp.where(strict, kk[i] * gam[i], 0.0) for i in ps])
            ed = [jnp.exp2(dc[i]) for i in ps]
            uw = [_mm(t_inv[i], jnp.concatenate([v[i] * beta[i], kb[i] * ed[i]], axis=1)) for i in ps]
            qs = [q[i] * scale for i in ps]
            qk = [_mm_nt(qs[i], k[i]) * gam[i] for i in ps]
            kd = [k[i] * jnp.exp2(dc[i][CHUNK - 1:CHUNK, :] - dc[i]) for i in ps]
            x1 = [_mm(qk[i], uw[i]) for i in ps]
            x2 = [_mm_tn(kd[i], uw[i]) for i in ps]
            for i, (cc, h) in enumerate(probs):
                lhs_s[cc, h] = jnp.concatenate([qs[i] * ed[i] - x1[i][:, HD_D:], x2[i][:, HD_D:]], axis=0).astype(BF16)
                o0_s[cc, h] = x1[i][:, :HD_D]
                n_s[cc, h] = x2[i][:, :HD_D]
            for rr in range(GDN_GROUP):
                conv_rows(cur_slot, gi * GDN_GROUP + rr)

            def step(cc):
                c = gi * GDN_GROUP + cc
                rws = pl.ds(c * CHUNK, CHUNK)
                d_last = jnp.exp2(col_d[pl.ds(c * CHUNK + CHUNK - 1, 1), :])
                st = [s_st[h] for h in hs]
                r = [_mm(lhs_s[cc, h], st[h]) for h in hs]
                for h in hs:
                    o = o0_s[cc, h] + r[h][:CHUNK]
                    s_st[h] = d_last[:, h:h + 1] * st[h] + n_s[cc, h] - r[h][CHUNK:]
                    on = o * lax.rsqrt(jnp.mean(o * o, axis=-1, keepdims=True) + EPS) * on_ref[...]
                    z = z_ref[rws, h * HD_D:(h + 1) * HD_D].astype(F32)
                    y_ref[rws, h * HD_D:(h + 1) * HD_D] = (on * _silu(z)).astype(y_ref.dtype)

            for cc in range(GDN_GROUP):
                step(cc)

        for gi in range(nct // GDN_GROUP):
            group(gi)
        save_tail()


def _gdn(p3, g3, gt4, cw, al_c, dt_c, al_r, dt_r, onorm, ts):
    b, s, _ = p3.shape
    w_d = 1024
    h_d = w_d // HD_D
    nct = ts // CHUNK
    nt = s // ts
    n_tiles = b * nt

    def conv_spec(sec):
        def imap(t, sec=sec):
            tc = jnp.minimum(t, n_tiles - 1)
            return (tc // nt, tc % nt, 2 + sec)
        return pl.BlockSpec((None, ts, w_d), imap)

    def prev_map(last):
        def imap(t):
            tp = jnp.maximum(t - 1, 0)
            return (tp // nt, tp % nt, last)
        return imap

    def gt_map(t):
        tp = jnp.maximum(t - 1, 0)
        return (tp // nt, 0, tp % nt, 0)

    return pl.pallas_call(
        functools.partial(_gdn_body, nt=nt),
        grid=(n_tiles + 1,),
        in_specs=[
            conv_spec(0), conv_spec(1), conv_spec(2),
            pl.BlockSpec((None, ts, w_d), prev_map(5)),
            pl.BlockSpec((None, ts, LANES), prev_map(0)),
            pl.BlockSpec((None, 2 * h_d, nct, CHUNK), gt_map),
            pl.BlockSpec((CONV_K, 3 * w_d), lambda t: (0, 0)),
            pl.BlockSpec((1, LANES), lambda t: (0, 0)),
            pl.BlockSpec((1, LANES), lambda t: (0, 0)),
            pl.BlockSpec((h_d, 1, 1), lambda t: (0, 0, 0)),
            pl.BlockSpec((h_d, 1, 1), lambda t: (0, 0, 0)),
            pl.BlockSpec((1, HD_D), lambda t: (0, 0)),
        ],
        out_specs=pl.BlockSpec((None, ts, w_d), prev_map(0)),
        out_shape=jax.ShapeDtypeStruct((b, s, w_d), BF16),
        scratch_shapes=[
            pltpu.VMEM((h_d, HD_D, HD_D), F32),
            pltpu.VMEM((16, 3 * w_d), BF16),
            pltpu.VMEM((2, ts, 3 * w_d), F32),
            pltpu.VMEM((ts, LANES), F32),
            pltpu.VMEM((ts, LANES), F32),
            pltpu.VMEM((h_d, nct, CHUNK), F32),
            pltpu.VMEM((GDN_GROUP, h_d, CHUNK + HD_D, HD_D), BF16),
            pltpu.VMEM((GDN_GROUP, h_d, CHUNK, HD_D), F32),
            pltpu.VMEM((GDN_GROUP, h_d, HD_D, HD_D), F32),
        ],
        compiler_params=_params(1),
        name="gated_delta",
    )(p3, p3, p3, p3, g3, gt4, cw, al_c, dt_c, al_r, dt_r, onorm)


def _rel_row(rel_bias):
    t = (np.arange(REL_W) + CHUNK - 1) % REL_W
    idx = np.clip(LOOKBACK * CHUNK + CHUNK - 1 - t, -REL_MAX, REL_MAX) + REL_MAX
    return rel_bias[:, idx].astype(F32)[:, None, :]


def _pad_lanes(v, n=LANES):
    return jnp.pad(v, ((0, 0), (0, n - v.shape[1])))


def _pad_rows(v, n=LANES):
    return jnp.pad(v, ((0, n - v.shape[0]), (0, 0)))


def _weight_t(w):
    return jnp.swapaxes(w, 0, 1).astype(BF16)


def _gates_t(g3, n, length):
    b, s, _ = g3.shape
    return jnp.transpose(g3[:, :, :n], (0, 2, 1)).reshape(b, n, s // length, length)


def kernel(x, ev_norm, ev_w_in, ev_if_bias, ev_qn_gain, ev_kn_gain, ev_rel_bias, ev_w_out, od_norm, od_w_in, od_conv_c_w, od_conv_c_b, od_gate_w, od_gate_b, od_lambda, od_conv_d_w, od_a_log, od_dt_bias, od_onorm, od_w_out):
    b, s, d = x.shape
    m = b * s
    half = d // 2
    tm = min(1024, m)
    ts = min(512, s)
    assert half == 1024 and m % tm == 0 and s % ts == 0 and s % (CHUNK * ATTN_GROUP) == 0 and ts % MLSTM_L == 0
    x2 = x.reshape(m, d)
    depth = ev_norm.shape[0] + od_norm.shape[0]
    for layer in range(depth):
        j = layer // 2
        if layer % 2 == 0:
            n_main = 9 * half
            w_t = _weight_t(ev_w_in[j])
            p, g = _inproj(x2, ev_norm[j].reshape(1, d), w_t, _pad_rows(w_t[n_main:]), n_main, tm, n_main // 4)
            p3, g3 = p.reshape(b, s, n_main), g.reshape(b, s, LANES)
            ya = _attn(p3, ev_qn_gain[j].reshape(1, HD_A), ev_kn_gain[j].reshape(1, HD_A), _rel_row(ev_rel_bias[j]))
            bias = ev_if_bias[j].astype(F32)
            yb = _mlstm(p3, _gates_t(g3, 2 * H_B, MLSTM_L), bias.reshape(-1, 1, 1), ts)
            x2 = _outproj(ya.reshape(m, half), yb.reshape(m, half), ev_w_out[j], x2, tm, 1024)
        else:
            n_main = 6 * half
            h_d = half // HD_D
            w_t = _weight_t(od_w_in[j])
            p, g = _inproj(x2, od_norm[j].reshape(1, d), w_t, _pad_rows(w_t[n_main:]), n_main, tm, n_main // 3)
            p3, g3 = p.reshape(b, s, n_main), g.reshape(b, s, LANES)
            yc = _rglru(p3, od_conv_c_w[j], od_conv_c_b[j].reshape(1, -1), od_gate_w[j].astype(BF16),
                        od_gate_b[j].reshape(1, -1), od_lambda[j].reshape(1, -1))
            al, dt = od_a_log[j].astype(F32), od_dt_bias[j].astype(F32)
            yd = _gdn(p3, g3, _gates_t(g3, 2 * h_d, CHUNK), od_conv_d_w[j], _pad_lanes(al.reshape(1, -1)),
                      _pad_lanes(dt.reshape(1, -1)), al.reshape(-1, 1, 1), dt.reshape(-1, 1, 1),
                      od_onorm[j].reshape(1, HD_D), ts)
            x2 = _outproj(yc.reshape(m, half), yd.reshape(m, half), od_w_out[j], x2, tm, 1024)
    return x2.reshape(b, s, d)
```

```python
import functools

import numpy as np
import jax
import jax.numpy as jnp
from jax import lax
from jax.experimental import pallas as pl
from jax.experimental.pallas import tpu as pltpu

F32 = jnp.float32
BF16 = jnp.bfloat16

CHUNK = 64
EPS = 1e-6
NEG = -1e30
CONV_K = 4
LOG2E = 1.4426950408889634
LANES = 128
HD_A = 128
LOOKBACK = 8
BAND = (LOOKBACK + 1) * CHUNK
REL_MAX = 256
REL_W = 640
ATTN_GROUP = 16
ATTN_HEADS = 2
H_B = 4
MLSTM_L = 256
N_BLK_C = 8
C_RG = 8.0
RG_COLS = 1024
HD_D = 128
GDN_GROUP = 4

V7X_VMEM_BYTES = 64 * 1024 * 1024
VMEM_LIMIT = V7X_VMEM_BYTES - 8 * 1024 * 1024


def _params(n_axes):
    return pltpu.CompilerParams(dimension_semantics=("arbitrary",) * n_axes, vmem_limit_bytes=VMEM_LIMIT)


def _mm(a, b):
    return jnp.dot(a.astype(BF16), b.astype(BF16), preferred_element_type=F32)


def _mm_nt(a, b):
    return lax.dot_general(a.astype(BF16), b.astype(BF16), (((1,), (1,)), ((), ())), preferred_element_type=F32)


def _mm_tn(a, b):
    return lax.dot_general(a.astype(BF16), b.astype(BF16), (((0,), (0,)), ((), ())), preferred_element_type=F32)


def _mm_f32(a, b):
    return jnp.dot(a, b, preferred_element_type=F32, precision=lax.Precision.HIGHEST)


def _log1p(e):
    u = 1.0 + e
    return jnp.where(u == 1.0, e, jnp.log(u) * (e / (u - 1.0)))


def _softplus(x):
    return jnp.maximum(x, 0.0) + _log1p(jnp.exp(-jnp.abs(x)))


def _sigmoid(x):
    return 1.0 / (1.0 + jnp.exp2(x * -LOG2E))


def _silu(x):
    return x * _sigmoid(x)


def _iota2(shape, axis):
    return lax.broadcasted_iota(jnp.int32, shape, axis)


def _chunk_cumsum(x, length):
    pos = _iota2(x.shape, 0) & (length - 1)
    d = 1
    while d < length:
        x = x + jnp.where(pos >= d, pltpu.roll(x, d, 0), 0.0)
        d *= 2
    return x


def _triu(n):
    r, c = _iota2((n, n), 0), _iota2((n, n), 1)
    return jnp.where(r <= c, 1.0, 0.0).astype(F32)


def _conv_shift_matrix():
    r, c = _iota2((CHUNK, 16 + CHUNK), 0), _iota2((CHUNK, 16 + CHUNK), 1)
    return jnp.concatenate([c == 16 + r - d for d in range(CONV_K)], axis=0).astype(BF16)


def _inproj_body(x_ref, g_ref, w_ref, wg_ref, o_ref, og_ref, xn_ref):
    @pl.when(pl.program_id(1) == 0)
    def _():
        x = x_ref[...]
        ms = jnp.mean(x * x, axis=-1, keepdims=True)
        xn = ((x * lax.rsqrt(ms + EPS)) * g_ref[...]).astype(BF16)
        xn_ref[...] = xn
        og_ref[...] = _mm_nt(xn, wg_ref[...])

    o_ref[...] = _mm_nt(xn_ref[...], w_ref[...]).astype(o_ref.dtype)


def _inproj(x2, g, w_t, wg_t, n, tm, tn):
    m, d = x2.shape
    return pl.pallas_call(
        _inproj_body,
        grid=(m // tm, n // tn),
        in_specs=[
            pl.BlockSpec((tm, d), lambda i, j: (i, 0)),
            pl.BlockSpec((1, d), lambda i, j: (0, 0)),
            pl.BlockSpec((tn, d), lambda i, j: (j, 0)),
            pl.BlockSpec((LANES, d), lambda i, j: (0, 0)),
        ],
        out_specs=[
            pl.BlockSpec((tm, tn), lambda i, j: (i, j)),
            pl.BlockSpec((tm, LANES), lambda i, j: (i, 0)),
        ],
        out_shape=[jax.ShapeDtypeStruct((m, n), BF16), jax.ShapeDtypeStruct((m, LANES), F32)],
        scratch_shapes=[pltpu.VMEM((tm, d), BF16)],
        compiler_params=_params(2),
        name="inproj",
    )(x2, g, w_t, wg_t)


def _outproj_body(ya_ref, yb_ref, wa_ref, wb_ref, x_ref, o_ref, wa_s, wb_s):
    @pl.when(pl.program_id(1) == 0)
    def _():
        wa_s[...] = wa_ref[...].astype(BF16)
        wb_s[...] = wb_ref[...].astype(BF16)

    acc = jnp.dot(ya_ref[...], wa_s[...], preferred_element_type=F32)
    acc = acc + jnp.dot(yb_ref[...], wb_s[...], preferred_element_type=F32)
    o_ref[...] = x_ref[...] + acc


def _outproj(ya, yb, w, x2, tm, tn):
    m, d = x2.shape
    kh = ya.shape[1]
    return pl.pallas_call(
        _outproj_body,
        grid=(d // tn, m // tm),
        in_specs=[
            pl.BlockSpec((tm, kh), lambda j, i: (i, 0)),
            pl.BlockSpec((tm, kh), lambda j, i: (i, 0)),
            pl.BlockSpec((kh, tn), lambda j, i: (0, j)),
            pl.BlockSpec((kh, tn), lambda j, i: (1, j)),
            pl.BlockSpec((tm, tn), lambda j, i: (i, j)),
        ],
        out_specs=pl.BlockSpec((tm, tn), lambda j, i: (i, j)),
        out_shape=jax.ShapeDtypeStruct((m, d), F32),
        scratch_shapes=[pltpu.VMEM((kh, tn), BF16), pltpu.VMEM((kh, tn), BF16)],
        compiler_params=_params(2),
        name="outproj",
    )(ya, yb, w, w, x2)


def _attn_body(q_ref, k_ref, v_ref, z_ref, qg_ref, kg_ref, c_ref, o_ref, qs, ks, vs, bias_s):
    s = q_ref.shape[0]
    pad = LOOKBACK * CHUNK
    col = _iota2((CHUNK, BAND), 1)
    for hh in range(ATTN_HEADS):
        hc = slice(hh * HD_A, (hh + 1) * HD_A)
        q = q_ref[:, hc].astype(F32)
        q = q * lax.rsqrt(jnp.mean(q * q, axis=-1, keepdims=True) + EPS) * (qg_ref[...] * (HD_A ** -0.5 * LOG2E))
        qs[hh] = q.astype(BF16)
        k = k_ref[:, hc].astype(F32)
        k = k * lax.rsqrt(jnp.mean(k * k, axis=-1, keepdims=True) + EPS) * kg_ref[...]
        ks[hh, 0:pad, :] = jnp.zeros((pad, HD_A), BF16)
        ks[hh, pad:pad + s, :] = k.astype(BF16)
        vs[hh, 0:pad, :] = jnp.zeros((pad, HD_A), BF16)
        vs[hh, pad:pad + s, :] = v_ref[:, hc]
        cb = jnp.broadcast_to(c_ref[hh] * LOG2E, (CHUNK, REL_W))
        bias_s[hh] = pltpu.roll(cb, 0, 1, stride=1, stride_axis=0)[:, :BAND]

        for it in range(s // (CHUNK * ATTN_GROUP)):
            gs = range(ATTN_GROUP)
            n = [it * ATTN_GROUP + g for g in gs]
            r0 = [n[g] * CHUNK for g in gs]
            sc = [_mm_nt(qs[hh, pl.ds(r0[g], CHUNK), :], ks[hh, pl.ds(r0[g], BAND), :]) + bias_s[hh] for g in gs]
            sc = [jnp.where(col >= (LOOKBACK - n[g]) * CHUNK, sc[g], NEG) if n[g] < LOOKBACK else sc[g] for g in gs]
            p = [jnp.exp2(sc[g] - jnp.max(sc[g], axis=-1, keepdims=True)) for g in gs]
            pv = [_mm(p[g], vs[hh, pl.ds(r0[g], BAND), :]) for g in gs]
            for g in gs:
                o = pv[g] * (1.0 / jnp.sum(p[g], axis=-1, keepdims=True))
                z = z_ref[pl.ds(r0[g], CHUNK), hc].astype(F32)
                o_ref[pl.ds(r0[g], CHUNK), hc] = (o * _silu(z)).astype(o_ref.dtype)


def _attn(p3, qg, kg, crel):
    b, s, _ = p3.shape
    h_a = crel.shape[0]
    hp = h_a // ATTN_HEADS
    wh = ATTN_HEADS * HD_A

    def col_spec(sec):
        return pl.BlockSpec((None, s, wh), lambda bi, h, sec=sec: (bi, 0, sec * hp + h))

    return pl.pallas_call(
        _attn_body,
        grid=(b, hp),
        in_specs=[
            col_spec(0), col_spec(1), col_spec(2), col_spec(3),
            pl.BlockSpec((1, HD_A), lambda bi, h: (0, 0)),
            pl.BlockSpec((1, HD_A), lambda bi, h: (0, 0)),
            pl.BlockSpec((ATTN_HEADS, 1, REL_W), lambda bi, h: (h, 0, 0)),
        ],
        out_specs=pl.BlockSpec((None, s, wh), lambda bi, h: (bi, 0, h)),
        out_shape=jax.ShapeDtypeStruct((b, s, h_a * HD_A), BF16),
        scratch_shapes=[
            pltpu.VMEM((ATTN_HEADS, s, HD_A), BF16),
            pltpu.VMEM((ATTN_HEADS, s + LOOKBACK * CHUNK, HD_A), BF16),
            pltpu.VMEM((ATTN_HEADS, s + LOOKBACK * CHUNK, HD_A), BF16),
            pltpu.VMEM((ATTN_HEADS, CHUNK, BAND), F32),
        ],
        compiler_params=_params(2),
        name="chunk_attn",
    )(p3, p3, p3, p3, qg, kg, crel)


def _mlstm_body(q_ref, k_ref, v_ref, o_ref, z_ref, gt_ref, br_ref, y_ref, c_st, n_st, m_st, row_b, row_i):
    ts = q_ref.shape[0]
    ln = MLSTM_L
    nct = ts // ln
    hd = q_ref.shape[1] // H_B
    scale = hd ** -0.5
    j = pl.program_id(1)
    hs = range(H_B)
    hc = [slice(h * hd, (h + 1) * hd) for h in hs]

    @pl.when(j == 0)
    def _():
        c_st[...] = jnp.zeros_like(c_st)
        n_st[...] = jnp.zeros_like(n_st)
        m_st[...] = jnp.zeros_like(m_st)
        gt = gt_ref[...] + br_ref[...]
        for h in hs:
            row_b[h] = _mm_f32(-_softplus(-gt[H_B + h]), _triu(ln)) * LOG2E
            row_i[h] = gt[h] * LOG2E

    tril = _iota2((ln, ln), 1) <= _iota2((ln, ln), 0)
    log2_scale = float(np.log2(scale))
    m = [m_st[h][0:1, 0:1] for h in hs]
    cm = [c_st[h] for h in hs]
    nm = [n_st[h][0:1, :] for h in hs]
    chs = [(c, h) for c in range(nct) for h in hs]
    rows = [slice(c * ln, (c + 1) * ln) for c in range(nct)]
    li_r = {(c, h): row_i[h, pl.ds(j * nct + c, 1), :] for c, h in chs}
    bc_r = {(c, h): row_b[h, pl.ds(j * nct + c, 1), :] for c, h in chs}
    pad = jnp.zeros((LANES - 2 * H_B, ln), F32)
    cols = [jnp.concatenate([li_r[c, h] for h in hs] + [bc_r[c, h] for h in hs] + [pad], axis=0).T
            for c in range(nct)]
    lic = {(c, h): cols[c][:, h:h + 1] for c, h in chs}
    bc = {(c, h): cols[c][:, H_B + h:H_B + h + 1] for c, h in chs}
    rl = {ch: li_r[ch] - bc_r[ch] for ch in chs}
    b_last = {ch: bc[ch][ln - 1:ln, :] for ch in chs}
    m_in, m_new = {}, {}
    for c, h in chs:
        m_in[c, h] = m[h]
        m_new[c, h] = jnp.maximum(b_last[c, h] + m[h], jnp.max(b_last[c, h] + rl[c, h], axis=-1, keepdims=True))
        m[h] = m_new[c, h]
    qn = {(c, h): q_ref[rows[c], hc[h]] for c, h in chs}
    kn = {(c, h): k_ref[rows[c], hc[h]] for c, h in chs}
    vn = {(c, h): v_ref[rows[c], hc[h]] for c, h in chs}
    qk = {ch: _mm_nt(qn[ch], kn[ch]) for ch in chs}
    dmat = {ch: jnp.where(tril, bc[ch] + rl[ch], NEG) for ch in chs}
    mt = {ch: jnp.maximum(bc[ch] + m_in[ch], jnp.max(dmat[ch], axis=-1, keepdims=True)) for ch in chs}
    cmt = {ch: mt[ch] - log2_scale for ch in chs}
    w_inter = {ch: jnp.exp2(bc[ch] + m_in[ch] - cmt[ch]) for ch in chs}
    p = {ch: jnp.exp2(dmat[ch] - cmt[ch]) * qk[ch] for ch in chs}
    pv = {ch: _mm(p[ch], vn[ch]) for ch in chs}
    ws_c = {ch: jnp.exp2(b_last[ch] - bc[ch] + lic[ch] - m_new[ch]).astype(BF16) for ch in chs}
    ws_r = {ch: jnp.exp2(b_last[ch] + rl[ch] - m_new[ch]) for ch in chs}
    upd = {ch: _mm_tn(kn[ch] * ws_c[ch], vn[ch]) for ch in chs}
    n_add = {ch: _mm(jnp.broadcast_to(ws_r[ch], (8, ln)), kn[ch])[0:1, :] for ch in chs}
    psum = {ch: jnp.sum(p[ch], axis=-1, keepdims=True) for ch in chs}
    gate = {}
    for c, h in chs:
        og = o_ref[rows[c], hc[h]].astype(F32)
        zg = z_ref[rows[c], hc[h]].astype(F32)
        gate[c, h] = zg * (1.0 / ((1.0 + jnp.exp2(og * -LOG2E)) * (1.0 + jnp.exp2(zg * -LOG2E))))
    for c in range(nct):
        qc = {h: _mm(qn[c, h], cm[h]) for h in hs}
        qn_dot = {h: _mm_nt(qn[c, h], jnp.broadcast_to(nm[h], (8, hd)))[:, 0:1] for h in hs}
        for h in hs:
            num = w_inter[c, h] * qc[h] + pv[c, h]
            den = w_inter[c, h] * qn_dot[h] + psum[c, h]
            hout = num * (1.0 / jnp.maximum(jnp.abs(den), jnp.exp2(-mt[c, h])))
            y_ref[rows[c], hc[h]] = (hout * gate[c, h]).astype(y_ref.dtype)
            w_c = jnp.exp2(b_last[c, h] + m_in[c, h] - m_new[c, h])
            cm[h] = w_c * cm[h] + upd[c, h]
            nm[h] = w_c * nm[h] + n_add[c, h]
    for h in hs:
        c_st[h] = cm[h]
        n_st[h] = jnp.broadcast_to(nm[h], n_st.shape[1:])
        m_st[h] = jnp.broadcast_to(m[h], m_st.shape[1:])


def _mlstm(p3, gt4, bias_row, ts):
    b, s, _ = p3.shape
    w_b = 1024
    hd = w_b // H_B
    ncs = s // MLSTM_L

    def col_spec(sec):
        return pl.BlockSpec((None, ts, w_b), lambda bi, j, sec=sec: (bi, j, 4 + sec))

    return pl.pallas_call(
        _mlstm_body,
        grid=(b, s // ts),
        in_specs=[
            col_spec(0), col_spec(1), col_spec(2), col_spec(3), col_spec(4),
            pl.BlockSpec((None, 2 * H_B, ncs, MLSTM_L), lambda bi, j: (bi, 0, 0, 0)),
            pl.BlockSpec((2 * H_B, 1, 1), lambda bi, j: (0, 0, 0)),
        ],
        out_specs=pl.BlockSpec((None, ts, w_b), lambda bi, j: (bi, j, 0)),
        out_shape=jax.ShapeDtypeStruct((b, s, w_b), BF16),
        scratch_shapes=[
            pltpu.VMEM((H_B, hd, hd), F32),
            pltpu.VMEM((H_B, 8, hd), F32),
            pltpu.VMEM((H_B, 8, LANES), F32),
            pltpu.VMEM((H_B, ncs, MLSTM_L), F32),
            pltpu.VMEM((H_B, ncs, MLSTM_L), F32),
        ],
        compiler_params=_params(2),
        name="mlstm",
    )(p3, p3, p3, p3, p3, gt4, bias_row)


def _rglru_body(x_ref, z_ref, cw_ref, cb_ref, gw_ref, gbr_ref, gbi_ref, lam_ref, y_ref, a_s, b_s):
    s, wc = x_ref.shape
    blk = gw_ref.shape[1]
    shift = _conv_shift_matrix()
    sub = _iota2((s // 8, 8, blk), 1)
    for n in range(wc // blk):
        cs = slice(n * blk, (n + 1) * blk)
        cw = cw_ref[:, cs]
        parts = []
        for c in range(s // CHUNK):
            if c == 0:
                xw = jnp.concatenate([jnp.zeros((16, blk), BF16), x_ref[0:CHUNK, cs]], axis=0)
            else:
                xw = x_ref[c * CHUNK - 16:(c + 1) * CHUNK, cs]
            sh = jnp.dot(shift, xw, preferred_element_type=F32)
            acc = cw[CONV_K - 1:CONV_K, :] * sh[0:CHUNK] + cb_ref[:, cs]
            for d in range(1, CONV_K):
                acc = acc + cw[CONV_K - 1 - d:CONV_K - d, :] * sh[d * CHUNK:(d + 1) * CHUNK]
            parts.append(acc)
        xc = jnp.concatenate(parts, axis=0)
        gates = _mm(xc, gw_ref[n])
        r = _sigmoid(gates[:, :blk] + gbr_ref[:, cs])
        i = _sigmoid(gates[:, blk:] + gbi_ref[:, cs])
        nla = r * (C_RG * _softplus(-lam_ref[:, cs]))
        a = jnp.exp2(nla * -LOG2E)
        var = jnp.tanh(nla) * (a * a + 1.0)
        bb = jnp.where(var > 0.0, var * lax.rsqrt(var), 0.0) * (i * xc)
        a = a.reshape(s // 8, 8, blk)
        bb = bb.reshape(s // 8, 8, blk)
        for d in (1, 2, 4):
            keep = sub >= d
            a_sh = jnp.where(keep, pltpu.roll(a, d, 1), 1.0)
            b_sh = jnp.where(keep, pltpu.roll(bb, d, 1), 0.0)
            bb = a * b_sh + bb
            a = a * a_sh
        a_s[:, cs] = a.reshape(s, blk)
        b_s[:, cs] = bb.reshape(s, blk)

    def group(gi, carry):
        rows = pl.ds(pl.multiple_of(gi * 8, 8), 8)
        h = a_s[rows, :] * carry + b_s[rows, :]
        b_s[rows, :] = h
        return jnp.broadcast_to(h[7:8, :], h.shape)

    lax.fori_loop(0, s // 8, group, jnp.zeros((8, wc), F32), unroll=8)
    z = z_ref[...].astype(F32)
    y_ref[...] = (b_s[...] * _silu(z)).astype(y_ref.dtype)


def _rglru(p3, cw, cb, gw, gb, lam):
    b, s, _ = p3.shape
    w_c = cw.shape[1]
    blk = w_c // N_BLK_C
    nsp = w_c // RG_COLS
    return pl.pallas_call(
        _rglru_body,
        grid=(b, nsp),
        in_specs=[
            pl.BlockSpec((None, s, RG_COLS), lambda bi, n: (bi, 0, n)),
            pl.BlockSpec((None, s, RG_COLS), lambda bi, n: (bi, 0, nsp + n)),
            pl.BlockSpec((CONV_K, RG_COLS), lambda bi, n: (0, n)),
            pl.BlockSpec((1, RG_COLS), lambda bi, n: (0, n)),
            pl.BlockSpec((RG_COLS // blk, blk, 2 * blk), lambda bi, n: (n, 0, 0)),
            pl.BlockSpec((1, RG_COLS), lambda bi, n: (0, n)),
            pl.BlockSpec((1, RG_COLS), lambda bi, n: (0, nsp + n)),
            pl.BlockSpec((1, RG_COLS), lambda bi, n: (0, n)),
        ],
        out_specs=pl.BlockSpec((None, s, RG_COLS), lambda bi, n: (bi, 0, n)),
        out_shape=jax.ShapeDtypeStruct((b, s, w_c), BF16),
        scratch_shapes=[pltpu.VMEM((s, RG_COLS), F32), pltpu.VMEM((s, RG_COLS), F32)],
        compiler_params=_params(2),
        name="rglru",
    )(p3, p3, cw, cb, gw, gb, gb, lam)


def _unit_lower_inverse(a_list):
    r, c = _iota2((CHUNK, CHUNK), 0), _iota2((CHUNK, CHUNK), 1)
    eye = jnp.where(r == c, 1.0, 0.0).astype(F32)

    def pair_mask(sh):
        rb, cb_ = r >> sh, c >> sh
        return ((rb & 1) == 1) & (cb_ == rb - 1)

    ts = [eye - jnp.where(pair_mask(0), a, 0.0) for a in a_list]
    for sh in range(1, 6):
        mask = pair_mask(sh)
        x1 = [_mm(t, jnp.where(mask, a, 0.0)) for t, a in zip(ts, a_list)]
        x2 = [_mm(x, t) for x, t in zip(x1, ts)]
        ts = [t - x for t, x in zip(ts, x2)]
    return ts


def _gdn_body(q_ref, k_ref, v_ref, z_ref, g_ref, gt_ref, cw_ref, al_c_ref, dt_c_ref, al_r_ref, dt_r_ref, on_ref,
              y_ref, s_st, tail, qkv, col_d, col_b, row_d, lhs_s, o0_s, n_s, *, nt):
    ts = q_ref.shape[0]
    nct = ts // CHUNK
    w_d = q_ref.shape[1]
    h_d = w_d // HD_D
    scale = HD_D ** -0.5
    t = pl.program_id(0)
    srcs = (q_ref, k_ref, v_ref)
    shift = _conv_shift_matrix()

    def conv_block(slot, r0, first, cbk):
        ref = srcs[cbk // h_d]
        ci = slice((cbk % h_d) * HD_D, (cbk % h_d + 1) * HD_D)
        cs = slice(cbk * HD_D, (cbk + 1) * HD_D)
        cur = ref[pl.ds(r0, CHUNK), ci]
        if isinstance(first, bool):
            halo = tail[:, cs] if first else ref[pl.ds(r0 - 16, 16), ci]
        else:
            prev = ref[pl.ds(pl.multiple_of(jnp.maximum(r0 - 16, 0), 16), 16), ci]
            halo = jnp.where(first, tail[:, cs], prev)
        xw = jnp.concatenate([halo, cur], axis=0)
        sh = jnp.dot(shift, xw, preferred_element_type=F32)
        acc = cw_ref[CONV_K - 1:CONV_K, cs] * sh[0:CHUNK]
        for d in range(1, CONV_K):
            acc = acc + cw_ref[CONV_K - 1 - d:CONV_K - d, cs] * sh[d * CHUNK:(d + 1) * CHUNK]
        acc = _silu(acc)
        if cbk < 2 * h_d:
            acc = acc * lax.rsqrt(jnp.sum(acc * acc, axis=-1, keepdims=True) + EPS)
        qkv[slot, pl.ds(r0, CHUNK), cs] = acc

    def conv_rows(slot, rb):
        r0 = rb * CHUNK if isinstance(rb, int) else pl.multiple_of(rb * CHUNK, CHUNK)
        for cbk in range(3 * h_d):
            conv_block(slot, r0, rb == 0, cbk)

    def save_tail():
        keep = lax.rem(t + 1, nt) != 0
        for w, ref in enumerate(srcs):
            tail[:, w * w_d:(w + 1) * w_d] = jnp.where(keep, ref[ts - 16:ts, :], 0.0)

    @pl.when(t == 0)
    def _():
        tail[...] = jnp.zeros_like(tail)

        def rows(rb, carry):
            conv_rows(0, rb)
            return carry

        lax.fori_loop(0, nct, rows, 0)
        save_tail()

    @pl.when(t > 0)
    def _():
        cur_slot = lax.rem(t, 2)
        prv_slot = 1 - cur_slot

        @pl.when(lax.rem(t - 1, nt) == 0)
        def _():
            s_st[...] = jnp.zeros_like(s_st)

        g = g_ref[...]
        col_b[...] = _sigmoid(g)
        col_d[...] = _chunk_cumsum(-jnp.exp(al_c_ref[...]) * _softplus(g + dt_c_ref[...]), CHUNK) * LOG2E
        gt = gt_ref[...]
        g_r = -jnp.exp(al_r_ref[...]) * _softplus(gt[0:h_d] + dt_r_ref[...])
        row_d[...] = _mm_f32(g_r.reshape(h_d * nct, CHUNK), _triu(CHUNK)).reshape(h_d, nct, CHUNK) * LOG2E

        r_i, c_i = _iota2((CHUNK, CHUNK), 0), _iota2((CHUNK, CHUNK), 1)
        tril = c_i <= r_i
        strict = c_i < r_i
        hs = range(h_d)

        def group(gi):
            probs = [(cc, h) for cc in range(GDN_GROUP) for h in hs]
            cidx = [gi * GDN_GROUP + cc for cc in range(GDN_GROUP)]
            rows = [pl.ds(ci * CHUNK, CHUNK) for ci in cidx]
            dc_all = [col_d[r, :] for r in rows]
            beta_all = [col_b[r, :] for r in rows]
            dc = [dc_all[cc][:, h:h + 1] for cc, h in probs]
            beta = [beta_all[cc][:, h_d + h:h_d + h + 1] for cc, h in probs]
            q = [qkv[prv_slot, rows[cc], h * HD_D:(h + 1) * HD_D] for cc, h in probs]
            k = [qkv[prv_slot, rows[cc], w_d + h * HD_D:w_d + (h + 1) * HD_D] for cc, h in probs]
            v = [qkv[prv_slot, rows[cc], 2 * w_d + h * HD_D:2 * w_d + (h + 1) * HD_D] for cc, h in probs]
            ps = range(len(probs))
            gam = [jnp.exp2(jnp.where(tril, dc[i] - row_d[probs[i][1], pl.ds(cidx[probs[i][0]], 1), :], NEG)) for i in ps]
            kb = [k[i] * beta[i] for i in ps]
            kk = [_mm_nt(kb[i], k[i]) for i in ps]
            t_inv = _unit_lower_inverse([jnp.where(strict, kk[i] * gam[i], 0.0) for i in ps])
            ed = [jnp.exp2(dc[i]) for i in ps]
            uw = [_mm(t_inv[i], jnp.concatenate([v[i] * beta[i], kb[i] * ed[i]], axis=1)) for i in ps]
            qs = [q[i] * scale for i in ps]
            qk = [_mm_nt(qs[i], k[i]) * gam[i] for i in ps]
            kd = [k[i] * jnp.exp2(dc[i][CHUNK - 1:CHUNK, :] - dc[i]) for i in ps]
            x1 = [_mm(qk[i], uw[i]) for i in ps]
            x2 = [_mm_tn(kd[i], uw[i]) for i in ps]
            for i, (cc, h) in enumerate(probs):
                lhs_s[cc, h] = jnp.concatenate([qs[i] * ed[i] - x1[i][:, HD_D:], x2[i][:, HD_D:]], axis=0).astype(BF16)
                o0_s[cc, h] = x1[i][:, :HD_D]
                n_s[cc, h] = x2[i][:, :HD_D]
            for rr in range(GDN_GROUP):
                conv_rows(cur_slot, gi * GDN_GROUP + rr)

            def step(cc):
                c = gi * GDN_GROUP + cc
                rws = pl.ds(c * CHUNK, CHUNK)
                d_last = jnp.exp2(col_d[pl.ds(c * CHUNK + CHUNK - 1, 1), :])
                st = [s_st[h] for h in hs]
                r = [_mm(lhs_s[cc, h], st[h]) for h in hs]
                for h in hs:
                    o = o0_s[cc, h] + r[h][:CHUNK]
                    s_st[h] = d_last[:, h:h + 1] * st[h] + n_s[cc, h] - r[h][CHUNK:]
                    on = o * lax.rsqrt(jnp.mean(o * o, axis=-1, keepdims=True) + EPS) * on_ref[...]
                    z = z_ref[rws, h * HD_D:(h + 1) * HD_D].astype(F32)
                    y_ref[rws, h * HD_D:(h + 1) * HD_D] = (on * _silu(z)).astype(y_ref.dtype)

            for cc in range(GDN_GROUP):
                step(cc)

        for gi in range(nct // GDN_GROUP):
            group(gi)
        save_tail()


def _gdn(p3, g3, gt4, cw, al_c, dt_c, al_r, dt_r, onorm, ts):
    b, s, _ = p3.shape
    w_d = 1024
    h_d = w_d // HD_D
    nct = ts // CHUNK
    nt = s // ts
    n_tiles = b * nt

    def conv_spec(sec):
        def imap(t, sec=sec):
            tc = jnp.minimum(t, n_tiles - 1)
            return (tc // nt, tc % nt, 2 + sec)
        return pl.BlockSpec((None, ts, w_d), imap)

    def prev_map(last):
        def imap(t):
            tp = jnp.maximum(t - 1, 0)
            return (tp // nt, tp % nt, last)
        return imap

    def gt_map(t):
        tp = jnp.maximum(t - 1, 0)
        return (tp // nt, 0, tp % nt, 0)

    return pl.pallas_call(
        functools.partial(_gdn_body, nt=nt),
        grid=(n_tiles + 1,),
        in_specs=[
            conv_spec(0), conv_spec(1), conv_spec(2),
            pl.BlockSpec((None, ts, w_d), prev_map(5)),
            pl.BlockSpec((None, ts, LANES), prev_map(0)),
            pl.BlockSpec((None, 2 * h_d, nct, CHUNK), gt_map),
            pl.BlockSpec((CONV_K, 3 * w_d), lambda t: (0, 0)),
            pl.BlockSpec((1, LANES), lambda t: (0, 0)),
            pl.BlockSpec((1, LANES), lambda t: (0, 0)),
            pl.BlockSpec((h_d, 1, 1), lambda t: (0, 0, 0)),
            pl.BlockSpec((h_d, 1, 1), lambda t: (0, 0, 0)),
            pl.BlockSpec((1, HD_D), lambda t: (0, 0)),
        ],
        out_specs=pl.BlockSpec((None, ts, w_d), prev_map(0)),
        out_shape=jax.ShapeDtypeStruct((b, s, w_d), BF16),
        scratch_shapes=[
            pltpu.VMEM((h_d, HD_D, HD_D), F32),
            pltpu.VMEM((16, 3 * w_d), BF16),
            pltpu.VMEM((2, ts, 3 * w_d), F32),
            pltpu.VMEM((ts, LANES), F32),
            pltpu.VMEM((ts, LANES), F32),
            pltpu.VMEM((h_d, nct, CHUNK), F32),
            pltpu.VMEM((GDN_GROUP, h_d, CHUNK + HD_D, HD_D), BF16),
            pltpu.VMEM((GDN_GROUP, h_d, CHUNK, HD_D), F32),
            pltpu.VMEM((GDN_GROUP, h_d, HD_D, HD_D), F32),
        ],
        compiler_params=_params(1),
        name="gated_delta",
    )(p3, p3, p3, p3, g3, gt4, cw, al_c, dt_c, al_r, dt_r, onorm)


def _rel_row(rel_bias):
    t = (np.arange(REL_W) + CHUNK - 1) % REL_W
    idx = np.clip(LOOKBACK * CHUNK + CHUNK - 1 - t, -REL_MAX, REL_MAX) + REL_MAX
    return rel_bias[:, idx].astype(F32)[:, None, :]


def _pad_lanes(v, n=LANES):
    return jnp.pad(v, ((0, 0), (0, n - v.shape[1])))


def _pad_rows(v, n=LANES):
    return jnp.pad(v, ((0, n - v.shape[0]), (0, 0)))


def _weight_t(w):
    return jnp.swapaxes(w, 0, 1).astype(BF16)


def _gates_t(g3, n, length):
    b, s, _ = g3.shape
    return jnp.transpose(g3[:, :, :n], (0, 2, 1)).reshape(b, n, s // length, length)


def kernel(x, ev_norm, ev_w_in, ev_if_bias, ev_qn_gain, ev_kn_gain, ev_rel_bias, ev_w_out, od_norm, od_w_in, od_conv_c_w, od_conv_c_b, od_gate_w, od_gate_b, od_lambda, od_conv_d_w, od_a_log, od_dt_bias, od_onorm, od_w_out):
    b, s, d = x.shape
    m = b * s
    half = d // 2
    tm = min(1024, m)
    ts = min(512, s)
    assert half == 1024 and m % tm == 0 and s % ts == 0 and s % (CHUNK * ATTN_GROUP) == 0 and ts % MLSTM_L == 0
    x2 = x.reshape(m, d)
    depth = ev_norm.shape[0] + od_norm.shape[0]
    for layer in range(depth):
        j = layer // 2
        if layer % 2 == 0:
            n_main = 9 * half
            w_t = _weight_t(ev_w_in[j])
            p, g = _inproj(x2, ev_norm[j].reshape(1, d), w_t, _pad_rows(w_t[n_main:]), n_main, tm, n_main // 4)
            p3, g3 = p.reshape(b, s, n_main), g.reshape(b, s, LANES)
            ya = _attn(p3, ev_qn_gain[j].reshape(1, HD_A), ev_kn_gain[j].reshape(1, HD_A), _rel_row(ev_rel_bias[j]))
            bias = ev_if_bias[j].astype(F32)
            yb = _mlstm(p3, _gates_t(g3, 2 * H_B, MLSTM_L), bias.reshape(-1, 1, 1), ts)
            x2 = _outproj(ya.reshape(m, half), yb.reshape(m, half), ev_w_out[j], x2, tm, 1024)
        else:
            n_main = 6 * half
            h_d = half // HD_D
            w_t = _weight_t(od_w_in[j])
            p, g = _inproj(x2, od_norm[j].reshape(1, d), w_t, _pad_rows(w_t[n_main:]), n_main, tm, n_main // 3)
            p3, g3 = p.reshape(b, s, n_main), g.reshape(b, s, LANES)
            yc = _rglru(p3, od_conv_c_w[j], od_conv_c_b[j].reshape(1, -1), od_gate_w[j].astype(BF16),
                        od_gate_b[j].reshape(1, -1), od_lambda[j].reshape(1, -1))
            al, dt = od_a_log[j].astype(F32), od_dt_bias[j].astype(F32)
            yd = _gdn(p3, g3, _gates_t(g3, 2 * h_d, CHUNK), od_conv_d_w[j], _pad_lanes(al.reshape(1, -1)),
                      _pad_lanes(dt.reshape(1, -1)), al.reshape(-1, 1, 1), dt.reshape(-1, 1, 1),
                      od_onorm[j].reshape(1, HD_D), ts)
            x2 = _outproj(yc.reshape(m, half), yd.reshape(m, half), od_w_out[j], x2, tm, 1024)
    return x2.reshape(b, s, d)
```

```python
import functools

import numpy as np
import jax
import jax.numpy as jnp
from jax import lax
from jax.experimental import pallas as pl
from jax.experimental.pallas import tpu as pltpu

F32 = jnp.float32
BF16 = jnp.bfloat16

CHUNK = 64
EPS = 1e-6
NEG = -1e30
CONV_K = 4
LOG2E = 1.4426950408889634
LANES = 128
HD_A = 128
LOOKBACK = 8
BAND = (LOOKBACK + 1) * CHUNK
REL_MAX = 256
REL_W = 640
ATTN_GROUP = 16
ATTN_HEADS = 2
H_B = 4
MLSTM_L = 256
N_BLK_C = 8
C_RG = 8.0
RG_COLS = 1024
HD_D = 128
GDN_GROUP = 4

V7X_VMEM_BYTES = 64 * 1024 * 1024
VMEM_LIMIT = V7X_VMEM_BYTES - 8 * 1024 * 1024


def _params(n_axes):
    return pltpu.CompilerParams(dimension_semantics=("arbitrary",) * n_axes, vmem_limit_bytes=VMEM_LIMIT)


def _mm(a, b):
    return jnp.dot(a.astype(BF16), b.astype(BF16), preferred_element_type=F32)


def _mm_nt(a, b):
    return lax.dot_general(a.astype(BF16), b.astype(BF16), (((1,), (1,)), ((), ())), preferred_element_type=F32)


def _mm_tn(a, b):
    return lax.dot_general(a.astype(BF16), b.astype(BF16), (((0,), (0,)), ((), ())), preferred_element_type=F32)


def _mm_f32(a, b):
    return jnp.dot(a, b, preferred_element_type=F32, precision=lax.Precision.HIGHEST)


def _log1p(e):
    u = 1.0 + e
    return jnp.where(u == 1.0, e, jnp.log(u) * (e / (u - 1.0)))


def _softplus(x):
    return jnp.maximum(x, 0.0) + _log1p(jnp.exp(-jnp.abs(x)))


def _sigmoid(x):
    return 1.0 / (1.0 + jnp.exp2(x * -LOG2E))


def _silu(x):
    return x * _sigmoid(x)


def _iota2(shape, axis):
    return lax.broadcasted_iota(jnp.int32, shape, axis)


def _chunk_cumsum(x, length):
    pos = _iota2(x.shape, 0) & (length - 1)
    d = 1
    while d < length:
        x = x + jnp.where(pos >= d, pltpu.roll(x, d, 0), 0.0)
        d *= 2
    return x


def _triu(n):
    r, c = _iota2((n, n), 0), _iota2((n, n), 1)
    return jnp.where(r <= c, 1.0, 0.0).astype(F32)


def _inproj_body(x_ref, g_ref, w_ref, wg_ref, o_ref, og_ref, xn_ref):
    @pl.when(pl.program_id(1) == 0)
    def _():
        x = x_ref[...]
        ms = jnp.mean(x * x, axis=-1, keepdims=True)
        xn = ((x * lax.rsqrt(ms + EPS)) * g_ref[...]).astype(BF16)
        xn_ref[...] = xn
        og_ref[...] = _mm_nt(xn, wg_ref[...])

    o_ref[...] = _mm_nt(xn_ref[...], w_ref[...]).astype(o_ref.dtype)


def _inproj(x2, g, w_t, wg_t, n, tm, tn):
    m, d = x2.shape
    return pl.pallas_call(
        _inproj_body,
        grid=(m // tm, n // tn),
        in_specs=[
            pl.BlockSpec((tm, d), lambda i, j: (i, 0)),
            pl.BlockSpec((1, d), lambda i, j: (0, 0)),
            pl.BlockSpec((tn, d), lambda i, j: (j, 0)),
            pl.BlockSpec((LANES, d), lambda i, j: (0, 0)),
        ],
        out_specs=[
            pl.BlockSpec((tm, tn), lambda i, j: (i, j)),
            pl.BlockSpec((tm, LANES), lambda i, j: (i, 0)),
        ],
        out_shape=[jax.ShapeDtypeStruct((m, n), BF16), jax.ShapeDtypeStruct((m, LANES), F32)],
        scratch_shapes=[pltpu.VMEM((tm, d), BF16)],
        compiler_params=_params(2),
        name="inproj",
    )(x2, g, w_t, wg_t)


def _outproj_body(ya_ref, yb_ref, wa_ref, wb_ref, x_ref, o_ref, wa_s, wb_s):
    @pl.when(pl.program_id(1) == 0)
    def _():
        wa_s[...] = wa_ref[...].astype(BF16)
        wb_s[...] = wb_ref[...].astype(BF16)

    acc = jnp.dot(ya_ref[...], wa_s[...], preferred_element_type=F32)
    acc = acc + jnp.dot(yb_ref[...], wb_s[...], preferred_element_type=F32)
    o_ref[...] = x_ref[...] + acc


def _outproj(ya, yb, w, x2, tm, tn):
    m, d = x2.shape
    kh = ya.shape[1]
    return pl.pallas_call(
        _outproj_body,
        grid=(d // tn, m // tm),
        in_specs=[
            pl.BlockSpec((tm, kh), lambda j, i: (i, 0)),
            pl.BlockSpec((tm, kh), lambda j, i: (i, 0)),
            pl.BlockSpec((kh, tn), lambda j, i: (0, j)),
            pl.BlockSpec((kh, tn), lambda j, i: (1, j)),
            pl.BlockSpec((tm, tn), lambda j, i: (i, j)),
        ],
        out_specs=pl.BlockSpec((tm, tn), lambda j, i: (i, j)),
        out_shape=jax.ShapeDtypeStruct((m, d), F32),
        scratch_shapes=[pltpu.VMEM((kh, tn), BF16), pltpu.VMEM((kh, tn), BF16)],
        compiler_params=_params(2),
        name="outproj",
    )(ya, yb, w, w, x2)


def _attn_body(q_ref, k_ref, v_ref, z_ref, qg_ref, kg_ref, c_ref, o_ref, qs, ks, vs, bias_s):
    s = q_ref.shape[0]
    pad = LOOKBACK * CHUNK
    col = _iota2((CHUNK, BAND), 1)
    for hh in range(ATTN_HEADS):
        hc = slice(hh * HD_A, (hh + 1) * HD_A)
        q = q_ref[:, hc].astype(F32)
        q = q * lax.rsqrt(jnp.mean(q * q, axis=-1, keepdims=True) + EPS) * (qg_ref[...] * (HD_A ** -0.5 * LOG2E))
        qs[hh] = q.astype(BF16)
        k = k_ref[:, hc].astype(F32)
        k = k * lax.rsqrt(jnp.mean(k * k, axis=-1, keepdims=True) + EPS) * kg_ref[...]
        ks[hh, 0:pad, :] = jnp.zeros((pad, HD_A), BF16)
        ks[hh, pad:pad + s, :] = k.astype(BF16)
        vs[hh, 0:pad, :] = jnp.zeros((pad, HD_A), BF16)
        vs[hh, pad:pad + s, :] = v_ref[:, hc]
        cb = jnp.broadcast_to(c_ref[hh] * LOG2E, (CHUNK, REL_W))
        bias_s[hh] = pltpu.roll(cb, 0, 1, stride=1, stride_axis=0)[:, :BAND]

        for it in range(s // (CHUNK * ATTN_GROUP)):
            gs = range(ATTN_GROUP)
            n = [it * ATTN_GROUP + g for g in gs]
            r0 = [n[g] * CHUNK for g in gs]
            sc = [_mm_nt(qs[hh, pl.ds(r0[g], CHUNK), :], ks[hh, pl.ds(r0[g], BAND), :]) + bias_s[hh] for g in gs]
            sc = [jnp.where(col >= (LOOKBACK - n[g]) * CHUNK, sc[g], NEG) if n[g] < LOOKBACK else sc[g] for g in gs]
            p = [jnp.exp2(sc[g] - jnp.max(sc[g], axis=-1, keepdims=True)) for g in gs]
            pv = [_mm(p[g], vs[hh, pl.ds(r0[g], BAND), :]) for g in gs]
            for g in gs:
                o = pv[g] * (1.0 / jnp.sum(p[g], axis=-1, keepdims=True))
                z = z_ref[pl.ds(r0[g], CHUNK), hc].astype(F32)
                o_ref[pl.ds(r0[g], CHUNK), hc] = (o * _silu(z)).astype(o_ref.dtype)


def _attn(p3, qg, kg, crel):
    b, s, _ = p3.shape
    h_a = crel.shape[0]
    hp = h_a // ATTN_HEADS
    wh = ATTN_HEADS * HD_A

    def col_spec(sec):
        return pl.BlockSpec((None, s, wh), lambda bi, h, sec=sec: (bi, 0, sec * hp + h))

    return pl.pallas_call(
        _attn_body,
        grid=(b, hp),
        in_specs=[
            col_spec(0), col_spec(1), col_spec(2), col_spec(3),
            pl.BlockSpec((1, HD_A), lambda bi, h: (0, 0)),
            pl.BlockSpec((1, HD_A), lambda bi, h: (0, 0)),
            pl.BlockSpec((ATTN_HEADS, 1, REL_W), lambda bi, h: (h, 0, 0)),
        ],
        out_specs=pl.BlockSpec((None, s, wh), lambda bi, h: (bi, 0, h)),
        out_shape=jax.ShapeDtypeStruct((b, s, h_a * HD_A), BF16),
        scratch_shapes=[
            pltpu.VMEM((ATTN_HEADS, s, HD_A), BF16),
            pltpu.VMEM((ATTN_HEADS, s + LOOKBACK * CHUNK, HD_A), BF16),
            pltpu.VMEM((ATTN_HEADS, s + LOOKBACK * CHUNK, HD_A), BF16),
            pltpu.VMEM((ATTN_HEADS, CHUNK, BAND), F32),
        ],
        compiler_params=_params(2),
        name="chunk_attn",
    )(p3, p3, p3, p3, qg, kg, crel)


def _mlstm_body(q_ref, k_ref, v_ref, o_ref, z_ref, gt_ref, br_ref, y_ref, c_st, n_st, m_st, row_b, row_i):
    ts = q_ref.shape[0]
    ln = MLSTM_L
    nct = ts // ln
    hd = q_ref.shape[1] // H_B
    scale = hd ** -0.5
    j = pl.program_id(1)
    hs = range(H_B)
    hc = [slice(h * hd, (h + 1) * hd) for h in hs]

    @pl.when(j == 0)
    def _():
        c_st[...] = jnp.zeros_like(c_st)
        n_st[...] = jnp.zeros_like(n_st)
        m_st[...] = jnp.zeros_like(m_st)
        gt = gt_ref[...] + br_ref[...]
        for h in hs:
            row_b[h] = _mm_f32(-_softplus(-gt[H_B + h]), _triu(ln)) * LOG2E
            row_i[h] = gt[h] * LOG2E

    tril = _iota2((ln, ln), 1) <= _iota2((ln, ln), 0)
    log2_scale = float(np.log2(scale))
    m = [m_st[h][0:1, 0:1] for h in hs]
    cm = [c_st[h] for h in hs]
    nm = [n_st[h][0:1, :] for h in hs]
    chs = [(c, h) for c in range(nct) for h in hs]
    rows = [slice(c * ln, (c + 1) * ln) for c in range(nct)]
    li_r = {(c, h): row_i[h, pl.ds(j * nct + c, 1), :] for c, h in chs}
    bc_r = {(c, h): row_b[h, pl.ds(j * nct + c, 1), :] for c, h in chs}
    pad = jnp.zeros((LANES - 2 * H_B, ln), F32)
    cols = [jnp.concatenate([li_r[c, h] for h in hs] + [bc_r[c, h] for h in hs] + [pad], axis=0).T
            for c in range(nct)]
    lic = {(c, h): cols[c][:, h:h + 1] for c, h in chs}
    bc = {(c, h): cols[c][:, H_B + h:H_B + h + 1] for c, h in chs}
    rl = {ch: li_r[ch] - bc_r[ch] for ch in chs}
    b_last = {ch: bc[ch][ln - 1:ln, :] for ch in chs}
    m_in, m_new = {}, {}
    for c, h in chs:
        m_in[c, h] = m[h]
        m_new[c, h] = jnp.maximum(b_last[c, h] + m[h], jnp.max(b_last[c, h] + rl[c, h], axis=-1, keepdims=True))
        m[h] = m_new[c, h]
    qn = {(c, h): q_ref[rows[c], hc[h]] for c, h in chs}
    kn = {(c, h): k_ref[rows[c], hc[h]] for c, h in chs}
    vn = {(c, h): v_ref[rows[c], hc[h]] for c, h in chs}
    qk = {ch: _mm_nt(qn[ch], kn[ch]) for ch in chs}
    dmat = {ch: jnp.where(tril, bc[ch] + rl[ch], NEG) for ch in chs}
    mt = {ch: jnp.maximum(bc[ch] + m_in[ch], jnp.max(dmat[ch], axis=-1, keepdims=True)) for ch in chs}
    cmt = {ch: mt[ch] - log2_scale for ch in chs}
    w_inter = {ch: jnp.exp2(bc[ch] + m_in[ch] - cmt[ch]) for ch in chs}
    p = {ch: jnp.exp2(dmat[ch] - cmt[ch]) * qk[ch] for ch in chs}
    pv = {ch: _mm(p[ch], vn[ch]) for ch in chs}
    ws_c = {ch: jnp.exp2(b_last[ch] - bc[ch] + lic[ch] - m_new[ch]).astype(BF16) for ch in chs}
    ws_r = {ch: jnp.exp2(b_last[ch] + rl[ch] - m_new[ch]) for ch in chs}
    upd = {ch: _mm_tn(kn[ch] * ws_c[ch], vn[ch]) for ch in chs}
    n_add = {ch: _mm(jnp.broadcast_to(ws_r[ch], (8, ln)), kn[ch])[0:1, :] for ch in chs}
    psum = {ch: jnp.sum(p[ch], axis=-1, keepdims=True) for ch in chs}
    gate = {}
    for c, h in chs:
        og = o_ref[rows[c], hc[h]].astype(F32)
        zg = z_ref[rows[c], hc[h]].astype(F32)
        gate[c, h] = zg * (1.0 / ((1.0 + jnp.exp2(og * -LOG2E)) * (1.0 + jnp.exp2(zg * -LOG2E))))
    for c in range(nct):
        qc = {h: _mm(qn[c, h], cm[h]) for h in hs}
        qn_dot = {h: _mm_nt(qn[c, h], jnp.broadcast_to(nm[h], (8, hd)))[:, 0:1] for h in hs}
        for h in hs:
            num = w_inter[c, h] * qc[h] + pv[c, h]
            den = w_inter[c, h] * qn_dot[h] + psum[c, h]
            hout = num * (1.0 / jnp.maximum(jnp.abs(den), jnp.exp2(-mt[c, h])))
            y_ref[rows[c], hc[h]] = (hout * gate[c, h]).astype(y_ref.dtype)
            w_c = jnp.exp2(b_last[c, h] + m_in[c, h] - m_new[c, h])
            cm[h] = w_c * cm[h] + upd[c, h]
            nm[h] = w_c * nm[h] + n_add[c, h]
    for h in hs:
        c_st[h] = cm[h]
        n_st[h] = jnp.broadcast_to(nm[h], n_st.shape[1:])
        m_st[h] = jnp.broadcast_to(m[h], m_st.shape[1:])


def _mlstm(p3, gt4, bias_row, ts):
    b, s, _ = p3.shape
    w_b = 1024
    hd = w_b // H_B
    ncs = s // MLSTM_L

    def col_spec(sec):
        return pl.BlockSpec((None, ts, w_b), lambda bi, j, sec=sec: (bi, j, 4 + sec))

    return pl.pallas_call(
        _mlstm_body,
        grid=(b, s // ts),
        in_specs=[
            col_spec(0), col_spec(1), col_spec(2), col_spec(3), col_spec(4),
            pl.BlockSpec((None, 2 * H_B, ncs, MLSTM_L), lambda bi, j: (bi, 0, 0, 0)),
            pl.BlockSpec((2 * H_B, 1, 1), lambda bi, j: (0, 0, 0)),
        ],
        out_specs=pl.BlockSpec((None, ts, w_b), lambda bi, j: (bi, j, 0)),
        out_shape=jax.ShapeDtypeStruct((b, s, w_b), BF16),
        scratch_shapes=[
            pltpu.VMEM((H_B, hd, hd), F32),
            pltpu.VMEM((H_B, 8, hd), F32),
            pltpu.VMEM((H_B, 8, LANES), F32),
            pltpu.VMEM((H_B, ncs, MLSTM_L), F32),
            pltpu.VMEM((H_B, ncs, MLSTM_L), F32),
        ],
        compiler_params=_params(2),
        name="mlstm",
    )(p3, p3, p3, p3, p3, gt4, bias_row)


def _rglru_body(x_ref, z_ref, cw_ref, cb_ref, gw_ref, gbr_ref, gbi_ref, lam_ref, y_ref, a_s, b_s):
    s, wc = x_ref.shape
    blk = gw_ref.shape[1]
    row8 = _iota2((8, blk), 0)
    sub = _iota2((s // 8, 8, blk), 1)
    for n in range(wc // blk):
        cs = slice(n * blk, (n + 1) * blk)
        x = x_ref[:, cs].astype(F32)
        cw = cw_ref[:, cs]
        xc = cw[CONV_K - 1:CONV_K, :] * x + cb_ref[:, cs]
        for d in range(1, CONV_K):
            xs = pltpu.roll(x, d, 0)
            xs = jnp.concatenate([jnp.where(row8 >= d, xs[0:8], 0.0), xs[8:]], axis=0)
            xc = xc + cw[CONV_K - 1 - d:CONV_K - d, :] * xs
        gates = _mm(xc, gw_ref[n])
        r = _sigmoid(gates[:, :blk] + gbr_ref[:, cs])
        i = _sigmoid(gates[:, blk:] + gbi_ref[:, cs])
        nla = r * (C_RG * _softplus(-lam_ref[:, cs]))
        a = jnp.exp2(nla * -LOG2E)
        var = jnp.tanh(nla) * (a * a + 1.0)
        bb = jnp.where(var > 0.0, var * lax.rsqrt(var), 0.0) * (i * xc)
        a = a.reshape(s // 8, 8, blk)
        bb = bb.reshape(s // 8, 8, blk)
        for d in (1, 2, 4):
            keep = sub >= d
            a_sh = jnp.where(keep, pltpu.roll(a, d, 1), 1.0)
            b_sh = jnp.where(keep, pltpu.roll(bb, d, 1), 0.0)
            bb = a * b_sh + bb
            a = a * a_sh
        a_s[:, cs] = a.reshape(s, blk)
        b_s[:, cs] = bb.reshape(s, blk)

    def group(gi, carry):
        rows = pl.ds(pl.multiple_of(gi * 8, 8), 8)
        h = a_s[rows, :] * carry + b_s[rows, :]
        b_s[rows, :] = h
        return jnp.broadcast_to(h[7:8, :], h.shape)

    lax.fori_loop(0, s // 8, group, jnp.zeros((8, wc), F32), unroll=8)
    z = z_ref[...].astype(F32)
    y_ref[...] = (b_s[...] * _silu(z)).astype(y_ref.dtype)


def _rglru(p3, cw, cb, gw, gb, lam):
    b, s, _ = p3.shape
    w_c = cw.shape[1]
    blk = w_c // N_BLK_C
    nsp = w_c // RG_COLS
    return pl.pallas_call(
        _rglru_body,
        grid=(b, nsp),
        in_specs=[
            pl.BlockSpec((None, s, RG_COLS), lambda bi, n: (bi, 0, n)),
            pl.BlockSpec((None, s, RG_COLS), lambda bi, n: (bi, 0, nsp + n)),
            pl.BlockSpec((CONV_K, RG_COLS), lambda bi, n: (0, n)),
            pl.BlockSpec((1, RG_COLS), lambda bi, n: (0, n)),
            pl.BlockSpec((RG_COLS // blk, blk, 2 * blk), lambda bi, n: (n, 0, 0)),
            pl.BlockSpec((1, RG_COLS), lambda bi, n: (0, n)),
            pl.BlockSpec((1, RG_COLS), lambda bi, n: (0, nsp + n)),
            pl.BlockSpec((1, RG_COLS), lambda bi, n: (0, n)),
        ],
        out_specs=pl.BlockSpec((None, s, RG_COLS), lambda bi, n: (bi, 0, n)),
        out_shape=jax.ShapeDtypeStruct((b, s, w_c), BF16),
        scratch_shapes=[pltpu.VMEM((s, RG_COLS), F32), pltpu.VMEM((s, RG_COLS), F32)],
        compiler_params=_params(2),
        name="rglru",
    )(p3, p3, cw, cb, gw, gb, gb, lam)


def _unit_lower_inverse(a_list):
    r, c = _iota2((CHUNK, CHUNK), 0), _iota2((CHUNK, CHUNK), 1)
    eye = jnp.where(r == c, 1.0, 0.0).astype(F32)

    def pair_mask(sh):
        rb, cb_ = r >> sh, c >> sh
        return ((rb & 1) == 1) & (cb_ == rb - 1)

    ts = [eye - jnp.where(pair_mask(0), a, 0.0) for a in a_list]
    for sh in range(1, 6):
        mask = pair_mask(sh)
        x1 = [_mm(t, jnp.where(mask, a, 0.0)) for t, a in zip(ts, a_list)]
        x2 = [_mm(x, t) for x, t in zip(x1, ts)]
        ts = [t - x for t, x in zip(ts, x2)]
    return ts


def _gdn_body(q_ref, k_ref, v_ref, z_ref, g_ref, gt_ref, cw_ref, al_c_ref, dt_c_ref, al_r_ref, dt_r_ref, on_ref,
              y_ref, s_st, tail, qkv, col_d, col_b, row_d, lhs_s, o0_s, n_s, *, nt):
    ts = q_ref.shape[0]
    nct = ts // CHUNK
    w_d = q_ref.shape[1]
    h_d = w_d // HD_D
    scale = HD_D ** -0.5
    t = pl.program_id(0)
    srcs = (q_ref, k_ref, v_ref)
    sh_r, sh_c = _iota2((CHUNK, 16 + CHUNK), 0), _iota2((CHUNK, 16 + CHUNK), 1)
    shift = jnp.concatenate([sh_c == 16 + sh_r - d for d in range(CONV_K)], axis=0).astype(BF16)

    def conv_block(slot, r0, first, cbk):
        ref = srcs[cbk // h_d]
        ci = slice((cbk % h_d) * HD_D, (cbk % h_d + 1) * HD_D)
        cs = slice(cbk * HD_D, (cbk + 1) * HD_D)
        cur = ref[pl.ds(r0, CHUNK), ci]
        if isinstance(first, bool):
            halo = tail[:, cs] if first else ref[pl.ds(r0 - 16, 16), ci]
        else:
            prev = ref[pl.ds(pl.multiple_of(jnp.maximum(r0 - 16, 0), 16), 16), ci]
            halo = jnp.where(first, tail[:, cs], prev)
        xw = jnp.concatenate([halo, cur], axis=0)
        sh = jnp.dot(shift, xw, preferred_element_type=F32)
        acc = cw_ref[CONV_K - 1:CONV_K, cs] * sh[0:CHUNK]
        for d in range(1, CONV_K):
            acc = acc + cw_ref[CONV_K - 1 - d:CONV_K - d, cs] * sh[d * CHUNK:(d + 1) * CHUNK]
        acc = _silu(acc)
        if cbk < 2 * h_d:
            acc = acc * lax.rsqrt(jnp.sum(acc * acc, axis=-1, keepdims=True) + EPS)
        qkv[slot, pl.ds(r0, CHUNK), cs] = acc

    def conv_rows(slot, rb):
        r0 = rb * CHUNK if isinstance(rb, int) else pl.multiple_of(rb * CHUNK, CHUNK)
        for cbk in range(3 * h_d):
            conv_block(slot, r0, rb == 0, cbk)

    def save_tail():
        keep = lax.rem(t + 1, nt) != 0
        for w, ref in enumerate(srcs):
            tail[:, w * w_d:(w + 1) * w_d] = jnp.where(keep, ref[ts - 16:ts, :], 0.0)

    @pl.when(t == 0)
    def _():
        tail[...] = jnp.zeros_like(tail)

        def rows(rb, carry):
            conv_rows(0, rb)
            return carry

        lax.fori_loop(0, nct, rows, 0)
        save_tail()

    @pl.when(t > 0)
    def _():
        cur_slot = lax.rem(t, 2)
        prv_slot = 1 - cur_slot

        @pl.when(lax.rem(t - 1, nt) == 0)
        def _():
            s_st[...] = jnp.zeros_like(s_st)

        g = g_ref[...]
        col_b[...] = _sigmoid(g)
        col_d[...] = _chunk_cumsum(-jnp.exp(al_c_ref[...]) * _softplus(g + dt_c_ref[...]), CHUNK) * LOG2E
        gt = gt_ref[...]
        g_r = -jnp.exp(al_r_ref[...]) * _softplus(gt[0:h_d] + dt_r_ref[...])
        row_d[...] = _mm_f32(g_r.reshape(h_d * nct, CHUNK), _triu(CHUNK)).reshape(h_d, nct, CHUNK) * LOG2E

        r_i, c_i = _iota2((CHUNK, CHUNK), 0), _iota2((CHUNK, CHUNK), 1)
        tril = c_i <= r_i
        strict = c_i < r_i
        hs = range(h_d)

        def group(gi):
            probs = [(cc, h) for cc in range(GDN_GROUP) for h in hs]
            cidx = [gi * GDN_GROUP + cc for cc in range(GDN_GROUP)]
            rows = [pl.ds(ci * CHUNK, CHUNK) for ci in cidx]
            dc_all = [col_d[r, :] for r in rows]
            beta_all = [col_b[r, :] for r in rows]
            dc = [dc_all[cc][:, h:h + 1] for cc, h in probs]
            beta = [beta_all[cc][:, h_d + h:h_d + h + 1] for cc, h in probs]
            q = [qkv[prv_slot, rows[cc], h * HD_D:(h + 1) * HD_D] for cc, h in probs]
            k = [qkv[prv_slot, rows[cc], w_d + h * HD_D:w_d + (h + 1) * HD_D] for cc, h in probs]
            v = [qkv[prv_slot, rows[cc], 2 * w_d + h * HD_D:2 * w_d + (h + 1) * HD_D] for cc, h in probs]
            ps = range(len(probs))
            gam = [jnp.exp2(jnp.where(tril, dc[i] - row_d[probs[i][1], pl.ds(cidx[probs[i][0]], 1), :], NEG)) for i in ps]
            kb = [k[i] * beta[i] for i in ps]
            kk = [_mm_nt(kb[i], k[i]) for i in ps]
            t_inv = _unit_lower_inverse([jnp.where(strict, kk[i] * gam[i], 0.0) for i in ps])
            ed = [jnp.exp2(dc[i]) for i in ps]
            uw = [_mm(t_inv[i], jnp.concatenate([v[i] * beta[i], kb[i] * ed[i]], axis=1)) for i in ps]
            qs = [q[i] * scale for i in ps]
            qk = [_mm_nt(qs[i], k[i]) * gam[i] for i in ps]
            kd = [k[i] * jnp.exp2(dc[i][CHUNK - 1:CHUNK, :] - dc[i]) for i in ps]
            x1 = [_mm(qk[i], uw[i]) for i in ps]
            x2 = [_mm_tn(kd[i], uw[i]) for i in ps]
            for i, (cc, h) in enumerate(probs):
                lhs_s[cc, h] = jnp.concatenate([qs[i] * ed[i] - x1[i][:, HD_D:], x2[i][:, HD_D:]], axis=0).astype(BF16)
                o0_s[cc, h] = x1[i][:, :HD_D]
                n_s[cc, h] = x2[i][:, :HD_D]
            for rr in range(GDN_GROUP):
                conv_rows(cur_slot, gi * GDN_GROUP + rr)

            def step(cc):
                c = gi * GDN_GROUP + cc
                rws = pl.ds(c * CHUNK, CHUNK)
                d_last = jnp.exp2(col_d[pl.ds(c * CHUNK + CHUNK - 1, 1), :])
                st = [s_st[h] for h in hs]
                r = [_mm(lhs_s[cc, h], st[h]) for h in hs]
                for h in hs:
                    o = o0_s[cc, h] + r[h][:CHUNK]
                    s_st[h] = d_last[:, h:h + 1] * st[h] + n_s[cc, h] - r[h][CHUNK:]
                    on = o * lax.rsqrt(jnp.mean(o * o, axis=-1, keepdims=True) + EPS) * on_ref[...]
                    z = z_ref[rws, h * HD_D:(h + 1) * HD_D].astype(F32)
                    y_ref[rws, h * HD_D:(h + 1) * HD_D] = (on * _silu(z)).astype(y_ref.dtype)

            for cc in range(GDN_GROUP):
                step(cc)

        for gi in range(nct // GDN_GROUP):
            group(gi)
        save_tail()


def _gdn(p3, g3, gt4, cw, al_c, dt_c, al_r, dt_r, onorm, ts):
    b, s, _ = p3.shape
    w_d = 1024
    h_d = w_d // HD_D
    nct = ts // CHUNK
    nt = s // ts
    n_tiles = b * nt

    def conv_spec(sec):
        def imap(t, sec=sec):
            tc = jnp.minimum(t, n_tiles - 1)
            return (tc // nt, tc % nt, 2 + sec)
        return pl.BlockSpec((None, ts, w_d), imap)

    def prev_map(last):
        def imap(t):
            tp = jnp.maximum(t - 1, 0)
            return (tp // nt, tp % nt, last)
        return imap

    def gt_map(t):
        tp = jnp.maximum(t - 1, 0)
        return (tp // nt, 0, tp % nt, 0)

    return pl.pallas_call(
        functools.partial(_gdn_body, nt=nt),
        grid=(n_tiles + 1,),
        in_specs=[
            conv_spec(0), conv_spec(1), conv_spec(2),
            pl.BlockSpec((None, ts, w_d), prev_map(5)),
            pl.BlockSpec((None, ts, LANES), prev_map(0)),
            pl.BlockSpec((None, 2 * h_d, nct, CHUNK), gt_map),
            pl.BlockSpec((CONV_K, 3 * w_d), lambda t: (0, 0)),
            pl.BlockSpec((1, LANES), lambda t: (0, 0)),
            pl.BlockSpec((1, LANES), lambda t: (0, 0)),
            pl.BlockSpec((h_d, 1, 1), lambda t: (0, 0, 0)),
            pl.BlockSpec((h_d, 1, 1), lambda t: (0, 0, 0)),
            pl.BlockSpec((1, HD_D), lambda t: (0, 0)),
        ],
        out_specs=pl.BlockSpec((None, ts, w_d), prev_map(0)),
        out_shape=jax.ShapeDtypeStruct((b, s, w_d), BF16),
        scratch_shapes=[
            pltpu.VMEM((h_d, HD_D, HD_D), F32),
            pltpu.VMEM((16, 3 * w_d), BF16),
            pltpu.VMEM((2, ts, 3 * w_d), F32),
            pltpu.VMEM((ts, LANES), F32),
            pltpu.VMEM((ts, LANES), F32),
            pltpu.VMEM((h_d, nct, CHUNK), F32),
            pltpu.VMEM((GDN_GROUP, h_d, CHUNK + HD_D, HD_D), BF16),
            pltpu.VMEM((GDN_GROUP, h_d, CHUNK, HD_D), F32),
            pltpu.VMEM((GDN_GROUP, h_d, HD_D, HD_D), F32),
        ],
        compiler_params=_params(1),
        name="gated_delta",
    )(p3, p3, p3, p3, g3, gt4, cw, al_c, dt_c, al_r, dt_r, onorm)


def _rel_row(rel_bias):
    t = (np.arange(REL_W) + CHUNK - 1) % REL_W
    idx = np.clip(LOOKBACK * CHUNK + CHUNK - 1 - t, -REL_MAX, REL_MAX) + REL_MAX
    return rel_bias[:, idx].astype(F32)[:, None, :]


def _pad_lanes(v, n=LANES):
    return jnp.pad(v, ((0, 0), (0, n - v.shape[1])))


def _pad_rows(v, n=LANES):
    return jnp.pad(v, ((0, n - v.shape[0]), (0, 0)))


def _weight_t(w):
    return jnp.swapaxes(w, 0, 1).astype(BF16)


def _gates_t(g3, n, length):
    b, s, _ = g3.shape
    return jnp.transpose(g3[:, :, :n], (0, 2, 1)).reshape(b, n, s // length, length)


def kernel(x, ev_norm, ev_w_in, ev_if_bias, ev_qn_gain, ev_kn_gain, ev_rel_bias, ev_w_out, od_norm, od_w_in, od_conv_c_w, od_conv_c_b, od_gate_w, od_gate_b, od_lambda, od_conv_d_w, od_a_log, od_dt_bias, od_onorm, od_w_out):
    b, s, d = x.shape
    m = b * s
    half = d // 2
    tm = min(1024, m)
    ts = min(512, s)
    ts_b = min(1024, s)
    assert half == 1024 and m % tm == 0 and s % ts == 0 and s % (CHUNK * ATTN_GROUP) == 0 and s % ts_b == 0
    assert ts_b % MLSTM_L == 0
    x2 = x.reshape(m, d)
    depth = ev_norm.shape[0] + od_norm.shape[0]
    for layer in range(depth):
        j = layer // 2
        if layer % 2 == 0:
            n_main = 9 * half
            w_t = _weight_t(ev_w_in[j])
            p, g = _inproj(x2, ev_norm[j].reshape(1, d), w_t, _pad_rows(w_t[n_main:]), n_main, tm, n_main // 4)
            p3, g3 = p.reshape(b, s, n_main), g.reshape(b, s, LANES)
            ya = _attn(p3, ev_qn_gain[j].reshape(1, HD_A), ev_kn_gain[j].reshape(1, HD_A), _rel_row(ev_rel_bias[j]))
            bias = ev_if_bias[j].astype(F32)
            yb = _mlstm(p3, _gates_t(g3, 2 * H_B, MLSTM_L), bias.reshape(-1, 1, 1), ts_b)
            x2 = _outproj(ya.reshape(m, half), yb.reshape(m, half), ev_w_out[j], x2, tm, 1024)
        else:
            n_main = 6 * half
            h_d = half // HD_D
            w_t = _weight_t(od_w_in[j])
            p, g = _inproj(x2, od_norm[j].reshape(1, d), w_t, _pad_rows(w_t[n_main:]), n_main, tm, n_main // 3)
            p3, g3 = p.reshape(b, s, n_main), g.reshape(b, s, LANES)
            yc = _rglru(p3, od_conv_c_w[j], od_conv_c_b[j].reshape(1, -1), od_gate_w[j].astype(BF16),
                        od_gate_b[j].reshape(1, -1), od_lambda[j].reshape(1, -1))
            al, dt = od_a_log[j].astype(F32), od_dt_bias[j].astype(F32)
            yd = _gdn(p3, g3, _gates_t(g3, 2 * h_d, CHUNK), od_conv_d_w[j], _pad_lanes(al.reshape(1, -1)),
                      _pad_lanes(dt.reshape(1, -1)), al.reshape(-1, 1, 1), dt.reshape(-1, 1, 1),
                      od_onorm[j].reshape(1, HD_D), ts)
            x2 = _outproj(yc.reshape(m, half), yd.reshape(m, half), od_w_out[j], x2, tm, 1024)
    return x2.reshape(b, s, d)
```

```python
import functools

import numpy as np
import jax
import jax.numpy as jnp
from jax import lax
from jax.experimental import pallas as pl
from jax.experimental.pallas import tpu as pltpu

F32 = jnp.float32
BF16 = jnp.bfloat16

CHUNK = 64
EPS = 1e-6
NEG = -1e30
CONV_K = 4
LOG2E = 1.4426950408889634
LANES = 128
HD_A = 128
LOOKBACK = 8
BAND = (LOOKBACK + 1) * CHUNK
REL_MAX = 256
REL_W = 640
ATTN_GROUP = 16
ATTN_HEADS = 2
H_B = 4
MLSTM_L = 256
N_BLK_C = 8
C_RG = 8.0
RG_COLS = 1024
HD_D = 128
GDN_GROUP = 4

V7X_VMEM_BYTES = 64 * 1024 * 1024
VMEM_LIMIT = V7X_VMEM_BYTES - 8 * 1024 * 1024


def _params(n_axes):
    return pltpu.CompilerParams(dimension_semantics=("arbitrary",) * n_axes, vmem_limit_bytes=VMEM_LIMIT)


def _mm(a, b):
    return jnp.dot(a.astype(BF16), b.astype(BF16), preferred_element_type=F32)


def _mm_nt(a, b):
    return lax.dot_general(a.astype(BF16), b.astype(BF16), (((1,), (1,)), ((), ())), preferred_element_type=F32)


def _mm_tn(a, b):
    return lax.dot_general(a.astype(BF16), b.astype(BF16), (((0,), (0,)), ((), ())), preferred_element_type=F32)


def _mm_f32(a, b):
    return jnp.dot(a, b, preferred_element_type=F32, precision=lax.Precision.HIGHEST)


def _log1p(e):
    u = 1.0 + e
    return jnp.where(u == 1.0, e, jnp.log(u) * (e / (u - 1.0)))


def _softplus(x):
    return jnp.maximum(x, 0.0) + _log1p(jnp.exp(-jnp.abs(x)))


def _sigmoid(x):
    return 1.0 / (1.0 + jnp.exp2(x * -LOG2E))


def _silu(x):
    return x * _sigmoid(x)


def _iota2(shape, axis):
    return lax.broadcasted_iota(jnp.int32, shape, axis)


def _chunk_cumsum(x, length):
    pos = _iota2(x.shape, 0) & (length - 1)
    d = 1
    while d < length:
        x = x + jnp.where(pos >= d, pltpu.roll(x, d, 0), 0.0)
        d *= 2
    return x


def _triu(n):
    r, c = _iota2((n, n), 0), _iota2((n, n), 1)
    return jnp.where(r <= c, 1.0, 0.0).astype(F32)


def _inproj_body(x_ref, g_ref, w_ref, wg_ref, o_ref, og_ref, xn_ref):
    @pl.when(pl.program_id(1) == 0)
    def _():
        x = x_ref[...]
        ms = jnp.mean(x * x, axis=-1, keepdims=True)
        xn = ((x * lax.rsqrt(ms + EPS)) * g_ref[...]).astype(BF16)
        xn_ref[...] = xn
        og_ref[...] = _mm_nt(xn, wg_ref[...])

    o_ref[...] = _mm_nt(xn_ref[...], w_ref[...]).astype(o_ref.dtype)


def _inproj(x2, g, w_t, wg_t, n, tm, tn):
    m, d = x2.shape
    return pl.pallas_call(
        _inproj_body,
        grid=(m // tm, n // tn),
        in_specs=[
            pl.BlockSpec((tm, d), lambda i, j: (i, 0)),
            pl.BlockSpec((1, d), lambda i, j: (0, 0)),
            pl.BlockSpec((tn, d), lambda i, j: (j, 0)),
            pl.BlockSpec((LANES, d), lambda i, j: (0, 0)),
        ],
        out_specs=[
            pl.BlockSpec((tm, tn), lambda i, j: (i, j)),
            pl.BlockSpec((tm, LANES), lambda i, j: (i, 0)),
        ],
        out_shape=[jax.ShapeDtypeStruct((m, n), BF16), jax.ShapeDtypeStruct((m, LANES), F32)],
        scratch_shapes=[pltpu.VMEM((tm, d), BF16)],
        compiler_params=_params(2),
        name="inproj",
    )(x2, g, w_t, wg_t)


def _outproj_body(ya_ref, yb_ref, wa_ref, wb_ref, x_ref, o_ref, wa_s, wb_s):
    @pl.when(pl.program_id(1) == 0)
    def _():
        wa_s[...] = wa_ref[...].astype(BF16)
        wb_s[...] = wb_ref[...].astype(BF16)

    acc = jnp.dot(ya_ref[...], wa_s[...], preferred_element_type=F32)
    acc = acc + jnp.dot(yb_ref[...], wb_s[...], preferred_element_type=F32)
    o_ref[...] = x_ref[...] + acc


def _outproj(ya, yb, w, x2, tm, tn):
    m, d = x2.shape
    kh = ya.shape[1]
    return pl.pallas_call(
        _outproj_body,
        grid=(d // tn, m // tm),
        in_specs=[
            pl.BlockSpec((tm, kh), lambda j, i: (i, 0)),
            pl.BlockSpec((tm, kh), lambda j, i: (i, 0)),
            pl.BlockSpec((kh, tn), lambda j, i: (0, j)),
            pl.BlockSpec((kh, tn), lambda j, i: (1, j)),
            pl.BlockSpec((tm, tn), lambda j, i: (i, j)),
        ],
        out_specs=pl.BlockSpec((tm, tn), lambda j, i: (i, j)),
        out_shape=jax.ShapeDtypeStruct((m, d), F32),
        scratch_shapes=[pltpu.VMEM((kh, tn), BF16), pltpu.VMEM((kh, tn), BF16)],
        compiler_params=_params(2),
        name="outproj",
    )(ya, yb, w, w, x2)


def _attn_body(q_ref, k_ref, v_ref, z_ref, qg_ref, kg_ref, c_ref, o_ref, qs, ks, vs, bias_s):
    s = q_ref.shape[0]
    pad = LOOKBACK * CHUNK
    col = _iota2((CHUNK, BAND), 1)
    for hh in range(ATTN_HEADS):
        hc = slice(hh * HD_A, (hh + 1) * HD_A)
        q = q_ref[:, hc].astype(F32)
        q = q * lax.rsqrt(jnp.mean(q * q, axis=-1, keepdims=True) + EPS) * (qg_ref[...] * (HD_A ** -0.5 * LOG2E))
        qs[hh] = q.astype(BF16)
        k = k_ref[:, hc].astype(F32)
        k = k * lax.rsqrt(jnp.mean(k * k, axis=-1, keepdims=True) + EPS) * kg_ref[...]
        ks[hh, 0:pad, :] = jnp.zeros((pad, HD_A), BF16)
        ks[hh, pad:pad + s, :] = k.astype(BF16)
        vs[hh, 0:pad, :] = jnp.zeros((pad, HD_A), BF16)
        vs[hh, pad:pad + s, :] = v_ref[:, hc]
        cb = jnp.broadcast_to(c_ref[hh] * LOG2E, (CHUNK, REL_W))
        bias_s[hh] = pltpu.roll(cb, 0, 1, stride=1, stride_axis=0)[:, :BAND]

        for it in range(s // (CHUNK * ATTN_GROUP)):
            gs = range(ATTN_GROUP)
            n = [it * ATTN_GROUP + g for g in gs]
            r0 = [n[g] * CHUNK for g in gs]
            sc = [_mm_nt(qs[hh, pl.ds(r0[g], CHUNK), :], ks[hh, pl.ds(r0[g], BAND), :]) + bias_s[hh] for g in gs]
            sc = [jnp.where(col >= (LOOKBACK - n[g]) * CHUNK, sc[g], NEG) if n[g] < LOOKBACK else sc[g] for g in gs]
            p = [jnp.exp2(sc[g] - jnp.max(sc[g], axis=-1, keepdims=True)) for g in gs]
            pv = [_mm(p[g], vs[hh, pl.ds(r0[g], BAND), :]) for g in gs]
            for g in gs:
                o = pv[g] * (1.0 / jnp.sum(p[g], axis=-1, keepdims=True))
                z = z_ref[pl.ds(r0[g], CHUNK), hc].astype(F32)
                o_ref[pl.ds(r0[g], CHUNK), hc] = (o * _silu(z)).astype(o_ref.dtype)


def _attn(p3, qg, kg, crel):
    b, s, _ = p3.shape
    h_a = crel.shape[0]
    hp = h_a // ATTN_HEADS
    wh = ATTN_HEADS * HD_A

    def col_spec(sec):
        return pl.BlockSpec((None, s, wh), lambda bi, h, sec=sec: (bi, 0, sec * hp + h))

    return pl.pallas_call(
        _attn_body,
        grid=(b, hp),
        in_specs=[
            col_spec(0), col_spec(1), col_spec(2), col_spec(3),
            pl.BlockSpec((1, HD_A), lambda bi, h: (0, 0)),
            pl.BlockSpec((1, HD_A), lambda bi, h: (0, 0)),
            pl.BlockSpec((ATTN_HEADS, 1, REL_W), lambda bi, h: (h, 0, 0)),
        ],
        out_specs=pl.BlockSpec((None, s, wh), lambda bi, h: (bi, 0, h)),
        out_shape=jax.ShapeDtypeStruct((b, s, h_a * HD_A), BF16),
        scratch_shapes=[
            pltpu.VMEM((ATTN_HEADS, s, HD_A), BF16),
            pltpu.VMEM((ATTN_HEADS, s + LOOKBACK * CHUNK, HD_A), BF16),
            pltpu.VMEM((ATTN_HEADS, s + LOOKBACK * CHUNK, HD_A), BF16),
            pltpu.VMEM((ATTN_HEADS, CHUNK, BAND), F32),
        ],
        compiler_params=_params(2),
        name="chunk_attn",
    )(p3, p3, p3, p3, qg, kg, crel)


def _mlstm_body(q_ref, k_ref, v_ref, o_ref, z_ref, gt_ref, br_ref, y_ref, c_st, n_st, m_st, row_b, row_i):
    ts = q_ref.shape[0]
    ln = MLSTM_L
    nct = ts // ln
    hd = q_ref.shape[1] // H_B
    scale = hd ** -0.5
    j = pl.program_id(1)
    hs = range(H_B)
    hc = [slice(h * hd, (h + 1) * hd) for h in hs]

    @pl.when(j == 0)
    def _():
        c_st[...] = jnp.zeros_like(c_st)
        n_st[...] = jnp.zeros_like(n_st)
        m_st[...] = jnp.zeros_like(m_st)
        gt = gt_ref[...] + br_ref[...]
        for h in hs:
            row_b[h] = _mm_f32(-_softplus(-gt[H_B + h]), _triu(ln)) * LOG2E
            row_i[h] = gt[h] * LOG2E

    tril = _iota2((ln, ln), 1) <= _iota2((ln, ln), 0)
    log2_scale = float(np.log2(scale))
    m = [m_st[h][0:1, 0:1] for h in hs]
    cm = [c_st[h] for h in hs]
    nm = [n_st[h][0:1, :] for h in hs]
    chs = [(c, h) for c in range(nct) for h in hs]
    rows = [slice(c * ln, (c + 1) * ln) for c in range(nct)]
    li_r = {(c, h): row_i[h, pl.ds(j * nct + c, 1), :] for c, h in chs}
    bc_r = {(c, h): row_b[h, pl.ds(j * nct + c, 1), :] for c, h in chs}
    pad = jnp.zeros((LANES - 2 * H_B, ln), F32)
    cols = [jnp.concatenate([li_r[c, h] for h in hs] + [bc_r[c, h] for h in hs] + [pad], axis=0).T
            for c in range(nct)]
    lic = {(c, h): cols[c][:, h:h + 1] for c, h in chs}
    bc = {(c, h): cols[c][:, H_B + h:H_B + h + 1] for c, h in chs}
    rl = {ch: li_r[ch] - bc_r[ch] for ch in chs}
    b_last = {ch: bc[ch][ln - 1:ln, :] for ch in chs}
    m_in, m_new = {}, {}
    for c, h in chs:
        m_in[c, h] = m[h]
        m_new[c, h] = jnp.maximum(b_last[c, h] + m[h], jnp.max(b_last[c, h] + rl[c, h], axis=-1, keepdims=True))
        m[h] = m_new[c, h]
    qn = {(c, h): q_ref[rows[c], hc[h]] for c, h in chs}
    kn = {(c, h): k_ref[rows[c], hc[h]] for c, h in chs}
    vn = {(c, h): v_ref[rows[c], hc[h]] for c, h in chs}
    qk = {ch: _mm_nt(qn[ch], kn[ch]) for ch in chs}
    dmat = {ch: jnp.where(tril, bc[ch] + rl[ch], NEG) for ch in chs}
    mt = {ch: jnp.maximum(bc[ch] + m_in[ch], jnp.max(dmat[ch], axis=-1, keepdims=True)) for ch in chs}
    cmt = {ch: mt[ch] - log2_scale for ch in chs}
    w_inter = {ch: jnp.exp2(bc[ch] + m_in[ch] - cmt[ch]) for ch in chs}
    p = {ch: jnp.exp2(dmat[ch] - cmt[ch]) * qk[ch] for ch in chs}
    pv = {ch: _mm(p[ch], vn[ch]) for ch in chs}
    ws_c = {ch: jnp.exp2(b_last[ch] - bc[ch] + lic[ch] - m_new[ch]).astype(BF16) for ch in chs}
    ws_r = {ch: jnp.exp2(b_last[ch] + rl[ch] - m_new[ch]) for ch in chs}
    upd = {ch: _mm_tn(kn[ch] * ws_c[ch], vn[ch]) for ch in chs}
    n_add = {ch: _mm(jnp.broadcast_to(ws_r[ch], (8, ln)), kn[ch])[0:1, :] for ch in chs}
    psum = {ch: jnp.sum(p[ch], axis=-1, keepdims=True) for ch in chs}
    gate = {}
    for c, h in chs:
        og = o_ref[rows[c], hc[h]].astype(F32)
        zg = z_ref[rows[c], hc[h]].astype(F32)
        gate[c, h] = zg * (1.0 / ((1.0 + jnp.exp2(og * -LOG2E)) * (1.0 + jnp.exp2(zg * -LOG2E))))
    for c in range(nct):
        qc = {h: _mm(qn[c, h], cm[h]) for h in hs}
        qn_dot = {h: _mm_nt(qn[c, h], jnp.broadcast_to(nm[h], (8, hd)))[:, 0:1] for h in hs}
        for h in hs:
            num = w_inter[c, h] * qc[h] + pv[c, h]
            den = w_inter[c, h] * qn_dot[h] + psum[c, h]
            hout = num * (1.0 / jnp.maximum(jnp.abs(den), jnp.exp2(-mt[c, h])))
            y_ref[rows[c], hc[h]] = (hout * gate[c, h]).astype(y_ref.dtype)
            w_c = jnp.exp2(b_last[c, h] + m_in[c, h] - m_new[c, h])
            cm[h] = w_c * cm[h] + upd[c, h]
            nm[h] = w_c * nm[h] + n_add[c, h]
    for h in hs:
        c_st[h] = cm[h]
        n_st[h] = jnp.broadcast_to(nm[h], n_st.shape[1:])
        m_st[h] = jnp.broadcast_to(m[h], m_st.shape[1:])


def _mlstm(p3, gt4, bias_row, ts):
    b, s, _ = p3.shape
    w_b = 1024
    hd = w_b // H_B
    ncs = s // MLSTM_L

    def col_spec(sec):
        return pl.BlockSpec((None, ts, w_b), lambda bi, j, sec=sec: (bi, j, 4 + sec))

    return pl.pallas_call(
        _mlstm_body,
        grid=(b, s // ts),
        in_specs=[
            col_spec(0), col_spec(1), col_spec(2), col_spec(3), col_spec(4),
            pl.BlockSpec((None, 2 * H_B, ncs, MLSTM_L), lambda bi, j: (bi, 0, 0, 0)),
            pl.BlockSpec((2 * H_B, 1, 1), lambda bi, j: (0, 0, 0)),
        ],
        out_specs=pl.BlockSpec((None, ts, w_b), lambda bi, j: (bi, j, 0)),
        out_shape=jax.ShapeDtypeStruct((b, s, w_b), BF16),
        scratch_shapes=[
            pltpu.VMEM((H_B, hd, hd), F32),
            pltpu.VMEM((H_B, 8, hd), F32),
            pltpu.VMEM((H_B, 8, LANES), F32),
            pltpu.VMEM((H_B, ncs, MLSTM_L), F32),
            pltpu.VMEM((H_B, ncs, MLSTM_L), F32),
        ],
        compiler_params=_params(2),
        name="mlstm",
    )(p3, p3, p3, p3, p3, gt4, bias_row)


def _rglru_body(x_ref, z_ref, cw_ref, cb_ref, gw_ref, gbr_ref, gbi_ref, lam_ref, y_ref, a_s, b_s):
    s, wc = x_ref.shape
    blk = gw_ref.shape[1]
    row8 = _iota2((8, blk), 0)
    sub = _iota2((s // 8, 8, blk), 1)
    for n in range(wc // blk):
        cs = slice(n * blk, (n + 1) * blk)
        x = x_ref[:, cs].astype(F32)
        cw = cw_ref[:, cs]
        xc = cw[CONV_K - 1:CONV_K, :] * x + cb_ref[:, cs]
        for d in range(1, CONV_K):
            xs = pltpu.roll(x, d, 0)
            xs = jnp.concatenate([jnp.where(row8 >= d, xs[0:8], 0.0), xs[8:]], axis=0)
            xc = xc + cw[CONV_K - 1 - d:CONV_K - d, :] * xs
        gates = _mm(xc, gw_ref[n])
        r = _sigmoid(gates[:, :blk] + gbr_ref[:, cs])
        i = _sigmoid(gates[:, blk:] + gbi_ref[:, cs])
        nla = r * (C_RG * _softplus(-lam_ref[:, cs]))
        a = jnp.exp2(nla * -LOG2E)
        var = jnp.tanh(nla) * (a * a + 1.0)
        bb = jnp.where(var > 0.0, var * lax.rsqrt(var), 0.0) * (i * xc)
        a = a.reshape(s // 8, 8, blk)
        bb = bb.reshape(s // 8, 8, blk)
        for d in (1, 2, 4):
            keep = sub >= d
            a_sh = jnp.where(keep, pltpu.roll(a, d, 1), 1.0)
            b_sh = jnp.where(keep, pltpu.roll(bb, d, 1), 0.0)
            bb = a * b_sh + bb
            a = a * a_sh
        a_s[:, cs] = a.reshape(s, blk)
        b_s[:, cs] = bb.reshape(s, blk)

    def group(gi, carry):
        rows = pl.ds(pl.multiple_of(gi * 8, 8), 8)
        h = a_s[rows, :] * carry + b_s[rows, :]
        b_s[rows, :] = h
        return jnp.broadcast_to(h[7:8, :], h.shape)

    lax.fori_loop(0, s // 8, group, jnp.zeros((8, wc), F32), unroll=8)
    z = z_ref[...].astype(F32)
    y_ref[...] = (b_s[...] * _silu(z)).astype(y_ref.dtype)


def _rglru(p3, cw, cb, gw, gb, lam):
    b, s, _ = p3.shape
    w_c = cw.shape[1]
    blk = w_c // N_BLK_C
    nsp = w_c // RG_COLS
    return pl.pallas_call(
        _rglru_body,
        grid=(b, nsp),
        in_specs=[
            pl.BlockSpec((None, s, RG_COLS), lambda bi, n: (bi, 0, n)),
            pl.BlockSpec((None, s, RG_COLS), lambda bi, n: (bi, 0, nsp + n)),
            pl.BlockSpec((CONV_K, RG_COLS), lambda bi, n: (0, n)),
            pl.BlockSpec((1, RG_COLS), lambda bi, n: (0, n)),
            pl.BlockSpec((RG_COLS // blk, blk, 2 * blk), lambda bi, n: (n, 0, 0)),
            pl.BlockSpec((1, RG_COLS), lambda bi, n: (0, n)),
            pl.BlockSpec((1, RG_COLS), lambda bi, n: (0, nsp + n)),
            pl.BlockSpec((1, RG_COLS), lambda bi, n: (0, n)),
        ],
        out_specs=pl.BlockSpec((None, s, RG_COLS), lambda bi, n: (bi, 0, n)),
        out_shape=jax.ShapeDtypeStruct((b, s, w_c), BF16),
        scratch_shapes=[pltpu.VMEM((s, RG_COLS), F32), pltpu.VMEM((s, RG_COLS), F32)],
        compiler_params=_params(2),
        name="rglru",
    )(p3, p3, cw, cb, gw, gb, gb, lam)


def _unit_lower_inverse(a_list):
    r, c = _iota2((CHUNK, CHUNK), 0), _iota2((CHUNK, CHUNK), 1)
    eye = jnp.where(r == c, 1.0, 0.0).astype(F32)

    def pair_mask(sh):
        rb, cb_ = r >> sh, c >> sh
        return ((rb & 1) == 1) & (cb_ == rb - 1)

    ts = [eye - jnp.where(pair_mask(0), a, 0.0) for a in a_list]
    for sh in range(1, 6):
        mask = pair_mask(sh)
        x1 = [_mm(t, jnp.where(mask, a, 0.0)) for t, a in zip(ts, a_list)]
        x2 = [_mm(x, t) for x, t in zip(x1, ts)]
        ts = [t - x for t, x in zip(ts, x2)]
    return ts


def _gdn_body(q_ref, k_ref, v_ref, z_ref, g_ref, gt_ref, cw_ref, al_c_ref, dt_c_ref, al_r_ref, dt_r_ref, on_ref,
              y_ref, s_st, tail, qkv, col_d, col_b, row_d, lhs_s, o0_s, n_s, *, nt):
    ts = q_ref.shape[0]
    nct = ts // CHUNK
    w_d = q_ref.shape[1]
    h_d = w_d // HD_D
    scale = HD_D ** -0.5
    t = pl.program_id(0)
    srcs = (q_ref, k_ref, v_ref)
    sh_r, sh_c = _iota2((CHUNK, 16 + CHUNK), 0), _iota2((CHUNK, 16 + CHUNK), 1)
    shift = jnp.concatenate([sh_c == 16 + sh_r - d for d in range(CONV_K)], axis=0).astype(BF16)

    def conv_block(slot, r0, first, cbk):
        ref = srcs[cbk // h_d]
        ci = slice((cbk % h_d) * HD_D, (cbk % h_d + 1) * HD_D)
        cs = slice(cbk * HD_D, (cbk + 1) * HD_D)
        cur = ref[pl.ds(r0, CHUNK), ci]
        if isinstance(first, bool):
            halo = tail[:, cs] if first else ref[pl.ds(r0 - 16, 16), ci]
        else:
            prev = ref[pl.ds(pl.multiple_of(jnp.maximum(r0 - 16, 0), 16), 16), ci]
            halo = jnp.where(first, tail[:, cs], prev)
        xw = jnp.concatenate([halo, cur], axis=0)
        sh = jnp.dot(shift, xw, preferred_element_type=F32)
        acc = cw_ref[CONV_K - 1:CONV_K, cs] * sh[0:CHUNK]
        for d in range(1, CONV_K):
            acc = acc + cw_ref[CONV_K - 1 - d:CONV_K - d, cs] * sh[d * CHUNK:(d + 1) * CHUNK]
        acc = _silu(acc)
        if cbk < 2 * h_d:
            acc = acc * lax.rsqrt(jnp.sum(acc * acc, axis=-1, keepdims=True) + EPS)
        qkv[slot, pl.ds(r0, CHUNK), cs] = acc

    def conv_rows(slot, rb):
        r0 = rb * CHUNK if isinstance(rb, int) else pl.multiple_of(rb * CHUNK, CHUNK)
        for cbk in range(3 * h_d):
            conv_block(slot, r0, rb == 0, cbk)

    def save_tail():
        keep = lax.rem(t + 1, nt) != 0
        for w, ref in enumerate(srcs):
            tail[:, w * w_d:(w + 1) * w_d] = jnp.where(keep, ref[ts - 16:ts, :], 0.0)

    @pl.when(t == 0)
    def _():
        tail[...] = jnp.zeros_like(tail)

        def rows(rb, carry):
            conv_rows(0, rb)
            return carry

        lax.fori_loop(0, nct, rows, 0)
        save_tail()

    @pl.when(t > 0)
    def _():
        cur_slot = lax.rem(t, 2)
        prv_slot = 1 - cur_slot

        @pl.when(lax.rem(t - 1, nt) == 0)
        def _():
            s_st[...] = jnp.zeros_like(s_st)

        g = g_ref[...]
        col_b[...] = _sigmoid(g)
        col_d[...] = _chunk_cumsum(-jnp.exp(al_c_ref[...]) * _softplus(g + dt_c_ref[...]), CHUNK) * LOG2E
        gt = gt_ref[...]
        g_r = -jnp.exp(al_r_ref[...]) * _softplus(gt[0:h_d] + dt_r_ref[...])
        row_d[...] = _mm_f32(g_r.reshape(h_d * nct, CHUNK), _triu(CHUNK)).reshape(h_d, nct, CHUNK) * LOG2E

        r_i, c_i = _iota2((CHUNK, CHUNK), 0), _iota2((CHUNK, CHUNK), 1)
        tril = c_i <= r_i
        strict = c_i < r_i
        hs = range(h_d)

        def group(gi):
            probs = [(cc, h) for cc in range(GDN_GROUP) for h in hs]
            cidx = [gi * GDN_GROUP + cc for cc in range(GDN_GROUP)]
            rows = [pl.ds(ci * CHUNK, CHUNK) for ci in cidx]
            dc_all = [col_d[r, :] for r in rows]
            beta_all = [col_b[r, :] for r in rows]
            dc = [dc_all[cc][:, h:h + 1] for cc, h in probs]
            beta = [beta_all[cc][:, h_d + h:h_d + h + 1] for cc, h in probs]
            q = [qkv[prv_slot, rows[cc], h * HD_D:(h + 1) * HD_D] for cc, h in probs]
            k = [qkv[prv_slot, rows[cc], w_d + h * HD_D:w_d + (h + 1) * HD_D] for cc, h in probs]
            v = [qkv[prv_slot, rows[cc], 2 * w_d + h * HD_D:2 * w_d + (h + 1) * HD_D] for cc, h in probs]
            ps = range(len(probs))
            gam = [jnp.exp2(jnp.where(tril, dc[i] - row_d[probs[i][1], pl.ds(cidx[probs[i][0]], 1), :], NEG)) for i in ps]
            kb = [k[i] * beta[i] for i in ps]
            kk = [_mm_nt(kb[i], k[i]) for i in ps]
            t_inv = _unit_lower_inverse([jnp.where(strict, kk[i] * gam[i], 0.0) for i in ps])
            ed = [jnp.exp2(dc[i]) for i in ps]
            uw = [_mm(t_inv[i], jnp.concatenate([v[i] * beta[i], kb[i] * ed[i]], axis=1)) for i in ps]
            qs = [q[i] * scale for i in ps]
            qk = [_mm_nt(qs[i], k[i]) * gam[i] for i in ps]
            kd = [k[i] * jnp.exp2(dc[i][CHUNK - 1:CHUNK, :] - dc[i]) for i in ps]
            x1 = [_mm(qk[i], uw[i]) for i in ps]
            x2 = [_mm_tn(kd[i], uw[i]) for i in ps]
            for i, (cc, h) in enumerate(probs):
                lhs_s[cc, h] = jnp.concatenate([qs[i] * ed[i] - x1[i][:, HD_D:], x2[i][:, HD_D:]], axis=0).astype(BF16)
                o0_s[cc, h] = x1[i][:, :HD_D]
                n_s[cc, h] = x2[i][:, :HD_D]
            for rr in range(GDN_GROUP):
                conv_rows(cur_slot, gi * GDN_GROUP + rr)

            def step(cc):
                c = gi * GDN_GROUP + cc
                rws = pl.ds(c * CHUNK, CHUNK)
                d_last = jnp.exp2(col_d[pl.ds(c * CHUNK + CHUNK - 1, 1), :])
                st = [s_st[h] for h in hs]
                r = [_mm(lhs_s[cc, h], st[h]) for h in hs]
                for h in hs:
                    o = o0_s[cc, h] + r[h][:CHUNK]
                    s_st[h] = d_last[:, h:h + 1] * st[h] + n_s[cc, h] - r[h][CHUNK:]
                    on = o * lax.rsqrt(jnp.mean(o * o, axis=-1, keepdims=True) + EPS) * on_ref[...]
                    z = z_ref[rws, h * HD_D:(h + 1) * HD_D].astype(F32)
                    y_ref[rws, h * HD_D:(h + 1) * HD_D] = (on * _silu(z)).astype(y_ref.dtype)

            for cc in range(GDN_GROUP):
                step(cc)

        for gi in range(nct // GDN_GROUP):
            group(gi)
        save_tail()


def _gdn(p3, g3, gt4, cw, al_c, dt_c, al_r, dt_r, onorm, ts):
    b, s, _ = p3.shape
    w_d = 1024
    h_d = w_d // HD_D
    nct = ts // CHUNK
    nt = s // ts
    n_tiles = b * nt

    def conv_spec(sec):
        def imap(t, sec=sec):
            tc = jnp.minimum(t, n_tiles - 1)
            return (tc // nt, tc % nt, 2 + sec)
        return pl.BlockSpec((None, ts, w_d), imap)

    def prev_map(last):
        def imap(t):
            tp = jnp.maximum(t - 1, 0)
            return (tp // nt, tp % nt, last)
        return imap

    def gt_map(t):
        tp = jnp.maximum(t - 1, 0)
        return (tp // nt, 0, tp % nt, 0)

    return pl.pallas_call(
        functools.partial(_gdn_body, nt=nt),
        grid=(n_tiles + 1,),
        in_specs=[
            conv_spec(0), conv_spec(1), conv_spec(2),
            pl.BlockSpec((None, ts, w_d), prev_map(5)),
            pl.BlockSpec((None, ts, LANES), prev_map(0)),
            pl.BlockSpec((None, 2 * h_d, nct, CHUNK), gt_map),
            pl.BlockSpec((CONV_K, 3 * w_d), lambda t: (0, 0)),
            pl.BlockSpec((1, LANES), lambda t: (0, 0)),
            pl.BlockSpec((1, LANES), lambda t: (0, 0)),
            pl.BlockSpec((h_d, 1, 1), lambda t: (0, 0, 0)),
            pl.BlockSpec((h_d, 1, 1), lambda t: (0, 0, 0)),
            pl.BlockSpec((1, HD_D), lambda t: (0, 0)),
        ],
        out_specs=pl.BlockSpec((None, ts, w_d), prev_map(0)),
        out_shape=jax.ShapeDtypeStruct((b, s, w_d), BF16),
        scratch_shapes=[
            pltpu.VMEM((h_d, HD_D, HD_D), F32),
            pltpu.VMEM((16, 3 * w_d), BF16),
            pltpu.VMEM((2, ts, 3 * w_d), F32),
            pltpu.VMEM((ts, LANES), F32),
            pltpu.VMEM((ts, LANES), F32),
            pltpu.VMEM((h_d, nct, CHUNK), F32),
            pltpu.VMEM((GDN_GROUP, h_d, CHUNK + HD_D, HD_D), BF16),
            pltpu.VMEM((GDN_GROUP, h_d, CHUNK, HD_D), F32),
            pltpu.VMEM((GDN_GROUP, h_d, HD_D, HD_D), F32),
        ],
        compiler_params=_params(1),
        name="gated_delta",
    )(p3, p3, p3, p3, g3, gt4, cw, al_c, dt_c, al_r, dt_r, onorm)


def _rel_row(rel_bias):
    t = (np.arange(REL_W) + CHUNK - 1) % REL_W
    idx = np.clip(LOOKBACK * CHUNK + CHUNK - 1 - t, -REL_MAX, REL_MAX) + REL_MAX
    return rel_bias[:, idx].astype(F32)[:, None, :]


def _pad_lanes(v, n=LANES):
    return jnp.pad(v, ((0, 0), (0, n - v.shape[1])))


def _pad_rows(v, n=LANES):
    return jnp.pad(v, ((0, n - v.shape[0]), (0, 0)))


def _weight_t(w):
    return jnp.swapaxes(w, 0, 1).astype(BF16)


def _gates_t(g3, n, length):
    b, s, _ = g3.shape
    return jnp.transpose(g3[:, :, :n], (0, 2, 1)).reshape(b, n, s // length, length)


def kernel(x, ev_norm, ev_w_in, ev_if_bias, ev_qn_gain, ev_kn_gain, ev_rel_bias, ev_w_out, od_norm, od_w_in, od_conv_c_w, od_conv_c_b, od_gate_w, od_gate_b, od_lambda, od_conv_d_w, od_a_log, od_dt_bias, od_onorm, od_w_out):
    b, s, d = x.shape
    m = b * s
    half = d // 2
    tm = min(1024, m)
    ts = min(512, s)
    assert half == 1024 and m % tm == 0 and s % ts == 0 and s % (CHUNK * ATTN_GROUP) == 0 and ts % MLSTM_L == 0
    x2 = x.reshape(m, d)
    depth = ev_norm.shape[0] + od_norm.shape[0]
    for layer in range(depth):
        j = layer // 2
        if layer % 2 == 0:
            n_main = 9 * half
            w_t = _weight_t(ev_w_in[j])
            p, g = _inproj(x2, ev_norm[j].reshape(1, d), w_t, _pad_rows(w_t[n_main:]), n_main, tm, n_main // 4)
            p3, g3 = p.reshape(b, s, n_main), g.reshape(b, s, LANES)
            ya = _attn(p3, ev_qn_gain[j].reshape(1, HD_A), ev_kn_gain[j].reshape(1, HD_A), _rel_row(ev_rel_bias[j]))
            bias = ev_if_bias[j].astype(F32)
            yb = _mlstm(p3, _gates_t(g3, 2 * H_B, MLSTM_L), bias.reshape(-1, 1, 1), MLSTM_L)
            x2 = _outproj(ya.reshape(m, half), yb.reshape(m, half), ev_w_out[j], x2, tm, 1024)
        else:
            n_main = 6 * half
            h_d = half // HD_D
            w_t = _weight_t(od_w_in[j])
            p, g = _inproj(x2, od_norm[j].reshape(1, d), w_t, _pad_rows(w_t[n_main:]), n_main, tm, n_main // 3)
            p3, g3 = p.reshape(b, s, n_main), g.reshape(b, s, LANES)
            yc = _rglru(p3, od_conv_c_w[j], od_conv_c_b[j].reshape(1, -1), od_gate_w[j].astype(BF16),
                        od_gate_b[j].reshape(1, -1), od_lambda[j].reshape(1, -1))
            al, dt = od_a_log[j].astype(F32), od_dt_bias[j].astype(F32)
            yd = _gdn(p3, g3, _gates_t(g3, 2 * h_d, CHUNK), od_conv_d_w[j], _pad_lanes(al.reshape(1, -1)),
                      _pad_lanes(dt.reshape(1, -1)), al.reshape(-1, 1, 1), dt.reshape(-1, 1, 1),
                      od_onorm[j].reshape(1, HD_D), ts)
            x2 = _outproj(yc.reshape(m, half), yd.reshape(m, half), od_w_out[j], x2, tm, 1024)
    return x2.reshape(b, s, d)
```
